```python
import jax, jax.numpy as jnp
from jax import lax
import numpy as np

D_MODEL = 1024
BATCH = 8
SEQ = 4096
DEPTH = 1

SSD_INNER = 2 * D_MODEL
SSD_HEAD_DIM = 64
SSD_HEADS = SSD_INNER // SSD_HEAD_DIM
SSD_GROUPS = 8
SSD_HEADS_PER_GROUP = SSD_HEADS // SSD_GROUPS
SSD_STATE = 128
SSD_CONV = 4
SSD_CHUNK = 128
SSD_CONV_DIM = SSD_INNER + 2 * SSD_GROUPS * SSD_STATE
SC_WIDTH = D_MODEL
SC_KERNEL = 3
N_EXPERTS = 64
TOP_K = 8
N_EXPERT_GROUPS = 8
TOPK_EXPERT_GROUPS = 4
D_EXPERT = 256
ROUTED_SCALE = 2.5
MOE_BLOCK = 128
LN_EPS = 1e-5
RMS_EPS = 1e-5
ALPHA = (2.0 * DEPTH) ** 0.25
BETA = (8.0 * DEPTH) ** -0.25
_IN_SIZES = (SSD_INNER, SSD_CONV_DIM, SSD_HEADS, SC_WIDTH, SC_WIDTH, SC_WIDTH, D_MODEL, D_MODEL)
IN_COLS = int(sum(_IN_SIZES))
IN_SPLITS = [int(v) for v in np.cumsum(_IN_SIZES)[:-1]]

kernel_name = "hybrid_ssd_shortconv_moe_block"


def layer_norm(x):
    xf = x.astype(jnp.float32)
    mu = jnp.mean(xf, axis=-1, keepdims=True)
    xc = xf - mu
    var = jnp.mean(xc * xc, axis=-1, keepdims=True)
    return xc * lax.rsqrt(var + LN_EPS)


def causal_depthwise_conv(u, w):
    k = w.shape[0]
    return lax.conv_general_dilated(
        u, w[:, None, :].astype(u.dtype), window_strides=(1,), padding=[(k - 1, 0)],
        dimension_numbers=("NWC", "WIO", "NWC"), feature_group_count=u.shape[-1])


def gated_rmsnorm(y, z, w):
    g = y.astype(jnp.float32) * jax.nn.silu(z.astype(jnp.float32))
    g = g.reshape(*y.shape[:-1], SSD_GROUPS, -1)
    g = g * lax.rsqrt(jnp.mean(g * g, axis=-1, keepdims=True) + RMS_EPS)
    return g.reshape(y.shape) * w.astype(jnp.float32)


def ssd_chunked(xg, dt, A, Bg, Cg):
    b, s, g, r, p = xg.shape
    n = Bg.shape[-1]
    nc = s // SSD_CHUNK
    l = SSD_CHUNK
    X = (xg * dt[..., None]).reshape(b, nc, l, g, r, p)
    dA = (dt * A).reshape(b, nc, l, g, r).transpose(0, 1, 3, 4, 2)
    Bc = Bg.reshape(b, nc, l, g, n)
    Cc = Cg.reshape(b, nc, l, g, n)
    A_cs = jnp.cumsum(dA, axis=-1)
    seg = A_cs[..., :, None] - A_cs[..., None, :]
    causal = jnp.tril(jnp.ones((l, l), dtype=bool))
    Lmat = jnp.exp(jnp.where(causal, seg, -jnp.inf))
    CB = jnp.einsum("bclgn,bcsgn->bcgls", Cc, Bc)
    scores = CB[:, :, :, None] * Lmat
    y_diag = jnp.einsum("bcgrls,bcsgrp->bclgrp", scores, X)
    decay_states = jnp.exp(A_cs[..., -1:] - A_cs).transpose(0, 1, 4, 2, 3)
    states = jnp.einsum("bclgn,bclgrp->bcgrpn", Bc, X * decay_states[..., None])
    chunk_decay = jnp.exp(A_cs[..., -1])

    def step(h, inp):
        st, dec = inp
        return dec[..., None, None] * h + st, h

    h0 = jnp.zeros((b, g, r, p, n), jnp.float32)
    _, prev = lax.scan(step, h0, (states.transpose(1, 0, 2, 3, 4, 5).astype(jnp.float32),
                                  chunk_decay.transpose(1, 0, 2, 3)))
    prev = prev.transpose(1, 0, 2, 3, 4, 5)
    out_decay = jnp.exp(A_cs).transpose(0, 1, 4, 2, 3)
    y_off = jnp.einsum("bclgn,bcgrpn->bclgrp", Cc, prev) * out_decay[..., None]
    return (y_diag + y_off).reshape(b, s, g, r, p)


def hybrid_mixer(h, w_in, ssd_conv_w, ssd_conv_b, ssd_dt_bias, ssd_A_log, ssd_D, ssd_norm_w,
                 w_ssd_out, sc_conv_w, w_sc_out, w_o):
    b, s, _ = h.shape
    z, xbc, dt_raw, sc_b, sc_c, sc_h, gate_a, gate_b = jnp.split(h @ w_in, IN_SPLITS, axis=-1)
    xbc = jax.nn.silu(causal_depthwise_conv(xbc, ssd_conv_w) + ssd_conv_b)
    xs, Bm, Cm = jnp.split(xbc, [SSD_INNER, SSD_INNER + SSD_GROUPS * SSD_STATE], axis=-1)
    dt = jax.nn.softplus(dt_raw.astype(jnp.float32) + ssd_dt_bias.astype(jnp.float32))
    A = -jnp.exp(ssd_A_log.astype(jnp.float32))
    G, R = SSD_GROUPS, SSD_HEADS_PER_GROUP
    xg = xs.reshape(b, s, G, R, SSD_HEAD_DIM)
    y = ssd_chunked(xg, dt.reshape(b, s, G, R), A.reshape(G, R),
                    Bm.reshape(b, s, G, SSD_STATE), Cm.reshape(b, s, G, SSD_STATE))
    y = y + ssd_D.reshape(G, R)[:, :, None] * xg
    y = gated_rmsnorm(y.reshape(b, s, SSD_INNER), z, ssd_norm_w).astype(h.dtype)
    y_a = y @ w_ssd_out
    u = causal_depthwise_conv(sc_c * sc_h, sc_conv_w)
    y_b = (sc_b * u) @ w_sc_out
    merged = jax.nn.sigmoid(gate_a) * y_a + jax.nn.sigmoid(gate_b) * y_b
    return merged @ w_o


def moe_ffn(h, router_w, router_bias, w_gate, w_up, w_down, sh_gate, sh_up, sh_down):
    b, s, d = h.shape
    T = b * s
    hf = h.reshape(T, d)
    scores = jax.nn.sigmoid((hf @ router_w).astype(jnp.float32))
    biased = scores + router_bias.astype(jnp.float32)
    grp = biased.reshape(T, N_EXPERT_GROUPS, N_EXPERTS // N_EXPERT_GROUPS)
    grp_score = lax.top_k(grp, 2)[0].sum(-1)
    _, grp_idx = lax.top_k(grp_score, TOPK_EXPERT_GROUPS)
    grp_mask = jax.nn.one_hot(grp_idx, N_EXPERT_GROUPS, dtype=jnp.float32).sum(1) > 0
    expert_mask = jnp.repeat(grp_mask, N_EXPERTS // N_EXPERT_GROUPS, axis=1)
    _, e_idx = lax.top_k(jnp.where(expert_mask, biased, -jnp.inf), TOP_K)
    gate = jnp.take_along_axis(scores, e_idx, axis=1)
    gate = gate / jnp.sum(gate, axis=-1, keepdims=True) * ROUTED_SCALE
    n_assign = T * TOP_K
    nb = (n_assign + MOE_BLOCK - 1) // MOE_BLOCK + N_EXPERTS
    n_rows = nb * MOE_BLOCK
    e_flat = e_idx.reshape(-1)
    tok_flat = jnp.arange(n_assign, dtype=jnp.int32) // TOP_K
    order = jnp.argsort(e_flat)
    e_sorted = e_flat[order]
    tok_sorted = tok_flat[order]
    gate_sorted = gate.reshape(-1)[order]
    counts = jnp.bincount(e_flat, length=N_EXPERTS)
    padded = (counts + MOE_BLOCK - 1) // MOE_BLOCK * MOE_BLOCK
    start = jnp.cumsum(counts) - counts
    pend = jnp.cumsum(padded)
    pstart = pend - padded
    dest = pstart[e_sorted] + (jnp.arange(n_assign) - start[e_sorted])
    row_tok = jnp.full((n_rows,), T, jnp.int32).at[dest].set(tok_sorted)
    row_gate = jnp.zeros((n_rows,), jnp.float32).at[dest].set(gate_sorted)
    block_expert = jnp.minimum(
        jnp.searchsorted(pend, jnp.arange(nb) * MOE_BLOCK, side="right"), N_EXPERTS - 1)
    h_pad = jnp.concatenate([hf, jnp.zeros((1, d), hf.dtype)], axis=0)
    xin = h_pad[row_tok].reshape(nb, MOE_BLOCK, d)

    def expert_block(args):
        xb, e = args
        return (jax.nn.silu(xb @ w_gate[e]) * (xb @ w_up[e])) @ w_down[e]

    yb = lax.map(expert_block, (xin, block_expert)).reshape(n_rows, d)
    routed = jax.ops.segment_sum(yb * row_gate[:, None], row_tok, num_segments=T + 1)[:T]
    shared = (jax.nn.silu(hf @ sh_gate) * (hf @ sh_up)) @ sh_down
    return (routed + shared).reshape(b, s, d)


def setup_inputs(seed: int = 0) -> dict:
    key = jax.random.key(seed)
    ks = jax.random.split(key, 32)
    L, D, E = DEPTH, D_MODEL, N_EXPERTS
    f32 = jnp.float32

    def nrm(k, shape, fan_in, scale=1.0):
        return jax.random.normal(k, shape, f32) * (scale * fan_in ** -0.5)

    u = jax.random.uniform(ks[5], (L, SSD_HEADS), f32)
    dt0 = jnp.exp(u * (np.log(0.1) - np.log(1e-3)) + np.log(1e-3))
    return {
        "x": jax.random.normal(ks[0], (BATCH, SEQ, D), f32),
        "c": jax.random.normal(ks[1], (BATCH, D), f32),
        "w_ada": nrm(ks[2], (L, D, 6 * D), D),
        "b_ada": 0.02 * jax.random.normal(ks[3], (L, 6 * D), f32),
        "w_in": nrm(ks[4], (L, D, IN_COLS), D),
        "ssd_conv_w": nrm(ks[6], (L, SSD_CONV, SSD_CONV_DIM), SSD_CONV),
        "ssd_conv_b": 0.02 * jax.random.normal(ks[7], (L, SSD_CONV_DIM), f32),
        "ssd_dt_bias": dt0 + jnp.log(-jnp.expm1(-dt0)),
        "ssd_A_log": jnp.log(jax.random.uniform(ks[8], (L, SSD_HEADS), f32, 1.0, 16.0)),
        "ssd_D": 1.0 + 0.1 * jax.random.normal(ks[9], (L, SSD_HEADS), f32),
        "ssd_norm_w": 1.0 + 0.1 * jax.random.normal(ks[10], (L, SSD_INNER), f32),
        "w_ssd_out": nrm(ks[11], (L, SSD_INNER, D), SSD_INNER, BETA),
        "sc_conv_w": nrm(ks[12], (L, SC_KERNEL, SC_WIDTH), SC_KERNEL),
        "w_sc_out": nrm(ks[13], (L, SC_WIDTH, D), SC_WIDTH, BETA),
        "w_o": nrm(ks[14], (L, D, D), D, BETA),
        "ln1_g": 1.0 + 0.1 * jax.random.normal(ks[15], (L, D), f32),
        "ln1_b": 0.02 * jax.random.normal(ks[16], (L, D), f32),
        "router_w": nrm(ks[17], (L, D, E), D),
        "router_bias": 0.01 * jax.random.normal(ks[18], (L, E), f32),
        "w_gate": nrm(ks[19], (L, E, D, D_EXPERT), D, BETA),
        "w_up": nrm(ks[20], (L, E, D, D_EXPERT), D, BETA),
        "w_down": nrm(ks[21], (L, E, D_EXPERT, D), D_EXPERT, BETA),
        "sh_gate": nrm(ks[22], (L, D, D_EXPERT), D, BETA),
        "sh_up": nrm(ks[23], (L, D, D_EXPERT), D, BETA),
        "sh_down": nrm(ks[24], (L, D_EXPERT, D), D_EXPERT, BETA),
        "ln2_g": 1.0 + 0.1 * jax.random.normal(ks[25], (L, D), f32),
        "ln2_b": 0.02 * jax.random.normal(ks[26], (L, D), f32),
    }


def reference(x, c, w_ada, b_ada, w_in, ssd_conv_w, ssd_conv_b, ssd_dt_bias, ssd_A_log, ssd_D,
              ssd_norm_w, w_ssd_out, sc_conv_w, w_sc_out, w_o, ln1_g, ln1_b, router_w,
              router_bias, w_gate, w_up, w_down, sh_gate, sh_up, sh_down, ln2_g, ln2_b):
    dtype = x.dtype
    for l in range(DEPTH):
        mod = jax.nn.silu(c) @ w_ada[l] + b_ada[l]
        sh1, sc1, g1, sh2, sc2, g2 = jnp.split(mod[:, None, :], 6, axis=-1)
        h = (layer_norm(x) * (1.0 + sc1) + sh1).astype(dtype)
        mix = hybrid_mixer(h, w_in[l], ssd_conv_w[l], ssd_conv_b[l], ssd_dt_bias[l], ssd_A_log[l],
                           ssd_D[l], ssd_norm_w[l], w_ssd_out[l], sc_conv_w[l], w_sc_out[l], w_o[l])
        x = (layer_norm(ALPHA * x + g1 * mix) * ln1_g[l] + ln1_b[l]).astype(dtype)
        h = (layer_norm(x) * (1.0 + sc2) + sh2).astype(dtype)
        ffn = moe_ffn(h, router_w[l], router_bias[l], w_gate[l], w_up[l], w_down[l],
                      sh_gate[l], sh_up[l], sh_down[l])
        x = (layer_norm(ALPHA * x + g2 * ffn) * ln2_g[l] + ln2_b[l]).astype(dtype)
    return x
```

```python
import functools

import jax
import jax.numpy as jnp
import numpy as np
from jax import lax
from jax.experimental import pallas as pl
from jax.experimental.pallas import tpu as pltpu

F32 = jnp.float32
BF16 = jnp.bfloat16

LANES = 128
SUBLANES = 8
VMEM_LIMIT_BYTES = 56 * 1024 * 1024

D_MODEL = 1024
SSD_INNER = 2048
SSD_HEAD_DIM = 64
SSD_HEADS = 32
SSD_GROUPS = 8
SSD_STATE = 128
SSD_CONV = 4
SSD_CHUNK = 128
GROUP_W = SSD_INNER // SSD_GROUPS
HEADS_PER_GROUP = SSD_HEADS // SSD_GROUPS
SC_KERNEL = 3
N_EXPERTS = 64
TOP_K = 8
N_EXPERT_GROUPS = 8
TOPK_EXPERT_GROUPS = 4
EXPERTS_PER_GROUP = N_EXPERTS // N_EXPERT_GROUPS
D_EXPERT = 256
ROUTED_SCALE = 2.5
LN_EPS = 1e-5
RMS_EPS = 1e-5
ALPHA = 2.0 ** 0.25
P_COLS = 2048 + 4096 + 5 * 1024
DT_COL0 = 2048 + 4096
NEG_BIG = -1e30


def _cparams(sem):
    return pltpu.CompilerParams(dimension_semantics=sem,
                                vmem_limit_bytes=VMEM_LIMIT_BYTES)


def _ln(x):
    mu = jnp.mean(x, axis=-1, keepdims=True)
    xc = x - mu
    var = jnp.mean(xc * xc, axis=-1, keepdims=True)
    return xc * lax.rsqrt(var + LN_EPS)


def _silu(x):
    return x * jax.nn.sigmoid(x)


def _split3(a):
    hi = a.astype(BF16)
    r1 = a - hi.astype(F32)
    mid = r1.astype(BF16)
    lo = (r1 - mid.astype(F32)).astype(BF16)
    return hi, mid, lo


def _dot(a, b):
    return jnp.dot(a, b, preferred_element_type=F32)


def _dot_exact01(a, m01):
    hi, mid, lo = _split3(a)
    return _dot(hi, m01) + _dot(mid, m01) + _dot(lo, m01)


def _mod_kernel(c_ref, w_ref, b_ref, o_ref):
    a = _silu(c_ref[...])
    o_ref[...] = jnp.dot(a, w_ref[...], precision=lax.Precision.HIGHEST,
                         preferred_element_type=F32) + b_ref[...]


def _mod(c, w, b):
    bsz, d = c.shape
    n = w.shape[1]
    tn = 1024
    return pl.pallas_call(
        _mod_kernel,
        grid=(n // tn,),
        in_specs=[pl.BlockSpec((bsz, d), lambda j: (0, 0)),
                  pl.BlockSpec((d, tn), lambda j: (0, j)),
                  pl.BlockSpec((1, tn), lambda j: (0, j))],
        out_specs=pl.BlockSpec((bsz, tn), lambda j: (0, j)),
        out_shape=jax.ShapeDtypeStruct((bsz, n), F32),
        compiler_params=_cparams(("arbitrary",)),
        name="mod",
    )(c, w, b.reshape(1, n))


def _inproj_kernel(x_ref, mod_ref, w_ref, wdt_ref, p_ref, dt_ref, h_scr):
    j = pl.program_id(1)

    @pl.when(j == 0)
    def _():
        m = mod_ref[0]
        h = _ln(x_ref[...]) * (1.0 + m[1:2, :]) + m[0:1, :]
        hb = h.astype(BF16)
        h_scr[...] = hb
        dt_ref[...] = _dot(hb, wdt_ref[...])

    p_ref[...] = _dot(h_scr[...], w_ref[...])


def _inproj(x2d, mod3, w_main, w_dt, seq, tm, tn):
    t, d = x2d.shape
    n = w_main.shape[1]
    return pl.pallas_call(
        _inproj_kernel,
        grid=(t // tm, n // tn),
        in_specs=[pl.BlockSpec((tm, d), lambda i, j: (i, 0)),
                  pl.BlockSpec((1, 6, d), lambda i, j: ((i * tm) // seq, 0, 0)),
                  pl.BlockSpec((d, tn), lambda i, j: (0, j)),
                  pl.BlockSpec((d, LANES), lambda i, j: (0, 0))],
        out_specs=[pl.BlockSpec((tm, tn), lambda i, j: (i, j)),
                   pl.BlockSpec((tm, LANES), lambda i, j: (i, 0))],
        out_shape=[jax.ShapeDtypeStruct((t, n), F32),
                   jax.ShapeDtypeStruct((t, LANES), F32)],
        scratch_shapes=[pltpu.VMEM((tm, d), BF16)],
        compiler_params=_cparams(("arbitrary", "arbitrary")),
        name="inproj",
    )(x2d, mod3, w_main, w_dt)


def _conv_taps(ext_ref, w_ref, n_taps, rows):
    acc = None
    for k in range(n_taps):
        start = SUBLANES - (n_taps - 1) + k
        term = w_ref[k:k + 1, :] * ext_ref[start:start + rows, :]
        acc = term if acc is None else acc + term
    return acc


def _ssd_kernel(xs_ref, bc_ref, z_ref, dtp_ref, cw_xs_ref, cw_bc_ref, cb_xs_ref,
                cb_bc_ref, dtb_ref, alog_ref, dvec_ref, nw_ref, e_ref,
                yn_ref, ext_xs, ext_bc, state):
    c = pl.program_id(1)
    L = SSD_CHUNK

    @pl.when(c == 0)
    def _():
        ext_xs[0:SUBLANES, :] = jnp.zeros((SUBLANES, SSD_INNER), F32)
        ext_bc[0:SUBLANES, :] = jnp.zeros((SUBLANES, SSD_INNER), F32)
        state[...] = jnp.zeros(state.shape, F32)

    ext_xs[SUBLANES:SUBLANES + L, :] = xs_ref[...]
    ext_bc[SUBLANES:SUBLANES + L, :] = bc_ref[...]
    xs = _silu(_conv_taps(ext_xs, cw_xs_ref, SSD_CONV, L) + cb_xs_ref[...])
    bc = _silu(_conv_taps(ext_bc, cw_bc_ref, SSD_CONV, L) + cb_bc_ref[...])
    ext_xs[0:SUBLANES, :] = ext_xs[L:L + SUBLANES, :]
    ext_bc[0:SUBLANES, :] = ext_bc[L:L + SUBLANES, :]

    x_dt = dtp_ref[...] + dtb_ref[...]
    dt = jnp.maximum(x_dt, 0.0) + jnp.log1p(jnp.exp(-jnp.abs(x_dt)))
    a_neg = -jnp.exp(alog_ref[...])
    d_a = dt * a_neg
    row = lax.broadcasted_iota(jnp.int32, (L, L), 0)
    col = lax.broadcasted_iota(jnp.int32, (L, L), 1)
    causal = col <= row
    tri = jnp.where(causal, 1.0, 0.0).astype(BF16)
    hi, mid, lo = _split3(d_a)
    a_cs = _dot(tri, hi) + _dot(tri, mid) + _dot(tri, lo)
    a_cs_t = a_cs.T
    a_last = a_cs[L - 1:L, :]
    stack = jnp.concatenate([dt, jnp.exp(a_cs), jnp.exp(a_last - a_cs)], axis=0)
    st_e = _dot_exact01(stack, e_ref[...])
    dt_e = st_e[0:L]
    od_e = st_e[L:2 * L]
    ds_e = st_e[2 * L:3 * L]
    cd_e = od_e[L - 1:L, :]

    x_f = xs * dt_e
    x_b = x_f.astype(BF16)
    xd_b = (x_f * ds_e).astype(BF16)
    lane_head = lax.broadcasted_iota(jnp.int32, (L, GROUP_W), 1) // SSD_HEAD_DIM

    for g in range(SSD_GROUPS):
        gc = slice(g * GROUP_W, (g + 1) * GROUP_W)
        b_g = bc[:, g * SSD_STATE:(g + 1) * SSD_STATE]
        c_g = bc[:, SSD_GROUPS * SSD_STATE + g * SSD_STATE:
                 SSD_GROUPS * SSD_STATE + (g + 1) * SSD_STATE].astype(BF16)
        b_gt = b_g.T.astype(BF16)
        cb = _dot(c_g, b_gt)
        x_g = x_b[:, gc]
        lhs = []
        rhs = []
        for r in range(HEADS_PER_GROUP):
            h = g * HEADS_PER_GROUP + r
            seg = a_cs[:, h:h + 1] - a_cs_t[h:h + 1, :]
            lmat = jnp.exp(jnp.where(causal, seg, NEG_BIG))
            lhs.append((cb * lmat).astype(BF16))
            rhs.append(jnp.where(lane_head == r, x_g, jnp.zeros_like(x_g)))
        y_diag = _dot(jnp.concatenate(lhs, axis=1), jnp.concatenate(rhs, axis=0))
        st = state[g]
        y_off = _dot(c_g, st.astype(BF16)) * od_e[:, gc]
        state[g] = cd_e[:, gc] * st + _dot(b_gt, xd_b[:, gc])
        y = y_diag + y_off + dvec_ref[:, gc] * xs[:, gc]
        gt = y * _silu(z_ref[:, gc])
        ms = jnp.mean(gt * gt, axis=-1, keepdims=True)
        yn_ref[:, gc] = gt * lax.rsqrt(ms + RMS_EPS) * nw_ref[:, gc]


def _ssd(p, dtp, cw, cb, dtb, alog, dvec, nw, e01, bsz, seq):
    t = p.shape[0]
    L = SSD_CHUNK
    nc = seq // L
    row_map = lambda b, c: (b * nc + c, 0)

    def pcol(k):
        return pl.BlockSpec((L, SSD_INNER), lambda b, c: (b * nc + c, k))

    const = lambda b, c: (0, 0)
    return pl.pallas_call(
        _ssd_kernel,
        grid=(bsz, nc),
        in_specs=[pcol(1), pcol(2), pcol(0),
                  pl.BlockSpec((L, LANES), row_map),
                  pl.BlockSpec((SSD_CONV, SSD_INNER), lambda b, c: (0, 0)),
                  pl.BlockSpec((SSD_CONV, SSD_INNER), lambda b, c: (0, 1)),
                  pl.BlockSpec((1, SSD_INNER), lambda b, c: (0, 0)),
                  pl.BlockSpec((1, SSD_INNER), lambda b, c: (0, 1)),
                  pl.BlockSpec((1, LANES), const),
                  pl.BlockSpec((1, LANES), const),
                  pl.BlockSpec((1, SSD_INNER), const),
                  pl.BlockSpec((1, SSD_INNER), const),
                  pl.BlockSpec((LANES, SSD_INNER), const)],
        out_specs=pl.BlockSpec((L, SSD_INNER), row_map),
        out_shape=jax.ShapeDtypeStruct((t, SSD_INNER), F32),
        scratch_shapes=[pltpu.VMEM((L + SUBLANES, SSD_INNER), F32),
                        pltpu.VMEM((L + SUBLANES, SSD_INNER), F32),
                        pltpu.VMEM((SSD_GROUPS, SSD_STATE, GROUP_W), F32)],
        compiler_params=_cparams(("arbitrary", "arbitrary")),
        name="ssd",
    )(p, p, p, dtp, cw, cw, cb, cb, dtb, alog, dvec, nw, e01)


def _merge_kernel(yn_ref, scb_ref, scc_ref, sch_ref, ga_ref, gb_ref, x_ref, mod_ref,
                  wssd_ref, wsc_ref, wo_ref, cw_ref, lng_ref, lnb_ref, wr_ref,
                  x1_ref, h2_ref, lg_ref, ext):
    s = pl.program_id(1)
    tm = x_ref.shape[0]

    @pl.when(s == 0)
    def _():
        ext[0:SUBLANES, :] = jnp.zeros((SUBLANES, D_MODEL), F32)

    ext[SUBLANES:SUBLANES + tm, :] = scc_ref[...] * sch_ref[...]
    u = _conv_taps(ext, cw_ref, SC_KERNEL, tm)
    ext[0:SUBLANES, :] = ext[tm:tm + SUBLANES, :]

    m = mod_ref[0]
    y_b = _dot((scb_ref[...] * u).astype(BF16), wsc_ref[...])
    y_a = _dot(yn_ref[...].astype(BF16), wssd_ref[...])
    merged = jax.nn.sigmoid(ga_ref[...]) * y_a + jax.nn.sigmoid(gb_ref[...]) * y_b
    mix = _dot(merged.astype(BF16), wo_ref[...])
    x1 = _ln(ALPHA * x_ref[...] + m[2:3, :] * mix) * lng_ref[...] + lnb_ref[...]
    x1_ref[...] = x1
    h2 = _ln(x1) * (1.0 + m[4:5, :]) + m[3:4, :]
    h2_ref[...] = h2.astype(BF16)
    lg_ref[...] = jnp.dot(h2, wr_ref[...], precision=lax.Precision.HIGHEST,
                          preferred_element_type=F32)


def _merge(yn, p, x2d, mod3, wssd, wsc, wo, cw, lng, lnb, wr, bsz, seq, tm):
    t, d = x2d.shape
    ns = seq // tm
    row_map = lambda b, s: (b * ns + s, 0)
    const = lambda b, s: (0, 0)
    col0 = (2048 + 4096) // d

    def pcol(k):
        return pl.BlockSpec((tm, d), lambda b, s: (b * ns + s, col0 + k))

    return pl.pallas_call(
        _merge_kernel,
        grid=(bsz, ns),
        in_specs=[pl.BlockSpec((tm, SSD_INNER), row_map),
                  pcol(0), pcol(1), pcol(2), pcol(3), pcol(4),
                  pl.BlockSpec((tm, d), row_map),
                  pl.BlockSpec((1, 6, d), lambda b, s: (b, 0, 0)),
                  pl.BlockSpec((SSD_INNER, d), const),
                  pl.BlockSpec((d, d), const),
                  pl.BlockSpec((d, d), const),
                  pl.BlockSpec((SC_KERNEL, d), const),
                  pl.BlockSpec((1, d), const),
                  pl.BlockSpec((1, d), const),
                  pl.BlockSpec((d, LANES), const)],
        out_specs=[pl.BlockSpec((tm, d), row_map),
                   pl.BlockSpec((tm, d), row_map),
                   pl.BlockSpec((tm, LANES), row_map)],
        out_shape=[jax.ShapeDtypeStruct((t, d), F32),
                   jax.ShapeDtypeStruct((t, d), BF16),
                   jax.ShapeDtypeStruct((t, LANES), F32)],
        scratch_shapes=[pltpu.VMEM((tm + SUBLANES, d), F32)],
        compiler_params=_cparams(("arbitrary", "arbitrary")),
        name="merge",
    )(yn, p, p, p, p, p, x2d, mod3, wssd, wsc, wo, cw, lng, lnb, wr)


def _first_argmax_mask(v, idx, n):
    m = jnp.max(v, axis=0, keepdims=True)
    first = jnp.min(jnp.where(v == m, idx, n), axis=0, keepdims=True)
    return idx == first, m


def _route_kernel(lg_ref, bias_ref, g_ref):
    tm = lg_ref.shape[0]
    ne, ng, eg = N_EXPERTS, N_EXPERT_GROUPS, EXPERTS_PER_GROUP
    lt = lg_ref[...].T[0:ne, :]
    scores = jax.nn.sigmoid(lt)
    biased = scores + bias_ref[...][0:ne, 0:1]
    grp = biased.reshape(ng, eg, tm)
    idx_e = lax.broadcasted_iota(jnp.int32, (ng, eg, tm), 1)
    m1 = jnp.max(grp, axis=1, keepdims=True)
    first = jnp.min(jnp.where(grp == m1, idx_e, eg), axis=1, keepdims=True)
    m2 = jnp.max(jnp.where(idx_e == first, NEG_BIG, grp), axis=1, keepdims=True)
    gscore = m1 + m2
    idx_g = lax.broadcasted_iota(jnp.int32, (ng, 1, tm), 0)
    gsel = jnp.zeros((ng, 1, tm), F32)
    for _ in range(TOPK_EXPERT_GROUPS):
        hit, _m = _first_argmax_mask(gscore, idx_g, ng)
        gsel = jnp.where(hit, 1.0, gsel)
        gscore = jnp.where(hit, NEG_BIG, gscore)
    emask = jnp.broadcast_to(gsel, (ng, eg, tm)).reshape(ne, tm)
    cand = jnp.where(emask > 0.0, biased, NEG_BIG)
    idx_x = lax.broadcasted_iota(jnp.int32, (ne, tm), 0)
    sel = jnp.zeros((ne, tm), F32)
    for _ in range(TOP_K):
        hit, _m = _first_argmax_mask(cand, idx_x, ne)
        sel = jnp.where(hit, 1.0, sel)
        cand = jnp.where(hit, NEG_BIG, cand)
    gate = jnp.where(sel > 0.0, scores, 0.0)
    gate = gate / jnp.sum(gate, axis=0, keepdims=True) * ROUTED_SCALE
    rest = lax.broadcasted_iota(jnp.int32, (LANES - ne, tm), 0)
    tail = jnp.where(rest == 0, 1.0, 0.0).astype(F32)
    g_ref[...] = jnp.concatenate([gate, tail], axis=0).T


def _route(logits, bias_col, tm):
    t = logits.shape[0]
    return pl.pallas_call(
        _route_kernel,
        grid=(t // tm,),
        in_specs=[pl.BlockSpec((tm, LANES), lambda i: (i, 0)),
                  pl.BlockSpec((LANES, LANES), lambda i: (0, 0))],
        out_specs=pl.BlockSpec((tm, LANES), lambda i: (i, 0)),
        out_shape=jax.ShapeDtypeStruct((t, LANES), F32),
        compiler_params=_cparams(("arbitrary",)),
        name="route",
    )(logits, bias_col)


def _moe_kernel(h_ref, g_ref, x1_ref, mod_ref, wgu_ref, wd_ref, lng_ref, lnb_ref,
                o_ref, acc):
    e = pl.program_id(1)
    ne = pl.num_programs(1)

    @pl.when(e == 0)
    def _():
        acc[...] = jnp.zeros(acc.shape, F32)

    sel = (lax.broadcasted_iota(jnp.int32, (LANES, D_EXPERT), 0) == e)
    onehot = jnp.where(sel, 1.0, 0.0).astype(BF16)
    gcol = _dot_exact01(g_ref[...], onehot)
    a = _dot(h_ref[...], wgu_ref[0])
    act = _silu(a[:, 0:D_EXPERT]) * a[:, D_EXPERT:2 * D_EXPERT] * gcol
    acc[...] += _dot(act.astype(BF16), wd_ref[0])

    @pl.when(e == ne - 1)
    def _():
        m = mod_ref[0]
        r = ALPHA * x1_ref[...] + m[5:6, :] * acc[...]
        o_ref[...] = _ln(r) * lng_ref[...] + lnb_ref[...]


def _moe(h2, gates, x1, mod3, wgu, wd, lng, lnb, seq, tm):
    t, d = x1.shape
    ne = wgu.shape[0]
    row = lambda i, e: (i, 0)
    const = lambda i, e: (0, 0)
    return pl.pallas_call(
        _moe_kernel,
        grid=(t // tm, ne),
        in_specs=[pl.BlockSpec((tm, d), row),
                  pl.BlockSpec((tm, LANES), row),
                  pl.BlockSpec((tm, d), row),
                  pl.BlockSpec((1, 6, d), lambda i, e: ((i * tm) // seq, 0, 0)),
                  pl.BlockSpec((1, d, 2 * D_EXPERT), lambda i, e: (e, 0, 0)),
                  pl.BlockSpec((1, D_EXPERT, d), lambda i, e: (e, 0, 0)),
                  pl.BlockSpec((1, d), const),
                  pl.BlockSpec((1, d), const)],
        out_specs=pl.BlockSpec((tm, d), row),
        out_shape=jax.ShapeDtypeStruct((t, d), F32),
        scratch_shapes=[pltpu.VMEM((tm, d), F32)],
        compiler_params=_cparams(("arbitrary", "arbitrary")),
        name="moe",
    )(h2, gates, x1, mod3, wgu, wd, lng, lnb)


def _pad_lanes(v, fill=0.0):
    out = jnp.full((1, LANES), fill, F32)
    return out.at[0, :v.shape[0]].set(v.astype(F32))


def kernel(x, c, w_ada, b_ada, w_in, ssd_conv_w, ssd_conv_b, ssd_dt_bias, ssd_A_log, ssd_D,
           ssd_norm_w, w_ssd_out, sc_conv_w, w_sc_out, w_o, ln1_g, ln1_b, router_w,
           router_bias, w_gate, w_up, w_down, sh_gate, sh_up, sh_down, ln2_g, ln2_b):
    bsz, seq, d = x.shape
    t = bsz * seq
    depth = w_in.shape[0]
    tm_proj = min(512, seq)
    tm_merge = min(256, seq)
    tm_route = min(1024, t)
    tm_moe = min(1024, seq)

    e01 = (np.arange(LANES)[:, None] == (np.arange(SSD_INNER)[None, :] // SSD_HEAD_DIM))
    e01 = jnp.asarray(e01, BF16)

    xf = x.reshape(t, d)
    for l in range(depth):
        w_l = w_in[l]
        w_main = jnp.concatenate([w_l[:, :DT_COL0], w_l[:, DT_COL0 + SSD_HEADS:]],
                                 axis=1).astype(BF16)
        w_dt = jnp.pad(w_l[:, DT_COL0:DT_COL0 + SSD_HEADS],
                       ((0, 0), (0, LANES - SSD_HEADS))).astype(BF16)
        wgu = jnp.concatenate(
            [jnp.concatenate([w_gate[l], w_up[l]], axis=-1),
             jnp.concatenate([sh_gate[l], sh_up[l]], axis=-1)[None]], axis=0).astype(BF16)
        wd = jnp.concatenate([w_down[l], sh_down[l][None]], axis=0).astype(BF16)
        wr = jnp.pad(router_w[l], ((0, 0), (0, LANES - N_EXPERTS)))
        bias_col = jnp.zeros((LANES, LANES), F32).at[:N_EXPERTS, 0].set(router_bias[l])
        dvec = jnp.repeat(ssd_D[l], SSD_HEAD_DIM).reshape(1, SSD_INNER)

        mod3 = _mod(c, w_ada[l], b_ada[l]).reshape(bsz, 6, d)
        p, dtp = _inproj(xf, mod3, w_main, w_dt, seq, tm_proj, 1024)
        yn = _ssd(p, dtp, ssd_conv_w[l], ssd_conv_b[l].reshape(1, -1),
                  _pad_lanes(ssd_dt_bias[l]), _pad_lanes(ssd_A_log[l]), dvec,
                  ssd_norm_w[l].reshape(1, -1), e01, bsz, seq)
        x1, h2, logits = _merge(yn, p, xf, mod3, w_ssd_out[l].astype(BF16),
                                w_sc_out[l].astype(BF16), w_o[l].astype(BF16),
                                sc_conv_w[l], ln1_g[l].reshape(1, d), ln1_b[l].reshape(1, d),
                                wr, bsz, seq, tm_merge)
        gates = _route(logits, bias_col, tm_route)
        xf = _moe(h2, gates, x1, mod3, wgu, wd, ln2_g[l].reshape(1, d),
                  ln2_b[l].reshape(1, d), seq, tm_moe)
    return xf.reshape(bsz, seq, d)
```

```python
import functools

import jax
import jax.numpy as jnp
import numpy as np
from jax import lax
from jax.experimental import pallas as pl
from jax.experimental.pallas import tpu as pltpu
from jax.experimental.pallas import tpu_sc as plsc

F32 = jnp.float32
BF16 = jnp.bfloat16
U32 = jnp.uint32

LANES = 128
SUBLANES = 8
VMEM_LIMIT_BYTES = 56 * 1024 * 1024

D_MODEL = 1024
SSD_INNER = 2048
SSD_HEAD_DIM = 64
SSD_HEADS = 32
SSD_GROUPS = 8
SSD_STATE = 128
SSD_CONV = 4
SSD_CHUNK = 128
GROUP_W = SSD_INNER // SSD_GROUPS
HEADS_PER_GROUP = SSD_HEADS // SSD_GROUPS
SC_KERNEL = 3
N_EXPERTS = 64
TOP_K = 8
N_EXPERT_GROUPS = 8
TOPK_EXPERT_GROUPS = 4
EXPERTS_PER_GROUP = N_EXPERTS // N_EXPERT_GROUPS
D_EXPERT = 256
ROUTED_SCALE = 2.5
LN_EPS = 1e-5
RMS_EPS = 1e-5
ALPHA = 2.0 ** 0.25
P_COLS = 2048 + 4096 + 5 * 1024
DT_COL0 = 2048 + 4096
NEG_BIG = -1e30


def _cparams(sem):
    return pltpu.CompilerParams(dimension_semantics=sem,
                                vmem_limit_bytes=VMEM_LIMIT_BYTES)


def _ln(x):
    mu = jnp.mean(x, axis=-1, keepdims=True)
    xc = x - mu
    var = jnp.mean(xc * xc, axis=-1, keepdims=True)
    return xc * lax.rsqrt(var + LN_EPS)


def _silu(x):
    return x * jax.nn.sigmoid(x)


def _split3(a):
    hi = a.astype(BF16)
    r1 = a - hi.astype(F32)
    mid = r1.astype(BF16)
    lo = (r1 - mid.astype(F32)).astype(BF16)
    return hi, mid, lo


def _dot(a, b):
    return jnp.dot(a, b, preferred_element_type=F32)


def _dot_exact01(a, m01):
    hi, mid, lo = _split3(a)
    return _dot(hi, m01) + _dot(mid, m01) + _dot(lo, m01)


def _dot01_exact(m01, a):
    hi, mid, lo = _split3(a)
    return _dot(m01, hi) + _dot(m01, mid) + _dot(m01, lo)


def _pack_pair(a, b):
    ab = pltpu.bitcast(a.astype(BF16).astype(F32), U32)
    bb = pltpu.bitcast(b.astype(BF16).astype(F32), U32)
    return ab | lax.shift_right_logical(bb, jnp.uint32(16))


def _unpack_pair(w):
    a = pltpu.bitcast(w & jnp.uint32(0xFFFF0000), F32)
    b = pltpu.bitcast(lax.shift_left(w, jnp.uint32(16)), F32)
    return a, b


def _mod_kernel(c_ref, w_ref, b_ref, o_ref):
    a = _silu(c_ref[...])
    o_ref[...] = jnp.dot(a, w_ref[...], precision=lax.Precision.HIGHEST,
                         preferred_element_type=F32) + b_ref[...]


def _mod(c, w, b):
    bsz, d = c.shape
    n = w.shape[1]
    tn = 1024
    return pl.pallas_call(
        _mod_kernel,
        grid=(n // tn,),
        in_specs=[pl.BlockSpec((bsz, d), lambda j: (0, 0)),
                  pl.BlockSpec((d, tn), lambda j: (0, j)),
                  pl.BlockSpec((1, tn), lambda j: (0, j))],
        out_specs=pl.BlockSpec((bsz, tn), lambda j: (0, j)),
        out_shape=jax.ShapeDtypeStruct((bsz, n), F32),
        compiler_params=_cparams(("arbitrary",)),
        name="mod",
    )(c, w, b.reshape(1, n))


def _inproj_kernel(x_ref, mod_ref, w_ref, wdt_ref, p_ref, dt_ref, h_scr):
    j = pl.program_id(1)

    @pl.when(j == 0)
    def _():
        m = mod_ref[0]
        h = _ln(x_ref[...]) * (1.0 + m[1:2, :]) + m[0:1, :]
        hb = h.astype(BF16)
        h_scr[...] = hb
        dt_ref[...] = _dot(hb, wdt_ref[...])

    p_ref[...] = _dot(h_scr[...], w_ref[...])


def _inproj(x2d, mod3, w_main, w_dt, seq, tm, tn):
    t, d = x2d.shape
    n = w_main.shape[1]
    return pl.pallas_call(
        _inproj_kernel,
        grid=(t // tm, n // tn),
        in_specs=[pl.BlockSpec((tm, d), lambda i, j: (i, 0)),
                  pl.BlockSpec((1, 6, d), lambda i, j: ((i * tm) // seq, 0, 0)),
                  pl.BlockSpec((d, tn), lambda i, j: (0, j)),
                  pl.BlockSpec((d, LANES), lambda i, j: (0, 0))],
        out_specs=[pl.BlockSpec((tm, tn), lambda i, j: (i, j)),
                   pl.BlockSpec((tm, LANES), lambda i, j: (i, 0))],
        out_shape=[jax.ShapeDtypeStruct((t, n), F32),
                   jax.ShapeDtypeStruct((t, LANES), F32)],
        scratch_shapes=[pltpu.VMEM((tm, d), BF16)],
        compiler_params=_cparams(("arbitrary", "arbitrary")),
        name="inproj",
    )(x2d, mod3, w_main, w_dt)


def _conv_taps(ext_ref, w_ref, n_taps, rows):
    acc = None
    for k in range(n_taps):
        start = SUBLANES - (n_taps - 1) + k
        term = w_ref[k:k + 1, :] * ext_ref[start:start + rows, :]
        acc = term if acc is None else acc + term
    return acc


def _ssd_kernel(xs_ref, bc_ref, z_ref, dtp_ref, cw_xs_ref, cw_bc_ref, cb_xs_ref,
                cb_bc_ref, dtb_ref, alog_ref, dvec_ref, nw_ref, e_ref,
                yn_ref, ext_xs, ext_bc, state):
    c = pl.program_id(1)
    L = SSD_CHUNK

    @pl.when(c == 0)
    def _():
        ext_xs[0:SUBLANES, :] = jnp.zeros((SUBLANES, SSD_INNER), F32)
        ext_bc[0:SUBLANES, :] = jnp.zeros((SUBLANES, SSD_INNER), F32)
        state[...] = jnp.zeros(state.shape, F32)

    ext_xs[SUBLANES:SUBLANES + L, :] = xs_ref[...]
    ext_bc[SUBLANES:SUBLANES + L, :] = bc_ref[...]
    xs = _silu(_conv_taps(ext_xs, cw_xs_ref, SSD_CONV, L) + cb_xs_ref[...])
    bc = _silu(_conv_taps(ext_bc, cw_bc_ref, SSD_CONV, L) + cb_bc_ref[...])
    ext_xs[0:SUBLANES, :] = ext_xs[L:L + SUBLANES, :]
    ext_bc[0:SUBLANES, :] = ext_bc[L:L + SUBLANES, :]

    x_dt = dtp_ref[...] + dtb_ref[...]
    dt = jnp.maximum(x_dt, 0.0) + jnp.log1p(jnp.exp(-jnp.abs(x_dt)))
    a_neg = -jnp.exp(alog_ref[...])
    d_a = dt * a_neg
    row = lax.broadcasted_iota(jnp.int32, (L, L), 0)
    col = lax.broadcasted_iota(jnp.int32, (L, L), 1)
    causal = col <= row
    tri = jnp.where(causal, 1.0, 0.0).astype(BF16)
    hi, mid, lo = _split3(d_a)
    a_cs = _dot(tri, hi) + _dot(tri, mid) + _dot(tri, lo)
    a_cs_t = a_cs.T
    a_last = a_cs[L - 1:L, :]
    stack = jnp.concatenate([dt, jnp.exp(a_cs), jnp.exp(a_last - a_cs)], axis=0)
    st_e = _dot_exact01(stack, e_ref[...])
    dt_e = st_e[0:L]
    od_e = st_e[L:2 * L]
    ds_e = st_e[2 * L:3 * L]
    cd_e = od_e[L - 1:L, :]

    x_f = xs * dt_e
    x_b = x_f.astype(BF16)
    xd_b = (x_f * ds_e).astype(BF16)
    lane_head = lax.broadcasted_iota(jnp.int32, (L, GROUP_W), 1) // SSD_HEAD_DIM

    for g in range(SSD_GROUPS):
        gc = slice(g * GROUP_W, (g + 1) * GROUP_W)
        b_g = bc[:, g * SSD_STATE:(g + 1) * SSD_STATE]
        c_g = bc[:, SSD_GROUPS * SSD_STATE + g * SSD_STATE:
                 SSD_GROUPS * SSD_STATE + (g + 1) * SSD_STATE].astype(BF16)
        b_gt = b_g.T.astype(BF16)
        cb = _dot(c_g, b_gt)
        x_g = x_b[:, gc]
        lhs = []
        rhs = []
        for r in range(HEADS_PER_GROUP):
            h = g * HEADS_PER_GROUP + r
            seg = a_cs[:, h:h + 1] - a_cs_t[h:h + 1, :]
            lmat = jnp.exp(jnp.where(causal, seg, NEG_BIG))
            lhs.append((cb * lmat).astype(BF16))
            rhs.append(jnp.where(lane_head == r, x_g, jnp.zeros_like(x_g)))
        y_diag = _dot(jnp.concatenate(lhs, axis=1), jnp.concatenate(rhs, axis=0))
        st = state[g]
        y_off = _dot(c_g, st.astype(BF16)) * od_e[:, gc]
        state[g] = cd_e[:, gc] * st + _dot(b_gt, xd_b[:, gc])
        y = y_diag + y_off + dvec_ref[:, gc] * xs[:, gc]
        gt = y * _silu(z_ref[:, gc])
        ms = jnp.mean(gt * gt, axis=-1, keepdims=True)
        yn_ref[:, gc] = gt * lax.rsqrt(ms + RMS_EPS) * nw_ref[:, gc]


def _ssd(p, dtp, cw, cb, dtb, alog, dvec, nw, e01, bsz, seq):
    t = p.shape[0]
    L = SSD_CHUNK
    nc = seq // L
    row_map = lambda b, c: (b * nc + c, 0)

    def pcol(k):
        return pl.BlockSpec((L, SSD_INNER), lambda b, c: (b * nc + c, k))

    const = lambda b, c: (0, 0)
    return pl.pallas_call(
        _ssd_kernel,
        grid=(bsz, nc),
        in_specs=[pcol(1), pcol(2), pcol(0),
                  pl.BlockSpec((L, LANES), row_map),
                  pl.BlockSpec((SSD_CONV, SSD_INNER), lambda b, c: (0, 0)),
                  pl.BlockSpec((SSD_CONV, SSD_INNER), lambda b, c: (0, 1)),
                  pl.BlockSpec((1, SSD_INNER), lambda b, c: (0, 0)),
                  pl.BlockSpec((1, SSD_INNER), lambda b, c: (0, 1)),
                  pl.BlockSpec((1, LANES), const),
                  pl.BlockSpec((1, LANES), const),
                  pl.BlockSpec((1, SSD_INNER), const),
                  pl.BlockSpec((1, SSD_INNER), const),
                  pl.BlockSpec((LANES, SSD_INNER), const)],
        out_specs=pl.BlockSpec((L, SSD_INNER), row_map),
        out_shape=jax.ShapeDtypeStruct((t, SSD_INNER), F32),
        scratch_shapes=[pltpu.VMEM((L + SUBLANES, SSD_INNER), F32),
                        pltpu.VMEM((L + SUBLANES, SSD_INNER), F32),
                        pltpu.VMEM((SSD_GROUPS, SSD_STATE, GROUP_W), F32)],
        compiler_params=_cparams(("arbitrary", "arbitrary")),
        name="ssd",
    )(p, p, p, dtp, cw, cw, cb, cb, dtb, alog, dvec, nw, e01)


def _merge_kernel(yn_ref, scb_ref, scc_ref, sch_ref, ga_ref, gb_ref, x_ref, mod_ref,
                  wssd_ref, wsc_ref, wo_ref, cw_ref, lng_ref, lnb_ref, wr_ref,
                  x1_ref, h2_ref, h2p_ref, lg_ref, ext):
    s = pl.program_id(1)
    tm = x_ref.shape[0]

    @pl.when(s == 0)
    def _():
        ext[0:SUBLANES, :] = jnp.zeros((SUBLANES, D_MODEL), F32)

    ext[SUBLANES:SUBLANES + tm, :] = scc_ref[...] * sch_ref[...]
    u = _conv_taps(ext, cw_ref, SC_KERNEL, tm)
    ext[0:SUBLANES, :] = ext[tm:tm + SUBLANES, :]

    m = mod_ref[0]
    y_b = _dot((scb_ref[...] * u).astype(BF16), wsc_ref[...])
    y_a = _dot(yn_ref[...].astype(BF16), wssd_ref[...])
    merged = jax.nn.sigmoid(ga_ref[...]) * y_a + jax.nn.sigmoid(gb_ref[...]) * y_b
    mix = _dot(merged.astype(BF16), wo_ref[...])
    x1 = _ln(ALPHA * x_ref[...] + m[2:3, :] * mix) * lng_ref[...] + lnb_ref[...]
    x1_ref[...] = x1
    h2 = _ln(x1) * (1.0 + m[4:5, :]) + m[3:4, :]
    h2_ref[...] = h2.astype(BF16)
    h2p_ref[...] = _pack_pair(h2[:, 0:D_MODEL // 2], h2[:, D_MODEL // 2:D_MODEL])
    lg_ref[...] = jnp.dot(h2, wr_ref[...], precision=lax.Precision.HIGHEST,
                          preferred_element_type=F32)


def _merge(yn, p, x2d, mod3, wssd, wsc, wo, cw, lng, lnb, wr, bsz, seq, tm):
    t, d = x2d.shape
    ns = seq // tm
    row_map = lambda b, s: (b * ns + s, 0)
    const = lambda b, s: (0, 0)
    col0 = (2048 + 4096) // d

    def pcol(k):
        return pl.BlockSpec((tm, d), lambda b, s: (b * ns + s, col0 + k))

    return pl.pallas_call(
        _merge_kernel,
        grid=(bsz, ns),
        in_specs=[pl.BlockSpec((tm, SSD_INNER), row_map),
                  pcol(0), pcol(1), pcol(2), pcol(3), pcol(4),
                  pl.BlockSpec((tm, d), row_map),
                  pl.BlockSpec((1, 6, d), lambda b, s: (b, 0, 0)),
                  pl.BlockSpec((SSD_INNER, d), const),
                  pl.BlockSpec((d, d), const),
                  pl.BlockSpec((d, d), const),
                  pl.BlockSpec((SC_KERNEL, d), const),
                  pl.BlockSpec((1, d), const),
                  pl.BlockSpec((1, d), const),
                  pl.BlockSpec((d, LANES), const)],
        out_specs=[pl.BlockSpec((tm, d), row_map),
                   pl.BlockSpec((tm, d), row_map),
                   pl.BlockSpec((tm, d // 2), row_map),
                   pl.BlockSpec((tm, LANES), row_map)],
        out_shape=[jax.ShapeDtypeStruct((t, d), F32),
                   jax.ShapeDtypeStruct((t, d), BF16),
                   jax.ShapeDtypeStruct((t, d // 2), U32),
                   jax.ShapeDtypeStruct((t, LANES), F32)],
        scratch_shapes=[pltpu.VMEM((tm + SUBLANES, d), F32)],
        compiler_params=_cparams(("arbitrary", "arbitrary")),
        name="merge",
    )(yn, p, p, p, p, p, x2d, mod3, wssd, wsc, wo, cw, lng, lnb, wr)


def _first_argmax_mask(v, idx, n):
    m = jnp.max(v, axis=0, keepdims=True)
    first = jnp.min(jnp.where(v == m, idx, n), axis=0, keepdims=True)
    return idx == first, m


def _route_kernel(lg_ref, bias_ref, su_ref, g_ref, dest_ref, cnt_ref, counts, carry, start):
    p = pl.program_id(0)
    i = pl.program_id(1)
    tm = lg_ref.shape[0]
    ne, ng, eg = N_EXPERTS, N_EXPERT_GROUPS, EXPERTS_PER_GROUP
    lt = lg_ref[...].T[0:ne, :]
    scores = jax.nn.sigmoid(lt)
    biased = scores + bias_ref[...][0:ne, 0:1]
    grp = biased.reshape(ng, eg, tm)
    idx_e = lax.broadcasted_iota(jnp.int32, (ng, eg, tm), 1)
    m1 = jnp.max(grp, axis=1, keepdims=True)
    first = jnp.min(jnp.where(grp == m1, idx_e, eg), axis=1, keepdims=True)
    m2 = jnp.max(jnp.where(idx_e == first, NEG_BIG, grp), axis=1, keepdims=True)
    gscore = m1 + m2
    idx_g = lax.broadcasted_iota(jnp.int32, (ng, 1, tm), 0)
    gsel = jnp.zeros((ng, 1, tm), F32)
    for _ in range(TOPK_EXPERT_GROUPS):
        hit, _m = _first_argmax_mask(gscore, idx_g, ng)
        gsel = jnp.where(hit, 1.0, gsel)
        gscore = jnp.where(hit, NEG_BIG, gscore)
    emask = jnp.broadcast_to(gsel, (ng, eg, tm)).reshape(ne, tm)
    cand = jnp.where(emask > 0.0, biased, NEG_BIG)
    idx_x = lax.broadcasted_iota(jnp.int32, (ne, tm), 0)
    sel = jnp.zeros((ne, tm), F32)
    picks = []
    for _ in range(TOP_K):
        hit, _m = _first_argmax_mask(cand, idx_x, ne)
        sel = jnp.where(hit, 1.0, sel)
        cand = jnp.where(hit, NEG_BIG, cand)
        picks.append(hit)
    n_e = jnp.broadcast_to(jnp.sum(sel, axis=1, keepdims=True), (ne, LANES))

    @pl.when((p == 0) & (i == 0))
    def _():
        counts[...] = jnp.zeros(counts.shape, F32)

    @pl.when(p == 0)
    def _():
        counts[...] += n_e

    @pl.when((p == 1) & (i == 0))
    def _():
        r = lax.broadcasted_iota(jnp.int32, (ne, ne), 0)
        cc = lax.broadcasted_iota(jnp.int32, (ne, ne), 1)
        below = jnp.where(cc < r, 1.0, 0.0).astype(BF16)
        start[...] = _dot01_exact(below, counts[...])
        carry[...] = jnp.zeros(carry.shape, F32)
        cnt_ref[...] = counts[...]

    @pl.when(p == 1)
    def _():
        before = _dot(sel.astype(BF16), su_ref[...])
        slot = before + carry[:, 0:1] + start[:, 0:1]
        dest_rows = []
        gate_rows = []
        for hit in picks:
            dest_rows.append(jnp.sum(jnp.where(hit, slot, 0.0), axis=0, keepdims=True))
            gate_rows.append(jnp.sum(jnp.where(hit, scores, 0.0), axis=0, keepdims=True))
        dest_ref[...] = jnp.concatenate(dest_rows, axis=0).astype(jnp.int32)
        gate = jnp.concatenate(gate_rows, axis=0)
        gate = gate / jnp.sum(gate, axis=0, keepdims=True) * ROUTED_SCALE
        pad = jnp.zeros((LANES - TOP_K, tm), F32)
        g_ref[...] = jnp.concatenate([gate, pad], axis=0).T
        carry[...] += n_e


def _route(logits, bias_col, su, tm):
    t = logits.shape[0]
    return pl.pallas_call(
        _route_kernel,
        grid=(2, t // tm),
        in_specs=[pl.BlockSpec((tm, LANES), lambda p, i: (i, 0)),
                  pl.BlockSpec((LANES, LANES), lambda p, i: (0, 0)),
                  pl.BlockSpec((tm, tm), lambda p, i: (0, 0))],
        out_specs=[pl.BlockSpec((tm, LANES), lambda p, i: (i * p, 0)),
                   pl.BlockSpec((TOP_K, tm), lambda p, i: (0, i * p)),
                   pl.BlockSpec((N_EXPERTS, LANES), lambda p, i: (0, 0))],
        out_shape=[jax.ShapeDtypeStruct((t, LANES), F32),
                   jax.ShapeDtypeStruct((TOP_K, t), jnp.int32),
                   jax.ShapeDtypeStruct((N_EXPERTS, LANES), F32)],
        scratch_shapes=[pltpu.VMEM((N_EXPERTS, LANES), F32),
                        pltpu.VMEM((N_EXPERTS, LANES), F32),
                        pltpu.VMEM((N_EXPERTS, LANES), F32)],
        compiler_params=_cparams(("arbitrary", "arbitrary")),
        name="route",
    )(logits, bias_col, su)


SC_WINDOW = 128
SC_ROW_WORDS = 256


def _sc_mesh():
    return plsc.VectorSubcoreMesh(core_axis_name="c", subcore_axis_name="s")


def _sc_scatter_rows(rows, idx, n_out):
    n_rows, w = rows.shape
    n_k = idx.shape[0]

    @pl.kernel(out_type=jax.ShapeDtypeStruct((n_out, w), rows.dtype), mesh=_sc_mesh(),
               scratch_types=[])
    def scatter(x_hbm, i_hbm, o_hbm):
        def body(x_vmem, i_vmem):
            for k in range(n_k):
                pltpu.sync_copy(x_vmem, o_hbm.at[i_vmem.at[k]])

        pltpu.emit_pipeline(
            body,
            grid=(n_rows // SC_WINDOW,),
            in_specs=[pl.BlockSpec((SC_WINDOW, w), index_map=lambda i: (i, 0)),
                      pl.BlockSpec((n_k, SC_WINDOW), index_map=lambda i: (0, i))],
            out_specs=[],
            core_axis_name=("c", "s"),
            dimension_semantics=(pltpu.PARALLEL,),
        )(x_hbm, i_hbm)

    return scatter(rows, idx)


def _sc_gather_rows(rows, idx):
    n = idx.shape[1]
    w = rows.shape[1]

    @pl.kernel(out_type=jax.ShapeDtypeStruct((n, w), rows.dtype), mesh=_sc_mesh(),
               scratch_types=[])
    def gather(x_hbm, i_hbm, o_hbm):
        def body(i_vmem, o_vmem):
            pltpu.sync_copy(x_hbm.at[i_vmem.at[0]], o_vmem)

        pltpu.emit_pipeline(
            body,
            grid=(n // SC_WINDOW,),
            in_specs=[pl.BlockSpec((1, SC_WINDOW), index_map=lambda i: (0, i))],
            out_specs=[pl.BlockSpec((SC_WINDOW, w), index_map=lambda i: (i, 0))],
            core_axis_name=("c", "s"),
            dimension_semantics=(pltpu.PARALLEL,),
        )(i_hbm, o_hbm)

    return gather(rows, idx)


def _gmm_kernel(vb_ref, ve_ref, vlo_ref, vhi_ref, x_ref, wgu_ref, wd_ref, o_ref, acc):
    v = pl.program_id(0)
    lo = vlo_ref[v]
    hi = vhi_ref[v]
    bm = x_ref.shape[0]

    @pl.when(hi > lo)
    def _():
        xa, xb = _unpack_pair(x_ref[...])
        x = jnp.concatenate([xa.astype(BF16), xb.astype(BF16)], axis=1)
        a = _dot(x, wgu_ref[0])
        act = _silu(a[:, 0:D_EXPERT]) * a[:, D_EXPERT:2 * D_EXPERT]
        r = lax.broadcasted_iota(jnp.int32, (bm, D_EXPERT), 0)
        act = jnp.where((r >= lo) & (r < hi), act, 0.0)
        y = _dot(act.astype(BF16), wd_ref[0])

        @pl.when(lo == 0)
        def _():
            acc[...] = y

        @pl.when(lo > 0)
        def _():
            acc[...] += y

        half = D_MODEL // 2
        o_ref[...] = _pack_pair(acc[:, 0:half], acc[:, half:D_MODEL])


def _gmm(xin, wgu, wd, vb, ve, vlo, vhi, bm):
    n_rows, half = xin.shape
    d = 2 * half
    n_visits = vb.shape[0]
    grid_spec = pltpu.PrefetchScalarGridSpec(
        num_scalar_prefetch=4,
        grid=(n_visits,),
        in_specs=[pl.BlockSpec((bm, half), lambda v, vb, ve, vlo, vhi: (vb[v], 0)),
                  pl.BlockSpec((1, d, 2 * D_EXPERT), lambda v, vb, ve, vlo, vhi: (ve[v], 0, 0)),
                  pl.BlockSpec((1, D_EXPERT, d), lambda v, vb, ve, vlo, vhi: (ve[v], 0, 0))],
        out_specs=pl.BlockSpec((bm, half), lambda v, vb, ve, vlo, vhi: (vb[v], 0)),
        scratch_shapes=[pltpu.VMEM((bm, d), F32)],
    )
    return pl.pallas_call(
        _gmm_kernel,
        grid_spec=grid_spec,
        out_shape=jax.ShapeDtypeStruct((n_rows, half), U32),
        compiler_params=_cparams(("arbitrary",)),
        name="gmm",
    )(vb, ve, vlo, vhi, xin, wgu, wd)


def _visit_plan(counts, n_blocks, bm):
    ne = counts.shape[0]
    n_visits = n_blocks + ne
    end = jnp.cumsum(counts)
    start = end - counts
    has = counts > 0
    b0 = start // bm
    nb = jnp.where(has, (end - 1) // bm - b0 + 1, 0)
    vend = jnp.cumsum(nb)
    vstart = vend - nb
    total = vend[-1]
    v = jnp.arange(n_visits, dtype=jnp.int32)
    valid = v < total
    vc = jnp.minimum(v, total - 1)
    e_v = jnp.minimum(jnp.searchsorted(vend, vc, side="right"), ne - 1).astype(jnp.int32)
    blk = (b0[e_v] + vc - vstart[e_v]).astype(jnp.int32)
    lo = jnp.clip(start[e_v] - blk * bm, 0, bm)
    hi = jnp.clip(end[e_v] - blk * bm, 0, bm)
    lo = jnp.where(valid, lo, 0).astype(jnp.int32)
    hi = jnp.where(valid, hi, 0).astype(jnp.int32)
    return blk, e_v, lo, hi


def _combine_kernel(yg_ref, g_ref, h_ref, x1_ref, mod_ref, wsgu_ref, wsd_ref, lng_ref,
                    lnb_ref, o_ref):
    half = D_MODEL // 2
    g = g_ref[...]
    a = _dot(h_ref[...], wsgu_ref[...])
    act = _silu(a[:, 0:D_EXPERT]) * a[:, D_EXPERT:2 * D_EXPERT]
    shared = _dot(act.astype(BF16), wsd_ref[...])
    f_a = shared[:, 0:half]
    f_b = shared[:, half:D_MODEL]
    for k in range(TOP_K):
        ya, yb = _unpack_pair(yg_ref[k])
        gk = g[:, k:k + 1]
        f_a = f_a + gk * ya
        f_b = f_b + gk * yb
    ffn = jnp.concatenate([f_a, f_b], axis=1)
    m = mod_ref[0]
    r = ALPHA * x1_ref[...] + m[5:6, :] * ffn
    o_ref[...] = _ln(r) * lng_ref[...] + lnb_ref[...]


def _combine(yg, gates, h2, x1, mod3, wsgu, wsd, lng, lnb, seq, tm):
    t, d = x1.shape
    row = lambda i: (i, 0)
    const = lambda i: (0, 0)
    return pl.pallas_call(
        _combine_kernel,
        grid=(t // tm,),
        in_specs=[pl.BlockSpec((TOP_K, tm, d // 2), lambda i: (0, i, 0)),
                  pl.BlockSpec((tm, LANES), row),
                  pl.BlockSpec((tm, d), row),
                  pl.BlockSpec((tm, d), row),
                  pl.BlockSpec((1, 6, d), lambda i: ((i * tm) // seq, 0, 0)),
                  pl.BlockSpec((d, 2 * D_EXPERT), const),
                  pl.BlockSpec((D_EXPERT, d), const),
                  pl.BlockSpec((1, d), const),
                  pl.BlockSpec((1, d), const)],
        out_specs=pl.BlockSpec((tm, d), row),
        out_shape=jax.ShapeDtypeStruct((t, d), F32),
        compiler_params=_cparams(("arbitrary",)),
        name="combine",
    )(yg, gates, h2, x1, mod3, wsgu, wsd, lng, lnb)


def _pad_lanes(v, fill=0.0):
    out = jnp.full((1, LANES), fill, F32)
    return out.at[0, :v.shape[0]].set(v.astype(F32))


def kernel(x, c, w_ada, b_ada, w_in, ssd_conv_w, ssd_conv_b, ssd_dt_bias, ssd_A_log, ssd_D,
           ssd_norm_w, w_ssd_out, sc_conv_w, w_sc_out, w_o, ln1_g, ln1_b, router_w,
           router_bias, w_gate, w_up, w_down, sh_gate, sh_up, sh_down, ln2_g, ln2_b):
    bsz, seq, d = x.shape
    t = bsz * seq
    depth = w_in.shape[0]
    tm_proj = min(512, seq)
    tm_merge = min(256, seq)
    tm_route = min(1024, t)
    tm_comb = min(256, seq)
    bm_gmm = 512
    n_assign = t * TOP_K
    parts = (d // 2) // SC_ROW_WORDS

    e01 = (np.arange(LANES)[:, None] == (np.arange(SSD_INNER)[None, :] // SSD_HEAD_DIM))
    e01 = jnp.asarray(e01, BF16)
    su = jnp.asarray(np.arange(tm_route)[:, None] < np.arange(tm_route)[None, :], BF16)
    part_off = jnp.arange(parts, dtype=jnp.int32)

    xf = x.reshape(t, d)
    for l in range(depth):
        w_l = w_in[l]
        w_main = jnp.concatenate([w_l[:, :DT_COL0], w_l[:, DT_COL0 + SSD_HEADS:]],
                                 axis=1).astype(BF16)
        w_dt = jnp.pad(w_l[:, DT_COL0:DT_COL0 + SSD_HEADS],
                       ((0, 0), (0, LANES - SSD_HEADS))).astype(BF16)
        wgu = jnp.concatenate([w_gate[l], w_up[l]], axis=-1).astype(BF16)
        wd = w_down[l].astype(BF16)
        wsgu = jnp.concatenate([sh_gate[l], sh_up[l]], axis=-1).astype(BF16)
        wsd = sh_down[l].astype(BF16)
        wr = jnp.pad(router_w[l], ((0, 0), (0, LANES - N_EXPERTS)))
        bias_col = jnp.zeros((LANES, LANES), F32).at[:N_EXPERTS, 0].set(router_bias[l])
        dvec = jnp.repeat(ssd_D[l], SSD_HEAD_DIM).reshape(1, SSD_INNER)

        mod3 = _mod(c, w_ada[l], b_ada[l]).reshape(bsz, 6, d)
        p, dtp = _inproj(xf, mod3, w_main, w_dt, seq, tm_proj, 1024)
        yn = _ssd(p, dtp, ssd_conv_w[l], ssd_conv_b[l].reshape(1, -1),
                  _pad_lanes(ssd_dt_bias[l]), _pad_lanes(ssd_A_log[l]), dvec,
                  ssd_norm_w[l].reshape(1, -1), e01, bsz, seq)
        x1, h2, h2p, logits = _merge(yn, p, xf, mod3, w_ssd_out[l].astype(BF16),
                                     w_sc_out[l].astype(BF16), w_o[l].astype(BF16),
                                     sc_conv_w[l], ln1_g[l].reshape(1, d),
                                     ln1_b[l].reshape(1, d), wr, bsz, seq, tm_merge)
        gates, dest, cnt = _route(logits, bias_col, su, tm_route)
        dest_p = (dest[:, :, None] * parts + part_off).reshape(TOP_K, t * parts)
        vb, ve, vlo, vhi = _visit_plan(cnt[:, 0].astype(jnp.int32), n_assign // bm_gmm, bm_gmm)
        xin = _sc_scatter_rows(h2p.reshape(t * parts, SC_ROW_WORDS), dest_p, n_assign * parts)
        yb = _gmm(xin.reshape(n_assign, d // 2), wgu, wd, vb, ve, vlo, vhi, bm_gmm)
        yg = _sc_gather_rows(yb.reshape(n_assign * parts, SC_ROW_WORDS),
                             dest_p.reshape(1, n_assign * parts))
        xf = _combine(yg.reshape(TOP_K, t, d // 2), gates, h2, x1, mod3, wsgu, wsd,
                      ln2_g[l].reshape(1, d), ln2_b[l].reshape(1, d), seq, tm_comb)
    return xf.reshape(bsz, seq, d)
```

```python
import jax
import jax.numpy as jnp
import numpy as np
from jax import lax
from jax.experimental import pallas as pl
from jax.experimental.pallas import tpu as pltpu
from jax.experimental.pallas import tpu_sc as plsc

F32 = jnp.float32
BF16 = jnp.bfloat16
U32 = jnp.uint32

LANES = 128
SUBLANES = 8
VMEM_LIMIT_BYTES = 56 * 1024 * 1024

D_MODEL = 1024
SSD_INNER = 2048
SSD_HEAD_DIM = 64
SSD_HEADS = 32
SSD_GROUPS = 8
SSD_STATE = 128
SSD_CONV = 4
SSD_CHUNK = 128
GROUP_W = SSD_INNER // SSD_GROUPS
HEADS_PER_GROUP = SSD_HEADS // SSD_GROUPS
SC_KERNEL = 3
N_EXPERTS = 64
TOP_K = 8
N_EXPERT_GROUPS = 8
TOPK_EXPERT_GROUPS = 4
EXPERTS_PER_GROUP = N_EXPERTS // N_EXPERT_GROUPS
D_EXPERT = 256
ROUTED_SCALE = 2.5
LN_EPS = 1e-5
RMS_EPS = 1e-5
ALPHA = 2.0 ** 0.25
DT_COL0 = 2048 + 4096
NEG_BIG = -1e30
SC_WINDOW = 128
SC_ROW_WORDS = 256
GMM_ROWS = 256


def _cparams(sem):
    return pltpu.CompilerParams(dimension_semantics=sem,
                                vmem_limit_bytes=VMEM_LIMIT_BYTES)


def _ln(x):
    mu = jnp.mean(x, axis=-1, keepdims=True)
    xc = x - mu
    var = jnp.mean(xc * xc, axis=-1, keepdims=True)
    return xc * lax.rsqrt(var + LN_EPS)


def _silu(x):
    return x * jax.nn.sigmoid(x)


def _split3(a):
    hi = a.astype(BF16)
    r1 = a - hi.astype(F32)
    mid = r1.astype(BF16)
    lo = (r1 - mid.astype(F32)).astype(BF16)
    return hi, mid, lo


def _dot(a, b):
    return jnp.dot(a, b, preferred_element_type=F32)


def _dot_exact01(a, m01):
    hi, mid, lo = _split3(a)
    return _dot(hi, m01) + _dot(mid, m01) + _dot(lo, m01)


def _dot01_exact(m01, a):
    hi, mid, lo = _split3(a)
    return _dot(m01, hi) + _dot(m01, mid) + _dot(m01, lo)


def _pack_pair(a, b):
    ab = pltpu.bitcast(a.astype(BF16).astype(F32), U32)
    bb = pltpu.bitcast(b.astype(BF16).astype(F32), U32)
    return ab | lax.shift_right_logical(bb, jnp.uint32(16))


def _unpack_pair(w):
    a = pltpu.bitcast(w & jnp.uint32(0xFFFF0000), F32)
    b = pltpu.bitcast(lax.shift_left(w, jnp.uint32(16)), F32)
    return a, b


def _mod_kernel(c_ref, w_ref, b_ref, o_ref):
    a = _silu(c_ref[...])
    o_ref[...] = jnp.dot(a, w_ref[...], precision=lax.Precision.HIGHEST,
                         preferred_element_type=F32) + b_ref[...]


def _mod(c, w, b):
    bsz, d = c.shape
    n = w.shape[1]
    tn = 1024
    return pl.pallas_call(
        _mod_kernel,
        grid=(n // tn,),
        in_specs=[pl.BlockSpec((bsz, d), lambda j: (0, 0)),
                  pl.BlockSpec((d, tn), lambda j: (0, j)),
                  pl.BlockSpec((1, tn), lambda j: (0, j))],
        out_specs=pl.BlockSpec((bsz, tn), lambda j: (0, j)),
        out_shape=jax.ShapeDtypeStruct((bsz, n), F32),
        compiler_params=_cparams(("arbitrary",)),
        name="mod",
    )(c, w, b.reshape(1, n))


def _inproj_kernel(x_ref, mod_ref, w_ref, wdt_ref, p_ref, dt_ref, h_scr):
    j = pl.program_id(1)

    @pl.when(j == 0)
    def _():
        m = mod_ref[0]
        h = _ln(x_ref[...]) * (1.0 + m[1:2, :]) + m[0:1, :]
        hb = h.astype(BF16)
        h_scr[...] = hb
        dt_ref[...] = _dot(hb, wdt_ref[...])

    p_ref[...] = _dot(h_scr[...], w_ref[...]).astype(p_ref.dtype)


def _inproj(x2d, mod3, w_main, w_dt, seq, tm, tn):
    t, d = x2d.shape
    n = w_main.shape[1]
    return pl.pallas_call(
        _inproj_kernel,
        grid=(t // tm, n // tn),
        in_specs=[pl.BlockSpec((tm, d), lambda i, j: (i, 0)),
                  pl.BlockSpec((1, 6, d), lambda i, j: ((i * tm) // seq, 0, 0)),
                  pl.BlockSpec((d, tn), lambda i, j: (0, j)),
                  pl.BlockSpec((d, LANES), lambda i, j: (0, 0))],
        out_specs=[pl.BlockSpec((tm, tn), lambda i, j: (i, j)),
                   pl.BlockSpec((tm, LANES), lambda i, j: (i, 0))],
        out_shape=[jax.ShapeDtypeStruct((t, n), BF16),
                   jax.ShapeDtypeStruct((t, LANES), F32)],
        scratch_shapes=[pltpu.VMEM((tm, d), BF16)],
        compiler_params=_cparams(("arbitrary", "arbitrary")),
        name="inproj",
    )(x2d, mod3, w_main, w_dt)


def _conv_taps(ext_ref, w_ref, n_taps, rows):
    acc = None
    for k in range(n_taps):
        start = SUBLANES - (n_taps - 1) + k
        term = w_ref[k:k + 1, :] * ext_ref[start:start + rows, :]
        acc = term if acc is None else acc + term
    return acc


def _ssd_kernel(xs_ref, bc_ref, z_ref, dtp_ref, cw_xs_ref, cw_bc_ref, cb_xs_ref,
                cb_bc_ref, dtb_ref, alog_ref, dvec_ref, nw_ref, e_ref,
                yn_ref, ext_xs, ext_bc, state):
    c = pl.program_id(1)
    L = SSD_CHUNK

    @pl.when(c == 0)
    def _():
        ext_xs[0:SUBLANES, :] = jnp.zeros((SUBLANES, SSD_INNER), F32)
        ext_bc[0:SUBLANES, :] = jnp.zeros((SUBLANES, SSD_INNER), F32)
        state[...] = jnp.zeros(state.shape, F32)

    ext_xs[SUBLANES:SUBLANES + L, :] = xs_ref[...].astype(F32)
    ext_bc[SUBLANES:SUBLANES + L, :] = bc_ref[...].astype(F32)
    xs = _silu(_conv_taps(ext_xs, cw_xs_ref, SSD_CONV, L) + cb_xs_ref[...])
    bc = _silu(_conv_taps(ext_bc, cw_bc_ref, SSD_CONV, L) + cb_bc_ref[...])
    ext_xs[0:SUBLANES, :] = ext_xs[L:L + SUBLANES, :]
    ext_bc[0:SUBLANES, :] = ext_bc[L:L + SUBLANES, :]

    x_dt = dtp_ref[...] + dtb_ref[...]
    dt = jnp.maximum(x_dt, 0.0) + jnp.log1p(jnp.exp(-jnp.abs(x_dt)))
    a_neg = -jnp.exp(alog_ref[...])
    d_a = dt * a_neg
    row = lax.broadcasted_iota(jnp.int32, (L, L), 0)
    col = lax.broadcasted_iota(jnp.int32, (L, L), 1)
    causal = col <= row
    tri = jnp.where(causal, 1.0, 0.0).astype(BF16)
    hi, mid, lo = _split3(d_a)
    a_cs = _dot(tri, hi) + _dot(tri, mid) + _dot(tri, lo)
    a_cs_t = a_cs.T
    a_last = a_cs[L - 1:L, :]
    stack = jnp.concatenate([dt, jnp.exp(a_cs), jnp.exp(a_last - a_cs)], axis=0)
    st_e = _dot_exact01(stack, e_ref[...])
    dt_e = st_e[0:L]
    od_e = st_e[L:2 * L]
    ds_e = st_e[2 * L:3 * L]
    cd_e = od_e[L - 1:L, :]

    x_f = xs * dt_e
    x_b = x_f.astype(BF16)
    xd_b = (x_f * ds_e).astype(BF16)
    lane_head = lax.broadcasted_iota(jnp.int32, (L, GROUP_W), 1) // SSD_HEAD_DIM

    for g in range(SSD_GROUPS):
        gc = slice(g * GROUP_W, (g + 1) * GROUP_W)
        b_g = bc[:, g * SSD_STATE:(g + 1) * SSD_STATE]
        c_g = bc[:, SSD_GROUPS * SSD_STATE + g * SSD_STATE:
                 SSD_GROUPS * SSD_STATE + (g + 1) * SSD_STATE].astype(BF16)
        b_gt = b_g.T.astype(BF16)
        cb = _dot(c_g, b_gt)
        x_g = x_b[:, gc]
        lhs = []
        rhs = []
        for r in range(HEADS_PER_GROUP):
            h = g * HEADS_PER_GROUP + r
            seg = a_cs[:, h:h + 1] - a_cs_t[h:h + 1, :]
            lmat = jnp.exp(jnp.where(causal, seg, NEG_BIG))
            lhs.append((cb * lmat).astype(BF16))
            rhs.append(jnp.where(lane_head == r, x_g, jnp.zeros_like(x_g)))
        y_diag = _dot(jnp.concatenate(lhs, axis=1), jnp.concatenate(rhs, axis=0))
        st = state[g]
        y_off = _dot(c_g, st.astype(BF16)) * od_e[:, gc]
        state[g] = cd_e[:, gc] * st + _dot(b_gt, xd_b[:, gc])
        y = y_diag + y_off + dvec_ref[:, gc] * xs[:, gc]
        gt = y * _silu(z_ref[:, gc].astype(F32))
        ms = jnp.mean(gt * gt, axis=-1, keepdims=True)
        yn_ref[:, gc] = (gt * lax.rsqrt(ms + RMS_EPS) * nw_ref[:, gc]).astype(yn_ref.dtype)


def _ssd(p, dtp, cw, cb, dtb, alog, dvec, nw, e01, bsz, seq):
    t = p.shape[0]
    L = SSD_CHUNK
    nc = seq // L
    row_map = lambda b, c: (b * nc + c, 0)

    def pcol(k):
        return pl.BlockSpec((L, SSD_INNER), lambda b, c: (b * nc + c, k))

    const = lambda b, c: (0, 0)
    return pl.pallas_call(
        _ssd_kernel,
        grid=(bsz, nc),
        in_specs=[pcol(1), pcol(2), pcol(0),
                  pl.BlockSpec((L, LANES), row_map),
                  pl.BlockSpec((SSD_CONV, SSD_INNER), lambda b, c: (0, 0)),
                  pl.BlockSpec((SSD_CONV, SSD_INNER), lambda b, c: (0, 1)),
                  pl.BlockSpec((1, SSD_INNER), lambda b, c: (0, 0)),
                  pl.BlockSpec((1, SSD_INNER), lambda b, c: (0, 1)),
                  pl.BlockSpec((1, LANES), const),
                  pl.BlockSpec((1, LANES), const),
                  pl.BlockSpec((1, SSD_INNER), const),
                  pl.BlockSpec((1, SSD_INNER), const),
                  pl.BlockSpec((LANES, SSD_INNER), const)],
        out_specs=pl.BlockSpec((L, SSD_INNER), row_map),
        out_shape=jax.ShapeDtypeStruct((t, SSD_INNER), BF16),
        scratch_shapes=[pltpu.VMEM((L + SUBLANES, SSD_INNER), F32),
                        pltpu.VMEM((L + SUBLANES, SSD_INNER), F32),
                        pltpu.VMEM((SSD_GROUPS, SSD_STATE, GROUP_W), F32)],
        compiler_params=_cparams(("arbitrary", "arbitrary")),
        name="ssd",
    )(p, p, p, dtp, cw, cw, cb, cb, dtb, alog, dvec, nw, e01)


def _merge_kernel(yn_ref, scb_ref, scc_ref, sch_ref, ga_ref, gb_ref, x_ref, mod_ref,
                  wssd_ref, wsc_ref, wo_ref, cw_ref, lng_ref, lnb_ref, wr_ref,
                  x1_ref, h2_ref, h2p_ref, lg_ref, ext):
    s = pl.program_id(1)
    tm = x_ref.shape[0]

    @pl.when(s == 0)
    def _():
        ext[0:SUBLANES, :] = jnp.zeros((SUBLANES, D_MODEL), F32)

    ext[SUBLANES:SUBLANES + tm, :] = scc_ref[...].astype(F32) * sch_ref[...].astype(F32)
    u = _conv_taps(ext, cw_ref, SC_KERNEL, tm)
    ext[0:SUBLANES, :] = ext[tm:tm + SUBLANES, :]

    m = mod_ref[0]
    y_b = _dot((scb_ref[...].astype(F32) * u).astype(BF16), wsc_ref[...])
    y_a = _dot(yn_ref[...], wssd_ref[...])
    merged = (jax.nn.sigmoid(ga_ref[...].astype(F32)) * y_a
              + jax.nn.sigmoid(gb_ref[...].astype(F32)) * y_b)
    mix = _dot(merged.astype(BF16), wo_ref[...])
    x1 = _ln(ALPHA * x_ref[...] + m[2:3, :] * mix) * lng_ref[...] + lnb_ref[...]
    x1_ref[...] = x1
    h2 = _ln(x1) * (1.0 + m[4:5, :]) + m[3:4, :]
    h2_ref[...] = h2.astype(BF16)
    w = _pack_pair(h2[:, 0:D_MODEL // 2], h2[:, D_MODEL // 2:D_MODEL])
    h2p_ref[0] = w[:, 0:SC_ROW_WORDS]
    h2p_ref[1] = w[:, SC_ROW_WORDS:2 * SC_ROW_WORDS]
    lg_ref[...] = jnp.dot(h2, wr_ref[...], precision=lax.Precision.HIGHEST,
                          preferred_element_type=F32)


def _merge(yn, p, x2d, mod3, wssd, wsc, wo, cw, lng, lnb, wr, bsz, seq, tm):
    t, d = x2d.shape
    ns = seq // tm
    row_map = lambda b, s: (b * ns + s, 0)
    const = lambda b, s: (0, 0)
    col0 = (2048 + 4096) // d

    def pcol(k):
        return pl.BlockSpec((tm, d), lambda b, s: (b * ns + s, col0 + k))

    return pl.pallas_call(
        _merge_kernel,
        grid=(bsz, ns),
        in_specs=[pl.BlockSpec((tm, SSD_INNER), row_map),
                  pcol(0), pcol(1), pcol(2), pcol(3), pcol(4),
                  pl.BlockSpec((tm, d), row_map),
                  pl.BlockSpec((1, 6, d), lambda b, s: (b, 0, 0)),
                  pl.BlockSpec((SSD_INNER, d), const),
                  pl.BlockSpec((d, d), const),
                  pl.BlockSpec((d, d), const),
                  pl.BlockSpec((SC_KERNEL, d), const),
                  pl.BlockSpec((1, d), const),
                  pl.BlockSpec((1, d), const),
                  pl.BlockSpec((d, LANES), const)],
        out_specs=[pl.BlockSpec((tm, d), row_map),
                   pl.BlockSpec((tm, d), row_map),
                   pl.BlockSpec((2, tm, SC_ROW_WORDS), lambda b, s: (0, b * ns + s, 0)),
                   pl.BlockSpec((tm, LANES), row_map)],
        out_shape=[jax.ShapeDtypeStruct((t, d), F32),
                   jax.ShapeDtypeStruct((t, d), BF16),
                   jax.ShapeDtypeStruct((2, t, SC_ROW_WORDS), U32),
                   jax.ShapeDtypeStruct((t, LANES), F32)],
        scratch_shapes=[pltpu.VMEM((tm + SUBLANES, d), F32)],
        compiler_params=_cparams(("arbitrary", "arbitrary")),
        name="merge",
    )(yn, p, p, p, p, p, x2d, mod3, wssd, wsc, wo, cw, lng, lnb, wr)


def _first_argmax_mask(v, idx, n):
    m = jnp.max(v, axis=0, keepdims=True)
    first = jnp.min(jnp.where(v == m, idx, n), axis=0, keepdims=True)
    return idx == first, m


def _route_kernel(lg_ref, bias_ref, su_ref, g_ref, dest_ref, cnt_ref, counts, carry, start):
    p = pl.program_id(0)
    i = pl.program_id(1)
    tm = lg_ref.shape[0]
    ne, ng, eg = N_EXPERTS, N_EXPERT_GROUPS, EXPERTS_PER_GROUP
    lt = lg_ref[...].T[0:ne, :]
    scores = jax.nn.sigmoid(lt)
    biased = scores + bias_ref[...][0:ne, 0:1]
    grp = biased.reshape(ng, eg, tm)
    idx_e = lax.broadcasted_iota(jnp.int32, (ng, eg, tm), 1)
    m1 = jnp.max(grp, axis=1, keepdims=True)
    first = jnp.min(jnp.where(grp == m1, idx_e, eg), axis=1, keepdims=True)
    m2 = jnp.max(jnp.where(idx_e == first, NEG_BIG, grp), axis=1, keepdims=True)
    gscore = m1 + m2
    idx_g = lax.broadcasted_iota(jnp.int32, (ng, 1, tm), 0)
    gsel = jnp.zeros((ng, 1, tm), F32)
    for _ in range(TOPK_EXPERT_GROUPS):
        hit, _m = _first_argmax_mask(gscore, idx_g, ng)
        gsel = jnp.where(hit, 1.0, gsel)
        gscore = jnp.where(hit, NEG_BIG, gscore)
    emask = jnp.broadcast_to(gsel, (ng, eg, tm)).reshape(ne, tm)
    cand = jnp.where(emask > 0.0, biased, NEG_BIG)
    idx_x = lax.broadcasted_iota(jnp.int32, (ne, tm), 0)
    sel = jnp.zeros((ne, tm), F32)
    picks = []
    for _ in range(TOP_K):
        hit, _m = _first_argmax_mask(cand, idx_x, ne)
        sel = jnp.where(hit, 1.0, sel)
        cand = jnp.where(hit, NEG_BIG, cand)
        picks.append(hit)
    n_e = jnp.broadcast_to(jnp.sum(sel, axis=1, keepdims=True), (ne, LANES))

    @pl.when((p == 0) & (i == 0))
    def _():
        counts[...] = jnp.zeros(counts.shape, F32)

    @pl.when(p == 0)
    def _():
        counts[...] += n_e

    @pl.when((p == 1) & (i == 0))
    def _():
        r = lax.broadcasted_iota(jnp.int32, (ne, ne), 0)
        cc = lax.broadcasted_iota(jnp.int32, (ne, ne), 1)
        below = jnp.where(cc < r, 1.0, 0.0).astype(BF16)
        start[...] = _dot01_exact(below, counts[...])
        carry[...] = jnp.zeros(carry.shape, F32)
        cnt_ref[...] = counts[...]

    @pl.when(p == 1)
    def _():
        before = _dot(sel.astype(BF16), su_ref[...])
        slot = before + carry[:, 0:1] + start[:, 0:1]
        dest_rows = []
        gate_rows = []
        for hit in picks:
            dest_rows.append(jnp.sum(jnp.where(hit, slot, 0.0), axis=0, keepdims=True))
            gate_rows.append(jnp.sum(jnp.where(hit, scores, 0.0), axis=0, keepdims=True))
        dest_ref[...] = jnp.concatenate(dest_rows, axis=0).astype(jnp.int32)
        gate = jnp.concatenate(gate_rows, axis=0)
        gate = gate / jnp.sum(gate, axis=0, keepdims=True) * ROUTED_SCALE
        pad = jnp.zeros((LANES - TOP_K, tm), F32)
        g_ref[...] = jnp.concatenate([gate, pad], axis=0).T
        carry[...] += n_e


def _route(logits, bias_col, su, tm):
    t = logits.shape[0]
    return pl.pallas_call(
        _route_kernel,
        grid=(2, t // tm),
        in_specs=[pl.BlockSpec((tm, LANES), lambda p, i: (i, 0)),
                  pl.BlockSpec((LANES, LANES), lambda p, i: (0, 0)),
                  pl.BlockSpec((tm, tm), lambda p, i: (0, 0))],
        out_specs=[pl.BlockSpec((tm, LANES), lambda p, i: (i * p, 0)),
                   pl.BlockSpec((TOP_K, tm), lambda p, i: (0, i * p)),
                   pl.BlockSpec((N_EXPERTS, LANES), lambda p, i: (0, 0))],
        out_shape=[jax.ShapeDtypeStruct((t, LANES), F32),
                   jax.ShapeDtypeStruct((TOP_K, t), jnp.int32),
                   jax.ShapeDtypeStruct((N_EXPERTS, LANES), F32)],
        scratch_shapes=[pltpu.VMEM((N_EXPERTS, LANES), F32),
                        pltpu.VMEM((N_EXPERTS, LANES), F32),
                        pltpu.VMEM((N_EXPERTS, LANES), F32)],
        compiler_params=_cparams(("arbitrary", "arbitrary")),
        name="route",
    )(logits, bias_col, su)


def _sc_mesh():
    return plsc.VectorSubcoreMesh(core_axis_name="c", subcore_axis_name="s")


def _sc_scatter_rows(rows, idx, n_out):
    n_rows, w = rows.shape
    n_k = idx.shape[0]

    @pl.kernel(out_type=jax.ShapeDtypeStruct((n_out, w), rows.dtype), mesh=_sc_mesh(),
               scratch_types=[])
    def scatter(x_hbm, i_hbm, o_hbm):
        def body(x_vmem, i_vmem):
            for k in range(n_k):
                pltpu.sync_copy(x_vmem, o_hbm.at[i_vmem.at[k]])

        pltpu.emit_pipeline(
            body,
            grid=(n_rows // SC_WINDOW,),
            in_specs=[pl.BlockSpec((SC_WINDOW, w), index_map=lambda i: (i, 0)),
                      pl.BlockSpec((n_k, SC_WINDOW), index_map=lambda i: (0, i))],
            out_specs=[],
            core_axis_name=("c", "s"),
            dimension_semantics=(pltpu.PARALLEL,),
        )(x_hbm, i_hbm)

    return scatter(rows, idx)


def _sc_gather_rows(rows, idx):
    n = idx.shape[1]
    w = rows.shape[1]

    @pl.kernel(out_type=jax.ShapeDtypeStruct((n, w), rows.dtype), mesh=_sc_mesh(),
               scratch_types=[])
    def gather(x_hbm, i_hbm, o_hbm):
        def body(i_vmem, o_vmem):
            pltpu.sync_copy(x_hbm.at[i_vmem.at[0]], o_vmem)

        pltpu.emit_pipeline(
            body,
            grid=(n // SC_WINDOW,),
            in_specs=[pl.BlockSpec((1, SC_WINDOW), index_map=lambda i: (0, i))],
            out_specs=[pl.BlockSpec((SC_WINDOW, w), index_map=lambda i: (i, 0))],
            core_axis_name=("c", "s"),
            dimension_semantics=(pltpu.PARALLEL,),
        )(i_hbm, o_hbm)

    return gather(rows, idx)


def _gmm_kernel(vb_ref, ve_ref, vlo_ref, vhi_ref, x_ref, wgu_ref, wd_ref, o_ref):
    v = pl.program_id(0)
    lo = vlo_ref[v]
    hi = vhi_ref[v]
    bm = x_ref.shape[1]
    half = D_MODEL // 2

    @pl.when(hi > lo)
    def _():
        packed = []
        for c in range(bm // GMM_ROWS):
            rows = slice(c * GMM_ROWS, (c + 1) * GMM_ROWS)
            a0, b0 = _unpack_pair(x_ref[0, rows, :])
            a1, b1 = _unpack_pair(x_ref[1, rows, :])
            x = jnp.concatenate([a0.astype(BF16), a1.astype(BF16),
                                 b0.astype(BF16), b1.astype(BF16)], axis=1)
            a = _dot(x, wgu_ref[0])
            act = _silu(a[:, 0:D_EXPERT]) * a[:, D_EXPERT:2 * D_EXPERT]
            y = _dot(act.astype(BF16), wd_ref[0])
            packed.append(_pack_pair(y[:, 0:half], y[:, half:D_MODEL]))

        @pl.when(lo == 0)
        def _():
            for c, w in enumerate(packed):
                rows = slice(c * GMM_ROWS, (c + 1) * GMM_ROWS)
                o_ref[0, rows, :] = w[:, 0:SC_ROW_WORDS]
                o_ref[1, rows, :] = w[:, SC_ROW_WORDS:half]

        @pl.when(lo > 0)
        def _():
            for c, w in enumerate(packed):
                rows = slice(c * GMM_ROWS, (c + 1) * GMM_ROWS)
                r = lax.broadcasted_iota(jnp.int32, (GMM_ROWS, SC_ROW_WORDS), 0) + c * GMM_ROWS
                mine = (r >= lo) & (r < hi)
                o_ref[0, rows, :] = jnp.where(mine, w[:, 0:SC_ROW_WORDS], o_ref[0, rows, :])
                o_ref[1, rows, :] = jnp.where(mine, w[:, SC_ROW_WORDS:half], o_ref[1, rows, :])


def _gmm(xin, wgu, wd, vb, ve, vlo, vhi, bm):
    _, n_rows, w = xin.shape
    d = D_MODEL
    n_visits = vb.shape[0]
    grid_spec = pltpu.PrefetchScalarGridSpec(
        num_scalar_prefetch=4,
        grid=(n_visits,),
        in_specs=[pl.BlockSpec((2, bm, w), lambda v, vb, ve, vlo, vhi: (0, vb[v], 0)),
                  pl.BlockSpec((1, d, 2 * D_EXPERT), lambda v, vb, ve, vlo, vhi: (ve[v], 0, 0)),
                  pl.BlockSpec((1, D_EXPERT, d), lambda v, vb, ve, vlo, vhi: (ve[v], 0, 0))],
        out_specs=pl.BlockSpec((2, bm, w), lambda v, vb, ve, vlo, vhi: (0, vb[v], 0)),
    )
    return pl.pallas_call(
        _gmm_kernel,
        grid_spec=grid_spec,
        out_shape=jax.ShapeDtypeStruct((2, n_rows, w), U32),
        compiler_params=_cparams(("arbitrary",)),
        name="gmm",
    )(vb, ve, vlo, vhi, xin, wgu, wd)


def _visit_plan(counts, n_blocks, bm):
    ne = counts.shape[0]
    n_visits = n_blocks + ne
    end = jnp.cumsum(counts)
    start = end - counts
    b0 = start // bm
    nb = jnp.where(counts > 0, (end - 1) // bm - b0 + 1, 0)
    vend = jnp.cumsum(nb)
    vstart = vend - nb
    total = vend[-1]
    v = jnp.arange(n_visits, dtype=jnp.int32)
    valid = v < total
    vc = jnp.minimum(v, total - 1)
    e_v = jnp.minimum(jnp.sum((vend[None, :] <= vc[:, None]).astype(jnp.int32), axis=1), ne - 1)
    onehot = (e_v[:, None] == jnp.arange(ne, dtype=jnp.int32)[None, :]).astype(jnp.int32)
    pick = lambda tab: jnp.sum(onehot * tab[None, :].astype(jnp.int32), axis=1)
    blk = pick(b0) + vc - pick(vstart)
    lo = jnp.clip(pick(start) - blk * bm, 0, bm)
    hi = jnp.clip(pick(end) - blk * bm, 0, bm)
    lo = jnp.where(valid, lo, 0)
    hi = jnp.where(valid, hi, 0)
    i32 = lambda a: a.astype(jnp.int32)
    return i32(blk), i32(e_v), i32(lo), i32(hi)


def _combine_kernel(yg_ref, g_ref, h_ref, x1_ref, mod_ref, wsgu_ref, wsd_ref, lng_ref,
                    lnb_ref, o_ref):
    q = D_MODEL // 4
    g = g_ref[...]
    a = _dot(h_ref[...], wsgu_ref[...])
    act = _silu(a[:, 0:D_EXPERT]) * a[:, D_EXPERT:2 * D_EXPERT]
    shared = _dot(act.astype(BF16), wsd_ref[...])
    f = [shared[:, j * q:(j + 1) * q] for j in range(4)]
    for k in range(TOP_K):
        a0, b0 = _unpack_pair(yg_ref[0, k])
        a1, b1 = _unpack_pair(yg_ref[1, k])
        gk = g[:, k:k + 1]
        f = [f[0] + gk * a0, f[1] + gk * a1, f[2] + gk * b0, f[3] + gk * b1]
    ffn = jnp.concatenate(f, axis=1)
    m = mod_ref[0]
    r = ALPHA * x1_ref[...] + m[5:6, :] * ffn
    o_ref[...] = _ln(r) * lng_ref[...] + lnb_ref[...]


def _combine(yg, gates, h2, x1, mod3, wsgu, wsd, lng, lnb, seq, tm):
    t, d = x1.shape
    row = lambda i: (i, 0)
    const = lambda i: (0, 0)
    return pl.pallas_call(
        _combine_kernel,
        grid=(t // tm,),
        in_specs=[pl.BlockSpec((2, TOP_K, tm, SC_ROW_WORDS), lambda i: (0, 0, i, 0)),
                  pl.BlockSpec((tm, LANES), row),
                  pl.BlockSpec((tm, d), row),
                  pl.BlockSpec((tm, d), row),
                  pl.BlockSpec((1, 6, d), lambda i: ((i * tm) // seq, 0, 0)),
                  pl.BlockSpec((d, 2 * D_EXPERT), const),
                  pl.BlockSpec((D_EXPERT, d), const),
                  pl.BlockSpec((1, d), const),
                  pl.BlockSpec((1, d), const)],
        out_specs=pl.BlockSpec((tm, d), row),
        out_shape=jax.ShapeDtypeStruct((t, d), F32),
        compiler_params=_cparams(("arbitrary",)),
        name="combine",
    )(yg, gates, h2, x1, mod3, wsgu, wsd, lng, lnb)


def _pad_lanes(v, fill=0.0):
    out = jnp.full((1, LANES), fill, F32)
    return out.at[0, :v.shape[0]].set(v.astype(F32))


def kernel(x, c, w_ada, b_ada, w_in, ssd_conv_w, ssd_conv_b, ssd_dt_bias, ssd_A_log, ssd_D,
           ssd_norm_w, w_ssd_out, sc_conv_w, w_sc_out, w_o, ln1_g, ln1_b, router_w,
           router_bias, w_gate, w_up, w_down, sh_gate, sh_up, sh_down, ln2_g, ln2_b):
    bsz, seq, d = x.shape
    t = bsz * seq
    depth = w_in.shape[0]
    tm_proj = min(1024, seq)
    tm_merge = min(256, seq)
    tm_route = min(1024, t)
    tm_comb = min(256, seq)
    bm_gmm = 512
    n_assign = t * TOP_K

    e01 = (np.arange(LANES)[:, None] == (np.arange(SSD_INNER)[None, :] // SSD_HEAD_DIM))
    e01 = jnp.asarray(e01, BF16)
    su = jnp.asarray(np.arange(tm_route)[:, None] < np.arange(tm_route)[None, :], BF16)

    xf = x.reshape(t, d)
    for l in range(depth):
        w_l = w_in[l]
        w_main = jnp.concatenate([w_l[:, :DT_COL0], w_l[:, DT_COL0 + SSD_HEADS:]],
                                 axis=1).astype(BF16)
        w_dt = jnp.pad(w_l[:, DT_COL0:DT_COL0 + SSD_HEADS],
                       ((0, 0), (0, LANES - SSD_HEADS))).astype(BF16)
        wgu = jnp.concatenate([w_gate[l], w_up[l]], axis=-1).astype(BF16)
        wd = w_down[l].astype(BF16)
        wsgu = jnp.concatenate([sh_gate[l], sh_up[l]], axis=-1).astype(BF16)
        wsd = sh_down[l].astype(BF16)
        wr = jnp.pad(router_w[l], ((0, 0), (0, LANES - N_EXPERTS)))
        bias_col = jnp.zeros((LANES, LANES), F32).at[:N_EXPERTS, 0].set(router_bias[l])
        dvec = jnp.repeat(ssd_D[l], SSD_HEAD_DIM).reshape(1, SSD_INNER)

        mod3 = _mod(c, w_ada[l], b_ada[l]).reshape(bsz, 6, d)
        p, dtp = _inproj(xf, mod3, w_main, w_dt, seq, tm_proj, 1024)
        yn = _ssd(p, dtp, ssd_conv_w[l], ssd_conv_b[l].reshape(1, -1),
                  _pad_lanes(ssd_dt_bias[l]), _pad_lanes(ssd_A_log[l]), dvec,
                  ssd_norm_w[l].reshape(1, -1), e01, bsz, seq)
        x1, h2, h2p, logits = _merge(yn, p, xf, mod3, w_ssd_out[l].astype(BF16),
                                     w_sc_out[l].astype(BF16), w_o[l].astype(BF16),
                                     sc_conv_w[l], ln1_g[l].reshape(1, d),
                                     ln1_b[l].reshape(1, d), wr, bsz, seq, tm_merge)
        gates, dest, cnt = _route(logits, bias_col, su, tm_route)
        dest_both = jnp.concatenate([dest, dest + n_assign], axis=1)
        dest_flat = dest.reshape(1, n_assign)
        dest_flat = jnp.concatenate([dest_flat, dest_flat + n_assign], axis=1)
        vb, ve, vlo, vhi = _visit_plan(cnt[:, 0].astype(jnp.int32), n_assign // bm_gmm, bm_gmm)
        xin = _sc_scatter_rows(h2p.reshape(2 * t, SC_ROW_WORDS), dest_both, 2 * n_assign)
        yb = _gmm(xin.reshape(2, n_assign, SC_ROW_WORDS), wgu, wd, vb, ve, vlo, vhi, bm_gmm)
        yg = _sc_gather_rows(yb.reshape(2 * n_assign, SC_ROW_WORDS), dest_flat)
        xf = _combine(yg.reshape(2, TOP_K, t, SC_ROW_WORDS), gates, h2, x1, mod3, wsgu, wsd,
                      ln2_g[l].reshape(1, d), ln2_b[l].reshape(1, d), seq, tm_comb)
    return xf.reshape(bsz, seq, d)
```

```python
import jax
import jax.numpy as jnp
import numpy as np
from jax import lax
from jax.experimental import pallas as pl
from jax.experimental.pallas import tpu as pltpu
from jax.experimental.pallas import tpu_sc as plsc

F32 = jnp.float32
BF16 = jnp.bfloat16
U32 = jnp.uint32

LANES = 128
SUBLANES = 8
VMEM_LIMIT_BYTES = 56 * 1024 * 1024

D_MODEL = 1024
SSD_INNER = 2048
SSD_HEAD_DIM = 64
SSD_HEADS = 32
SSD_GROUPS = 8
SSD_STATE = 128
SSD_CONV = 4
SSD_CHUNK = 128
GROUP_W = SSD_INNER // SSD_GROUPS
HEADS_PER_GROUP = SSD_HEADS // SSD_GROUPS
SC_KERNEL = 3
N_EXPERTS = 64
TOP_K = 8
N_EXPERT_GROUPS = 8
TOPK_EXPERT_GROUPS = 4
EXPERTS_PER_GROUP = N_EXPERTS // N_EXPERT_GROUPS
D_EXPERT = 256
ROUTED_SCALE = 2.5
LN_EPS = 1e-5
RMS_EPS = 1e-5
ALPHA = 2.0 ** 0.25
DT_COL0 = 2048 + 4096
NEG_BIG = -1e30
SC_WINDOW = 128
SC_ROW_WORDS = 256
GMM_ROWS = 256
MERGE_ROWS = 256


def _cparams(sem):
    return pltpu.CompilerParams(dimension_semantics=sem,
                                vmem_limit_bytes=VMEM_LIMIT_BYTES)


def _ln(x):
    mu = jnp.mean(x, axis=-1, keepdims=True)
    xc = x - mu
    var = jnp.mean(xc * xc, axis=-1, keepdims=True)
    return xc * lax.rsqrt(var + LN_EPS)


def _silu(x):
    h = 0.5 * x
    return h + h * jnp.tanh(h)


def _split3(a):
    hi = a.astype(BF16)
    r1 = a - hi.astype(F32)
    mid = r1.astype(BF16)
    lo = (r1 - mid.astype(F32)).astype(BF16)
    return hi, mid, lo


def _dot(a, b):
    return jnp.dot(a, b, preferred_element_type=F32)


def _dot_exact01(a, m01):
    hi, mid, lo = _split3(a)
    return _dot(hi, m01) + _dot(mid, m01) + _dot(lo, m01)


def _dot01_exact(m01, a):
    hi, mid, lo = _split3(a)
    return _dot(m01, hi) + _dot(m01, mid) + _dot(m01, lo)


def _pack_pair(a, b):
    ab = pltpu.bitcast(a.astype(BF16).astype(F32), U32)
    bb = pltpu.bitcast(b.astype(BF16).astype(F32), U32)
    return ab | lax.shift_right_logical(bb, jnp.uint32(16))


def _unpack_pair(w):
    a = pltpu.bitcast(w & jnp.uint32(0xFFFF0000), F32)
    b = pltpu.bitcast(lax.shift_left(w, jnp.uint32(16)), F32)
    return a, b


def _mod_kernel(c_ref, w_ref, b_ref, o_ref):
    a = _silu(c_ref[...])
    o_ref[...] = jnp.dot(a, w_ref[...], precision=lax.Precision.HIGHEST,
                         preferred_element_type=F32) + b_ref[...]


def _mod(c, w, b):
    bsz, d = c.shape
    n = w.shape[1]
    tn = 1024
    return pl.pallas_call(
        _mod_kernel,
        grid=(n // tn,),
        in_specs=[pl.BlockSpec((bsz, d), lambda j: (0, 0)),
                  pl.BlockSpec((d, tn), lambda j: (0, j)),
                  pl.BlockSpec((1, tn), lambda j: (0, j))],
        out_specs=pl.BlockSpec((bsz, tn), lambda j: (0, j)),
        out_shape=jax.ShapeDtypeStruct((bsz, n), F32),
        compiler_params=_cparams(("arbitrary",)),
        name="mod",
    )(c, w, b.reshape(1, n))


def _inproj_kernel(x_ref, mod_ref, w_ref, wdt_ref, p_ref, dt_ref, h_scr):
    j = pl.program_id(1)

    @pl.when(j == 0)
    def _():
        m = mod_ref[0]
        h = _ln(x_ref[...]) * (1.0 + m[1:2, :]) + m[0:1, :]
        hb = h.astype(BF16)
        h_scr[...] = hb
        dt_ref[...] = _dot(hb, wdt_ref[...])

    p_ref[...] = _dot(h_scr[...], w_ref[...]).astype(p_ref.dtype)


def _inproj(x2d, mod3, w_main, w_dt, seq, tm, tn):
    t, d = x2d.shape
    n = w_main.shape[1]
    return pl.pallas_call(
        _inproj_kernel,
        grid=(t // tm, n // tn),
        in_specs=[pl.BlockSpec((tm, d), lambda i, j: (i, 0)),
                  pl.BlockSpec((1, 6, d), lambda i, j: ((i * tm) // seq, 0, 0)),
                  pl.BlockSpec((d, tn), lambda i, j: (0, j)),
                  pl.BlockSpec((d, LANES), lambda i, j: (0, 0))],
        out_specs=[pl.BlockSpec((tm, tn), lambda i, j: (i, j)),
                   pl.BlockSpec((tm, LANES), lambda i, j: (i, 0))],
        out_shape=[jax.ShapeDtypeStruct((t, n), BF16),
                   jax.ShapeDtypeStruct((t, LANES), F32)],
        scratch_shapes=[pltpu.VMEM((tm, d), BF16)],
        compiler_params=_cparams(("arbitrary", "arbitrary")),
        name="inproj",
    )(x2d, mod3, w_main, w_dt)


def _shift_conv(cur_ref, prev_scr, s_ref, w_ref, b_ref):
    cur = cur_ref[...]
    rows = cur.shape[0]
    n_taps = w_ref.shape[0]
    shifted = _dot(s_ref[...], jnp.concatenate([prev_scr[...], cur], axis=0))
    acc = w_ref[n_taps - 1:n_taps, :] * cur.astype(F32) + b_ref[...]
    for k in range(n_taps - 1):
        acc = acc + w_ref[k:k + 1, :] * shifted[k * rows:(k + 1) * rows]
    prev_scr[...] = cur
    return acc


def _shift_matrix(rows, n_taps):
    s = np.zeros(((n_taps - 1) * rows, 2 * rows), np.float32)
    t = np.arange(rows)
    for k in range(n_taps - 1):
        s[k * rows + t, rows + t - (n_taps - 1) + k] = 1.0
    return jnp.asarray(s, BF16)


def _ssd_kernel(xs_ref, bc_ref, z_ref, dtp_ref, cw_xs_ref, cw_bc_ref, cb_xs_ref,
                cb_bc_ref, dtb_ref, alog_ref, dvec_ref, nw_ref, e_ref, s_ref,
                yn_ref, prev_xs, prev_bc, state):
    c = pl.program_id(1)
    L = SSD_CHUNK

    @pl.when(c == 0)
    def _():
        prev_xs[...] = jnp.zeros(prev_xs.shape, BF16)
        prev_bc[...] = jnp.zeros(prev_bc.shape, BF16)
        state[...] = jnp.zeros(state.shape, F32)

    xs = _silu(_shift_conv(xs_ref, prev_xs, s_ref, cw_xs_ref, cb_xs_ref))
    bc = _silu(_shift_conv(bc_ref, prev_bc, s_ref, cw_bc_ref, cb_bc_ref))

    x_dt = dtp_ref[...] + dtb_ref[...]
    dt = jnp.maximum(x_dt, 0.0) + jnp.log1p(jnp.exp(-jnp.abs(x_dt)))
    a_neg = -jnp.exp(alog_ref[...])
    d_a = dt * a_neg
    row = lax.broadcasted_iota(jnp.int32, (L, L), 0)
    col = lax.broadcasted_iota(jnp.int32, (L, L), 1)
    causal = col <= row
    tri = jnp.where(causal, 1.0, 0.0).astype(BF16)
    hi, mid, lo = _split3(d_a)
    a_cs = _dot(tri, hi) + _dot(tri, mid) + _dot(tri, lo)
    a_cs_t = a_cs.T
    a_last = a_cs[L - 1:L, :]
    stack = jnp.concatenate([dt, jnp.exp(a_cs), jnp.exp(a_last - a_cs)], axis=0)
    s_hi = stack.astype(BF16)
    s_mid = (stack - s_hi.astype(F32)).astype(BF16)
    st_e = _dot(jnp.concatenate([s_hi, s_mid], axis=1), e_ref[...])
    dt_e = st_e[0:L]
    od_e = st_e[L:2 * L]
    ds_e = st_e[2 * L:3 * L]
    cd_e = od_e[L - 1:L, :]

    x_f = xs * dt_e
    x_b = x_f.astype(BF16)
    xd_b = (x_f * ds_e).astype(BF16)
    lane_head = lax.broadcasted_iota(jnp.int32, (L, GROUP_W), 1) // SSD_HEAD_DIM

    for g in range(SSD_GROUPS):
        gc = slice(g * GROUP_W, (g + 1) * GROUP_W)
        b_g = bc[:, g * SSD_STATE:(g + 1) * SSD_STATE]
        c_g = bc[:, SSD_GROUPS * SSD_STATE + g * SSD_STATE:
                 SSD_GROUPS * SSD_STATE + (g + 1) * SSD_STATE].astype(BF16)
        b_gt = b_g.T.astype(BF16)
        cb = _dot(c_g, b_gt)
        x_g = x_b[:, gc]
        lhs = []
        rhs = []
        for r in range(HEADS_PER_GROUP):
            h = g * HEADS_PER_GROUP + r
            seg = a_cs[:, h:h + 1] - a_cs_t[h:h + 1, :]
            lmat = jnp.exp(jnp.where(causal, seg, NEG_BIG))
            lhs.append((cb * lmat).astype(BF16))
            rhs.append(jnp.where(lane_head == r, x_g, jnp.zeros_like(x_g)))
        y_diag = _dot(jnp.concatenate(lhs, axis=1), jnp.concatenate(rhs, axis=0))
        st = state[g]
        y_off = _dot(c_g, st.astype(BF16)) * od_e[:, gc]
        state[g] = cd_e[:, gc] * st + _dot(b_gt, xd_b[:, gc])
        y = y_diag + y_off + dvec_ref[:, gc] * xs[:, gc]
        gt = y * _silu(z_ref[:, gc].astype(F32))
        ms = jnp.mean(gt * gt, axis=-1, keepdims=True)
        yn_ref[:, gc] = (gt * lax.rsqrt(ms + RMS_EPS) * nw_ref[:, gc]).astype(yn_ref.dtype)


def _ssd(p, dtp, cw, cb, dtb, alog, dvec, nw, e01, bsz, seq):
    t = p.shape[0]
    L = SSD_CHUNK
    nc = seq // L
    row_map = lambda b, c: (b * nc + c, 0)

    def pcol(k):
        return pl.BlockSpec((L, SSD_INNER), lambda b, c: (b * nc + c, k))

    const = lambda b, c: (0, 0)
    return pl.pallas_call(
        _ssd_kernel,
        grid=(bsz, nc),
        in_specs=[pcol(1), pcol(2), pcol(0),
                  pl.BlockSpec((L, LANES), row_map),
                  pl.BlockSpec((SSD_CONV, SSD_INNER), lambda b, c: (0, 0)),
                  pl.BlockSpec((SSD_CONV, SSD_INNER), lambda b, c: (0, 1)),
                  pl.BlockSpec((1, SSD_INNER), lambda b, c: (0, 0)),
                  pl.BlockSpec((1, SSD_INNER), lambda b, c: (0, 1)),
                  pl.BlockSpec((1, LANES), const),
                  pl.BlockSpec((1, LANES), const),
                  pl.BlockSpec((1, SSD_INNER), const),
                  pl.BlockSpec((1, SSD_INNER), const),
                  pl.BlockSpec((2 * LANES, SSD_INNER), const),
                  pl.BlockSpec(((SSD_CONV - 1) * L, 2 * L), const)],
        out_specs=pl.BlockSpec((L, SSD_INNER), row_map),
        out_shape=jax.ShapeDtypeStruct((t, SSD_INNER), BF16),
        scratch_shapes=[pltpu.VMEM((L, SSD_INNER), BF16),
                        pltpu.VMEM((L, SSD_INNER), BF16),
                        pltpu.VMEM((SSD_GROUPS, SSD_STATE, GROUP_W), F32)],
        compiler_params=_cparams(("arbitrary", "arbitrary")),
        name="ssd",
    )(p, p, p, dtp, cw, cw, cb, cb, dtb, alog, dvec, nw, e01, _shift_matrix(L, SSD_CONV))


def _merge_kernel(yn_ref, scb_ref, scc_ref, sch_ref, ga_ref, gb_ref, x_ref, mod_ref,
                  wssd_ref, wsc_ref, wo_ref, cw_ref, lng_ref, lnb_ref, wr_ref,
                  x1_ref, h2_ref, h2p_ref, lg_ref, ext):
    s = pl.program_id(1)
    tm = x_ref.shape[0]

    @pl.when(s == 0)
    def _():
        ext[0:SUBLANES, :] = jnp.zeros((SUBLANES, D_MODEL), F32)

    ext[SUBLANES:SUBLANES + tm, :] = scc_ref[...].astype(F32) * sch_ref[...].astype(F32)
    m = mod_ref[0]
    for c in range(tm // MERGE_ROWS):
        r0 = c * MERGE_ROWS
        rows = slice(r0, r0 + MERGE_ROWS)
        u = None
        for k in range(SC_KERNEL):
            start = SUBLANES + r0 - (SC_KERNEL - 1) + k
            term = cw_ref[k:k + 1, :] * ext[start:start + MERGE_ROWS, :]
            u = term if u is None else u + term
        y_b = _dot((scb_ref[rows, :].astype(F32) * u).astype(BF16), wsc_ref[...])
        y_a = _dot(yn_ref[rows, :], wssd_ref[...])
        merged = (jax.nn.sigmoid(ga_ref[rows, :].astype(F32)) * y_a
                  + jax.nn.sigmoid(gb_ref[rows, :].astype(F32)) * y_b)
        mix = _dot(merged.astype(BF16), wo_ref[...])
        x1 = _ln(ALPHA * x_ref[rows, :] + m[2:3, :] * mix) * lng_ref[...] + lnb_ref[...]
        x1_ref[rows, :] = x1
        h2 = _ln(x1) * (1.0 + m[4:5, :]) + m[3:4, :]
        h2_hi = h2.astype(BF16)
        h2_ref[rows, :] = h2_hi
        w = _pack_pair(h2[:, 0:D_MODEL // 2], h2[:, D_MODEL // 2:D_MODEL])
        h2p_ref[0, rows, :] = w[:, 0:SC_ROW_WORDS]
        h2p_ref[1, rows, :] = w[:, SC_ROW_WORDS:2 * SC_ROW_WORDS]
        h2_lo = (h2 - h2_hi.astype(F32)).astype(BF16)
        both = _dot(h2_hi, wr_ref[...])
        lg_ref[rows, :] = (both[:, 0:LANES] + both[:, LANES:2 * LANES]
                           + _dot(h2_lo, wr_ref[:, 0:LANES]))
    ext[0:SUBLANES, :] = ext[tm:tm + SUBLANES, :]


def _merge(yn, p, x2d, mod3, wssd, wsc, wo, cw, lng, lnb, wr, bsz, seq, tm):
    t, d = x2d.shape
    ns = seq // tm
    row_map = lambda b, s: (b * ns + s, 0)
    const = lambda b, s: (0, 0)
    col0 = (2048 + 4096) // d

    def pcol(k):
        return pl.BlockSpec((tm, d), lambda b, s: (b * ns + s, col0 + k))

    return pl.pallas_call(
        _merge_kernel,
        grid=(bsz, ns),
        in_specs=[pl.BlockSpec((tm, SSD_INNER), row_map),
                  pcol(0), pcol(1), pcol(2), pcol(3), pcol(4),
                  pl.BlockSpec((tm, d), row_map),
                  pl.BlockSpec((1, 6, d), lambda b, s: (b, 0, 0)),
                  pl.BlockSpec((SSD_INNER, d), const, pipeline_mode=pl.Buffered(1)),
                  pl.BlockSpec((d, d), const, pipeline_mode=pl.Buffered(1)),
                  pl.BlockSpec((d, d), const, pipeline_mode=pl.Buffered(1)),
                  pl.BlockSpec((SC_KERNEL, d), const),
                  pl.BlockSpec((1, d), const),
                  pl.BlockSpec((1, d), const),
                  pl.BlockSpec((d, 2 * LANES), const, pipeline_mode=pl.Buffered(1))],
        out_specs=[pl.BlockSpec((tm, d), row_map),
                   pl.BlockSpec((tm, d), row_map),
                   pl.BlockSpec((2, tm, SC_ROW_WORDS), lambda b, s: (0, b * ns + s, 0)),
                   pl.BlockSpec((tm, LANES), row_map)],
        out_shape=[jax.ShapeDtypeStruct((t, d), F32),
                   jax.ShapeDtypeStruct((t, d), BF16),
                   jax.ShapeDtypeStruct((2, t, SC_ROW_WORDS), U32),
                   jax.ShapeDtypeStruct((t, LANES), F32)],
        scratch_shapes=[pltpu.VMEM((tm + SUBLANES, d), F32)],
        compiler_params=_cparams(("arbitrary", "arbitrary")),
        name="merge",
    )(yn, p, p, p, p, p, x2d, mod3, wssd, wsc, wo, cw, lng, lnb, wr)


def _first_argmax_mask(v, idx, n):
    m = jnp.max(v, axis=0, keepdims=True)
    first = jnp.min(jnp.where(v == m, idx, n), axis=0, keepdims=True)
    return idx == first, m


def _route_kernel(lg_ref, bias_ref, su_ref, g_ref, dest_ref, cnt_ref, counts, carry, start):
    p = pl.program_id(0)
    i = pl.program_id(1)
    tm = lg_ref.shape[0]
    ne, ng, eg = N_EXPERTS, N_EXPERT_GROUPS, EXPERTS_PER_GROUP
    lt = lg_ref[...].T[0:ne, :]
    scores = jax.nn.sigmoid(lt)
    biased = scores + bias_ref[...][0:ne, 0:1]
    grp = biased.reshape(ng, eg, tm)
    idx_e = lax.broadcasted_iota(jnp.int32, (ng, eg, tm), 1)
    m1 = jnp.max(grp, axis=1, keepdims=True)
    first = jnp.min(jnp.where(grp == m1, idx_e, eg), axis=1, keepdims=True)
    m2 = jnp.max(jnp.where(idx_e == first, NEG_BIG, grp), axis=1, keepdims=True)
    gscore = m1 + m2
    idx_g = lax.broadcasted_iota(jnp.int32, (ng, 1, tm), 0)
    gsel = jnp.zeros((ng, 1, tm), F32)
    for _ in range(TOPK_EXPERT_GROUPS):
        hit, _m = _first_argmax_mask(gscore, idx_g, ng)
        gsel = jnp.where(hit, 1.0, gsel)
        gscore = jnp.where(hit, NEG_BIG, gscore)
    emask = jnp.broadcast_to(gsel, (ng, eg, tm)).reshape(ne, tm)
    cand = jnp.where(emask > 0.0, biased, NEG_BIG)
    idx_x = lax.broadcasted_iota(jnp.int32, (ne, tm), 0)
    sel = jnp.zeros((ne, tm), F32)
    picks = []
    for _ in range(TOP_K):
        hit, _m = _first_argmax_mask(cand, idx_x, ne)
        sel = jnp.where(hit, 1.0, sel)
        cand = jnp.where(hit, NEG_BIG, cand)
        picks.append(hit)
    n_e = jnp.broadcast_to(jnp.sum(sel, axis=1, keepdims=True), (ne, LANES))

    @pl.when((p == 0) & (i == 0))
    def _():
        counts[...] = jnp.zeros(counts.shape, F32)

    @pl.when(p == 0)
    def _():
        counts[...] += n_e

    @pl.when((p == 1) & (i == 0))
    def _():
        r = lax.broadcasted_iota(jnp.int32, (ne, ne), 0)
        cc = lax.broadcasted_iota(jnp.int32, (ne, ne), 1)
        below = jnp.where(cc < r, 1.0, 0.0).astype(BF16)
        start[...] = _dot01_exact(below, counts[...])
        carry[...] = jnp.zeros(carry.shape, F32)
        cnt_ref[...] = counts[...]

    @pl.when(p == 1)
    def _():
        before = _dot(sel.astype(BF16), su_ref[...])
        slot = before + carry[:, 0:1] + start[:, 0:1]
        dest_rows = []
        gate_rows = []
        for hit in picks:
            dest_rows.append(jnp.sum(jnp.where(hit, slot, 0.0), axis=0, keepdims=True))
            gate_rows.append(jnp.sum(jnp.where(hit, scores, 0.0), axis=0, keepdims=True))
        dest_ref[...] = jnp.concatenate(dest_rows, axis=0).astype(jnp.int32)
        gate = jnp.concatenate(gate_rows, axis=0)
        gate = gate / jnp.sum(gate, axis=0, keepdims=True) * ROUTED_SCALE
        pad = jnp.zeros((LANES - TOP_K, tm), F32)
        g_ref[...] = jnp.concatenate([gate, pad], axis=0).T
        carry[...] += n_e


def _route(logits, bias_col, su, tm):
    t = logits.shape[0]
    return pl.pallas_call(
        _route_kernel,
        grid=(2, t // tm),
        in_specs=[pl.BlockSpec((tm, LANES), lambda p, i: (i, 0)),
                  pl.BlockSpec((LANES, LANES), lambda p, i: (0, 0)),
                  pl.BlockSpec((tm, tm), lambda p, i: (0, 0))],
        out_specs=[pl.BlockSpec((tm, LANES), lambda p, i: (i * p, 0)),
                   pl.BlockSpec((TOP_K, tm), lambda p, i: (0, i * p)),
                   pl.BlockSpec((N_EXPERTS, LANES), lambda p, i: (0, 0))],
        out_shape=[jax.ShapeDtypeStruct((t, LANES), F32),
                   jax.ShapeDtypeStruct((TOP_K, t), jnp.int32),
                   jax.ShapeDtypeStruct((N_EXPERTS, LANES), F32)],
        scratch_shapes=[pltpu.VMEM((N_EXPERTS, LANES), F32),
                        pltpu.VMEM((N_EXPERTS, LANES), F32),
                        pltpu.VMEM((N_EXPERTS, LANES), F32)],
        compiler_params=_cparams(("arbitrary", "arbitrary")),
        name="route",
    )(logits, bias_col, su)


def _sc_mesh():
    return plsc.VectorSubcoreMesh(core_axis_name="c", subcore_axis_name="s")


def _sc_scatter_rows(rows, idx, n_out):
    n_rows, w = rows.shape
    n_k = idx.shape[0]

    @pl.kernel(out_type=jax.ShapeDtypeStruct((n_out, w), rows.dtype), mesh=_sc_mesh(),
               scratch_types=[])
    def scatter(x_hbm, i_hbm, o_hbm):
        def body(x_vmem, i_vmem):
            for k in range(n_k):
                pltpu.sync_copy(x_vmem, o_hbm.at[i_vmem.at[k]])

        pltpu.emit_pipeline(
            body,
            grid=(n_rows // SC_WINDOW,),
            in_specs=[pl.BlockSpec((SC_WINDOW, w), index_map=lambda i: (i, 0)),
                      pl.BlockSpec((n_k, SC_WINDOW), index_map=lambda i: (0, i))],
            out_specs=[],
            core_axis_name=("c", "s"),
            dimension_semantics=(pltpu.PARALLEL,),
        )(x_hbm, i_hbm)

    return scatter(rows, idx)


def _sc_gather_rows(rows, idx):
    n = idx.shape[1]
    w = rows.shape[1]

    @pl.kernel(out_type=jax.ShapeDtypeStruct((n, w), rows.dtype), mesh=_sc_mesh(),
               scratch_types=[])
    def gather(x_hbm, i_hbm, o_hbm):
        def body(i_vmem, o_vmem):
            pltpu.sync_copy(x_hbm.at[i_vmem.at[0]], o_vmem)

        pltpu.emit_pipeline(
            body,
            grid=(n // SC_WINDOW,),
            in_specs=[pl.BlockSpec((1, SC_WINDOW), index_map=lambda i: (0, i))],
            out_specs=[pl.BlockSpec((SC_WINDOW, w), index_map=lambda i: (i, 0))],
            core_axis_name=("c", "s"),
            dimension_semantics=(pltpu.PARALLEL,),
        )(i_hbm, o_hbm)

    return gather(rows, idx)


def _gmm_kernel(vb_ref, ve_ref, vlo_ref, vhi_ref, x_ref, wgu_ref, wd_ref, o_ref):
    v = pl.program_id(0)
    lo = vlo_ref[v]
    hi = vhi_ref[v]
    bm = x_ref.shape[1]
    half = D_MODEL // 2

    def expert_rows(rows):
        a0, b0 = _unpack_pair(x_ref[0, rows, :])
        a1, b1 = _unpack_pair(x_ref[1, rows, :])
        x = jnp.concatenate([a0.astype(BF16), a1.astype(BF16),
                             b0.astype(BF16), b1.astype(BF16)], axis=1)
        a = _dot(x, wgu_ref[0])
        act = _silu(a[:, 0:D_EXPERT]) * a[:, D_EXPERT:2 * D_EXPERT]
        y = _dot(act.astype(BF16), wd_ref[0])
        return _pack_pair(y[:, 0:half], y[:, half:D_MODEL])

    def write_rows(rows):
        w = expert_rows(rows)
        o_ref[0, rows, :] = w[:, 0:SC_ROW_WORDS]
        o_ref[1, rows, :] = w[:, SC_ROW_WORDS:half]

    @pl.when((lo == 0) & (hi == bm))
    def _():
        for c in range(bm // GMM_ROWS):
            write_rows(slice(c * GMM_ROWS, (c + 1) * GMM_ROWS))

    @pl.when((lo == 0) & (hi > 0) & (hi < bm))
    def _():
        for c in range(bm // GMM_ROWS):
            @pl.when(hi > c * GMM_ROWS)
            def _():
                write_rows(slice(c * GMM_ROWS, (c + 1) * GMM_ROWS))

    @pl.when((hi > lo) & (lo > 0))
    def _():
        for c in range(bm // GMM_ROWS):
            rows = slice(c * GMM_ROWS, (c + 1) * GMM_ROWS)

            @pl.when((hi > c * GMM_ROWS) & (lo < (c + 1) * GMM_ROWS))
            def _():
                w = expert_rows(rows)
                r = lax.broadcasted_iota(jnp.int32, (GMM_ROWS, SC_ROW_WORDS), 0) + c * GMM_ROWS
                mine = (r >= lo) & (r < hi)
                o_ref[0, rows, :] = jnp.where(mine, w[:, 0:SC_ROW_WORDS], o_ref[0, rows, :])
                o_ref[1, rows, :] = jnp.where(mine, w[:, SC_ROW_WORDS:half], o_ref[1, rows, :])


def _gmm(xin, wgu, wd, vb, ve, vlo, vhi, bm):
    _, n_rows, w = xin.shape
    d = D_MODEL
    n_visits = vb.shape[0]
    grid_spec = pltpu.PrefetchScalarGridSpec(
        num_scalar_prefetch=4,
        grid=(n_visits,),
        in_specs=[pl.BlockSpec((2, bm, w), lambda v, vb, ve, vlo, vhi: (0, vb[v], 0)),
                  pl.BlockSpec((1, d, 2 * D_EXPERT), lambda v, vb, ve, vlo, vhi: (ve[v], 0, 0)),
                  pl.BlockSpec((1, D_EXPERT, d), lambda v, vb, ve, vlo, vhi: (ve[v], 0, 0))],
        out_specs=pl.BlockSpec((2, bm, w), lambda v, vb, ve, vlo, vhi: (0, vb[v], 0)),
    )
    return pl.pallas_call(
        _gmm_kernel,
        grid_spec=grid_spec,
        out_shape=jax.ShapeDtypeStruct((2, n_rows, w), U32),
        compiler_params=_cparams(("arbitrary",)),
        name="gmm",
    )(vb, ve, vlo, vhi, xin, wgu, wd)


def _visit_plan(counts, n_blocks, bm):
    ne = counts.shape[0]
    n_visits = n_blocks + ne
    end = jnp.cumsum(counts)
    start = end - counts
    b0 = start // bm
    nb = jnp.where(counts > 0, (end - 1) // bm - b0 + 1, 0)
    vend = jnp.cumsum(nb)
    vstart = vend - nb
    total = vend[-1]
    v = jnp.arange(n_visits, dtype=jnp.int32)
    valid = v < total
    vc = jnp.minimum(v, total - 1)
    e_v = jnp.minimum(jnp.sum((vend[None, :] <= vc[:, None]).astype(jnp.int32), axis=1), ne - 1)
    onehot = (e_v[:, None] == jnp.arange(ne, dtype=jnp.int32)[None, :]).astype(jnp.int32)
    pick = lambda tab: jnp.sum(onehot * tab[None, :].astype(jnp.int32), axis=1)
    blk = pick(b0) + vc - pick(vstart)
    lo = jnp.clip(pick(start) - blk * bm, 0, bm)
    hi = jnp.clip(pick(end) - blk * bm, 0, bm)
    lo = jnp.where(valid, lo, 0)
    hi = jnp.where(valid, hi, 0)
    i32 = lambda a: a.astype(jnp.int32)
    return i32(blk), i32(e_v), i32(lo), i32(hi)


def _combine_kernel(yg_ref, g_ref, h_ref, x1_ref, mod_ref, wsgu_ref, wsd_ref, lng_ref,
                    lnb_ref, o_ref):
    q = D_MODEL // 4
    g = g_ref[...]
    a = _dot(h_ref[...], wsgu_ref[...])
    act = _silu(a[:, 0:D_EXPERT]) * a[:, D_EXPERT:2 * D_EXPERT]
    shared = _dot(act.astype(BF16), wsd_ref[...])
    f = [shared[:, j * q:(j + 1) * q] for j in range(4)]
    for k in range(TOP_K):
        a0, b0 = _unpack_pair(yg_ref[0, k])
        a1, b1 = _unpack_pair(yg_ref[1, k])
        gk = g[:, k:k + 1]
        f = [f[0] + gk * a0, f[1] + gk * a1, f[2] + gk * b0, f[3] + gk * b1]
    ffn = jnp.concatenate(f, axis=1)
    m = mod_ref[0]
    r = ALPHA * x1_ref[...] + m[5:6, :] * ffn
    o_ref[...] = _ln(r) * lng_ref[...] + lnb_ref[...]


def _combine(yg, gates, h2, x1, mod3, wsgu, wsd, lng, lnb, seq, tm):
    t, d = x1.shape
    row = lambda i: (i, 0)
    const = lambda i: (0, 0)
    return pl.pallas_call(
        _combine_kernel,
        grid=(t // tm,),
        in_specs=[pl.BlockSpec((2, TOP_K, tm, SC_ROW_WORDS), lambda i: (0, 0, i, 0)),
                  pl.BlockSpec((tm, LANES), row),
                  pl.BlockSpec((tm, d), row),
                  pl.BlockSpec((tm, d), row),
                  pl.BlockSpec((1, 6, d), lambda i: ((i * tm) // seq, 0, 0)),
                  pl.BlockSpec((d, 2 * D_EXPERT), const),
                  pl.BlockSpec((D_EXPERT, d), const),
                  pl.BlockSpec((1, d), const),
                  pl.BlockSpec((1, d), const)],
        out_specs=pl.BlockSpec((tm, d), row),
        out_shape=jax.ShapeDtypeStruct((t, d), F32),
        compiler_params=_cparams(("arbitrary",)),
        name="combine",
    )(yg, gates, h2, x1, mod3, wsgu, wsd, lng, lnb)


def _pad_lanes(v, fill=0.0):
    out = jnp.full((1, LANES), fill, F32)
    return out.at[0, :v.shape[0]].set(v.astype(F32))


def kernel(x, c, w_ada, b_ada, w_in, ssd_conv_w, ssd_conv_b, ssd_dt_bias, ssd_A_log, ssd_D,
           ssd_norm_w, w_ssd_out, sc_conv_w, w_sc_out, w_o, ln1_g, ln1_b, router_w,
           router_bias, w_gate, w_up, w_down, sh_gate, sh_up, sh_down, ln2_g, ln2_b):
    bsz, seq, d = x.shape
    t = bsz * seq
    depth = w_in.shape[0]
    tm_proj = min(1024, seq)
    tm_merge = min(512, seq)
    tm_route = min(1024, t)
    tm_comb = min(512, seq)
    bm_gmm = 512
    n_assign = t * TOP_K

    e01 = (np.arange(LANES)[:, None] == (np.arange(SSD_INNER)[None, :] // SSD_HEAD_DIM))
    e01 = jnp.asarray(np.concatenate([e01, e01], axis=0), BF16)
    su = jnp.asarray(np.arange(tm_route)[:, None] < np.arange(tm_route)[None, :], BF16)

    xf = x.reshape(t, d)
    for l in range(depth):
        w_l = w_in[l]
        w_main = jnp.concatenate([w_l[:, :DT_COL0], w_l[:, DT_COL0 + SSD_HEADS:]],
                                 axis=1).astype(BF16)
        w_dt = jnp.pad(w_l[:, DT_COL0:DT_COL0 + SSD_HEADS],
                       ((0, 0), (0, LANES - SSD_HEADS))).astype(BF16)
        wgu = jnp.concatenate([w_gate[l], w_up[l]], axis=-1).astype(BF16)
        wd = w_down[l].astype(BF16)
        wsgu = jnp.concatenate([sh_gate[l], sh_up[l]], axis=-1).astype(BF16)
        wsd = sh_down[l].astype(BF16)
        wr = jnp.pad(router_w[l], ((0, 0), (0, LANES - N_EXPERTS)))
        wr_hi = wr.astype(BF16)
        wr = jnp.concatenate([wr_hi, (wr - wr_hi.astype(F32)).astype(BF16)], axis=1)
        bias_col = jnp.zeros((LANES, LANES), F32).at[:N_EXPERTS, 0].set(router_bias[l])
        dvec = jnp.repeat(ssd_D[l], SSD_HEAD_DIM).reshape(1, SSD_INNER)

        mod3 = _mod(c, w_ada[l], b_ada[l]).reshape(bsz, 6, d)
        p, dtp = _inproj(xf, mod3, w_main, w_dt, seq, tm_proj, w_main.shape[1] // 4)
        yn = _ssd(p, dtp, ssd_conv_w[l], ssd_conv_b[l].reshape(1, -1),
                  _pad_lanes(ssd_dt_bias[l]), _pad_lanes(ssd_A_log[l]), dvec,
                  ssd_norm_w[l].reshape(1, -1), e01, bsz, seq)
        x1, h2, h2p, logits = _merge(yn, p, xf, mod3, w_ssd_out[l].astype(BF16),
                                     w_sc_out[l].astype(BF16), w_o[l].astype(BF16),
                                     sc_conv_w[l], ln1_g[l].reshape(1, d),
                                     ln1_b[l].reshape(1, d), wr, bsz, seq, tm_merge)
        gates, dest, cnt = _route(logits, bias_col, su, tm_route)
        dest_both = jnp.concatenate([dest, dest + n_assign], axis=1)
        dest_flat = dest.reshape(1, n_assign)
        dest_flat = jnp.concatenate([dest_flat, dest_flat + n_assign], axis=1)
        vb, ve, vlo, vhi = _visit_plan(cnt[:, 0].astype(jnp.int32), n_assign // bm_gmm, bm_gmm)
        xin = _sc_scatter_rows(h2p.reshape(2 * t, SC_ROW_WORDS), dest_both, 2 * n_assign)
        yb = _gmm(xin.reshape(2, n_assign, SC_ROW_WORDS), wgu, wd, vb, ve, vlo, vhi, bm_gmm)
        yg = _sc_gather_rows(yb.reshape(2 * n_assign, SC_ROW_WORDS), dest_flat)
        xf = _combine(yg.reshape(2, TOP_K, t, SC_ROW_WORDS), gates, h2, x1, mod3, wsgu, wsd,
                      ln2_g[l].reshape(1, d), ln2_b[l].reshape(1, d), seq, tm_comb)
    return xf.reshape(bsz, seq, d)
```

```python
import jax
import jax.numpy as jnp
import numpy as np
from jax import lax
from jax.experimental import pallas as pl
from jax.experimental.pallas import tpu as pltpu
from jax.experimental.pallas import tpu_sc as plsc

F32 = jnp.float32
BF16 = jnp.bfloat16
U32 = jnp.uint32

LANES = 128
SUBLANES = 8
VMEM_LIMIT_BYTES = 56 * 1024 * 1024

D_MODEL = 1024
SSD_INNER = 2048
SSD_HEAD_DIM = 64
SSD_HEADS = 32
SSD_GROUPS = 8
SSD_STATE = 128
SSD_CONV = 4
SSD_CHUNK = 128
GROUP_W = SSD_INNER // SSD_GROUPS
HEADS_PER_GROUP = SSD_HEADS // SSD_GROUPS
SC_KERNEL = 3
N_EXPERTS = 64
TOP_K = 8
N_EXPERT_GROUPS = 8
TOPK_EXPERT_GROUPS = 4
EXPERTS_PER_GROUP = N_EXPERTS // N_EXPERT_GROUPS
D_EXPERT = 256
ROUTED_SCALE = 2.5
LN_EPS = 1e-5
RMS_EPS = 1e-5
ALPHA = 2.0 ** 0.25
DT_COL0 = 2048 + 4096
NEG_BIG = -1e30
SC_WINDOW = 128
SC_ROW_WORDS = 256
GMM_ROWS = 256
MERGE_ROWS = 256


def _cparams(sem):
    return pltpu.CompilerParams(dimension_semantics=sem,
                                vmem_limit_bytes=VMEM_LIMIT_BYTES)


def _ln(x):
    mu = jnp.mean(x, axis=-1, keepdims=True)
    xc = x - mu
    var = jnp.mean(xc * xc, axis=-1, keepdims=True)
    return xc * lax.rsqrt(var + LN_EPS)


def _silu(x):
    h = 0.5 * x
    return h + h * jnp.tanh(h)


def _split3(a):
    hi = a.astype(BF16)
    r1 = a - hi.astype(F32)
    mid = r1.astype(BF16)
    lo = (r1 - mid.astype(F32)).astype(BF16)
    return hi, mid, lo


def _dot(a, b):
    return jnp.dot(a, b, preferred_element_type=F32)


def _dot_exact01(a, m01):
    hi, mid, lo = _split3(a)
    return _dot(hi, m01) + _dot(mid, m01) + _dot(lo, m01)


def _dot01_exact(m01, a):
    hi, mid, lo = _split3(a)
    return _dot(m01, hi) + _dot(m01, mid) + _dot(m01, lo)


def _pack_pair(a, b):
    ab = pltpu.bitcast(a.astype(BF16).astype(F32), U32)
    bb = pltpu.bitcast(b.astype(BF16).astype(F32), U32)
    return ab | lax.shift_right_logical(bb, jnp.uint32(16))


def _unpack_pair(w):
    a = pltpu.bitcast(w & jnp.uint32(0xFFFF0000), F32)
    b = pltpu.bitcast(lax.shift_left(w, jnp.uint32(16)), F32)
    return a, b


def _mod_kernel(c_ref, w_ref, b_ref, o_ref):
    a = _silu(c_ref[...])
    o_ref[...] = jnp.dot(a, w_ref[...], precision=lax.Precision.HIGHEST,
                         preferred_element_type=F32) + b_ref[...]


def _mod(c, w, b):
    bsz, d = c.shape
    n = w.shape[1]
    tn = 1024
    return pl.pallas_call(
        _mod_kernel,
        grid=(n // tn,),
        in_specs=[pl.BlockSpec((bsz, d), lambda j: (0, 0)),
                  pl.BlockSpec((d, tn), lambda j: (0, j)),
                  pl.BlockSpec((1, tn), lambda j: (0, j))],
        out_specs=pl.BlockSpec((bsz, tn), lambda j: (0, j)),
        out_shape=jax.ShapeDtypeStruct((bsz, n), F32),
        compiler_params=_cparams(("arbitrary",)),
        name="mod",
    )(c, w, b.reshape(1, n))


def _inproj_kernel(x_ref, mod_ref, w_ref, wdt_ref, p_ref, dt_ref, h_scr):
    j = pl.program_id(1)

    @pl.when(j == 0)
    def _():
        m = mod_ref[0]
        h = _ln(x_ref[...]) * (1.0 + m[1:2, :]) + m[0:1, :]
        hb = h.astype(BF16)
        h_scr[...] = hb
        dt_ref[...] = _dot(hb, wdt_ref[...])

    p_ref[...] = _dot(h_scr[...], w_ref[...]).astype(p_ref.dtype)


def _inproj(x2d, mod3, w_main, w_dt, seq, tm, tn):
    t, d = x2d.shape
    n = w_main.shape[1]
    return pl.pallas_call(
        _inproj_kernel,
        grid=(t // tm, n // tn),
        in_specs=[pl.BlockSpec((tm, d), lambda i, j: (i, 0)),
                  pl.BlockSpec((1, 6, d), lambda i, j: ((i * tm) // seq, 0, 0)),
                  pl.BlockSpec((d, tn), lambda i, j: (0, j)),
                  pl.BlockSpec((d, LANES), lambda i, j: (0, 0))],
        out_specs=[pl.BlockSpec((tm, tn), lambda i, j: (i, j)),
                   pl.BlockSpec((tm, LANES), lambda i, j: (i, 0))],
        out_shape=[jax.ShapeDtypeStruct((t, n), BF16),
                   jax.ShapeDtypeStruct((t, LANES), F32)],
        scratch_shapes=[pltpu.VMEM((tm, d), BF16)],
        compiler_params=_cparams(("arbitrary", "arbitrary")),
        name="inproj",
    )(x2d, mod3, w_main, w_dt)


def _shift_conv(cur, prev_scr, s_ref, w_ref, b_ref):
    rows = cur.shape[0]
    n_taps = w_ref.shape[0]
    shifted = _dot(s_ref[...], jnp.concatenate([prev_scr[...], cur], axis=0))
    acc = w_ref[n_taps - 1:n_taps, :] * cur.astype(F32) + b_ref[...]
    for k in range(n_taps - 1):
        acc = acc + w_ref[k:k + 1, :] * shifted[k * rows:(k + 1) * rows]
    prev_scr[...] = cur
    return acc


def _shift_matrix(rows, n_taps):
    s = np.zeros(((n_taps - 1) * rows, 2 * rows), np.float32)
    t = np.arange(rows)
    for k in range(n_taps - 1):
        s[k * rows + t, rows + t - (n_taps - 1) + k] = 1.0
    return jnp.asarray(s, BF16)


def _ssd_kernel(xs_ref, bc_ref, z_ref, dtp_ref, cw_xs_ref, cw_bc_ref, cb_xs_ref,
                cb_bc_ref, dtb_ref, alog_ref, dvec_ref, nw_ref, e_ref, s_ref,
                yn_ref, prev_xs, prev_bc, state):
    @pl.when(pl.program_id(1) == 0)
    def _():
        prev_xs[...] = jnp.zeros(prev_xs.shape, BF16)
        prev_bc[...] = jnp.zeros(prev_bc.shape, BF16)
        state[...] = jnp.zeros(state.shape, F32)

    for cc in range(xs_ref.shape[0] // SSD_CHUNK):
        _ssd_chunk(slice(cc * SSD_CHUNK, (cc + 1) * SSD_CHUNK), xs_ref, bc_ref, z_ref, dtp_ref,
                   cw_xs_ref, cw_bc_ref, cb_xs_ref, cb_bc_ref, dtb_ref, alog_ref, dvec_ref,
                   nw_ref, e_ref, s_ref, yn_ref, prev_xs, prev_bc, state)


def _ssd_chunk(rows, xs_ref, bc_ref, z_ref, dtp_ref, cw_xs_ref, cw_bc_ref, cb_xs_ref,
               cb_bc_ref, dtb_ref, alog_ref, dvec_ref, nw_ref, e_ref, s_ref,
               yn_ref, prev_xs, prev_bc, state):
    L = SSD_CHUNK
    xs = _silu(_shift_conv(xs_ref[rows, :], prev_xs, s_ref, cw_xs_ref, cb_xs_ref))
    bc = _silu(_shift_conv(bc_ref[rows, :], prev_bc, s_ref, cw_bc_ref, cb_bc_ref))

    x_dt = dtp_ref[rows, :] + dtb_ref[...]
    dt = jnp.maximum(x_dt, 0.0) + jnp.log1p(jnp.exp(-jnp.abs(x_dt)))
    a_neg = -jnp.exp(alog_ref[...])
    d_a = dt * a_neg
    row = lax.broadcasted_iota(jnp.int32, (L, L), 0)
    col = lax.broadcasted_iota(jnp.int32, (L, L), 1)
    causal = col <= row
    tri = jnp.where(causal, 1.0, 0.0).astype(BF16)
    hi, mid, lo = _split3(d_a)
    a_cs = _dot(tri, hi) + _dot(tri, mid) + _dot(tri, lo)
    a_cs_t = a_cs.T
    a_last = a_cs[L - 1:L, :]
    stack = jnp.concatenate([dt, jnp.exp(a_cs), jnp.exp(a_last - a_cs)], axis=0)
    s_hi = stack.astype(BF16)
    s_mid = (stack - s_hi.astype(F32)).astype(BF16)
    st_e = _dot(jnp.concatenate([s_hi, s_mid], axis=1), e_ref[...])
    dt_e = st_e[0:L]
    od_e = st_e[L:2 * L]
    ds_e = st_e[2 * L:3 * L]
    cd_e = od_e[L - 1:L, :]

    x_f = xs * dt_e
    x_b = x_f.astype(BF16)
    xd_b = (x_f * ds_e).astype(BF16)
    lane_head = lax.broadcasted_iota(jnp.int32, (L, GROUP_W), 1) // SSD_HEAD_DIM

    for g in range(SSD_GROUPS):
        gc = slice(g * GROUP_W, (g + 1) * GROUP_W)
        b_g = bc[:, g * SSD_STATE:(g + 1) * SSD_STATE]
        c_g = bc[:, SSD_GROUPS * SSD_STATE + g * SSD_STATE:
                 SSD_GROUPS * SSD_STATE + (g + 1) * SSD_STATE].astype(BF16)
        b_gt = b_g.T.astype(BF16)
        cb = _dot(c_g, b_gt)
        x_g = x_b[:, gc]
        lhs = []
        rhs = []
        for r in range(HEADS_PER_GROUP):
            h = g * HEADS_PER_GROUP + r
            seg = a_cs[:, h:h + 1] - a_cs_t[h:h + 1, :]
            lmat = jnp.exp(jnp.where(causal, seg, NEG_BIG))
            lhs.append((cb * lmat).astype(BF16))
            rhs.append(jnp.where(lane_head == r, x_g, jnp.zeros_like(x_g)))
        y_diag = _dot(jnp.concatenate(lhs, axis=1), jnp.concatenate(rhs, axis=0))
        st = state[g]
        y_off = _dot(c_g, st.astype(BF16)) * od_e[:, gc]
        state[g] = cd_e[:, gc] * st + _dot(b_gt, xd_b[:, gc])
        y = y_diag + y_off + dvec_ref[:, gc] * xs[:, gc]
        gt = y * _silu(z_ref[rows, gc].astype(F32))
        ms = jnp.mean(gt * gt, axis=-1, keepdims=True)
        yn_ref[rows, gc] = (gt * lax.rsqrt(ms + RMS_EPS) * nw_ref[:, gc]).astype(yn_ref.dtype)


def _ssd(p, dtp, cw, cb, dtb, alog, dvec, nw, e01, bsz, seq, rows):
    t = p.shape[0]
    L = SSD_CHUNK
    nc = seq // rows
    row_map = lambda b, c: (b * nc + c, 0)

    def pcol(k):
        return pl.BlockSpec((rows, SSD_INNER), lambda b, c: (b * nc + c, k))

    const = lambda b, c: (0, 0)
    return pl.pallas_call(
        _ssd_kernel,
        grid=(bsz, nc),
        in_specs=[pcol(1), pcol(2), pcol(0),
                  pl.BlockSpec((rows, LANES), row_map),
                  pl.BlockSpec((SSD_CONV, SSD_INNER), lambda b, c: (0, 0)),
                  pl.BlockSpec((SSD_CONV, SSD_INNER), lambda b, c: (0, 1)),
                  pl.BlockSpec((1, SSD_INNER), lambda b, c: (0, 0)),
                  pl.BlockSpec((1, SSD_INNER), lambda b, c: (0, 1)),
                  pl.BlockSpec((1, LANES), const),
                  pl.BlockSpec((1, LANES), const),
                  pl.BlockSpec((1, SSD_INNER), const),
                  pl.BlockSpec((1, SSD_INNER), const),
                  pl.BlockSpec((2 * LANES, SSD_INNER), const),
                  pl.BlockSpec(((SSD_CONV - 1) * L, 2 * L), const)],
        out_specs=pl.BlockSpec((rows, SSD_INNER), row_map),
        out_shape=jax.ShapeDtypeStruct((t, SSD_INNER), BF16),
        scratch_shapes=[pltpu.VMEM((L, SSD_INNER), BF16),
                        pltpu.VMEM((L, SSD_INNER), BF16),
                        pltpu.VMEM((SSD_GROUPS, SSD_STATE, GROUP_W), F32)],
        compiler_params=_cparams(("arbitrary", "arbitrary")),
        name="ssd",
    )(p, p, p, dtp, cw, cw, cb, cb, dtb, alog, dvec, nw, e01, _shift_matrix(L, SSD_CONV))


def _merge_kernel(yn_ref, scb_ref, scc_ref, sch_ref, ga_ref, gb_ref, x_ref, mod_ref,
                  wssd_ref, wsc_ref, wo_ref, cw_ref, lng_ref, lnb_ref, wr_ref,
                  x1_ref, h2_ref, h2p_ref, lg_ref, ext):
    s = pl.program_id(1)
    tm = x_ref.shape[0]

    @pl.when(s == 0)
    def _():
        ext[0:SUBLANES, :] = jnp.zeros((SUBLANES, D_MODEL), F32)

    ext[SUBLANES:SUBLANES + tm, :] = scc_ref[...].astype(F32) * sch_ref[...].astype(F32)
    m = mod_ref[0]
    for c in range(tm // MERGE_ROWS):
        r0 = c * MERGE_ROWS
        rows = slice(r0, r0 + MERGE_ROWS)
        u = None
        for k in range(SC_KERNEL):
            start = SUBLANES + r0 - (SC_KERNEL - 1) + k
            term = cw_ref[k:k + 1, :] * ext[start:start + MERGE_ROWS, :]
            u = term if u is None else u + term
        y_b = _dot((scb_ref[rows, :].astype(F32) * u).astype(BF16), wsc_ref[...])
        y_a = _dot(yn_ref[rows, :], wssd_ref[...])
        merged = (jax.nn.sigmoid(ga_ref[rows, :].astype(F32)) * y_a
                  + jax.nn.sigmoid(gb_ref[rows, :].astype(F32)) * y_b)
        mix = _dot(merged.astype(BF16), wo_ref[...])
        x1 = _ln(ALPHA * x_ref[rows, :] + m[2:3, :] * mix) * lng_ref[...] + lnb_ref[...]
        x1_ref[rows, :] = x1
        h2 = _ln(x1) * (1.0 + m[4:5, :]) + m[3:4, :]
        h2_hi = h2.astype(BF16)
        h2_ref[rows, :] = h2_hi
        w = _pack_pair(h2[:, 0:D_MODEL // 2], h2[:, D_MODEL // 2:D_MODEL])
        h2p_ref[0, rows, :] = w[:, 0:SC_ROW_WORDS]
        h2p_ref[1, rows, :] = w[:, SC_ROW_WORDS:2 * SC_ROW_WORDS]
        h2_lo = (h2 - h2_hi.astype(F32)).astype(BF16)
        both = _dot(h2_hi, wr_ref[...])
        lg_ref[rows, :] = (both[:, 0:LANES] + both[:, LANES:2 * LANES]
                           + _dot(h2_lo, wr_ref[:, 0:LANES]))
    ext[0:SUBLANES, :] = ext[tm:tm + SUBLANES, :]


def _merge(yn, p, x2d, mod3, wssd, wsc, wo, cw, lng, lnb, wr, bsz, seq, tm):
    t, d = x2d.shape
    ns = seq // tm
    row_map = lambda b, s: (b * ns + s, 0)
    const = lambda b, s: (0, 0)
    col0 = (2048 + 4096) // d

    def pcol(k):
        return pl.BlockSpec((tm, d), lambda b, s: (b * ns + s, col0 + k))

    return pl.pallas_call(
        _merge_kernel,
        grid=(bsz, ns),
        in_specs=[pl.BlockSpec((tm, SSD_INNER), row_map),
                  pcol(0), pcol(1), pcol(2), pcol(3), pcol(4),
                  pl.BlockSpec((tm, d), row_map),
                  pl.BlockSpec((1, 6, d), lambda b, s: (b, 0, 0)),
                  pl.BlockSpec((SSD_INNER, d), const, pipeline_mode=pl.Buffered(1)),
                  pl.BlockSpec((d, d), const, pipeline_mode=pl.Buffered(1)),
                  pl.BlockSpec((d, d), const, pipeline_mode=pl.Buffered(1)),
                  pl.BlockSpec((SC_KERNEL, d), const),
                  pl.BlockSpec((1, d), const),
                  pl.BlockSpec((1, d), const),
                  pl.BlockSpec((d, 2 * LANES), const, pipeline_mode=pl.Buffered(1))],
        out_specs=[pl.BlockSpec((tm, d), row_map),
                   pl.BlockSpec((tm, d), row_map),
                   pl.BlockSpec((2, tm, SC_ROW_WORDS), lambda b, s: (0, b * ns + s, 0)),
                   pl.BlockSpec((tm, LANES), row_map)],
        out_shape=[jax.ShapeDtypeStruct((t, d), F32),
                   jax.ShapeDtypeStruct((t, d), BF16),
                   jax.ShapeDtypeStruct((2, t, SC_ROW_WORDS), U32),
                   jax.ShapeDtypeStruct((t, LANES), F32)],
        scratch_shapes=[pltpu.VMEM((tm + SUBLANES, d), F32)],
        compiler_params=_cparams(("arbitrary", "arbitrary")),
        name="merge",
    )(yn, p, p, p, p, p, x2d, mod3, wssd, wsc, wo, cw, lng, lnb, wr)


def _first_argmax_mask(v, idx, n):
    m = jnp.max(v, axis=0, keepdims=True)
    first = jnp.min(jnp.where(v == m, idx, n), axis=0, keepdims=True)
    return idx == first, m


def _route_kernel(lg_ref, bias_ref, su_ref, g_ref, dest_ref, cnt_ref, counts, carry, start):
    p = pl.program_id(0)
    i = pl.program_id(1)
    tm = lg_ref.shape[0]
    ne, ng, eg = N_EXPERTS, N_EXPERT_GROUPS, EXPERTS_PER_GROUP
    lt = lg_ref[...].T[0:ne, :]
    scores = jax.nn.sigmoid(lt)
    biased = scores + bias_ref[...][0:ne, 0:1]
    grp = biased.reshape(ng, eg, tm)
    idx_e = lax.broadcasted_iota(jnp.int32, (ng, eg, tm), 1)
    m1 = jnp.max(grp, axis=1, keepdims=True)
    first = jnp.min(jnp.where(grp == m1, idx_e, eg), axis=1, keepdims=True)
    m2 = jnp.max(jnp.where(idx_e == first, NEG_BIG, grp), axis=1, keepdims=True)
    gscore = m1 + m2
    idx_g = lax.broadcasted_iota(jnp.int32, (ng, 1, tm), 0)
    gsel = jnp.zeros((ng, 1, tm), F32)
    for _ in range(TOPK_EXPERT_GROUPS):
        hit, _m = _first_argmax_mask(gscore, idx_g, ng)
        gsel = jnp.where(hit, 1.0, gsel)
        gscore = jnp.where(hit, NEG_BIG, gscore)
    emask = jnp.broadcast_to(gsel, (ng, eg, tm)).reshape(ne, tm)
    cand = jnp.where(emask > 0.0, biased, NEG_BIG)
    idx_x = lax.broadcasted_iota(jnp.int32, (ne, tm), 0)
    sel = jnp.zeros((ne, tm), F32)
    picks = []
    for _ in range(TOP_K):
        hit, _m = _first_argmax_mask(cand, idx_x, ne)
        sel = jnp.where(hit, 1.0, sel)
        cand = jnp.where(hit, NEG_BIG, cand)
        picks.append(hit)
    n_e = jnp.broadcast_to(jnp.sum(sel, axis=1, keepdims=True), (ne, LANES))

    @pl.when((p == 0) & (i == 0))
    def _():
        counts[...] = jnp.zeros(counts.shape, F32)

    @pl.when(p == 0)
    def _():
        counts[...] += n_e

    @pl.when((p == 1) & (i == 0))
    def _():
        r = lax.broadcasted_iota(jnp.int32, (ne, ne), 0)
        cc = lax.broadcasted_iota(jnp.int32, (ne, ne), 1)
        below = jnp.where(cc < r, 1.0, 0.0).astype(BF16)
        start[...] = _dot01_exact(below, counts[...])
        carry[...] = jnp.zeros(carry.shape, F32)
        cnt_ref[...] = counts[...]

    @pl.when(p == 1)
    def _():
        before = _dot(sel.astype(BF16), su_ref[...])
        slot = before + carry[:, 0:1] + start[:, 0:1]
        dest_rows = []
        gate_rows = []
        for hit in picks:
            dest_rows.append(jnp.sum(jnp.where(hit, slot, 0.0), axis=0, keepdims=True))
            gate_rows.append(jnp.sum(jnp.where(hit, scores, 0.0), axis=0, keepdims=True))
        dest_ref[...] = jnp.concatenate(dest_rows, axis=0).astype(jnp.int32)
        gate = jnp.concatenate(gate_rows, axis=0)
        gate = gate / jnp.sum(gate, axis=0, keepdims=True) * ROUTED_SCALE
        pad = jnp.zeros((LANES - TOP_K, tm), F32)
        g_ref[...] = jnp.concatenate([gate, pad], axis=0).T
        carry[...] += n_e


def _route(logits, bias_col, su, tm):
    t = logits.shape[0]
    return pl.pallas_call(
        _route_kernel,
        grid=(2, t // tm),
        in_specs=[pl.BlockSpec((tm, LANES), lambda p, i: (i, 0)),
                  pl.BlockSpec((LANES, LANES), lambda p, i: (0, 0)),
                  pl.BlockSpec((tm, tm), lambda p, i: (0, 0))],
        out_specs=[pl.BlockSpec((tm, LANES), lambda p, i: (i * p, 0)),
                   pl.BlockSpec((TOP_K, tm), lambda p, i: (0, i * p)),
                   pl.BlockSpec((N_EXPERTS, LANES), lambda p, i: (0, 0))],
        out_shape=[jax.ShapeDtypeStruct((t, LANES), F32),
                   jax.ShapeDtypeStruct((TOP_K, t), jnp.int32),
                   jax.ShapeDtypeStruct((N_EXPERTS, LANES), F32)],
        scratch_shapes=[pltpu.VMEM((N_EXPERTS, LANES), F32),
                        pltpu.VMEM((N_EXPERTS, LANES), F32),
                        pltpu.VMEM((N_EXPERTS, LANES), F32)],
        compiler_params=_cparams(("arbitrary", "arbitrary")),
        name="route",
    )(logits, bias_col, su)


def _sc_mesh():
    return plsc.VectorSubcoreMesh(core_axis_name="c", subcore_axis_name="s")


def _sc_scatter_rows(rows, idx, n_out):
    n_rows, w = rows.shape
    n_k = idx.shape[0]

    @pl.kernel(out_type=jax.ShapeDtypeStruct((n_out, w), rows.dtype), mesh=_sc_mesh(),
               scratch_types=[])
    def scatter(x_hbm, i_hbm, o_hbm):
        def body(x_vmem, i_vmem):
            for k in range(n_k):
                pltpu.sync_copy(x_vmem, o_hbm.at[i_vmem.at[k]])

        pltpu.emit_pipeline(
            body,
            grid=(n_rows // SC_WINDOW,),
            in_specs=[pl.BlockSpec((SC_WINDOW, w), index_map=lambda i: (i, 0)),
                      pl.BlockSpec((n_k, SC_WINDOW), index_map=lambda i: (0, i))],
            out_specs=[],
            core_axis_name=("c", "s"),
            dimension_semantics=(pltpu.PARALLEL,),
        )(x_hbm, i_hbm)

    return scatter(rows, idx)


def _sc_gather_rows(rows, idx):
    n = idx.shape[1]
    w = rows.shape[1]

    @pl.kernel(out_type=jax.ShapeDtypeStruct((n, w), rows.dtype), mesh=_sc_mesh(),
               scratch_types=[])
    def gather(x_hbm, i_hbm, o_hbm):
        def body(i_vmem, o_vmem):
            pltpu.sync_copy(x_hbm.at[i_vmem.at[0]], o_vmem)

        pltpu.emit_pipeline(
            body,
            grid=(n // SC_WINDOW,),
            in_specs=[pl.BlockSpec((1, SC_WINDOW), index_map=lambda i: (0, i))],
            out_specs=[pl.BlockSpec((SC_WINDOW, w), index_map=lambda i: (i, 0))],
            core_axis_name=("c", "s"),
            dimension_semantics=(pltpu.PARALLEL,),
        )(i_hbm, o_hbm)

    return gather(rows, idx)


def _gmm_kernel(vb_ref, ve_ref, vlo_ref, vhi_ref, x_ref, wgu_ref, wd_ref, o_ref):
    v = pl.program_id(0)
    lo = vlo_ref[v]
    hi = vhi_ref[v]
    bm = x_ref.shape[1]
    half = D_MODEL // 2

    def expert_rows(rows):
        a0, b0 = _unpack_pair(x_ref[0, rows, :])
        a1, b1 = _unpack_pair(x_ref[1, rows, :])
        x = jnp.concatenate([a0.astype(BF16), a1.astype(BF16),
                             b0.astype(BF16), b1.astype(BF16)], axis=1)
        a = _dot(x, wgu_ref[0])
        act = _silu(a[:, 0:D_EXPERT]) * a[:, D_EXPERT:2 * D_EXPERT]
        y = _dot(act.astype(BF16), wd_ref[0])
        return _pack_pair(y[:, 0:half], y[:, half:D_MODEL])

    def write_rows(rows):
        w = expert_rows(rows)
        o_ref[0, rows, :] = w[:, 0:SC_ROW_WORDS]
        o_ref[1, rows, :] = w[:, SC_ROW_WORDS:half]

    @pl.when((lo == 0) & (hi == bm))
    def _():
        for c in range(bm // GMM_ROWS):
            write_rows(slice(c * GMM_ROWS, (c + 1) * GMM_ROWS))

    @pl.when((lo == 0) & (hi > 0) & (hi < bm))
    def _():
        for c in range(bm // GMM_ROWS):
            @pl.when(hi > c * GMM_ROWS)
            def _():
                write_rows(slice(c * GMM_ROWS, (c + 1) * GMM_ROWS))

    @pl.when((hi > lo) & (lo > 0))
    def _():
        for c in range(bm // GMM_ROWS):
            rows = slice(c * GMM_ROWS, (c + 1) * GMM_ROWS)

            @pl.when((hi > c * GMM_ROWS) & (lo < (c + 1) * GMM_ROWS))
            def _():
                w = expert_rows(rows)
                r = lax.broadcasted_iota(jnp.int32, (GMM_ROWS, SC_ROW_WORDS), 0) + c * GMM_ROWS
                mine = (r >= lo) & (r < hi)
                o_ref[0, rows, :] = jnp.where(mine, w[:, 0:SC_ROW_WORDS], o_ref[0, rows, :])
                o_ref[1, rows, :] = jnp.where(mine, w[:, SC_ROW_WORDS:half], o_ref[1, rows, :])


def _gmm(xin, wgu, wd, vb, ve, vlo, vhi, bm):
    _, n_rows, w = xin.shape
    d = D_MODEL
    n_visits = vb.shape[0]
    grid_spec = pltpu.PrefetchScalarGridSpec(
        num_scalar_prefetch=4,
        grid=(n_visits,),
        in_specs=[pl.BlockSpec((2, bm, w), lambda v, vb, ve, vlo, vhi: (0, vb[v], 0)),
                  pl.BlockSpec((1, d, 2 * D_EXPERT), lambda v, vb, ve, vlo, vhi: (ve[v], 0, 0)),
                  pl.BlockSpec((1, D_EXPERT, d), lambda v, vb, ve, vlo, vhi: (ve[v], 0, 0))],
        out_specs=pl.BlockSpec((2, bm, w), lambda v, vb, ve, vlo, vhi: (0, vb[v], 0)),
    )
    return pl.pallas_call(
        _gmm_kernel,
        grid_spec=grid_spec,
        out_shape=jax.ShapeDtypeStruct((2, n_rows, w), U32),
        compiler_params=_cparams(("arbitrary",)),
        name="gmm",
    )(vb, ve, vlo, vhi, xin, wgu, wd)


def _visit_plan(counts, n_blocks, bm):
    ne = counts.shape[0]
    n_visits = n_blocks + ne
    end = jnp.cumsum(counts)
    start = end - counts
    b0 = start // bm
    nb = jnp.where(counts > 0, (end - 1) // bm - b0 + 1, 0)
    vend = jnp.cumsum(nb)
    vstart = vend - nb
    total = vend[-1]
    v = jnp.arange(n_visits, dtype=jnp.int32)
    valid = v < total
    vc = jnp.minimum(v, total - 1)
    e_v = jnp.minimum(jnp.sum((vend[None, :] <= vc[:, None]).astype(jnp.int32), axis=1), ne - 1)
    onehot = (e_v[:, None] == jnp.arange(ne, dtype=jnp.int32)[None, :]).astype(jnp.int32)
    pick = lambda tab: jnp.sum(onehot * tab[None, :].astype(jnp.int32), axis=1)
    blk = pick(b0) + vc - pick(vstart)
    lo = jnp.clip(pick(start) - blk * bm, 0, bm)
    hi = jnp.clip(pick(end) - blk * bm, 0, bm)
    lo = jnp.where(valid, lo, 0)
    hi = jnp.where(valid, hi, 0)
    i32 = lambda a: a.astype(jnp.int32)
    return i32(blk), i32(e_v), i32(lo), i32(hi)


def _combine_kernel(yg_ref, g_ref, h_ref, x1_ref, mod_ref, wsgu_ref, wsd_ref, lng_ref,
                    lnb_ref, *rest):
    o_ref = rest[-1]
    q = D_MODEL // 4
    g = g_ref[...]
    a = _dot(h_ref[...], wsgu_ref[...])
    act = _silu(a[:, 0:D_EXPERT]) * a[:, D_EXPERT:2 * D_EXPERT]
    shared = _dot(act.astype(BF16), wsd_ref[...])
    f = [shared[:, j * q:(j + 1) * q] for j in range(4)]
    for k in range(TOP_K):
        a0, b0 = _unpack_pair(yg_ref[0, k])
        a1, b1 = _unpack_pair(yg_ref[1, k])
        gk = g[:, k:k + 1]
        f = [f[0] + gk * a0, f[1] + gk * a1, f[2] + gk * b0, f[3] + gk * b1]
    ffn = jnp.concatenate(f, axis=1)
    m = mod_ref[0]
    r = ALPHA * x1_ref[...] + m[5:6, :] * ffn
    o_ref[...] = _ln(r) * lng_ref[...] + lnb_ref[...]


def _combine(yg, gates, h2, x1, mod3, wsgu, wsd, lng, lnb, seq, tm, part, prev_out):
    t, d = x1.shape
    tp = yg.shape[2]
    b0 = part * (tp // tm)
    row = lambda i: (b0 + i, 0)
    const = lambda i: (0, 0)
    in_specs = [pl.BlockSpec((2, TOP_K, tm, SC_ROW_WORDS), lambda i: (0, 0, i, 0)),
                pl.BlockSpec((tm, LANES), row),
                pl.BlockSpec((tm, d), row),
                pl.BlockSpec((tm, d), row),
                pl.BlockSpec((1, 6, d), lambda i: (((b0 + i) * tm) // seq, 0, 0)),
                pl.BlockSpec((d, 2 * D_EXPERT), const),
                pl.BlockSpec((D_EXPERT, d), const),
                pl.BlockSpec((1, d), const),
                pl.BlockSpec((1, d), const)]
    args = [yg, gates, h2, x1, mod3, wsgu, wsd, lng, lnb]
    aliases = {}
    if prev_out is not None:
        in_specs.append(pl.BlockSpec(memory_space=pl.ANY))
        args.append(prev_out)
        aliases = {len(args) - 1: 0}
    return pl.pallas_call(
        _combine_kernel,
        grid=(tp // tm,),
        in_specs=in_specs,
        out_specs=pl.BlockSpec((tm, d), row),
        out_shape=jax.ShapeDtypeStruct((t, d), F32),
        input_output_aliases=aliases,
        compiler_params=_cparams(("arbitrary",)),
        name="combine",
    )(*args)


def _pad_lanes(v, fill=0.0):
    out = jnp.full((1, LANES), fill, F32)
    return out.at[0, :v.shape[0]].set(v.astype(F32))


def kernel(x, c, w_ada, b_ada, w_in, ssd_conv_w, ssd_conv_b, ssd_dt_bias, ssd_A_log, ssd_D,
           ssd_norm_w, w_ssd_out, sc_conv_w, w_sc_out, w_o, ln1_g, ln1_b, router_w,
           router_bias, w_gate, w_up, w_down, sh_gate, sh_up, sh_down, ln2_g, ln2_b):
    bsz, seq, d = x.shape
    t = bsz * seq
    depth = w_in.shape[0]
    tm_proj = min(1024, seq)
    tm_merge = min(512, seq)
    tm_route = min(1024, t)
    tm_comb = min(512, seq)
    bm_gmm = 1024
    n_assign = t * TOP_K
    n_parts = 4 if t % (4 * tm_comb) == 0 else 1

    e01 = (np.arange(LANES)[:, None] == (np.arange(SSD_INNER)[None, :] // SSD_HEAD_DIM))
    e01 = jnp.asarray(np.concatenate([e01, e01], axis=0), BF16)
    su = jnp.asarray(np.arange(tm_route)[:, None] < np.arange(tm_route)[None, :], BF16)

    xf = x.reshape(t, d)
    for l in range(depth):
        w_l = w_in[l]
        w_main = jnp.concatenate([w_l[:, :DT_COL0], w_l[:, DT_COL0 + SSD_HEADS:]],
                                 axis=1).astype(BF16)
        w_dt = jnp.pad(w_l[:, DT_COL0:DT_COL0 + SSD_HEADS],
                       ((0, 0), (0, LANES - SSD_HEADS))).astype(BF16)
        wgu = jnp.concatenate([w_gate[l], w_up[l]], axis=-1).astype(BF16)
        wd = w_down[l].astype(BF16)
        wsgu = jnp.concatenate([sh_gate[l], sh_up[l]], axis=-1).astype(BF16)
        wsd = sh_down[l].astype(BF16)
        wr = jnp.pad(router_w[l], ((0, 0), (0, LANES - N_EXPERTS)))
        wr_hi = wr.astype(BF16)
        wr = jnp.concatenate([wr_hi, (wr - wr_hi.astype(F32)).astype(BF16)], axis=1)
        bias_col = jnp.zeros((LANES, LANES), F32).at[:N_EXPERTS, 0].set(router_bias[l])
        dvec = jnp.repeat(ssd_D[l], SSD_HEAD_DIM).reshape(1, SSD_INNER)

        mod3 = _mod(c, w_ada[l], b_ada[l]).reshape(bsz, 6, d)
        p, dtp = _inproj(xf, mod3, w_main, w_dt, seq, tm_proj, w_main.shape[1] // 4)
        yn = _ssd(p, dtp, ssd_conv_w[l], ssd_conv_b[l].reshape(1, -1),
                  _pad_lanes(ssd_dt_bias[l]), _pad_lanes(ssd_A_log[l]), dvec,
                  ssd_norm_w[l].reshape(1, -1), e01, bsz, seq, 2 * SSD_CHUNK)
        x1, h2, h2p, logits = _merge(yn, p, xf, mod3, w_ssd_out[l].astype(BF16),
                                     w_sc_out[l].astype(BF16), w_o[l].astype(BF16),
                                     sc_conv_w[l], ln1_g[l].reshape(1, d),
                                     ln1_b[l].reshape(1, d), wr, bsz, seq, tm_merge)
        gates, dest, cnt = _route(logits, bias_col, su, tm_route)
        dest_both = jnp.concatenate([dest, dest + n_assign], axis=1)
        vb, ve, vlo, vhi = _visit_plan(cnt[:, 0].astype(jnp.int32), n_assign // bm_gmm, bm_gmm)
        xin = _sc_scatter_rows(h2p.reshape(2 * t, SC_ROW_WORDS), dest_both, 2 * n_assign)
        yb = _gmm(xin.reshape(2, n_assign, SC_ROW_WORDS), wgu, wd, vb, ve, vlo, vhi, bm_gmm)
        yb = yb.reshape(2 * n_assign, SC_ROW_WORDS)
        tp = t // n_parts
        out = None
        for j in range(n_parts):
            idx = dest[:, j * tp:(j + 1) * tp].reshape(1, TOP_K * tp)
            idx = jnp.concatenate([idx, idx + n_assign], axis=1)
            yg = _sc_gather_rows(yb, idx).reshape(2, TOP_K, tp, SC_ROW_WORDS)
            out = _combine(yg, gates, h2, x1, mod3, wsgu, wsd, ln2_g[l].reshape(1, d),
                           ln2_b[l].reshape(1, d), seq, tm_comb, j, out)
        xf = out
    return xf.reshape(bsz, seq, d)
```

```python
import jax
import jax.numpy as jnp
import numpy as np
from jax import lax
from jax.experimental import pallas as pl
from jax.experimental.pallas import tpu as pltpu
from jax.experimental.pallas import tpu_sc as plsc

F32 = jnp.float32
BF16 = jnp.bfloat16
U32 = jnp.uint32

LANES = 128
SUBLANES = 8
VMEM_LIMIT_BYTES = 56 * 1024 * 1024

D_MODEL = 1024
SSD_INNER = 2048
SSD_HEAD_DIM = 64
SSD_HEADS = 32
SSD_GROUPS = 8
SSD_STATE = 128
SSD_CONV = 4
SSD_CHUNK = 128
GROUP_W = SSD_INNER // SSD_GROUPS
HEADS_PER_GROUP = SSD_HEADS // SSD_GROUPS
SC_KERNEL = 3
N_EXPERTS = 64
TOP_K = 8
N_EXPERT_GROUPS = 8
TOPK_EXPERT_GROUPS = 4
EXPERTS_PER_GROUP = N_EXPERTS // N_EXPERT_GROUPS
D_EXPERT = 256
ROUTED_SCALE = 2.5
LN_EPS = 1e-5
RMS_EPS = 1e-5
ALPHA = 2.0 ** 0.25
DT_COL0 = 2048 + 4096
NEG_BIG = -1e30
SC_WINDOW = 128
SC_ROW_WORDS = 256
GMM_ROWS = 256
MERGE_ROWS = 256


def _cparams(sem):
    return pltpu.CompilerParams(dimension_semantics=sem,
                                vmem_limit_bytes=VMEM_LIMIT_BYTES)


def _ln(x):
    mu = jnp.mean(x, axis=-1, keepdims=True)
    xc = x - mu
    var = jnp.mean(xc * xc, axis=-1, keepdims=True)
    return xc * lax.rsqrt(var + LN_EPS)


def _silu(x):
    h = 0.5 * x
    return h + h * jnp.tanh(h)


def _split3(a):
    hi = a.astype(BF16)
    r1 = a - hi.astype(F32)
    mid = r1.astype(BF16)
    lo = (r1 - mid.astype(F32)).astype(BF16)
    return hi, mid, lo


def _dot(a, b):
    return jnp.dot(a, b, preferred_element_type=F32)


def _dot_exact01(a, m01):
    hi, mid, lo = _split3(a)
    return _dot(hi, m01) + _dot(mid, m01) + _dot(lo, m01)


def _dot01_exact(m01, a):
    hi, mid, lo = _split3(a)
    return _dot(m01, hi) + _dot(m01, mid) + _dot(m01, lo)


def _pack_pair(a, b):
    ab = pltpu.bitcast(a.astype(BF16).astype(F32), U32)
    bb = pltpu.bitcast(b.astype(BF16).astype(F32), U32)
    return ab | lax.shift_right_logical(bb, jnp.uint32(16))


def _unpack_pair(w):
    a = pltpu.bitcast(w & jnp.uint32(0xFFFF0000), F32)
    b = pltpu.bitcast(lax.shift_left(w, jnp.uint32(16)), F32)
    return a, b


def _mod_kernel(c_ref, w_ref, b_ref, o_ref):
    a = _silu(c_ref[...])
    o_ref[...] = jnp.dot(a, w_ref[...], precision=lax.Precision.HIGHEST,
                         preferred_element_type=F32) + b_ref[...]


def _mod(c, w, b):
    bsz, d = c.shape
    n = w.shape[1]
    tn = 1024
    return pl.pallas_call(
        _mod_kernel,
        grid=(n // tn,),
        in_specs=[pl.BlockSpec((bsz, d), lambda j: (0, 0)),
                  pl.BlockSpec((d, tn), lambda j: (0, j)),
                  pl.BlockSpec((1, tn), lambda j: (0, j))],
        out_specs=pl.BlockSpec((bsz, tn), lambda j: (0, j)),
        out_shape=jax.ShapeDtypeStruct((bsz, n), F32),
        compiler_params=_cparams(("arbitrary",)),
        name="mod",
    )(c, w, b.reshape(1, n))


def _inproj_kernel(x_ref, mod_ref, w_ref, wdt_ref, p_ref, dt_ref, h_scr):
    j = pl.program_id(1)

    @pl.when(j == 0)
    def _():
        m = mod_ref[0]
        h = _ln(x_ref[...]) * (1.0 + m[1:2, :]) + m[0:1, :]
        hb = h.astype(BF16)
        h_scr[...] = hb
        dt_ref[...] = _dot(hb, wdt_ref[...])

    p_ref[...] = _dot(h_scr[...], w_ref[...]).astype(p_ref.dtype)


def _inproj(x2d, mod3, w_main, w_dt, seq, tm, tn, tok0, t):
    d = x2d.shape[1]
    n = w_main.shape[1]
    i0 = tok0 // tm
    return pl.pallas_call(
        _inproj_kernel,
        grid=(t // tm, n // tn),
        in_specs=[pl.BlockSpec((tm, d), lambda i, j: (i0 + i, 0)),
                  pl.BlockSpec((1, 6, d), lambda i, j: (((i0 + i) * tm) // seq, 0, 0)),
                  pl.BlockSpec((d, tn), lambda i, j: (0, j)),
                  pl.BlockSpec((d, LANES), lambda i, j: (0, 0))],
        out_specs=[pl.BlockSpec((tm, tn), lambda i, j: (i, j)),
                   pl.BlockSpec((tm, LANES), lambda i, j: (i, 0))],
        out_shape=[jax.ShapeDtypeStruct((t, n), BF16),
                   jax.ShapeDtypeStruct((t, LANES), F32)],
        scratch_shapes=[pltpu.VMEM((tm, d), BF16)],
        compiler_params=_cparams(("arbitrary", "arbitrary")),
        name="inproj",
    )(x2d, mod3, w_main, w_dt)


def _shift_conv(cur, prev_scr, s_ref, w_ref, b_ref):
    rows = cur.shape[0]
    n_taps = w_ref.shape[0]
    shifted = _dot(s_ref[...], jnp.concatenate([prev_scr[...], cur], axis=0))
    acc = w_ref[n_taps - 1:n_taps, :] * cur.astype(F32) + b_ref[...]
    for k in range(n_taps - 1):
        acc = acc + w_ref[k:k + 1, :] * shifted[k * rows:(k + 1) * rows]
    prev_scr[...] = cur
    return acc


def _shift_matrix(rows, n_taps):
    s = np.zeros(((n_taps - 1) * rows, 2 * rows), np.float32)
    t = np.arange(rows)
    for k in range(n_taps - 1):
        s[k * rows + t, rows + t - (n_taps - 1) + k] = 1.0
    return jnp.asarray(s, BF16)


def _ssd_kernel(xs_ref, bc_ref, z_ref, dtp_ref, cw_xs_ref, cw_bc_ref, cb_xs_ref,
                cb_bc_ref, dtb_ref, alog_ref, dvec_ref, nw_ref, e_ref, s_ref,
                yn_ref, prev_xs, prev_bc, state):
    @pl.when(pl.program_id(1) == 0)
    def _():
        prev_xs[...] = jnp.zeros(prev_xs.shape, BF16)
        prev_bc[...] = jnp.zeros(prev_bc.shape, BF16)
        state[...] = jnp.zeros(state.shape, F32)

    for cc in range(xs_ref.shape[0] // SSD_CHUNK):
        _ssd_chunk(slice(cc * SSD_CHUNK, (cc + 1) * SSD_CHUNK), xs_ref, bc_ref, z_ref, dtp_ref,
                   cw_xs_ref, cw_bc_ref, cb_xs_ref, cb_bc_ref, dtb_ref, alog_ref, dvec_ref,
                   nw_ref, e_ref, s_ref, yn_ref, prev_xs, prev_bc, state)


def _ssd_chunk(rows, xs_ref, bc_ref, z_ref, dtp_ref, cw_xs_ref, cw_bc_ref, cb_xs_ref,
               cb_bc_ref, dtb_ref, alog_ref, dvec_ref, nw_ref, e_ref, s_ref,
               yn_ref, prev_xs, prev_bc, state):
    L = SSD_CHUNK
    xs = _silu(_shift_conv(xs_ref[rows, :], prev_xs, s_ref, cw_xs_ref, cb_xs_ref))
    bc = _silu(_shift_conv(bc_ref[rows, :], prev_bc, s_ref, cw_bc_ref, cb_bc_ref))

    x_dt = dtp_ref[rows, :] + dtb_ref[...]
    dt = jnp.maximum(x_dt, 0.0) + jnp.log1p(jnp.exp(-jnp.abs(x_dt)))
    a_neg = -jnp.exp(alog_ref[...])
    d_a = dt * a_neg
    row = lax.broadcasted_iota(jnp.int32, (L, L), 0)
    col = lax.broadcasted_iota(jnp.int32, (L, L), 1)
    causal = col <= row
    tri = jnp.where(causal, 1.0, 0.0).astype(BF16)
    hi, mid, lo = _split3(d_a)
    a_cs = _dot(tri, hi) + _dot(tri, mid) + _dot(tri, lo)
    a_cs_t = a_cs.T
    a_last = a_cs[L - 1:L, :]
    stack = jnp.concatenate([dt, jnp.exp(a_cs), jnp.exp(a_last - a_cs)], axis=0)
    s_hi = stack.astype(BF16)
    s_mid = (stack - s_hi.astype(F32)).astype(BF16)
    st_e = _dot(jnp.concatenate([s_hi, s_mid], axis=1), e_ref[...])
    dt_e = st_e[0:L]
    od_e = st_e[L:2 * L]
    ds_e = st_e[2 * L:3 * L]
    cd_e = od_e[L - 1:L, :]

    x_f = xs * dt_e
    x_b = x_f.astype(BF16)
    xd_b = (x_f * ds_e).astype(BF16)
    lane_head = lax.broadcasted_iota(jnp.int32, (L, GROUP_W), 1) // SSD_HEAD_DIM

    for g in range(SSD_GROUPS):
        gc = slice(g * GROUP_W, (g + 1) * GROUP_W)
        b_g = bc[:, g * SSD_STATE:(g + 1) * SSD_STATE]
        c_g = bc[:, SSD_GROUPS * SSD_STATE + g * SSD_STATE:
                 SSD_GROUPS * SSD_STATE + (g + 1) * SSD_STATE].astype(BF16)
        b_gt = b_g.T.astype(BF16)
        cb = _dot(c_g, b_gt)
        x_g = x_b[:, gc]
        lhs = []
        rhs = []
        for r in range(HEADS_PER_GROUP):
            h = g * HEADS_PER_GROUP + r
            seg = a_cs[:, h:h + 1] - a_cs_t[h:h + 1, :]
            lmat = jnp.exp(jnp.where(causal, seg, NEG_BIG))
            lhs.append((cb * lmat).astype(BF16))
            rhs.append(jnp.where(lane_head == r, x_g, jnp.zeros_like(x_g)))
        y_diag = _dot(jnp.concatenate(lhs, axis=1), jnp.concatenate(rhs, axis=0))
        st = state[g]
        y_off = _dot(c_g, st.astype(BF16)) * od_e[:, gc]
        state[g] = cd_e[:, gc] * st + _dot(b_gt, xd_b[:, gc])
        y = y_diag + y_off + dvec_ref[:, gc] * xs[:, gc]
        gt = y * _silu(z_ref[rows, gc].astype(F32))
        ms = jnp.mean(gt * gt, axis=-1, keepdims=True)
        yn_ref[rows, gc] = (gt * lax.rsqrt(ms + RMS_EPS) * nw_ref[:, gc]).astype(yn_ref.dtype)


def _ssd(p, dtp, cw, cb, dtb, alog, dvec, nw, e01, bsz, seq, rows):
    t = p.shape[0]
    L = SSD_CHUNK
    nc = seq // rows
    row_map = lambda b, c: (b * nc + c, 0)

    def pcol(k):
        return pl.BlockSpec((rows, SSD_INNER), lambda b, c: (b * nc + c, k))

    const = lambda b, c: (0, 0)
    return pl.pallas_call(
        _ssd_kernel,
        grid=(bsz, nc),
        in_specs=[pcol(1), pcol(2), pcol(0),
                  pl.BlockSpec((rows, LANES), row_map),
                  pl.BlockSpec((SSD_CONV, SSD_INNER), lambda b, c: (0, 0)),
                  pl.BlockSpec((SSD_CONV, SSD_INNER), lambda b, c: (0, 1)),
                  pl.BlockSpec((1, SSD_INNER), lambda b, c: (0, 0)),
                  pl.BlockSpec((1, SSD_INNER), lambda b, c: (0, 1)),
                  pl.BlockSpec((1, LANES), const),
                  pl.BlockSpec((1, LANES), const),
                  pl.BlockSpec((1, SSD_INNER), const),
                  pl.BlockSpec((1, SSD_INNER), const),
                  pl.BlockSpec((2 * LANES, SSD_INNER), const),
                  pl.BlockSpec(((SSD_CONV - 1) * L, 2 * L), const)],
        out_specs=pl.BlockSpec((rows, SSD_INNER), row_map),
        out_shape=jax.ShapeDtypeStruct((t, SSD_INNER), BF16),
        scratch_shapes=[pltpu.VMEM((L, SSD_INNER), BF16),
                        pltpu.VMEM((L, SSD_INNER), BF16),
                        pltpu.VMEM((SSD_GROUPS, SSD_STATE, GROUP_W), F32)],
        compiler_params=_cparams(("arbitrary", "arbitrary")),
        name="ssd",
    )(p, p, p, dtp, cw, cw, cb, cb, dtb, alog, dvec, nw, e01, _shift_matrix(L, SSD_CONV))


def _merge_kernel(yn_ref, scb_ref, scc_ref, sch_ref, ga_ref, gb_ref, x_ref, mod_ref,
                  wssd_ref, wsc_ref, wo_ref, cw_ref, lng_ref, lnb_ref, wr_ref,
                  x1_ref, h2_ref, h2p_ref, lg_ref, ext):
    s = pl.program_id(1)
    tm = x_ref.shape[0]

    @pl.when(s == 0)
    def _():
        ext[0:SUBLANES, :] = jnp.zeros((SUBLANES, D_MODEL), F32)

    ext[SUBLANES:SUBLANES + tm, :] = scc_ref[...].astype(F32) * sch_ref[...].astype(F32)
    m = mod_ref[0]
    for c in range(tm // MERGE_ROWS):
        r0 = c * MERGE_ROWS
        rows = slice(r0, r0 + MERGE_ROWS)
        u = None
        for k in range(SC_KERNEL):
            start = SUBLANES + r0 - (SC_KERNEL - 1) + k
            term = cw_ref[k:k + 1, :] * ext[start:start + MERGE_ROWS, :]
            u = term if u is None else u + term
        y_b = _dot((scb_ref[rows, :].astype(F32) * u).astype(BF16), wsc_ref[...])
        y_a = _dot(yn_ref[rows, :], wssd_ref[...])
        merged = (jax.nn.sigmoid(ga_ref[rows, :].astype(F32)) * y_a
                  + jax.nn.sigmoid(gb_ref[rows, :].astype(F32)) * y_b)
        mix = _dot(merged.astype(BF16), wo_ref[...])
        x1 = _ln(ALPHA * x_ref[rows, :] + m[2:3, :] * mix) * lng_ref[...] + lnb_ref[...]
        x1_ref[rows, :] = x1
        h2 = _ln(x1) * (1.0 + m[4:5, :]) + m[3:4, :]
        h2_hi = h2.astype(BF16)
        h2_ref[rows, :] = h2_hi
        w = _pack_pair(h2[:, 0:D_MODEL // 2], h2[:, D_MODEL // 2:D_MODEL])
        h2p_ref[0, rows, :] = w[:, 0:SC_ROW_WORDS]
        h2p_ref[1, rows, :] = w[:, SC_ROW_WORDS:2 * SC_ROW_WORDS]
        h2_lo = (h2 - h2_hi.astype(F32)).astype(BF16)
        both = _dot(h2_hi, wr_ref[...])
        lg_ref[rows, :] = (both[:, 0:LANES] + both[:, LANES:2 * LANES]
                           + _dot(h2_lo, wr_ref[:, 0:LANES]))
    ext[0:SUBLANES, :] = ext[tm:tm + SUBLANES, :]


def _merge(yn, p, x2d, mod3, wssd, wsc, wo, cw, lng, lnb, wr, bsz, seq, tm, b0):
    t = bsz * seq
    d = x2d.shape[1]
    ns = seq // tm
    row_map = lambda b, s: (b * ns + s, 0)
    const = lambda b, s: (0, 0)
    col0 = (2048 + 4096) // d

    def pcol(k):
        return pl.BlockSpec((tm, d), lambda b, s: (b * ns + s, col0 + k))

    return pl.pallas_call(
        _merge_kernel,
        grid=(bsz, ns),
        in_specs=[pl.BlockSpec((tm, SSD_INNER), row_map),
                  pcol(0), pcol(1), pcol(2), pcol(3), pcol(4),
                  pl.BlockSpec((tm, d), lambda b, s: ((b0 + b) * ns + s, 0)),
                  pl.BlockSpec((1, 6, d), lambda b, s: (b0 + b, 0, 0)),
                  pl.BlockSpec((SSD_INNER, d), const, pipeline_mode=pl.Buffered(1)),
                  pl.BlockSpec((d, d), const, pipeline_mode=pl.Buffered(1)),
                  pl.BlockSpec((d, d), const, pipeline_mode=pl.Buffered(1)),
                  pl.BlockSpec((SC_KERNEL, d), const),
                  pl.BlockSpec((1, d), const),
                  pl.BlockSpec((1, d), const),
                  pl.BlockSpec((d, 2 * LANES), const, pipeline_mode=pl.Buffered(1))],
        out_specs=[pl.BlockSpec((tm, d), row_map),
                   pl.BlockSpec((tm, d), row_map),
                   pl.BlockSpec((2, tm, SC_ROW_WORDS), lambda b, s: (0, b * ns + s, 0)),
                   pl.BlockSpec((tm, LANES), row_map)],
        out_shape=[jax.ShapeDtypeStruct((t, d), F32),
                   jax.ShapeDtypeStruct((t, d), BF16),
                   jax.ShapeDtypeStruct((2, t, SC_ROW_WORDS), U32),
                   jax.ShapeDtypeStruct((t, LANES), F32)],
        scratch_shapes=[pltpu.VMEM((tm + SUBLANES, d), F32)],
        compiler_params=_cparams(("arbitrary", "arbitrary")),
        name="merge",
    )(yn, p, p, p, p, p, x2d, mod3, wssd, wsc, wo, cw, lng, lnb, wr)


def _first_argmax_mask(v, idx, n):
    m = jnp.max(v, axis=0, keepdims=True)
    first = jnp.min(jnp.where(v == m, idx, n), axis=0, keepdims=True)
    return idx == first, m


def _route_kernel(lg_ref, bias_ref, su_ref, g_ref, dest_ref, cnt_ref, counts, carry, start):
    p = pl.program_id(0)
    i = pl.program_id(1)
    tm = lg_ref.shape[0]
    ne, ng, eg = N_EXPERTS, N_EXPERT_GROUPS, EXPERTS_PER_GROUP
    lt = lg_ref[...].T[0:ne, :]
    scores = jax.nn.sigmoid(lt)
    biased = scores + bias_ref[...][0:ne, 0:1]
    grp = biased.reshape(ng, eg, tm)
    idx_e = lax.broadcasted_iota(jnp.int32, (ng, eg, tm), 1)
    m1 = jnp.max(grp, axis=1, keepdims=True)
    first = jnp.min(jnp.where(grp == m1, idx_e, eg), axis=1, keepdims=True)
    m2 = jnp.max(jnp.where(idx_e == first, NEG_BIG, grp), axis=1, keepdims=True)
    gscore = m1 + m2
    idx_g = lax.broadcasted_iota(jnp.int32, (ng, 1, tm), 0)
    gsel = jnp.zeros((ng, 1, tm), F32)
    for _ in range(TOPK_EXPERT_GROUPS):
        hit, _m = _first_argmax_mask(gscore, idx_g, ng)
        gsel = jnp.where(hit, 1.0, gsel)
        gscore = jnp.where(hit, NEG_BIG, gscore)
    emask = jnp.broadcast_to(gsel, (ng, eg, tm)).reshape(ne, tm)
    cand = jnp.where(emask > 0.0, biased, NEG_BIG)
    idx_x = lax.broadcasted_iota(jnp.int32, (ne, tm), 0)
    sel = jnp.zeros((ne, tm), F32)
    picks = []
    for _ in range(TOP_K):
        hit, _m = _first_argmax_mask(cand, idx_x, ne)
        sel = jnp.where(hit, 1.0, sel)
        cand = jnp.where(hit, NEG_BIG, cand)
        picks.append(hit)
    n_e = jnp.broadcast_to(jnp.sum(sel, axis=1, keepdims=True), (ne, LANES))

    @pl.when((p == 0) & (i == 0))
    def _():
        counts[...] = jnp.zeros(counts.shape, F32)

    @pl.when(p == 0)
    def _():
        counts[...] += n_e

    @pl.when((p == 1) & (i == 0))
    def _():
        r = lax.broadcasted_iota(jnp.int32, (ne, ne), 0)
        cc = lax.broadcasted_iota(jnp.int32, (ne, ne), 1)
        below = jnp.where(cc < r, 1.0, 0.0).astype(BF16)
        start[...] = _dot01_exact(below, counts[...])
        carry[...] = jnp.zeros(carry.shape, F32)
        cnt_ref[...] = counts[...]

    @pl.when(p == 1)
    def _():
        before = _dot(sel.astype(BF16), su_ref[...])
        slot = before + carry[:, 0:1] + start[:, 0:1]
        dest_rows = []
        gate_rows = []
        for hit in picks:
            dest_rows.append(jnp.sum(jnp.where(hit, slot, 0.0), axis=0, keepdims=True))
            gate_rows.append(jnp.sum(jnp.where(hit, scores, 0.0), axis=0, keepdims=True))
        dest_ref[...] = jnp.concatenate(dest_rows, axis=0).astype(jnp.int32)
        gate = jnp.concatenate(gate_rows, axis=0)
        gate = gate / jnp.sum(gate, axis=0, keepdims=True) * ROUTED_SCALE
        pad = jnp.zeros((LANES - TOP_K, tm), F32)
        g_ref[...] = jnp.concatenate([gate, pad], axis=0).T
        carry[...] += n_e


def _route(logits, bias_col, su, tm):
    t = logits.shape[0]
    return pl.pallas_call(
        _route_kernel,
        grid=(2, t // tm),
        in_specs=[pl.BlockSpec((tm, LANES), lambda p, i: (i, 0)),
                  pl.BlockSpec((LANES, LANES), lambda p, i: (0, 0)),
                  pl.BlockSpec((tm, tm), lambda p, i: (0, 0))],
        out_specs=[pl.BlockSpec((tm, LANES), lambda p, i: (i * p, 0)),
                   pl.BlockSpec((TOP_K, tm), lambda p, i: (0, i * p)),
                   pl.BlockSpec((N_EXPERTS, LANES), lambda p, i: (0, 0))],
        out_shape=[jax.ShapeDtypeStruct((t, LANES), F32),
                   jax.ShapeDtypeStruct((TOP_K, t), jnp.int32),
                   jax.ShapeDtypeStruct((N_EXPERTS, LANES), F32)],
        scratch_shapes=[pltpu.VMEM((N_EXPERTS, LANES), F32),
                        pltpu.VMEM((N_EXPERTS, LANES), F32),
                        pltpu.VMEM((N_EXPERTS, LANES), F32)],
        compiler_params=_cparams(("arbitrary", "arbitrary")),
        name="route",
    )(logits, bias_col, su)


def _sc_mesh():
    return plsc.VectorSubcoreMesh(core_axis_name="c", subcore_axis_name="s")


def _sc_scatter_rows(rows, idx, n_out):
    n_rows, w = rows.shape
    n_k = idx.shape[0]

    @pl.kernel(out_type=jax.ShapeDtypeStruct((n_out, w), rows.dtype), mesh=_sc_mesh(),
               scratch_types=[])
    def scatter(x_hbm, i_hbm, o_hbm):
        def body(x_vmem, i_vmem):
            for k in range(n_k):
                pltpu.sync_copy(x_vmem, o_hbm.at[i_vmem.at[k]])

        pltpu.emit_pipeline(
            body,
            grid=(n_rows // SC_WINDOW,),
            in_specs=[pl.BlockSpec((SC_WINDOW, w), index_map=lambda i: (i, 0)),
                      pl.BlockSpec((n_k, SC_WINDOW), index_map=lambda i: (0, i))],
            out_specs=[],
            core_axis_name=("c", "s"),
            dimension_semantics=(pltpu.PARALLEL,),
        )(x_hbm, i_hbm)

    return scatter(rows, idx)


def _sc_gather_rows(rows, idx):
    n = idx.shape[1]
    w = rows.shape[1]

    @pl.kernel(out_type=jax.ShapeDtypeStruct((n, w), rows.dtype), mesh=_sc_mesh(),
               scratch_types=[])
    def gather(x_hbm, i_hbm, o_hbm):
        def body(i_vmem, o_vmem):
            pltpu.sync_copy(x_hbm.at[i_vmem.at[0]], o_vmem)

        pltpu.emit_pipeline(
            body,
            grid=(n // SC_WINDOW,),
            in_specs=[pl.BlockSpec((1, SC_WINDOW), index_map=lambda i: (0, i))],
            out_specs=[pl.BlockSpec((SC_WINDOW, w), index_map=lambda i: (i, 0))],
            core_axis_name=("c", "s"),
            dimension_semantics=(pltpu.PARALLEL,),
        )(i_hbm, o_hbm)

    return gather(rows, idx)


def _gmm_kernel(vb_ref, ve_ref, vlo_ref, vhi_ref, x_ref, wgu_ref, wd_ref, o_ref):
    v = pl.program_id(0)
    lo = vlo_ref[v]
    hi = vhi_ref[v]
    bm = x_ref.shape[1]
    half = D_MODEL // 2

    def expert_rows(rows):
        a0, b0 = _unpack_pair(x_ref[0, rows, :])
        a1, b1 = _unpack_pair(x_ref[1, rows, :])
        x = jnp.concatenate([a0.astype(BF16), a1.astype(BF16),
                             b0.astype(BF16), b1.astype(BF16)], axis=1)
        a = _dot(x, wgu_ref[0])
        act = _silu(a[:, 0:D_EXPERT]) * a[:, D_EXPERT:2 * D_EXPERT]
        y = _dot(act.astype(BF16), wd_ref[0])
        return _pack_pair(y[:, 0:half], y[:, half:D_MODEL])

    def write_rows(rows):
        w = expert_rows(rows)
        o_ref[0, rows, :] = w[:, 0:SC_ROW_WORDS]
        o_ref[1, rows, :] = w[:, SC_ROW_WORDS:half]

    @pl.when((lo == 0) & (hi == bm))
    def _():
        for c in range(bm // GMM_ROWS):
            write_rows(slice(c * GMM_ROWS, (c + 1) * GMM_ROWS))

    @pl.when((lo == 0) & (hi > 0) & (hi < bm))
    def _():
        for c in range(bm // GMM_ROWS):
            @pl.when(hi > c * GMM_ROWS)
            def _():
                write_rows(slice(c * GMM_ROWS, (c + 1) * GMM_ROWS))

    @pl.when((hi > lo) & (lo > 0))
    def _():
        for c in range(bm // GMM_ROWS):
            rows = slice(c * GMM_ROWS, (c + 1) * GMM_ROWS)

            @pl.when((hi > c * GMM_ROWS) & (lo < (c + 1) * GMM_ROWS))
            def _():
                w = expert_rows(rows)
                r = lax.broadcasted_iota(jnp.int32, (GMM_ROWS, SC_ROW_WORDS), 0) + c * GMM_ROWS
                mine = (r >= lo) & (r < hi)
                o_ref[0, rows, :] = jnp.where(mine, w[:, 0:SC_ROW_WORDS], o_ref[0, rows, :])
                o_ref[1, rows, :] = jnp.where(mine, w[:, SC_ROW_WORDS:half], o_ref[1, rows, :])


def _gmm(xin, wgu, wd, vb, ve, vlo, vhi, bm):
    _, n_rows, w = xin.shape
    d = D_MODEL
    n_visits = vb.shape[0]
    grid_spec = pltpu.PrefetchScalarGridSpec(
        num_scalar_prefetch=4,
        grid=(n_visits,),
        in_specs=[pl.BlockSpec((2, bm, w), lambda v, vb, ve, vlo, vhi: (0, vb[v], 0)),
                  pl.BlockSpec((1, d, 2 * D_EXPERT), lambda v, vb, ve, vlo, vhi: (ve[v], 0, 0)),
                  pl.BlockSpec((1, D_EXPERT, d), lambda v, vb, ve, vlo, vhi: (ve[v], 0, 0))],
        out_specs=pl.BlockSpec((2, bm, w), lambda v, vb, ve, vlo, vhi: (0, vb[v], 0)),
    )
    return pl.pallas_call(
        _gmm_kernel,
        grid_spec=grid_spec,
        out_shape=jax.ShapeDtypeStruct((2, n_rows, w), U32),
        compiler_params=_cparams(("arbitrary",)),
        name="gmm",
    )(vb, ve, vlo, vhi, xin, wgu, wd)


def _visit_plan(counts, n_blocks, bm):
    ne = counts.shape[0]
    n_visits = n_blocks + ne
    end = jnp.cumsum(counts)
    start = end - counts
    b0 = start // bm
    nb = jnp.where(counts > 0, (end - 1) // bm - b0 + 1, 0)
    vend = jnp.cumsum(nb)
    vstart = vend - nb
    total = vend[-1]
    v = jnp.arange(n_visits, dtype=jnp.int32)
    valid = v < total
    vc = jnp.minimum(v, total - 1)
    e_v = jnp.minimum(jnp.sum((vend[None, :] <= vc[:, None]).astype(jnp.int32), axis=1), ne - 1)
    onehot = (e_v[:, None] == jnp.arange(ne, dtype=jnp.int32)[None, :]).astype(jnp.int32)
    pick = lambda tab: jnp.sum(onehot * tab[None, :].astype(jnp.int32), axis=1)
    blk = pick(b0) + vc - pick(vstart)
    lo = jnp.clip(pick(start) - blk * bm, 0, bm)
    hi = jnp.clip(pick(end) - blk * bm, 0, bm)
    lo = jnp.where(valid, lo, 0)
    hi = jnp.where(valid, hi, 0)
    i32 = lambda a: a.astype(jnp.int32)
    return i32(blk), i32(e_v), i32(lo), i32(hi)


def _combine_kernel(yg_ref, g_ref, h_ref, x1_ref, mod_ref, wsgu_ref, wsd_ref, lng_ref,
                    lnb_ref, *rest):
    o_ref = rest[-1]
    q = D_MODEL // 4
    g = g_ref[...]
    a = _dot(h_ref[...], wsgu_ref[...])
    act = _silu(a[:, 0:D_EXPERT]) * a[:, D_EXPERT:2 * D_EXPERT]
    shared = _dot(act.astype(BF16), wsd_ref[...])
    f = [shared[:, j * q:(j + 1) * q] for j in range(4)]
    for k in range(TOP_K):
        a0, b0 = _unpack_pair(yg_ref[0, k])
        a1, b1 = _unpack_pair(yg_ref[1, k])
        gk = g[:, k:k + 1]
        f = [f[0] + gk * a0, f[1] + gk * a1, f[2] + gk * b0, f[3] + gk * b1]
    ffn = jnp.concatenate(f, axis=1)
    m = mod_ref[0]
    r = ALPHA * x1_ref[...] + m[5:6, :] * ffn
    o_ref[...] = _ln(r) * lng_ref[...] + lnb_ref[...]


def _combine(yg, gates, h2, x1, mod3, wsgu, wsd, lng, lnb, seq, tm, tok0, t, prev_out):
    tp, d = x1.shape
    b0 = tok0 // tm
    row = lambda i: (i, 0)
    const = lambda i: (0, 0)
    in_specs = [pl.BlockSpec((2, TOP_K, tm, SC_ROW_WORDS), lambda i: (0, 0, i, 0)),
                pl.BlockSpec((tm, LANES), row),
                pl.BlockSpec((tm, d), row),
                pl.BlockSpec((tm, d), row),
                pl.BlockSpec((1, 6, d), lambda i: (((b0 + i) * tm) // seq, 0, 0)),
                pl.BlockSpec((d, 2 * D_EXPERT), const),
                pl.BlockSpec((D_EXPERT, d), const),
                pl.BlockSpec((1, d), const),
                pl.BlockSpec((1, d), const)]
    args = [yg, gates, h2, x1, mod3, wsgu, wsd, lng, lnb]
    aliases = {}
    if prev_out is not None:
        in_specs.append(pl.BlockSpec(memory_space=pl.ANY))
        args.append(prev_out)
        aliases = {len(args) - 1: 0}
    return pl.pallas_call(
        _combine_kernel,
        grid=(tp // tm,),
        in_specs=in_specs,
        out_specs=pl.BlockSpec((tm, d), lambda i: (b0 + i, 0)),
        out_shape=jax.ShapeDtypeStruct((t, d), F32),
        input_output_aliases=aliases,
        compiler_params=_cparams(("arbitrary",)),
        name="combine",
    )(*args)


def _pad_lanes(v, fill=0.0):
    out = jnp.full((1, LANES), fill, F32)
    return out.at[0, :v.shape[0]].set(v.astype(F32))


def kernel(x, c, w_ada, b_ada, w_in, ssd_conv_w, ssd_conv_b, ssd_dt_bias, ssd_A_log, ssd_D,
           ssd_norm_w, w_ssd_out, sc_conv_w, w_sc_out, w_o, ln1_g, ln1_b, router_w,
           router_bias, w_gate, w_up, w_down, sh_gate, sh_up, sh_down, ln2_g, ln2_b):
    bsz, seq, d = x.shape
    t = bsz * seq
    depth = w_in.shape[0]
    tm_proj = min(1024, seq)
    tm_merge = min(512, seq)
    tm_route = min(1024, t)
    tm_comb = min(512, seq)
    bm_gmm = 1024
    n_groups = 2 if bsz % 2 == 0 else 1
    bg = bsz // n_groups
    tg = bg * seq
    tm_route = min(tm_route, tg)
    n_assign = tg * TOP_K

    e01 = (np.arange(LANES)[:, None] == (np.arange(SSD_INNER)[None, :] // SSD_HEAD_DIM))
    e01 = jnp.asarray(np.concatenate([e01, e01], axis=0), BF16)
    su = jnp.asarray(np.arange(tm_route)[:, None] < np.arange(tm_route)[None, :], BF16)

    xf = x.reshape(t, d)
    for l in range(depth):
        w_l = w_in[l]
        w_main = jnp.concatenate([w_l[:, :DT_COL0], w_l[:, DT_COL0 + SSD_HEADS:]],
                                 axis=1).astype(BF16)
        w_dt = jnp.pad(w_l[:, DT_COL0:DT_COL0 + SSD_HEADS],
                       ((0, 0), (0, LANES - SSD_HEADS))).astype(BF16)
        wgu = jnp.concatenate([w_gate[l], w_up[l]], axis=-1).astype(BF16)
        wd = w_down[l].astype(BF16)
        wsgu = jnp.concatenate([sh_gate[l], sh_up[l]], axis=-1).astype(BF16)
        wsd = sh_down[l].astype(BF16)
        wr = jnp.pad(router_w[l], ((0, 0), (0, LANES - N_EXPERTS)))
        wr_hi = wr.astype(BF16)
        wr = jnp.concatenate([wr_hi, (wr - wr_hi.astype(F32)).astype(BF16)], axis=1)
        bias_col = jnp.zeros((LANES, LANES), F32).at[:N_EXPERTS, 0].set(router_bias[l])
        dvec = jnp.repeat(ssd_D[l], SSD_HEAD_DIM).reshape(1, SSD_INNER)

        mod3 = _mod(c, w_ada[l], b_ada[l]).reshape(bsz, 6, d)
        x_in = xf

        def mix_and_route(g):
            p, dtp = _inproj(x_in, mod3, w_main, w_dt, seq, tm_proj, w_main.shape[1] // 4,
                             g * tg, tg)
            yn = _ssd(p, dtp, ssd_conv_w[l], ssd_conv_b[l].reshape(1, -1),
                      _pad_lanes(ssd_dt_bias[l]), _pad_lanes(ssd_A_log[l]), dvec,
                      ssd_norm_w[l].reshape(1, -1), e01, bg, seq, 2 * SSD_CHUNK)
            x1, h2, h2p, logits = _merge(yn, p, x_in, mod3, w_ssd_out[l].astype(BF16),
                                         w_sc_out[l].astype(BF16), w_o[l].astype(BF16),
                                         sc_conv_w[l], ln1_g[l].reshape(1, d),
                                         ln1_b[l].reshape(1, d), wr, bg, seq, tm_merge, g * bg)
            gates, dest, cnt = _route(logits, bias_col, su, tm_route)
            dest_both = jnp.concatenate([dest, dest + n_assign], axis=1)
            plan = _visit_plan(cnt[:, 0].astype(jnp.int32), n_assign // bm_gmm, bm_gmm)
            xin = _sc_scatter_rows(h2p.reshape(2 * tg, SC_ROW_WORDS), dest_both, 2 * n_assign)
            return dict(x1=x1, h2=h2, gates=gates, dest=dest, plan=plan, xin=xin)

        def experts(st):
            yb = _gmm(st["xin"].reshape(2, n_assign, SC_ROW_WORDS), wgu, wd, *st["plan"], bm_gmm)
            idx = st["dest"].reshape(1, n_assign)
            idx = jnp.concatenate([idx, idx + n_assign], axis=1)
            yg = _sc_gather_rows(yb.reshape(2 * n_assign, SC_ROW_WORDS), idx)
            return yg.reshape(2, TOP_K, tg, SC_ROW_WORDS)

        stages = [mix_and_route(g) for g in range(n_groups)]
        gathered = [experts(st) for st in stages]
        out = None
        for g, (st, yg) in enumerate(zip(stages, gathered)):
            out = _combine(yg, st["gates"], st["h2"], st["x1"], mod3, wsgu, wsd,
                           ln2_g[l].reshape(1, d), ln2_b[l].reshape(1, d), seq, tm_comb,
                           g * tg, t, out)
        xf = out
    return xf.reshape(bsz, seq, d)
```

```python
import jax
import jax.numpy as jnp
import numpy as np
from jax import lax
from jax.experimental import pallas as pl
from jax.experimental.pallas import tpu as pltpu
from jax.experimental.pallas import tpu_sc as plsc

F32 = jnp.float32
BF16 = jnp.bfloat16
U32 = jnp.uint32

LANES = 128
SUBLANES = 8
VMEM_LIMIT_BYTES = 56 * 1024 * 1024

D_MODEL = 1024
SSD_INNER = 2048
SSD_HEAD_DIM = 64
SSD_HEADS = 32
SSD_GROUPS = 8
SSD_STATE = 128
SSD_CONV = 4
SSD_CHUNK = 128
GROUP_W = SSD_INNER // SSD_GROUPS
HEADS_PER_GROUP = SSD_HEADS // SSD_GROUPS
SC_KERNEL = 3
N_EXPERTS = 64
TOP_K = 8
N_EXPERT_GROUPS = 8
TOPK_EXPERT_GROUPS = 4
EXPERTS_PER_GROUP = N_EXPERTS // N_EXPERT_GROUPS
D_EXPERT = 256
ROUTED_SCALE = 2.5
LN_EPS = 1e-5
RMS_EPS = 1e-5
ALPHA = 2.0 ** 0.25
DT_COL0 = 2048 + 4096
NEG_BIG = -1e30
SC_WINDOW = 128
SC_ROW_WORDS = 256
GMM_ROWS = 256
MERGE_ROWS = 256


def _cparams(sem):
    return pltpu.CompilerParams(dimension_semantics=sem,
                                vmem_limit_bytes=VMEM_LIMIT_BYTES)


def _ln(x):
    mu = jnp.mean(x, axis=-1, keepdims=True)
    xc = x - mu
    var = jnp.mean(xc * xc, axis=-1, keepdims=True)
    return xc * lax.rsqrt(var + LN_EPS)


def _silu(x):
    h = 0.5 * x
    return h + h * jnp.tanh(h)


def _split3(a):
    hi = a.astype(BF16)
    r1 = a - hi.astype(F32)
    mid = r1.astype(BF16)
    lo = (r1 - mid.astype(F32)).astype(BF16)
    return hi, mid, lo


def _dot(a, b):
    return jnp.dot(a, b, preferred_element_type=F32)


def _dot_exact01(a, m01):
    hi, mid, lo = _split3(a)
    return _dot(hi, m01) + _dot(mid, m01) + _dot(lo, m01)


def _dot01_exact(m01, a):
    hi, mid, lo = _split3(a)
    return _dot(m01, hi) + _dot(m01, mid) + _dot(m01, lo)


def _pack_pair(a, b):
    ab = pltpu.bitcast(a.astype(BF16).astype(F32), U32)
    bb = pltpu.bitcast(b.astype(BF16).astype(F32), U32)
    return ab | lax.shift_right_logical(bb, jnp.uint32(16))


def _unpack_pair(w):
    a = pltpu.bitcast(w & jnp.uint32(0xFFFF0000), F32)
    b = pltpu.bitcast(lax.shift_left(w, jnp.uint32(16)), F32)
    return a, b


def _mod_kernel(c_ref, w_ref, b_ref, o_ref):
    a = _silu(c_ref[...])
    o_ref[...] = jnp.dot(a, w_ref[...], precision=lax.Precision.HIGHEST,
                         preferred_element_type=F32) + b_ref[...]


def _mod(c, w, b):
    bsz, d = c.shape
    n = w.shape[1]
    tn = 1024
    return pl.pallas_call(
        _mod_kernel,
        grid=(n // tn,),
        in_specs=[pl.BlockSpec((bsz, d), lambda j: (0, 0)),
                  pl.BlockSpec((d, tn), lambda j: (0, j)),
                  pl.BlockSpec((1, tn), lambda j: (0, j))],
        out_specs=pl.BlockSpec((bsz, tn), lambda j: (0, j)),
        out_shape=jax.ShapeDtypeStruct((bsz, n), F32),
        compiler_params=_cparams(("arbitrary",)),
        name="mod",
    )(c, w, b.reshape(1, n))


def _inproj_kernel(x_ref, mod_ref, w_ref, wdt_ref, p_ref, dt_ref, h_scr):
    j = pl.program_id(1)

    @pl.when(j == 0)
    def _():
        m = mod_ref[0]
        h = _ln(x_ref[...]) * (1.0 + m[1:2, :]) + m[0:1, :]
        hb = h.astype(BF16)
        h_scr[...] = hb
        dt_ref[...] = _dot(hb, wdt_ref[...])

    p_ref[...] = _dot(h_scr[...], w_ref[...]).astype(p_ref.dtype)


def _inproj(x2d, mod3, w_main, w_dt, seq, tm, tn, tok0, t):
    d = x2d.shape[1]
    n = w_main.shape[1]
    i0 = tok0 // tm
    return pl.pallas_call(
        _inproj_kernel,
        grid=(t // tm, n // tn),
        in_specs=[pl.BlockSpec((tm, d), lambda i, j: (i0 + i, 0)),
                  pl.BlockSpec((1, 6, d), lambda i, j: (((i0 + i) * tm) // seq, 0, 0)),
                  pl.BlockSpec((d, tn), lambda i, j: (0, j)),
                  pl.BlockSpec((d, LANES), lambda i, j: (0, 0))],
        out_specs=[pl.BlockSpec((tm, tn), lambda i, j: (i, j)),
                   pl.BlockSpec((tm, LANES), lambda i, j: (i, 0))],
        out_shape=[jax.ShapeDtypeStruct((t, n), BF16),
                   jax.ShapeDtypeStruct((t, LANES), F32)],
        scratch_shapes=[pltpu.VMEM((tm, d), BF16)],
        compiler_params=_cparams(("arbitrary", "arbitrary")),
        name="inproj",
    )(x2d, mod3, w_main, w_dt)


def _shift_conv(cur, prev_scr, s_ref, w_ref, b_ref):
    rows = cur.shape[0]
    n_taps = w_ref.shape[0]
    shifted = _dot(s_ref[...], jnp.concatenate([prev_scr[...], cur], axis=0))
    acc = w_ref[n_taps - 1:n_taps, :] * cur.astype(F32) + b_ref[...]
    for k in range(n_taps - 1):
        acc = acc + w_ref[k:k + 1, :] * shifted[k * rows:(k + 1) * rows]
    prev_scr[...] = cur
    return acc


def _shift_matrix(rows, n_taps):
    s = np.zeros(((n_taps - 1) * rows, 2 * rows), np.float32)
    t = np.arange(rows)
    for k in range(n_taps - 1):
        s[k * rows + t, rows + t - (n_taps - 1) + k] = 1.0
    return jnp.asarray(s, BF16)


def _ssd_kernel(xs_ref, bc_ref, z_ref, dtp_ref, cw_xs_ref, cw_bc_ref, cb_xs_ref,
                cb_bc_ref, dtb_ref, alog_ref, dvec_ref, nw_ref, e_ref, s_ref,
                yn_ref, prev_xs, prev_bc, state):
    @pl.when(pl.program_id(1) == 0)
    def _():
        prev_xs[...] = jnp.zeros(prev_xs.shape, BF16)
        prev_bc[...] = jnp.zeros(prev_bc.shape, BF16)
        state[...] = jnp.zeros(state.shape, F32)

    for cc in range(xs_ref.shape[0] // SSD_CHUNK):
        _ssd_chunk(slice(cc * SSD_CHUNK, (cc + 1) * SSD_CHUNK), xs_ref, bc_ref, z_ref, dtp_ref,
                   cw_xs_ref, cw_bc_ref, cb_xs_ref, cb_bc_ref, dtb_ref, alog_ref, dvec_ref,
                   nw_ref, e_ref, s_ref, yn_ref, prev_xs, prev_bc, state)


def _ssd_chunk(rows, xs_ref, bc_ref, z_ref, dtp_ref, cw_xs_ref, cw_bc_ref, cb_xs_ref,
               cb_bc_ref, dtb_ref, alog_ref, dvec_ref, nw_ref, e_ref, s_ref,
               yn_ref, prev_xs, prev_bc, state):
    L = SSD_CHUNK
    xs = _silu(_shift_conv(xs_ref[rows, :], prev_xs, s_ref, cw_xs_ref, cb_xs_ref))
    bc = _silu(_shift_conv(bc_ref[rows, :], prev_bc, s_ref, cw_bc_ref, cb_bc_ref))

    x_dt = dtp_ref[rows, :] + dtb_ref[...]
    dt = jnp.maximum(x_dt, 0.0) + jnp.log1p(jnp.exp(-jnp.abs(x_dt)))
    a_neg = -jnp.exp(alog_ref[...])
    d_a = dt * a_neg
    row = lax.broadcasted_iota(jnp.int32, (L, L), 0)
    col = lax.broadcasted_iota(jnp.int32, (L, L), 1)
    causal = col <= row
    tri = jnp.where(causal, 1.0, 0.0).astype(BF16)
    hi, mid, lo = _split3(d_a)
    a_cs = _dot(tri, hi) + _dot(tri, mid) + _dot(tri, lo)
    a_cs_t = a_cs.T
    a_last = a_cs[L - 1:L, :]
    stack = jnp.concatenate([dt, jnp.exp(a_cs), jnp.exp(a_last - a_cs)], axis=0)
    s_hi = stack.astype(BF16)
    s_mid = (stack - s_hi.astype(F32)).astype(BF16)
    st_e = _dot(jnp.concatenate([s_hi, s_mid], axis=1), e_ref[...])
    dt_e = st_e[0:L]
    od_e = st_e[L:2 * L]
    ds_e = st_e[2 * L:3 * L]
    cd_e = od_e[L - 1:L, :]

    x_f = xs * dt_e
    x_b = x_f.astype(BF16)
    xd_b = (x_f * ds_e).astype(BF16)
    lane_head = lax.broadcasted_iota(jnp.int32, (L, GROUP_W), 1) // SSD_HEAD_DIM

    for g in range(SSD_GROUPS):
        gc = slice(g * GROUP_W, (g + 1) * GROUP_W)
        b_g = bc[:, g * SSD_STATE:(g + 1) * SSD_STATE]
        c_g = bc[:, SSD_GROUPS * SSD_STATE + g * SSD_STATE:
                 SSD_GROUPS * SSD_STATE + (g + 1) * SSD_STATE].astype(BF16)
        b_gt = b_g.T.astype(BF16)
        cb = _dot(c_g, b_gt)
        x_g = x_b[:, gc]
        lhs = []
        rhs = []
        for r in range(HEADS_PER_GROUP):
            h = g * HEADS_PER_GROUP + r
            seg = a_cs[:, h:h + 1] - a_cs_t[h:h + 1, :]
            lmat = jnp.exp(jnp.where(causal, seg, NEG_BIG))
            lhs.append((cb * lmat).astype(BF16))
            rhs.append(jnp.where(lane_head == r, x_g, jnp.zeros_like(x_g)))
        y_diag = _dot(jnp.concatenate(lhs, axis=1), jnp.concatenate(rhs, axis=0))
        st = state[g]
        y_off = _dot(c_g, st.astype(BF16)) * od_e[:, gc]
        state[g] = cd_e[:, gc] * st + _dot(b_gt, xd_b[:, gc])
        y = y_diag + y_off + dvec_ref[:, gc] * xs[:, gc]
        gt = y * _silu(z_ref[rows, gc].astype(F32))
        ms = jnp.mean(gt * gt, axis=-1, keepdims=True)
        yn_ref[rows, gc] = (gt * lax.rsqrt(ms + RMS_EPS) * nw_ref[:, gc]).astype(yn_ref.dtype)


def _ssd(p, dtp, cw, cb, dtb, alog, dvec, nw, e01, bsz, seq, rows):
    t = p.shape[0]
    L = SSD_CHUNK
    nc = seq // rows
    row_map = lambda b, c: (b * nc + c, 0)

    def pcol(k):
        return pl.BlockSpec((rows, SSD_INNER), lambda b, c: (b * nc + c, k))

    const = lambda b, c: (0, 0)
    return pl.pallas_call(
        _ssd_kernel,
        grid=(bsz, nc),
        in_specs=[pcol(1), pcol(2), pcol(0),
                  pl.BlockSpec((rows, LANES), row_map),
                  pl.BlockSpec((SSD_CONV, SSD_INNER), lambda b, c: (0, 0)),
                  pl.BlockSpec((SSD_CONV, SSD_INNER), lambda b, c: (0, 1)),
                  pl.BlockSpec((1, SSD_INNER), lambda b, c: (0, 0)),
                  pl.BlockSpec((1, SSD_INNER), lambda b, c: (0, 1)),
                  pl.BlockSpec((1, LANES), const),
                  pl.BlockSpec((1, LANES), const),
                  pl.BlockSpec((1, SSD_INNER), const),
                  pl.BlockSpec((1, SSD_INNER), const),
                  pl.BlockSpec((2 * LANES, SSD_INNER), const),
                  pl.BlockSpec(((SSD_CONV - 1) * L, 2 * L), const)],
        out_specs=pl.BlockSpec((rows, SSD_INNER), row_map),
        out_shape=jax.ShapeDtypeStruct((t, SSD_INNER), BF16),
        scratch_shapes=[pltpu.VMEM((L, SSD_INNER), BF16),
                        pltpu.VMEM((L, SSD_INNER), BF16),
                        pltpu.VMEM((SSD_GROUPS, SSD_STATE, GROUP_W), F32)],
        compiler_params=_cparams(("arbitrary", "arbitrary")),
        name="ssd",
    )(p, p, p, dtp, cw, cw, cb, cb, dtb, alog, dvec, nw, e01, _shift_matrix(L, SSD_CONV))


def _merge_kernel(yn_ref, scb_ref, scc_ref, sch_ref, ga_ref, gb_ref, x_ref, mod_ref,
                  wssd_ref, wsc_ref, wo_ref, cw_ref, lng_ref, lnb_ref, wr_ref,
                  x1_ref, h2_ref, h2p_ref, lg_ref, ext):
    s = pl.program_id(1)
    tm = x_ref.shape[0]

    @pl.when(s == 0)
    def _():
        ext[0:SUBLANES, :] = jnp.zeros((SUBLANES, D_MODEL), F32)

    ext[SUBLANES:SUBLANES + tm, :] = scc_ref[...].astype(F32) * sch_ref[...].astype(F32)
    m = mod_ref[0]
    for c in range(tm // MERGE_ROWS):
        r0 = c * MERGE_ROWS
        rows = slice(r0, r0 + MERGE_ROWS)
        u = None
        for k in range(SC_KERNEL):
            start = SUBLANES + r0 - (SC_KERNEL - 1) + k
            term = cw_ref[k:k + 1, :] * ext[start:start + MERGE_ROWS, :]
            u = term if u is None else u + term
        y_b = _dot((scb_ref[rows, :].astype(F32) * u).astype(BF16), wsc_ref[...])
        y_a = _dot(yn_ref[rows, :], wssd_ref[...])
        merged = (jax.nn.sigmoid(ga_ref[rows, :].astype(F32)) * y_a
                  + jax.nn.sigmoid(gb_ref[rows, :].astype(F32)) * y_b)
        mix = _dot(merged.astype(BF16), wo_ref[...])
        x1 = _ln(ALPHA * x_ref[rows, :] + m[2:3, :] * mix) * lng_ref[...] + lnb_ref[...]
        x1_ref[rows, :] = x1
        h2 = _ln(x1) * (1.0 + m[4:5, :]) + m[3:4, :]
        h2_hi = h2.astype(BF16)
        h2_ref[rows, :] = h2_hi
        w = _pack_pair(h2[:, 0:D_MODEL // 2], h2[:, D_MODEL // 2:D_MODEL])
        h2p_ref[0, rows, :] = w[:, 0:SC_ROW_WORDS]
        h2p_ref[1, rows, :] = w[:, SC_ROW_WORDS:2 * SC_ROW_WORDS]
        h2_lo = (h2 - h2_hi.astype(F32)).astype(BF16)
        both = _dot(h2_hi, wr_ref[...])
        lg_ref[rows, :] = (both[:, 0:LANES] + both[:, LANES:2 * LANES]
                           + _dot(h2_lo, wr_ref[:, 0:LANES]))
    ext[0:SUBLANES, :] = ext[tm:tm + SUBLANES, :]


def _merge(yn, p, x2d, mod3, wssd, wsc, wo, cw, lng, lnb, wr, bsz, seq, tm, b0):
    t = bsz * seq
    d = x2d.shape[1]
    ns = seq // tm
    row_map = lambda b, s: (b * ns + s, 0)
    const = lambda b, s: (0, 0)
    col0 = (2048 + 4096) // d

    def pcol(k):
        return pl.BlockSpec((tm, d), lambda b, s: (b * ns + s, col0 + k))

    return pl.pallas_call(
        _merge_kernel,
        grid=(bsz, ns),
        in_specs=[pl.BlockSpec((tm, SSD_INNER), row_map),
                  pcol(0), pcol(1), pcol(2), pcol(3), pcol(4),
                  pl.BlockSpec((tm, d), lambda b, s: ((b0 + b) * ns + s, 0)),
                  pl.BlockSpec((1, 6, d), lambda b, s: (b0 + b, 0, 0)),
                  pl.BlockSpec((SSD_INNER, d), const, pipeline_mode=pl.Buffered(1)),
                  pl.BlockSpec((d, d), const, pipeline_mode=pl.Buffered(1)),
                  pl.BlockSpec((d, d), const, pipeline_mode=pl.Buffered(1)),
                  pl.BlockSpec((SC_KERNEL, d), const),
                  pl.BlockSpec((1, d), const),
                  pl.BlockSpec((1, d), const),
                  pl.BlockSpec((d, 2 * LANES), const, pipeline_mode=pl.Buffered(1))],
        out_specs=[pl.BlockSpec((tm, d), row_map),
                   pl.BlockSpec((tm, d), row_map),
                   pl.BlockSpec((2, tm, SC_ROW_WORDS), lambda b, s: (0, b * ns + s, 0)),
                   pl.BlockSpec((tm, LANES), row_map)],
        out_shape=[jax.ShapeDtypeStruct((t, d), F32),
                   jax.ShapeDtypeStruct((t, d), BF16),
                   jax.ShapeDtypeStruct((2, t, SC_ROW_WORDS), U32),
                   jax.ShapeDtypeStruct((t, LANES), F32)],
        scratch_shapes=[pltpu.VMEM((tm + SUBLANES, d), F32)],
        compiler_params=_cparams(("arbitrary", "arbitrary")),
        name="merge",
    )(yn, p, p, p, p, p, x2d, mod3, wssd, wsc, wo, cw, lng, lnb, wr)


def _first_argmax_mask(v, idx, n):
    m = jnp.max(v, axis=0, keepdims=True)
    first = jnp.min(jnp.where(v == m, idx, n), axis=0, keepdims=True)
    return idx == first, m


def _route_kernel(lg_ref, bias_ref, su_ref, g_ref, dest_ref, cnt_ref, counts, carry, start):
    p = pl.program_id(0)
    i = pl.program_id(1)
    tm = lg_ref.shape[0]
    ne, ng, eg = N_EXPERTS, N_EXPERT_GROUPS, EXPERTS_PER_GROUP
    lt = lg_ref[...].T[0:ne, :]
    scores = jax.nn.sigmoid(lt)
    biased = scores + bias_ref[...][0:ne, 0:1]
    grp = biased.reshape(ng, eg, tm)
    idx_e = lax.broadcasted_iota(jnp.int32, (ng, eg, tm), 1)
    m1 = jnp.max(grp, axis=1, keepdims=True)
    first = jnp.min(jnp.where(grp == m1, idx_e, eg), axis=1, keepdims=True)
    m2 = jnp.max(jnp.where(idx_e == first, NEG_BIG, grp), axis=1, keepdims=True)
    gscore = m1 + m2
    idx_g = lax.broadcasted_iota(jnp.int32, (ng, 1, tm), 0)
    gsel = jnp.zeros((ng, 1, tm), F32)
    for _ in range(TOPK_EXPERT_GROUPS):
        hit, _m = _first_argmax_mask(gscore, idx_g, ng)
        gsel = jnp.where(hit, 1.0, gsel)
        gscore = jnp.where(hit, NEG_BIG, gscore)
    emask = jnp.broadcast_to(gsel, (ng, eg, tm)).reshape(ne, tm)
    cand = jnp.where(emask > 0.0, biased, NEG_BIG)
    idx_x = lax.broadcasted_iota(jnp.int32, (ne, tm), 0)
    sel = jnp.zeros((ne, tm), F32)
    picks = []
    for _ in range(TOP_K):
        hit, _m = _first_argmax_mask(cand, idx_x, ne)
        sel = jnp.where(hit, 1.0, sel)
        cand = jnp.where(hit, NEG_BIG, cand)
        picks.append(hit)
    n_e = jnp.broadcast_to(jnp.sum(sel, axis=1, keepdims=True), (ne, LANES))

    @pl.when((p == 0) & (i == 0))
    def _():
        counts[...] = jnp.zeros(counts.shape, F32)

    @pl.when(p == 0)
    def _():
        counts[...] += n_e

    @pl.when((p == 1) & (i == 0))
    def _():
        r = lax.broadcasted_iota(jnp.int32, (ne, ne), 0)
        cc = lax.broadcasted_iota(jnp.int32, (ne, ne), 1)
        below = jnp.where(cc < r, 1.0, 0.0).astype(BF16)
        start[...] = _dot01_exact(below, counts[...])
        carry[...] = jnp.zeros(carry.shape, F32)
        cnt_ref[...] = counts[...]

    @pl.when(p == 1)
    def _():
        before = _dot(sel.astype(BF16), su_ref[...])
        slot = before + carry[:, 0:1] + start[:, 0:1]
        dest_rows = []
        gate_rows = []
        for hit in picks:
            dest_rows.append(jnp.sum(jnp.where(hit, slot, 0.0), axis=0, keepdims=True))
            gate_rows.append(jnp.sum(jnp.where(hit, scores, 0.0), axis=0, keepdims=True))
        dest_ref[...] = jnp.concatenate(dest_rows, axis=0).astype(jnp.int32)
        gate = jnp.concatenate(gate_rows, axis=0)
        gate = gate / jnp.sum(gate, axis=0, keepdims=True) * ROUTED_SCALE
        pad = jnp.zeros((LANES - TOP_K, tm), F32)
        g_ref[...] = jnp.concatenate([gate, pad], axis=0).T
        carry[...] += n_e


def _route(logits, bias_col, su, tm):
    t = logits.shape[0]
    return pl.pallas_call(
        _route_kernel,
        grid=(2, t // tm),
        in_specs=[pl.BlockSpec((tm, LANES), lambda p, i: (i, 0)),
                  pl.BlockSpec((LANES, LANES), lambda p, i: (0, 0)),
                  pl.BlockSpec((tm, tm), lambda p, i: (0, 0))],
        out_specs=[pl.BlockSpec((tm, LANES), lambda p, i: (i * p, 0)),
                   pl.BlockSpec((TOP_K, tm), lambda p, i: (0, i * p)),
                   pl.BlockSpec((N_EXPERTS, LANES), lambda p, i: (0, 0))],
        out_shape=[jax.ShapeDtypeStruct((t, LANES), F32),
                   jax.ShapeDtypeStruct((TOP_K, t), jnp.int32),
                   jax.ShapeDtypeStruct((N_EXPERTS, LANES), F32)],
        scratch_shapes=[pltpu.VMEM((N_EXPERTS, LANES), F32),
                        pltpu.VMEM((N_EXPERTS, LANES), F32),
                        pltpu.VMEM((N_EXPERTS, LANES), F32)],
        compiler_params=_cparams(("arbitrary", "arbitrary")),
        name="route",
    )(logits, bias_col, su)


def _sc_mesh():
    return plsc.VectorSubcoreMesh(core_axis_name="c", subcore_axis_name="s")


def _sc_scatter_rows(rows, idx, n_out):
    n_rows, w = rows.shape
    n_k = idx.shape[0]

    @pl.kernel(out_type=jax.ShapeDtypeStruct((n_out, w), rows.dtype), mesh=_sc_mesh(),
               scratch_types=[])
    def scatter(x_hbm, i_hbm, o_hbm):
        def body(x_vmem, i_vmem):
            for k in range(n_k):
                pltpu.sync_copy(x_vmem, o_hbm.at[i_vmem.at[k]])

        pltpu.emit_pipeline(
            body,
            grid=(n_rows // SC_WINDOW,),
            in_specs=[pl.BlockSpec((SC_WINDOW, w), index_map=lambda i: (i, 0)),
                      pl.BlockSpec((n_k, SC_WINDOW), index_map=lambda i: (0, i))],
            out_specs=[],
            core_axis_name=("c", "s"),
            dimension_semantics=(pltpu.PARALLEL,),
        )(x_hbm, i_hbm)

    return scatter(rows, idx)


def _sc_gather_rows(rows, idx):
    n = idx.shape[1]
    w = rows.shape[1]

    @pl.kernel(out_type=jax.ShapeDtypeStruct((n, w), rows.dtype), mesh=_sc_mesh(),
               scratch_types=[])
    def gather(x_hbm, i_hbm, o_hbm):
        def body(i_vmem, o_vmem):
            pltpu.sync_copy(x_hbm.at[i_vmem.at[0]], o_vmem)

        pltpu.emit_pipeline(
            body,
            grid=(n // SC_WINDOW,),
            in_specs=[pl.BlockSpec((1, SC_WINDOW), index_map=lambda i: (0, i))],
            out_specs=[pl.BlockSpec((SC_WINDOW, w), index_map=lambda i: (i, 0))],
            core_axis_name=("c", "s"),
            dimension_semantics=(pltpu.PARALLEL,),
        )(i_hbm, o_hbm)

    return gather(rows, idx)


def _gmm_kernel(vb_ref, ve_ref, vlo_ref, vhi_ref, vnew_ref, x_ref, wg_ref, wu_ref, wdn_ref,
                o_ref, wgu_ref, wd_ref):
    v = pl.program_id(0)
    lo = vlo_ref[v]
    hi = vhi_ref[v]
    bm = x_ref.shape[1]
    half = D_MODEL // 2

    @pl.when(vnew_ref[v] == 1)
    def _():
        wgu_ref[:, 0:D_EXPERT] = wg_ref[0].astype(BF16)
        wgu_ref[:, D_EXPERT:2 * D_EXPERT] = wu_ref[0].astype(BF16)
        wd_ref[...] = wdn_ref[0].astype(BF16)

    def expert_rows(rows):
        a0, b0 = _unpack_pair(x_ref[0, rows, :])
        a1, b1 = _unpack_pair(x_ref[1, rows, :])
        x = jnp.concatenate([a0.astype(BF16), a1.astype(BF16),
                             b0.astype(BF16), b1.astype(BF16)], axis=1)
        a = _dot(x, wgu_ref[...])
        act = _silu(a[:, 0:D_EXPERT]) * a[:, D_EXPERT:2 * D_EXPERT]
        y = _dot(act.astype(BF16), wd_ref[...])
        return _pack_pair(y[:, 0:half], y[:, half:D_MODEL])

    def write_rows(rows):
        w = expert_rows(rows)
        o_ref[0, rows, :] = w[:, 0:SC_ROW_WORDS]
        o_ref[1, rows, :] = w[:, SC_ROW_WORDS:half]

    @pl.when((lo == 0) & (hi == bm))
    def _():
        for c in range(bm // GMM_ROWS):
            write_rows(slice(c * GMM_ROWS, (c + 1) * GMM_ROWS))

    @pl.when((lo == 0) & (hi > 0) & (hi < bm))
    def _():
        for c in range(bm // GMM_ROWS):
            @pl.when(hi > c * GMM_ROWS)
            def _():
                write_rows(slice(c * GMM_ROWS, (c + 1) * GMM_ROWS))

    @pl.when((hi > lo) & (lo > 0))
    def _():
        for c in range(bm // GMM_ROWS):
            rows = slice(c * GMM_ROWS, (c + 1) * GMM_ROWS)

            @pl.when((hi > c * GMM_ROWS) & (lo < (c + 1) * GMM_ROWS))
            def _():
                w = expert_rows(rows)
                r = lax.broadcasted_iota(jnp.int32, (GMM_ROWS, SC_ROW_WORDS), 0) + c * GMM_ROWS
                mine = (r >= lo) & (r < hi)
                o_ref[0, rows, :] = jnp.where(mine, w[:, 0:SC_ROW_WORDS], o_ref[0, rows, :])
                o_ref[1, rows, :] = jnp.where(mine, w[:, SC_ROW_WORDS:half], o_ref[1, rows, :])


def _gmm(xin, w_gate, w_up, w_down, vb, ve, vlo, vhi, vnew, bm):
    _, n_rows, w = xin.shape
    d = D_MODEL
    n_visits = vb.shape[0]
    blk = lambda v, vb, ve, vlo, vhi, vnew: (0, vb[v], 0)
    wsel = lambda v, vb, ve, vlo, vhi, vnew: (ve[v], 0, 0)
    grid_spec = pltpu.PrefetchScalarGridSpec(
        num_scalar_prefetch=5,
        grid=(n_visits,),
        in_specs=[pl.BlockSpec((2, bm, w), blk),
                  pl.BlockSpec((1, d, D_EXPERT), wsel),
                  pl.BlockSpec((1, d, D_EXPERT), wsel),
                  pl.BlockSpec((1, D_EXPERT, d), wsel)],
        out_specs=pl.BlockSpec((2, bm, w), blk),
        scratch_shapes=[pltpu.VMEM((d, 2 * D_EXPERT), BF16),
                        pltpu.VMEM((D_EXPERT, d), BF16)],
    )
    return pl.pallas_call(
        _gmm_kernel,
        grid_spec=grid_spec,
        out_shape=jax.ShapeDtypeStruct((2, n_rows, w), U32),
        compiler_params=_cparams(("arbitrary",)),
        name="gmm",
    )(vb, ve, vlo, vhi, vnew, xin, w_gate, w_up, w_down)


def _visit_plan(counts, n_blocks, bm):
    ne = counts.shape[0]
    n_visits = n_blocks + ne
    end = jnp.cumsum(counts)
    start = end - counts
    b0 = start // bm
    nb = jnp.where(counts > 0, (end - 1) // bm - b0 + 1, 0)
    vend = jnp.cumsum(nb)
    vstart = vend - nb
    total = vend[-1]
    v = jnp.arange(n_visits, dtype=jnp.int32)
    valid = v < total
    vc = jnp.minimum(v, total - 1)
    e_v = jnp.minimum(jnp.sum((vend[None, :] <= vc[:, None]).astype(jnp.int32), axis=1), ne - 1)
    onehot = (e_v[:, None] == jnp.arange(ne, dtype=jnp.int32)[None, :]).astype(jnp.int32)
    pick = lambda tab: jnp.sum(onehot * tab[None, :].astype(jnp.int32), axis=1)
    blk = pick(b0) + vc - pick(vstart)
    lo = jnp.clip(pick(start) - blk * bm, 0, bm)
    hi = jnp.clip(pick(end) - blk * bm, 0, bm)
    lo = jnp.where(valid, lo, 0)
    hi = jnp.where(valid, hi, 0)
    i32 = lambda a: a.astype(jnp.int32)
    e_v = i32(e_v)
    new = jnp.concatenate([jnp.ones((1,), jnp.int32), i32(e_v[1:] != e_v[:-1])])
    return i32(blk), e_v, i32(lo), i32(hi), new


def _combine_kernel(yg_ref, g_ref, h_ref, x1_ref, mod_ref, wsgu_ref, wsd_ref, lng_ref,
                    lnb_ref, *rest):
    o_ref = rest[-1]
    q = D_MODEL // 4
    g = g_ref[...]
    a = _dot(h_ref[...], wsgu_ref[...])
    act = _silu(a[:, 0:D_EXPERT]) * a[:, D_EXPERT:2 * D_EXPERT]
    shared = _dot(act.astype(BF16), wsd_ref[...])
    f = [shared[:, j * q:(j + 1) * q] for j in range(4)]
    for k in range(TOP_K):
        a0, b0 = _unpack_pair(yg_ref[0, k])
        a1, b1 = _unpack_pair(yg_ref[1, k])
        gk = g[:, k:k + 1]
        f = [f[0] + gk * a0, f[1] + gk * a1, f[2] + gk * b0, f[3] + gk * b1]
    ffn = jnp.concatenate(f, axis=1)
    m = mod_ref[0]
    r = ALPHA * x1_ref[...] + m[5:6, :] * ffn
    o_ref[...] = _ln(r) * lng_ref[...] + lnb_ref[...]


def _combine(yg, gates, h2, x1, mod3, wsgu, wsd, lng, lnb, seq, tm, tok0, t, prev_out):
    tp, d = x1.shape
    b0 = tok0 // tm
    row = lambda i: (i, 0)
    const = lambda i: (0, 0)
    in_specs = [pl.BlockSpec((2, TOP_K, tm, SC_ROW_WORDS), lambda i: (0, 0, i, 0)),
                pl.BlockSpec((tm, LANES), row),
                pl.BlockSpec((tm, d), row),
                pl.BlockSpec((tm, d), row),
                pl.BlockSpec((1, 6, d), lambda i: (((b0 + i) * tm) // seq, 0, 0)),
                pl.BlockSpec((d, 2 * D_EXPERT), const),
                pl.BlockSpec((D_EXPERT, d), const),
                pl.BlockSpec((1, d), const),
                pl.BlockSpec((1, d), const)]
    args = [yg, gates, h2, x1, mod3, wsgu, wsd, lng, lnb]
    aliases = {}
    if prev_out is not None:
        in_specs.append(pl.BlockSpec(memory_space=pl.ANY))
        args.append(prev_out)
        aliases = {len(args) - 1: 0}
    return pl.pallas_call(
        _combine_kernel,
        grid=(tp // tm,),
        in_specs=in_specs,
        out_specs=pl.BlockSpec((tm, d), lambda i: (b0 + i, 0)),
        out_shape=jax.ShapeDtypeStruct((t, d), F32),
        input_output_aliases=aliases,
        compiler_params=_cparams(("arbitrary",)),
        name="combine",
    )(*args)


def _pad_lanes(v, fill=0.0):
    out = jnp.full((1, LANES), fill, F32)
    return out.at[0, :v.shape[0]].set(v.astype(F32))


def kernel(x, c, w_ada, b_ada, w_in, ssd_conv_w, ssd_conv_b, ssd_dt_bias, ssd_A_log, ssd_D,
           ssd_norm_w, w_ssd_out, sc_conv_w, w_sc_out, w_o, ln1_g, ln1_b, router_w,
           router_bias, w_gate, w_up, w_down, sh_gate, sh_up, sh_down, ln2_g, ln2_b):
    bsz, seq, d = x.shape
    t = bsz * seq
    depth = w_in.shape[0]
    tm_proj = min(1024, seq)
    tm_merge = min(512, seq)
    tm_route = min(1024, t)
    tm_comb = min(512, seq)
    bm_gmm = 1024
    n_groups = 2 if bsz % 2 == 0 else 1
    bg = bsz // n_groups
    tg = bg * seq
    tm_route = min(tm_route, tg)
    n_assign = tg * TOP_K

    e01 = (np.arange(LANES)[:, None] == (np.arange(SSD_INNER)[None, :] // SSD_HEAD_DIM))
    e01 = jnp.asarray(np.concatenate([e01, e01], axis=0), BF16)
    su = jnp.asarray(np.arange(tm_route)[:, None] < np.arange(tm_route)[None, :], BF16)

    xf = x.reshape(t, d)
    for l in range(depth):
        w_l = w_in[l]
        w_main = jnp.concatenate([w_l[:, :DT_COL0], w_l[:, DT_COL0 + SSD_HEADS:]],
                                 axis=1).astype(BF16)
        w_dt = jnp.pad(w_l[:, DT_COL0:DT_COL0 + SSD_HEADS],
                       ((0, 0), (0, LANES - SSD_HEADS))).astype(BF16)
        wsgu = jnp.concatenate([sh_gate[l], sh_up[l]], axis=-1).astype(BF16)
        wsd = sh_down[l].astype(BF16)
        wr = jnp.pad(router_w[l], ((0, 0), (0, LANES - N_EXPERTS)))
        wr_hi = wr.astype(BF16)
        wr = jnp.concatenate([wr_hi, (wr - wr_hi.astype(F32)).astype(BF16)], axis=1)
        bias_col = jnp.zeros((LANES, LANES), F32).at[:N_EXPERTS, 0].set(router_bias[l])
        dvec = jnp.repeat(ssd_D[l], SSD_HEAD_DIM).reshape(1, SSD_INNER)

        mod3 = _mod(c, w_ada[l], b_ada[l]).reshape(bsz, 6, d)
        x_in = xf

        def mix_and_route(g):
            p, dtp = _inproj(x_in, mod3, w_main, w_dt, seq, tm_proj, w_main.shape[1] // 4,
                             g * tg, tg)
            yn = _ssd(p, dtp, ssd_conv_w[l], ssd_conv_b[l].reshape(1, -1),
                      _pad_lanes(ssd_dt_bias[l]), _pad_lanes(ssd_A_log[l]), dvec,
                      ssd_norm_w[l].reshape(1, -1), e01, bg, seq, 2 * SSD_CHUNK)
            x1, h2, h2p, logits = _merge(yn, p, x_in, mod3, w_ssd_out[l].astype(BF16),
                                         w_sc_out[l].astype(BF16), w_o[l].astype(BF16),
                                         sc_conv_w[l], ln1_g[l].reshape(1, d),
                                         ln1_b[l].reshape(1, d), wr, bg, seq, tm_merge, g * bg)
            gates, dest, cnt = _route(logits, bias_col, su, tm_route)
            dest_both = jnp.concatenate([dest, dest + n_assign], axis=1)
            plan = _visit_plan(cnt[:, 0].astype(jnp.int32), n_assign // bm_gmm, bm_gmm)
            xin = _sc_scatter_rows(h2p.reshape(2 * tg, SC_ROW_WORDS), dest_both, 2 * n_assign)
            return dict(x1=x1, h2=h2, gates=gates, dest=dest, plan=plan, xin=xin)

        def experts(st):
            yb = _gmm(st["xin"].reshape(2, n_assign, SC_ROW_WORDS), w_gate[l], w_up[l], w_down[l],
                      *st["plan"], bm_gmm)
            idx = st["dest"].reshape(1, n_assign)
            idx = jnp.concatenate([idx, idx + n_assign], axis=1)
            yg = _sc_gather_rows(yb.reshape(2 * n_assign, SC_ROW_WORDS), idx)
            return yg.reshape(2, TOP_K, tg, SC_ROW_WORDS)

        stages = [mix_and_route(g) for g in range(n_groups)]
        gathered = [experts(st) for st in stages]
        out = None
        for g, (st, yg) in enumerate(zip(stages, gathered)):
            out = _combine(yg, st["gates"], st["h2"], st["x1"], mod3, wsgu, wsd,
                           ln2_g[l].reshape(1, d), ln2_b[l].reshape(1, d), seq, tm_comb,
                           g * tg, t, out)
        xf = out
    return xf.reshape(bsz, seq, d)
```

```python
import jax
import jax.numpy as jnp
import numpy as np
from jax import lax
from jax.experimental import pallas as pl
from jax.experimental.pallas import tpu as pltpu
from jax.experimental.pallas import tpu_sc as plsc

F32 = jnp.float32
BF16 = jnp.bfloat16
U32 = jnp.uint32

LANES = 128
SUBLANES = 8
VMEM_LIMIT_BYTES = 56 * 1024 * 1024

D_MODEL = 1024
SSD_INNER = 2048
SSD_HEAD_DIM = 64
SSD_HEADS = 32
SSD_GROUPS = 8
SSD_STATE = 128
SSD_CONV = 4
SSD_CHUNK = 128
GROUP_W = SSD_INNER // SSD_GROUPS
HEADS_PER_GROUP = SSD_HEADS // SSD_GROUPS
SC_KERNEL = 3
N_EXPERTS = 64
TOP_K = 8
N_EXPERT_GROUPS = 8
TOPK_EXPERT_GROUPS = 4
EXPERTS_PER_GROUP = N_EXPERTS // N_EXPERT_GROUPS
D_EXPERT = 256
ROUTED_SCALE = 2.5
LN_EPS = 1e-5
RMS_EPS = 1e-5
ALPHA = 2.0 ** 0.25
DT_COL0 = 2048 + 4096
NEG_BIG = -1e30
SC_WINDOW = 128
SC_ROW_WORDS = 256
GMM_BLOCK = 1024
GMM_ROWS = 256
MERGE_ROWS = 256


def _cparams(sem):
    return pltpu.CompilerParams(dimension_semantics=sem,
                                vmem_limit_bytes=VMEM_LIMIT_BYTES)


def _ln(x):
    mu = jnp.mean(x, axis=-1, keepdims=True)
    xc = x - mu
    var = jnp.mean(xc * xc, axis=-1, keepdims=True)
    return xc * lax.rsqrt(var + LN_EPS)


def _silu(x):
    h = 0.5 * x
    return h + h * jnp.tanh(h)


def _split3(a):
    hi = a.astype(BF16)
    r1 = a - hi.astype(F32)
    mid = r1.astype(BF16)
    lo = (r1 - mid.astype(F32)).astype(BF16)
    return hi, mid, lo


def _dot(a, b):
    return jnp.dot(a, b, preferred_element_type=F32)


def _dot_exact01(a, m01):
    hi, mid, lo = _split3(a)
    return _dot(hi, m01) + _dot(mid, m01) + _dot(lo, m01)


def _dot01_exact(m01, a):
    hi, mid, lo = _split3(a)
    return _dot(m01, hi) + _dot(m01, mid) + _dot(m01, lo)


def _pack_pair(a, b):
    ab = pltpu.bitcast(a.astype(BF16).astype(F32), U32)
    bb = pltpu.bitcast(b.astype(BF16).astype(F32), U32)
    return ab | lax.shift_right_logical(bb, jnp.uint32(16))


def _unpack_pair(w):
    a = pltpu.bitcast(w & jnp.uint32(0xFFFF0000), F32)
    b = pltpu.bitcast(lax.shift_left(w, jnp.uint32(16)), F32)
    return a, b


def _mod_kernel(c_ref, w_ref, b_ref, o_ref):
    a = _silu(c_ref[...])
    o_ref[...] = jnp.dot(a, w_ref[...], precision=lax.Precision.HIGHEST,
                         preferred_element_type=F32) + b_ref[...]


def _mod(c, w, b):
    bsz, d = c.shape
    n = w.shape[1]
    tn = 1024
    return pl.pallas_call(
        _mod_kernel,
        grid=(n // tn,),
        in_specs=[pl.BlockSpec((bsz, d), lambda j: (0, 0)),
                  pl.BlockSpec((d, tn), lambda j: (0, j)),
                  pl.BlockSpec((1, tn), lambda j: (0, j))],
        out_specs=pl.BlockSpec((bsz, tn), lambda j: (0, j)),
        out_shape=jax.ShapeDtypeStruct((bsz, n), F32),
        compiler_params=_cparams(("arbitrary",)),
        name="mod",
    )(c, w, b.reshape(1, n))


def _inproj_kernel(x_ref, mod_ref, w_ref, wdt_ref, p_ref, dt_ref, h_scr):
    j = pl.program_id(1)

    @pl.when(j == 0)
    def _():
        m = mod_ref[0]
        h = _ln(x_ref[...]) * (1.0 + m[1:2, :]) + m[0:1, :]
        hb = h.astype(BF16)
        h_scr[...] = hb
        dt_ref[...] = _dot(hb, wdt_ref[...])

    p_ref[...] = _dot(h_scr[...], w_ref[...]).astype(p_ref.dtype)


def _inproj(x2d, mod3, w_main, w_dt, seq, tm, tn, tok0, t):
    d = x2d.shape[1]
    n = w_main.shape[1]
    i0 = tok0 // tm
    return pl.pallas_call(
        _inproj_kernel,
        grid=(t // tm, n // tn),
        in_specs=[pl.BlockSpec((tm, d), lambda i, j: (i0 + i, 0)),
                  pl.BlockSpec((1, 6, d), lambda i, j: (((i0 + i) * tm) // seq, 0, 0)),
                  pl.BlockSpec((d, tn), lambda i, j: (0, j)),
                  pl.BlockSpec((d, LANES), lambda i, j: (0, 0))],
        out_specs=[pl.BlockSpec((tm, tn), lambda i, j: (i, j)),
                   pl.BlockSpec((tm, LANES), lambda i, j: (i, 0))],
        out_shape=[jax.ShapeDtypeStruct((t, n), BF16),
                   jax.ShapeDtypeStruct((t, LANES), F32)],
        scratch_shapes=[pltpu.VMEM((tm, d), BF16)],
        compiler_params=_cparams(("arbitrary", "arbitrary")),
        name="inproj",
    )(x2d, mod3, w_main, w_dt)


def _shift_conv(cur, prev_scr, s_ref, w_ref, b_ref):
    rows = cur.shape[0]
    n_taps = w_ref.shape[0]
    shifted = _dot(s_ref[...], jnp.concatenate([prev_scr[...], cur], axis=0))
    acc = w_ref[n_taps - 1:n_taps, :] * cur.astype(F32) + b_ref[...]
    for k in range(n_taps - 1):
        acc = acc + w_ref[k:k + 1, :] * shifted[k * rows:(k + 1) * rows]
    prev_scr[...] = cur
    return acc


def _shift_matrix(rows, n_taps):
    s = np.zeros(((n_taps - 1) * rows, 2 * rows), np.float32)
    t = np.arange(rows)
    for k in range(n_taps - 1):
        s[k * rows + t, rows + t - (n_taps - 1) + k] = 1.0
    return jnp.asarray(s, BF16)


def _ssd_kernel(xs_ref, bc_ref, z_ref, dtp_ref, cw_xs_ref, cw_bc_ref, cb_xs_ref,
                cb_bc_ref, dtb_ref, alog_ref, dvec_ref, nw_ref, e_ref, s_ref,
                yn_ref, prev_xs, prev_bc, state):
    @pl.when(pl.program_id(1) == 0)
    def _():
        prev_xs[...] = jnp.zeros(prev_xs.shape, BF16)
        prev_bc[...] = jnp.zeros(prev_bc.shape, BF16)
        state[...] = jnp.zeros(state.shape, F32)

    for cc in range(xs_ref.shape[0] // SSD_CHUNK):
        _ssd_chunk(slice(cc * SSD_CHUNK, (cc + 1) * SSD_CHUNK), xs_ref, bc_ref, z_ref, dtp_ref,
                   cw_xs_ref, cw_bc_ref, cb_xs_ref, cb_bc_ref, dtb_ref, alog_ref, dvec_ref,
                   nw_ref, e_ref, s_ref, yn_ref, prev_xs, prev_bc, state)


def _ssd_chunk(rows, xs_ref, bc_ref, z_ref, dtp_ref, cw_xs_ref, cw_bc_ref, cb_xs_ref,
               cb_bc_ref, dtb_ref, alog_ref, dvec_ref, nw_ref, e_ref, s_ref,
               yn_ref, prev_xs, prev_bc, state):
    L = SSD_CHUNK
    xs = _silu(_shift_conv(xs_ref[rows, :], prev_xs, s_ref, cw_xs_ref, cb_xs_ref))
    bc = _silu(_shift_conv(bc_ref[rows, :], prev_bc, s_ref, cw_bc_ref, cb_bc_ref))

    x_dt = dtp_ref[rows, :] + dtb_ref[...]
    dt = jnp.maximum(x_dt, 0.0) + jnp.log1p(jnp.exp(-jnp.abs(x_dt)))
    a_neg = -jnp.exp(alog_ref[...])
    d_a = dt * a_neg
    row = lax.broadcasted_iota(jnp.int32, (L, L), 0)
    col = lax.broadcasted_iota(jnp.int32, (L, L), 1)
    causal = col <= row
    tri = jnp.where(causal, 1.0, 0.0).astype(BF16)
    hi, mid, lo = _split3(d_a)
    a_cs = _dot(tri, hi) + _dot(tri, mid) + _dot(tri, lo)
    a_cs_t = a_cs.T
    a_last = a_cs[L - 1:L, :]
    stack = jnp.concatenate([dt, jnp.exp(a_cs), jnp.exp(a_last - a_cs)], axis=0)
    s_hi = stack.astype(BF16)
    s_mid = (stack - s_hi.astype(F32)).astype(BF16)
    st_e = _dot(jnp.concatenate([s_hi, s_mid], axis=1), e_ref[...])
    dt_e = st_e[0:L]
    od_e = st_e[L:2 * L]
    ds_e = st_e[2 * L:3 * L]
    cd_e = od_e[L - 1:L, :]

    x_f = xs * dt_e
    x_b = x_f.astype(BF16)
    xd_b = (x_f * ds_e).astype(BF16)
    lane_head = lax.broadcasted_iota(jnp.int32, (L, GROUP_W), 1) // SSD_HEAD_DIM

    for g in range(SSD_GROUPS):
        gc = slice(g * GROUP_W, (g + 1) * GROUP_W)
        b_g = bc[:, g * SSD_STATE:(g + 1) * SSD_STATE]
        c_g = bc[:, SSD_GROUPS * SSD_STATE + g * SSD_STATE:
                 SSD_GROUPS * SSD_STATE + (g + 1) * SSD_STATE].astype(BF16)
        b_gt = b_g.T.astype(BF16)
        cb = _dot(c_g, b_gt)
        x_g = x_b[:, gc]
        lhs = []
        rhs = []
        for r in range(HEADS_PER_GROUP):
            h = g * HEADS_PER_GROUP + r
            seg = a_cs[:, h:h + 1] - a_cs_t[h:h + 1, :]
            lmat = jnp.exp(jnp.where(causal, seg, NEG_BIG))
            lhs.append((cb * lmat).astype(BF16))
            rhs.append(jnp.where(lane_head == r, x_g, jnp.zeros_like(x_g)))
        y_diag = _dot(jnp.concatenate(lhs, axis=1), jnp.concatenate(rhs, axis=0))
        st = state[g]
        y_off = _dot(c_g, st.astype(BF16)) * od_e[:, gc]
        state[g] = cd_e[:, gc] * st + _dot(b_gt, xd_b[:, gc])
        y = y_diag + y_off + dvec_ref[:, gc] * xs[:, gc]
        gt = y * _silu(z_ref[rows, gc].astype(F32))
        ms = jnp.mean(gt * gt, axis=-1, keepdims=True)
        yn_ref[rows, gc] = (gt * lax.rsqrt(ms + RMS_EPS) * nw_ref[:, gc]).astype(yn_ref.dtype)


def _ssd(p, dtp, cw, cb, dtb, alog, dvec, nw, e01, bsz, seq, rows):
    t = p.shape[0]
    L = SSD_CHUNK
    nc = seq // rows
    row_map = lambda b, c: (b * nc + c, 0)

    def pcol(k):
        return pl.BlockSpec((rows, SSD_INNER), lambda b, c: (b * nc + c, k))

    const = lambda b, c: (0, 0)
    return pl.pallas_call(
        _ssd_kernel,
        grid=(bsz, nc),
        in_specs=[pcol(1), pcol(2), pcol(0),
                  pl.BlockSpec((rows, LANES), row_map),
                  pl.BlockSpec((SSD_CONV, SSD_INNER), lambda b, c: (0, 0)),
                  pl.BlockSpec((SSD_CONV, SSD_INNER), lambda b, c: (0, 1)),
                  pl.BlockSpec((1, SSD_INNER), lambda b, c: (0, 0)),
                  pl.BlockSpec((1, SSD_INNER), lambda b, c: (0, 1)),
                  pl.BlockSpec((1, LANES), const),
                  pl.BlockSpec((1, LANES), const),
                  pl.BlockSpec((1, SSD_INNER), const),
                  pl.BlockSpec((1, SSD_INNER), const),
                  pl.BlockSpec((2 * LANES, SSD_INNER), const),
                  pl.BlockSpec(((SSD_CONV - 1) * L, 2 * L), const)],
        out_specs=pl.BlockSpec((rows, SSD_INNER), row_map),
        out_shape=jax.ShapeDtypeStruct((t, SSD_INNER), BF16),
        scratch_shapes=[pltpu.VMEM((L, SSD_INNER), BF16),
                        pltpu.VMEM((L, SSD_INNER), BF16),
                        pltpu.VMEM((SSD_GROUPS, SSD_STATE, GROUP_W), F32)],
        compiler_params=_cparams(("arbitrary", "arbitrary")),
        name="ssd",
    )(p, p, p, dtp, cw, cw, cb, cb, dtb, alog, dvec, nw, e01, _shift_matrix(L, SSD_CONV))


def _merge_kernel(yn_ref, scb_ref, scc_ref, sch_ref, ga_ref, gb_ref, x_ref, mod_ref,
                  wssd_ref, wsc_ref, wo_ref, cw_ref, lng_ref, lnb_ref, wr_ref,
                  x1_ref, h2_ref, h2p_ref, lg_ref, ext):
    s = pl.program_id(1)
    tm = x_ref.shape[0]

    @pl.when(s == 0)
    def _():
        ext[0:SUBLANES, :] = jnp.zeros((SUBLANES, D_MODEL), F32)

    ext[SUBLANES:SUBLANES + tm, :] = scc_ref[...].astype(F32) * sch_ref[...].astype(F32)
    m = mod_ref[0]
    for c in range(tm // MERGE_ROWS):
        r0 = c * MERGE_ROWS
        rows = slice(r0, r0 + MERGE_ROWS)
        u = None
        for k in range(SC_KERNEL):
            start = SUBLANES + r0 - (SC_KERNEL - 1) + k
            term = cw_ref[k:k + 1, :] * ext[start:start + MERGE_ROWS, :]
            u = term if u is None else u + term
        y_b = _dot((scb_ref[rows, :].astype(F32) * u).astype(BF16), wsc_ref[...])
        y_a = _dot(yn_ref[rows, :], wssd_ref[...])
        merged = (jax.nn.sigmoid(ga_ref[rows, :].astype(F32)) * y_a
                  + jax.nn.sigmoid(gb_ref[rows, :].astype(F32)) * y_b)
        mix = _dot(merged.astype(BF16), wo_ref[...])
        x1 = _ln(ALPHA * x_ref[rows, :] + m[2:3, :] * mix) * lng_ref[...] + lnb_ref[...]
        x1_ref[rows, :] = x1
        h2 = _ln(x1) * (1.0 + m[4:5, :]) + m[3:4, :]
        h2_hi = h2.astype(BF16)
        h2_ref[rows, :] = h2_hi
        w = _pack_pair(h2[:, 0:D_MODEL // 2], h2[:, D_MODEL // 2:D_MODEL])
        h2p_ref[0, rows, :] = w[:, 0:SC_ROW_WORDS]
        h2p_ref[1, rows, :] = w[:, SC_ROW_WORDS:2 * SC_ROW_WORDS]
        h2_lo = (h2 - h2_hi.astype(F32)).astype(BF16)
        both = _dot(h2_hi, wr_ref[...])
        lg_ref[rows, :] = (both[:, 0:LANES] + both[:, LANES:2 * LANES]
                           + _dot(h2_lo, wr_ref[:, 0:LANES]))
    ext[0:SUBLANES, :] = ext[tm:tm + SUBLANES, :]


def _merge(yn, p, x2d, mod3, wssd, wsc, wo, cw, lng, lnb, wr, bsz, seq, tm, b0):
    t = bsz * seq
    d = x2d.shape[1]
    ns = seq // tm
    row_map = lambda b, s: (b * ns + s, 0)
    const = lambda b, s: (0, 0)
    col0 = (2048 + 4096) // d

    def pcol(k):
        return pl.BlockSpec((tm, d), lambda b, s: (b * ns + s, col0 + k))

    return pl.pallas_call(
        _merge_kernel,
        grid=(bsz, ns),
        in_specs=[pl.BlockSpec((tm, SSD_INNER), row_map),
                  pcol(0), pcol(1), pcol(2), pcol(3), pcol(4),
                  pl.BlockSpec((tm, d), lambda b, s: ((b0 + b) * ns + s, 0)),
                  pl.BlockSpec((1, 6, d), lambda b, s: (b0 + b, 0, 0)),
                  pl.BlockSpec((SSD_INNER, d), const, pipeline_mode=pl.Buffered(1)),
                  pl.BlockSpec((d, d), const, pipeline_mode=pl.Buffered(1)),
                  pl.BlockSpec((d, d), const, pipeline_mode=pl.Buffered(1)),
                  pl.BlockSpec((SC_KERNEL, d), const),
                  pl.BlockSpec((1, d), const),
                  pl.BlockSpec((1, d), const),
                  pl.BlockSpec((d, 2 * LANES), const, pipeline_mode=pl.Buffered(1))],
        out_specs=[pl.BlockSpec((tm, d), row_map),
                   pl.BlockSpec((tm, d), row_map),
                   pl.BlockSpec((2, tm, SC_ROW_WORDS), lambda b, s: (0, b * ns + s, 0)),
                   pl.BlockSpec((tm, LANES), row_map)],
        out_shape=[jax.ShapeDtypeStruct((t, d), F32),
                   jax.ShapeDtypeStruct((t, d), BF16),
                   jax.ShapeDtypeStruct((2, t, SC_ROW_WORDS), U32),
                   jax.ShapeDtypeStruct((t, LANES), F32)],
        scratch_shapes=[pltpu.VMEM((tm + SUBLANES, d), F32)],
        compiler_params=_cparams(("arbitrary", "arbitrary")),
        name="merge",
    )(yn, p, p, p, p, p, x2d, mod3, wssd, wsc, wo, cw, lng, lnb, wr)


def _first_argmax_mask(v, idx, n):
    m = jnp.max(v, axis=0, keepdims=True)
    first = jnp.min(jnp.where(v == m, idx, n), axis=0, keepdims=True)
    return idx == first, m


def _route_kernel(lg_ref, bias_ref, su_ref, g_ref, dest_ref, cnt_ref, counts, carry, start):
    p = pl.program_id(0)
    i = pl.program_id(1)
    tm = lg_ref.shape[0]
    ne, ng, eg = N_EXPERTS, N_EXPERT_GROUPS, EXPERTS_PER_GROUP
    lt = lg_ref[...].T[0:ne, :]
    scores = jax.nn.sigmoid(lt)
    biased = scores + bias_ref[...][0:ne, 0:1]
    grp = biased.reshape(ng, eg, tm)
    idx_e = lax.broadcasted_iota(jnp.int32, (ng, eg, tm), 1)
    m1 = jnp.max(grp, axis=1, keepdims=True)
    first = jnp.min(jnp.where(grp == m1, idx_e, eg), axis=1, keepdims=True)
    m2 = jnp.max(jnp.where(idx_e == first, NEG_BIG, grp), axis=1, keepdims=True)
    gscore = m1 + m2
    idx_g = lax.broadcasted_iota(jnp.int32, (ng, 1, tm), 0)
    gsel = jnp.zeros((ng, 1, tm), F32)
    for _ in range(TOPK_EXPERT_GROUPS):
        hit, _m = _first_argmax_mask(gscore, idx_g, ng)
        gsel = jnp.where(hit, 1.0, gsel)
        gscore = jnp.where(hit, NEG_BIG, gscore)
    emask = jnp.broadcast_to(gsel, (ng, eg, tm)).reshape(ne, tm)
    cand = jnp.where(emask > 0.0, biased, NEG_BIG)
    idx_x = lax.broadcasted_iota(jnp.int32, (ne, tm), 0)
    sel = jnp.zeros((ne, tm), F32)
    picks = []
    for _ in range(TOP_K):
        hit, _m = _first_argmax_mask(cand, idx_x, ne)
        sel = jnp.where(hit, 1.0, sel)
        cand = jnp.where(hit, NEG_BIG, cand)
        picks.append(hit)
    n_e = jnp.broadcast_to(jnp.sum(sel, axis=1, keepdims=True), (ne, LANES))

    @pl.when((p == 0) & (i == 0))
    def _():
        counts[...] = jnp.zeros(counts.shape, F32)

    @pl.when(p == 0)
    def _():
        counts[...] += n_e

    @pl.when((p == 1) & (i == 0))
    def _():
        r = lax.broadcasted_iota(jnp.int32, (ne, ne), 0)
        cc = lax.broadcasted_iota(jnp.int32, (ne, ne), 1)
        below = jnp.where(cc < r, 1.0, 0.0).astype(BF16)
        padded = jnp.floor((counts[...] + (GMM_BLOCK - 1.0)) * (1.0 / GMM_BLOCK)) * GMM_BLOCK
        start[...] = _dot01_exact(below, padded)
        carry[...] = jnp.zeros(carry.shape, F32)
        cnt_ref[...] = counts[...]

    @pl.when(p == 1)
    def _():
        before = _dot(sel.astype(BF16), su_ref[...])
        slot = before + carry[:, 0:1] + start[:, 0:1]
        dest_rows = []
        gate_rows = []
        for hit in picks:
            dest_rows.append(jnp.sum(jnp.where(hit, slot, 0.0), axis=0, keepdims=True))
            gate_rows.append(jnp.sum(jnp.where(hit, scores, 0.0), axis=0, keepdims=True))
        dest_ref[...] = jnp.concatenate(dest_rows, axis=0).astype(jnp.int32)
        gate = jnp.concatenate(gate_rows, axis=0)
        gate = gate / jnp.sum(gate, axis=0, keepdims=True) * ROUTED_SCALE
        pad = jnp.zeros((LANES - TOP_K, tm), F32)
        g_ref[...] = jnp.concatenate([gate, pad], axis=0).T
        carry[...] += n_e


def _route(logits, bias_col, su, tm):
    t = logits.shape[0]
    return pl.pallas_call(
        _route_kernel,
        grid=(2, t // tm),
        in_specs=[pl.BlockSpec((tm, LANES), lambda p, i: (i, 0)),
                  pl.BlockSpec((LANES, LANES), lambda p, i: (0, 0)),
                  pl.BlockSpec((tm, tm), lambda p, i: (0, 0))],
        out_specs=[pl.BlockSpec((tm, LANES), lambda p, i: (i * p, 0)),
                   pl.BlockSpec((TOP_K, tm), lambda p, i: (0, i * p)),
                   pl.BlockSpec((N_EXPERTS, LANES), lambda p, i: (0, 0))],
        out_shape=[jax.ShapeDtypeStruct((t, LANES), F32),
                   jax.ShapeDtypeStruct((TOP_K, t), jnp.int32),
                   jax.ShapeDtypeStruct((N_EXPERTS, LANES), F32)],
        scratch_shapes=[pltpu.VMEM((N_EXPERTS, LANES), F32),
                        pltpu.VMEM((N_EXPERTS, LANES), F32),
                        pltpu.VMEM((N_EXPERTS, LANES), F32)],
        compiler_params=_cparams(("arbitrary", "arbitrary")),
        name="route",
    )(logits, bias_col, su)


def _sc_mesh():
    return plsc.VectorSubcoreMesh(core_axis_name="c", subcore_axis_name="s")


def _sc_scatter_rows(rows, idx, n_out):
    n_rows, w = rows.shape
    n_k = idx.shape[0]

    @pl.kernel(out_type=jax.ShapeDtypeStruct((n_out, w), rows.dtype), mesh=_sc_mesh(),
               scratch_types=[])
    def scatter(x_hbm, i_hbm, o_hbm):
        def body(x_vmem, i_vmem):
            for k in range(n_k):
                pltpu.sync_copy(x_vmem, o_hbm.at[i_vmem.at[k]])

        pltpu.emit_pipeline(
            body,
            grid=(n_rows // SC_WINDOW,),
            in_specs=[pl.BlockSpec((SC_WINDOW, w), index_map=lambda i: (i, 0)),
                      pl.BlockSpec((n_k, SC_WINDOW), index_map=lambda i: (0, i))],
            out_specs=[],
            core_axis_name=("c", "s"),
            dimension_semantics=(pltpu.PARALLEL,),
        )(x_hbm, i_hbm)

    return scatter(rows, idx)


def _sc_gather_rows(rows, idx):
    n = idx.shape[1]
    w = rows.shape[1]

    @pl.kernel(out_type=jax.ShapeDtypeStruct((n, w), rows.dtype), mesh=_sc_mesh(),
               scratch_types=[])
    def gather(x_hbm, i_hbm, o_hbm):
        def body(i_vmem, o_vmem):
            pltpu.sync_copy(x_hbm.at[i_vmem.at[0]], o_vmem)

        pltpu.emit_pipeline(
            body,
            grid=(n // SC_WINDOW,),
            in_specs=[pl.BlockSpec((1, SC_WINDOW), index_map=lambda i: (0, i))],
            out_specs=[pl.BlockSpec((SC_WINDOW, w), index_map=lambda i: (i, 0))],
            core_axis_name=("c", "s"),
            dimension_semantics=(pltpu.PARALLEL,),
        )(i_hbm, o_hbm)

    return gather(rows, idx)


def _gmm_kernel(vb_ref, ve_ref, vvalid_ref, vnew_ref, x_ref, wg_ref, wu_ref, wdn_ref,
                o_ref, wgu_ref, wd_ref):
    v = pl.program_id(0)
    valid = vvalid_ref[v]
    bm = x_ref.shape[1]
    half = D_MODEL // 2

    @pl.when(vnew_ref[v] == 1)
    def _():
        wgu_ref[:, 0:D_EXPERT] = wg_ref[0].astype(BF16)
        wgu_ref[:, D_EXPERT:2 * D_EXPERT] = wu_ref[0].astype(BF16)
        wd_ref[...] = wdn_ref[0].astype(BF16)

    @pl.when(valid > 0)
    def _():
        for c in range(bm // GMM_ROWS):
            rows = slice(c * GMM_ROWS, (c + 1) * GMM_ROWS)
            a0, b0 = _unpack_pair(x_ref[0, rows, :])
            a1, b1 = _unpack_pair(x_ref[1, rows, :])
            x = jnp.concatenate([a0.astype(BF16), a1.astype(BF16),
                                 b0.astype(BF16), b1.astype(BF16)], axis=1)
            a = _dot(x, wgu_ref[...])
            act = _silu(a[:, 0:D_EXPERT]) * a[:, D_EXPERT:2 * D_EXPERT]
            r = lax.broadcasted_iota(jnp.int32, (GMM_ROWS, D_EXPERT), 0) + c * GMM_ROWS
            act = jnp.where(r < valid, act, 0.0)
            y = _dot(act.astype(BF16), wd_ref[...])
            w = _pack_pair(y[:, 0:half], y[:, half:D_MODEL])
            o_ref[0, rows, :] = w[:, 0:SC_ROW_WORDS]
            o_ref[1, rows, :] = w[:, SC_ROW_WORDS:half]


def _gmm(xin, w_gate, w_up, w_down, vb, ve, vvalid, vnew, bm):
    _, n_rows, w = xin.shape
    d = D_MODEL
    n_visits = vb.shape[0]
    blk = lambda v, vb, ve, vvalid, vnew: (0, vb[v], 0)
    wsel = lambda v, vb, ve, vvalid, vnew: (ve[v], 0, 0)
    grid_spec = pltpu.PrefetchScalarGridSpec(
        num_scalar_prefetch=4,
        grid=(n_visits,),
        in_specs=[pl.BlockSpec((2, bm, w), blk),
                  pl.BlockSpec((1, d, D_EXPERT), wsel),
                  pl.BlockSpec((1, d, D_EXPERT), wsel),
                  pl.BlockSpec((1, D_EXPERT, d), wsel)],
        out_specs=pl.BlockSpec((2, bm, w), blk),
        scratch_shapes=[pltpu.VMEM((d, 2 * D_EXPERT), BF16),
                        pltpu.VMEM((D_EXPERT, d), BF16)],
    )
    return pl.pallas_call(
        _gmm_kernel,
        grid_spec=grid_spec,
        out_shape=jax.ShapeDtypeStruct((2, n_rows, w), U32),
        compiler_params=_cparams(("arbitrary",)),
        name="gmm",
    )(vb, ve, vvalid, vnew, xin, w_gate, w_up, w_down)


def _visit_plan(counts, n_blocks, bm):
    ne = counts.shape[0]
    nb = (counts + bm - 1) // bm
    vend = jnp.cumsum(nb)
    vstart = vend - nb
    total = vend[-1]
    v = jnp.arange(n_blocks, dtype=jnp.int32)
    vc = jnp.minimum(v, total - 1)
    e_v = jnp.minimum(jnp.sum((vend[None, :] <= vc[:, None]).astype(jnp.int32), axis=1), ne - 1)
    onehot = (e_v[:, None] == jnp.arange(ne, dtype=jnp.int32)[None, :]).astype(jnp.int32)
    pick = lambda tab: jnp.sum(onehot * tab[None, :].astype(jnp.int32), axis=1)
    valid = jnp.clip(pick(counts) - (vc - pick(vstart)) * bm, 0, bm)
    valid = jnp.where(v < total, valid, 0)
    i32 = lambda a: a.astype(jnp.int32)
    e_v = i32(e_v)
    new = jnp.concatenate([jnp.ones((1,), jnp.int32), i32(e_v[1:] != e_v[:-1])])
    return i32(vc), e_v, i32(valid), new


def _combine_kernel(yg_ref, g_ref, h_ref, x1_ref, mod_ref, wsgu_ref, wsd_ref, lng_ref,
                    lnb_ref, *rest):
    o_ref = rest[-1]
    q = D_MODEL // 4
    g = g_ref[...]
    a = _dot(h_ref[...], wsgu_ref[...])
    act = _silu(a[:, 0:D_EXPERT]) * a[:, D_EXPERT:2 * D_EXPERT]
    shared = _dot(act.astype(BF16), wsd_ref[...])
    f = [shared[:, j * q:(j + 1) * q] for j in range(4)]
    for k in range(TOP_K):
        a0, b0 = _unpack_pair(yg_ref[0, k])
        a1, b1 = _unpack_pair(yg_ref[1, k])
        gk = g[:, k:k + 1]
        f = [f[0] + gk * a0, f[1] + gk * a1, f[2] + gk * b0, f[3] + gk * b1]
    ffn = jnp.concatenate(f, axis=1)
    m = mod_ref[0]
    r = ALPHA * x1_ref[...] + m[5:6, :] * ffn
    o_ref[...] = _ln(r) * lng_ref[...] + lnb_ref[...]


def _combine(yg, gates, h2, x1, mod3, wsgu, wsd, lng, lnb, seq, tm, tok0, t, prev_out):
    tp, d = x1.shape
    b0 = tok0 // tm
    row = lambda i: (i, 0)
    const = lambda i: (0, 0)
    in_specs = [pl.BlockSpec((2, TOP_K, tm, SC_ROW_WORDS), lambda i: (0, 0, i, 0)),
                pl.BlockSpec((tm, LANES), row),
                pl.BlockSpec((tm, d), row),
                pl.BlockSpec((tm, d), row),
                pl.BlockSpec((1, 6, d), lambda i: (((b0 + i) * tm) // seq, 0, 0)),
                pl.BlockSpec((d, 2 * D_EXPERT), const),
                pl.BlockSpec((D_EXPERT, d), const),
                pl.BlockSpec((1, d), const),
                pl.BlockSpec((1, d), const)]
    args = [yg, gates, h2, x1, mod3, wsgu, wsd, lng, lnb]
    aliases = {}
    if prev_out is not None:
        in_specs.append(pl.BlockSpec(memory_space=pl.ANY))
        args.append(prev_out)
        aliases = {len(args) - 1: 0}
    return pl.pallas_call(
        _combine_kernel,
        grid=(tp // tm,),
        in_specs=in_specs,
        out_specs=pl.BlockSpec((tm, d), lambda i: (b0 + i, 0)),
        out_shape=jax.ShapeDtypeStruct((t, d), F32),
        input_output_aliases=aliases,
        compiler_params=_cparams(("arbitrary",)),
        name="combine",
    )(*args)


def _pad_lanes(v, fill=0.0):
    out = jnp.full((1, LANES), fill, F32)
    return out.at[0, :v.shape[0]].set(v.astype(F32))


def kernel(x, c, w_ada, b_ada, w_in, ssd_conv_w, ssd_conv_b, ssd_dt_bias, ssd_A_log, ssd_D,
           ssd_norm_w, w_ssd_out, sc_conv_w, w_sc_out, w_o, ln1_g, ln1_b, router_w,
           router_bias, w_gate, w_up, w_down, sh_gate, sh_up, sh_down, ln2_g, ln2_b):
    bsz, seq, d = x.shape
    t = bsz * seq
    depth = w_in.shape[0]
    tm_proj = min(1024, seq)
    tm_merge = min(512, seq)
    tm_route = min(1024, t)
    tm_comb = min(512, seq)
    bm_gmm = GMM_BLOCK
    n_groups = 2 if bsz % 2 == 0 else 1
    bg = bsz // n_groups
    tg = bg * seq
    tm_route = min(tm_route, tg)
    n_assign = tg * TOP_K
    n_rows = n_assign + N_EXPERTS * bm_gmm

    e01 = (np.arange(LANES)[:, None] == (np.arange(SSD_INNER)[None, :] // SSD_HEAD_DIM))
    e01 = jnp.asarray(np.concatenate([e01, e01], axis=0), BF16)
    su = jnp.asarray(np.arange(tm_route)[:, None] < np.arange(tm_route)[None, :], BF16)

    xf = x.reshape(t, d)
    for l in range(depth):
        w_l = w_in[l]
        w_main = jnp.concatenate([w_l[:, :DT_COL0], w_l[:, DT_COL0 + SSD_HEADS:]],
                                 axis=1).astype(BF16)
        w_dt = jnp.pad(w_l[:, DT_COL0:DT_COL0 + SSD_HEADS],
                       ((0, 0), (0, LANES - SSD_HEADS))).astype(BF16)
        wsgu = jnp.concatenate([sh_gate[l], sh_up[l]], axis=-1).astype(BF16)
        wsd = sh_down[l].astype(BF16)
        wr = jnp.pad(router_w[l], ((0, 0), (0, LANES - N_EXPERTS)))
        wr_hi = wr.astype(BF16)
        wr = jnp.concatenate([wr_hi, (wr - wr_hi.astype(F32)).astype(BF16)], axis=1)
        bias_col = jnp.zeros((LANES, LANES), F32).at[:N_EXPERTS, 0].set(router_bias[l])
        dvec = jnp.repeat(ssd_D[l], SSD_HEAD_DIM).reshape(1, SSD_INNER)

        mod3 = _mod(c, w_ada[l], b_ada[l]).reshape(bsz, 6, d)
        x_in = xf

        def mix_and_route(g):
            p, dtp = _inproj(x_in, mod3, w_main, w_dt, seq, tm_proj, w_main.shape[1] // 4,
                             g * tg, tg)
            yn = _ssd(p, dtp, ssd_conv_w[l], ssd_conv_b[l].reshape(1, -1),
                      _pad_lanes(ssd_dt_bias[l]), _pad_lanes(ssd_A_log[l]), dvec,
                      ssd_norm_w[l].reshape(1, -1), e01, bg, seq, 2 * SSD_CHUNK)
            x1, h2, h2p, logits = _merge(yn, p, x_in, mod3, w_ssd_out[l].astype(BF16),
                                         w_sc_out[l].astype(BF16), w_o[l].astype(BF16),
                                         sc_conv_w[l], ln1_g[l].reshape(1, d),
                                         ln1_b[l].reshape(1, d), wr, bg, seq, tm_merge, g * bg)
            gates, dest, cnt = _route(logits, bias_col, su, tm_route)
            dest_both = jnp.concatenate([dest, dest + n_rows], axis=1)
            plan = _visit_plan(cnt[:, 0].astype(jnp.int32), n_rows // bm_gmm, bm_gmm)
            xin = _sc_scatter_rows(h2p.reshape(2 * tg, SC_ROW_WORDS), dest_both, 2 * n_rows)
            return dict(x1=x1, h2=h2, gates=gates, dest=dest, plan=plan, xin=xin)

        def experts(st):
            yb = _gmm(st["xin"].reshape(2, n_rows, SC_ROW_WORDS), w_gate[l], w_up[l], w_down[l],
                      *st["plan"], bm_gmm)
            idx = st["dest"].reshape(1, n_assign)
            idx = jnp.concatenate([idx, idx + n_rows], axis=1)
            yg = _sc_gather_rows(yb.reshape(2 * n_rows, SC_ROW_WORDS), idx)
            return yg.reshape(2, TOP_K, tg, SC_ROW_WORDS)

        stages = [mix_and_route(g) for g in range(n_groups)]
        gathered = [experts(st) for st in stages]
        out = None
        for g, (st, yg) in enumerate(zip(stages, gathered)):
            out = _combine(yg, st["gates"], st["h2"], st["x1"], mod3, wsgu, wsd,
                           ln2_g[l].reshape(1, d), ln2_b[l].reshape(1, d), seq, tm_comb,
                           g * tg, t, out)
        xf = out
    return xf.reshape(bsz, seq, d)
```

```python
import jax
import jax.numpy as jnp
import numpy as np
from jax import lax
from jax.experimental import pallas as pl
from jax.experimental.pallas import tpu as pltpu
from jax.experimental.pallas import tpu_sc as plsc

F32 = jnp.float32
BF16 = jnp.bfloat16
U32 = jnp.uint32

LANES = 128
SUBLANES = 8
VMEM_LIMIT_BYTES = 56 * 1024 * 1024

D_MODEL = 1024
SSD_INNER = 2048
SSD_HEAD_DIM = 64
SSD_HEADS = 32
SSD_GROUPS = 8
SSD_STATE = 128
SSD_CONV = 4
SSD_CHUNK = 128
GROUP_W = SSD_INNER // SSD_GROUPS
HEADS_PER_GROUP = SSD_HEADS // SSD_GROUPS
SC_KERNEL = 3
N_EXPERTS = 64
TOP_K = 8
N_EXPERT_GROUPS = 8
TOPK_EXPERT_GROUPS = 4
EXPERTS_PER_GROUP = N_EXPERTS // N_EXPERT_GROUPS
D_EXPERT = 256
ROUTED_SCALE = 2.5
LN_EPS = 1e-5
RMS_EPS = 1e-5
ALPHA = 2.0 ** 0.25
DT_COL0 = 2048 + 4096
NEG_BIG = -1e30
SC_WINDOW = 128
SC_ROW_WORDS = 256
GMM_BLOCK = 1024
GMM_ROWS = 256
MERGE_ROWS = 256


def _cparams(sem):
    return pltpu.CompilerParams(dimension_semantics=sem,
                                vmem_limit_bytes=VMEM_LIMIT_BYTES)


def _ln(x):
    mu = jnp.mean(x, axis=-1, keepdims=True)
    xc = x - mu
    var = jnp.mean(xc * xc, axis=-1, keepdims=True)
    return xc * lax.rsqrt(var + LN_EPS)


def _silu(x):
    h = 0.5 * x
    return h + h * jnp.tanh(h)


def _split3(a):
    hi = a.astype(BF16)
    r1 = a - hi.astype(F32)
    mid = r1.astype(BF16)
    lo = (r1 - mid.astype(F32)).astype(BF16)
    return hi, mid, lo


def _dot(a, b):
    return jnp.dot(a, b, preferred_element_type=F32)


def _dot_exact01(a, m01):
    hi, mid, lo = _split3(a)
    return _dot(hi, m01) + _dot(mid, m01) + _dot(lo, m01)


def _dot01_exact(m01, a):
    hi, mid, lo = _split3(a)
    return _dot(m01, hi) + _dot(m01, mid) + _dot(m01, lo)


def _pack_pair(a, b):
    ab = pltpu.bitcast(a.astype(BF16).astype(F32), U32)
    bb = pltpu.bitcast(b.astype(BF16).astype(F32), U32)
    return ab | lax.shift_right_logical(bb, jnp.uint32(16))


def _unpack_pair(w):
    a = pltpu.bitcast(w & jnp.uint32(0xFFFF0000), F32)
    b = pltpu.bitcast(lax.shift_left(w, jnp.uint32(16)), F32)
    return a, b


def _mod_kernel(c_ref, w_ref, b_ref, o_ref):
    a = _silu(c_ref[...])
    o_ref[...] = jnp.dot(a, w_ref[...], precision=lax.Precision.HIGHEST,
                         preferred_element_type=F32) + b_ref[...]


def _mod(c, w, b):
    bsz, d = c.shape
    n = w.shape[1]
    tn = 1024
    return pl.pallas_call(
        _mod_kernel,
        grid=(n // tn,),
        in_specs=[pl.BlockSpec((bsz, d), lambda j: (0, 0)),
                  pl.BlockSpec((d, tn), lambda j: (0, j)),
                  pl.BlockSpec((1, tn), lambda j: (0, j))],
        out_specs=pl.BlockSpec((bsz, tn), lambda j: (0, j)),
        out_shape=jax.ShapeDtypeStruct((bsz, n), F32),
        compiler_params=_cparams(("arbitrary",)),
        name="mod",
    )(c, w, b.reshape(1, n))


def _inproj_kernel(x_ref, mod_ref, w_ref, wdt_ref, p_ref, dt_ref, h_scr):
    j = pl.program_id(1)

    @pl.when(j == 0)
    def _():
        m = mod_ref[0]
        h = _ln(x_ref[...]) * (1.0 + m[1:2, :]) + m[0:1, :]
        hb = h.astype(BF16)
        h_scr[...] = hb
        dt_ref[...] = _dot(hb, wdt_ref[...])

    p_ref[...] = _dot(h_scr[...], w_ref[...]).astype(p_ref.dtype)


def _inproj(x2d, mod3, w_main, w_dt, seq, tm, tn, tok0, t):
    d = x2d.shape[1]
    n = w_main.shape[1]
    i0 = tok0 // tm
    return pl.pallas_call(
        _inproj_kernel,
        grid=(t // tm, n // tn),
        in_specs=[pl.BlockSpec((tm, d), lambda i, j: (i0 + i, 0)),
                  pl.BlockSpec((1, 6, d), lambda i, j: (((i0 + i) * tm) // seq, 0, 0)),
                  pl.BlockSpec((d, tn), lambda i, j: (0, j)),
                  pl.BlockSpec((d, LANES), lambda i, j: (0, 0))],
        out_specs=[pl.BlockSpec((tm, tn), lambda i, j: (i, j)),
                   pl.BlockSpec((tm, LANES), lambda i, j: (i, 0))],
        out_shape=[jax.ShapeDtypeStruct((t, n), BF16),
                   jax.ShapeDtypeStruct((t, LANES), F32)],
        scratch_shapes=[pltpu.VMEM((tm, d), BF16)],
        compiler_params=_cparams(("arbitrary", "arbitrary")),
        name="inproj",
    )(x2d, mod3, w_main, w_dt)


def _shift_conv(cur, prev_scr, s_ref, w_ref, b_ref):
    rows = cur.shape[0]
    n_taps = w_ref.shape[0]
    shifted = _dot(s_ref[...], jnp.concatenate([prev_scr[...], cur], axis=0))
    acc = w_ref[n_taps - 1:n_taps, :] * cur.astype(F32) + b_ref[...]
    for k in range(n_taps - 1):
        acc = acc + w_ref[k:k + 1, :] * shifted[k * rows:(k + 1) * rows]
    prev_scr[...] = cur
    return acc


def _shift_matrix(rows, n_taps):
    s = np.zeros(((n_taps - 1) * rows, 2 * rows), np.float32)
    t = np.arange(rows)
    for k in range(n_taps - 1):
        s[k * rows + t, rows + t - (n_taps - 1) + k] = 1.0
    return jnp.asarray(s, BF16)


def _ssd_kernel(xs_ref, bc_ref, z_ref, dtp_ref, cw_xs_ref, cw_bc_ref, cb_xs_ref,
                cb_bc_ref, dtb_ref, alog_ref, dvec_ref, nw_ref, e_ref, s_ref,
                yn_ref, prev_xs, prev_bc, state):
    @pl.when(pl.program_id(1) == 0)
    def _():
        prev_xs[...] = jnp.zeros(prev_xs.shape, BF16)
        prev_bc[...] = jnp.zeros(prev_bc.shape, BF16)
        state[...] = jnp.zeros(state.shape, F32)

    for cc in range(xs_ref.shape[0] // SSD_CHUNK):
        _ssd_chunk(slice(cc * SSD_CHUNK, (cc + 1) * SSD_CHUNK), xs_ref, bc_ref, z_ref, dtp_ref,
                   cw_xs_ref, cw_bc_ref, cb_xs_ref, cb_bc_ref, dtb_ref, alog_ref, dvec_ref,
                   nw_ref, e_ref, s_ref, yn_ref, prev_xs, prev_bc, state)


def _ssd_chunk(rows, xs_ref, bc_ref, z_ref, dtp_ref, cw_xs_ref, cw_bc_ref, cb_xs_ref,
               cb_bc_ref, dtb_ref, alog_ref, dvec_ref, nw_ref, e_ref, s_ref,
               yn_ref, prev_xs, prev_bc, state):
    L = SSD_CHUNK
    xs = _silu(_shift_conv(xs_ref[rows, :], prev_xs, s_ref, cw_xs_ref, cb_xs_ref))
    bc = _silu(_shift_conv(bc_ref[rows, :], prev_bc, s_ref, cw_bc_ref, cb_bc_ref))

    x_dt = dtp_ref[rows, :] + dtb_ref[...]
    dt = jnp.maximum(x_dt, 0.0) + jnp.log1p(jnp.exp(-jnp.abs(x_dt)))
    a_neg = -jnp.exp(alog_ref[...])
    d_a = dt * a_neg
    row = lax.broadcasted_iota(jnp.int32, (L, L), 0)
    col = lax.broadcasted_iota(jnp.int32, (L, L), 1)
    causal = col <= row
    tri = jnp.where(causal, 1.0, 0.0).astype(BF16)
    hi, mid, lo = _split3(d_a)
    a_cs = _dot(tri, hi) + _dot(tri, mid) + _dot(tri, lo)
    a_cs_t = a_cs.T
    a_last = a_cs[L - 1:L, :]
    stack = jnp.concatenate([dt, jnp.exp(a_cs), jnp.exp(a_last - a_cs)], axis=0)
    s_hi = stack.astype(BF16)
    s_mid = (stack - s_hi.astype(F32)).astype(BF16)
    st_e = _dot(jnp.concatenate([s_hi, s_mid], axis=1), e_ref[...])
    dt_e = st_e[0:L]
    od_e = st_e[L:2 * L]
    ds_e = st_e[2 * L:3 * L]
    cd_e = od_e[L - 1:L, :]

    x_f = xs * dt_e
    x_b = x_f.astype(BF16)
    xd_b = (x_f * ds_e).astype(BF16)
    lane_head = lax.broadcasted_iota(jnp.int32, (L, GROUP_W), 1) // SSD_HEAD_DIM

    for g in range(SSD_GROUPS):
        gc = slice(g * GROUP_W, (g + 1) * GROUP_W)
        b_g = bc[:, g * SSD_STATE:(g + 1) * SSD_STATE]
        c_g = bc[:, SSD_GROUPS * SSD_STATE + g * SSD_STATE:
                 SSD_GROUPS * SSD_STATE + (g + 1) * SSD_STATE].astype(BF16)
        b_gt = b_g.T.astype(BF16)
        cb = _dot(c_g, b_gt)
        x_g = x_b[:, gc]
        lhs = []
        rhs = []
        for r in range(HEADS_PER_GROUP):
            h = g * HEADS_PER_GROUP + r
            seg = a_cs[:, h:h + 1] - a_cs_t[h:h + 1, :]
            lmat = jnp.exp(jnp.where(causal, seg, NEG_BIG))
            lhs.append((cb * lmat).astype(BF16))
            rhs.append(jnp.where(lane_head == r, x_g, jnp.zeros_like(x_g)))
        y_diag = _dot(jnp.concatenate(lhs, axis=1), jnp.concatenate(rhs, axis=0))
        st = state[g]
        y_off = _dot(c_g, st.astype(BF16)) * od_e[:, gc]
        state[g] = cd_e[:, gc] * st + _dot(b_gt, xd_b[:, gc])
        y = y_diag + y_off + dvec_ref[:, gc] * xs[:, gc]
        gt = y * _silu(z_ref[rows, gc].astype(F32))
        ms = jnp.mean(gt * gt, axis=-1, keepdims=True)
        yn_ref[rows, gc] = (gt * lax.rsqrt(ms + RMS_EPS) * nw_ref[:, gc]).astype(yn_ref.dtype)


def _ssd(p, dtp, cw, cb, dtb, alog, dvec, nw, e01, bsz, seq, rows):
    t = p.shape[0]
    L = SSD_CHUNK
    nc = seq // rows
    row_map = lambda b, c: (b * nc + c, 0)

    def pcol(k):
        return pl.BlockSpec((rows, SSD_INNER), lambda b, c: (b * nc + c, k))

    const = lambda b, c: (0, 0)
    return pl.pallas_call(
        _ssd_kernel,
        grid=(bsz, nc),
        in_specs=[pcol(1), pcol(2), pcol(0),
                  pl.BlockSpec((rows, LANES), row_map),
                  pl.BlockSpec((SSD_CONV, SSD_INNER), lambda b, c: (0, 0)),
                  pl.BlockSpec((SSD_CONV, SSD_INNER), lambda b, c: (0, 1)),
                  pl.BlockSpec((1, SSD_INNER), lambda b, c: (0, 0)),
                  pl.BlockSpec((1, SSD_INNER), lambda b, c: (0, 1)),
                  pl.BlockSpec((1, LANES), const),
                  pl.BlockSpec((1, LANES), const),
                  pl.BlockSpec((1, SSD_INNER), const),
                  pl.BlockSpec((1, SSD_INNER), const),
                  pl.BlockSpec((2 * LANES, SSD_INNER), const),
                  pl.BlockSpec(((SSD_CONV - 1) * L, 2 * L), const)],
        out_specs=pl.BlockSpec((rows, SSD_INNER), row_map),
        out_shape=jax.ShapeDtypeStruct((t, SSD_INNER), BF16),
        scratch_shapes=[pltpu.VMEM((L, SSD_INNER), BF16),
                        pltpu.VMEM((L, SSD_INNER), BF16),
                        pltpu.VMEM((SSD_GROUPS, SSD_STATE, GROUP_W), F32)],
        compiler_params=_cparams(("arbitrary", "arbitrary")),
        name="ssd",
    )(p, p, p, dtp, cw, cw, cb, cb, dtb, alog, dvec, nw, e01, _shift_matrix(L, SSD_CONV))


def _merge_kernel(yn_ref, scb_ref, scc_ref, sch_ref, ga_ref, gb_ref, x_ref, mod_ref,
                  wssd_ref, wsc_ref, wo_ref, cw_ref, lng_ref, lnb_ref, wr_ref,
                  x1_ref, h2_ref, h2p_ref, lg_ref, ext):
    s = pl.program_id(1)
    tm = x_ref.shape[0]

    @pl.when(s == 0)
    def _():
        ext[0:SUBLANES, :] = jnp.zeros((SUBLANES, D_MODEL), F32)

    ext[SUBLANES:SUBLANES + tm, :] = scc_ref[...].astype(F32) * sch_ref[...].astype(F32)
    m = mod_ref[0]
    for c in range(tm // MERGE_ROWS):
        r0 = c * MERGE_ROWS
        rows = slice(r0, r0 + MERGE_ROWS)
        u = None
        for k in range(SC_KERNEL):
            start = SUBLANES + r0 - (SC_KERNEL - 1) + k
            term = cw_ref[k:k + 1, :] * ext[start:start + MERGE_ROWS, :]
            u = term if u is None else u + term
        y_b = _dot((scb_ref[rows, :].astype(F32) * u).astype(BF16), wsc_ref[...])
        y_a = _dot(yn_ref[rows, :], wssd_ref[...])
        merged = (jax.nn.sigmoid(ga_ref[rows, :].astype(F32)) * y_a
                  + jax.nn.sigmoid(gb_ref[rows, :].astype(F32)) * y_b)
        mix = _dot(merged.astype(BF16), wo_ref[...])
        x1 = _ln(ALPHA * x_ref[rows, :] + m[2:3, :] * mix) * lng_ref[...] + lnb_ref[...]
        x1_ref[rows, :] = x1
        h2 = _ln(x1) * (1.0 + m[4:5, :]) + m[3:4, :]
        h2_hi = h2.astype(BF16)
        h2_ref[rows, :] = h2_hi
        w = _pack_pair(h2[:, 0:D_MODEL // 2], h2[:, D_MODEL // 2:D_MODEL])
        h2p_ref[0, rows, :] = w[:, 0:SC_ROW_WORDS]
        h2p_ref[1, rows, :] = w[:, SC_ROW_WORDS:2 * SC_ROW_WORDS]
        h2_lo = (h2 - h2_hi.astype(F32)).astype(BF16)
        both = _dot(h2_hi, wr_ref[...])
        lg_ref[rows, :] = (both[:, 0:LANES] + both[:, LANES:2 * LANES]
                           + _dot(h2_lo, wr_ref[:, 0:LANES]))
    ext[0:SUBLANES, :] = ext[tm:tm + SUBLANES, :]


def _merge(yn, p, x2d, mod3, wssd, wsc, wo, cw, lng, lnb, wr, bsz, seq, tm, b0):
    t = bsz * seq
    d = x2d.shape[1]
    ns = seq // tm
    row_map = lambda b, s: (b * ns + s, 0)
    const = lambda b, s: (0, 0)
    col0 = (2048 + 4096) // d

    def pcol(k):
        return pl.BlockSpec((tm, d), lambda b, s: (b * ns + s, col0 + k))

    return pl.pallas_call(
        _merge_kernel,
        grid=(bsz, ns),
        in_specs=[pl.BlockSpec((tm, SSD_INNER), row_map),
                  pcol(0), pcol(1), pcol(2), pcol(3), pcol(4),
                  pl.BlockSpec((tm, d), lambda b, s: ((b0 + b) * ns + s, 0)),
                  pl.BlockSpec((1, 6, d), lambda b, s: (b0 + b, 0, 0)),
                  pl.BlockSpec((SSD_INNER, d), const, pipeline_mode=pl.Buffered(1)),
                  pl.BlockSpec((d, d), const, pipeline_mode=pl.Buffered(1)),
                  pl.BlockSpec((d, d), const, pipeline_mode=pl.Buffered(1)),
                  pl.BlockSpec((SC_KERNEL, d), const),
                  pl.BlockSpec((1, d), const),
                  pl.BlockSpec((1, d), const),
                  pl.BlockSpec((d, 2 * LANES), const, pipeline_mode=pl.Buffered(1))],
        out_specs=[pl.BlockSpec((tm, d), row_map),
                   pl.BlockSpec((tm, d), row_map),
                   pl.BlockSpec((2, tm, SC_ROW_WORDS), lambda b, s: (0, b * ns + s, 0)),
                   pl.BlockSpec((tm, LANES), row_map)],
        out_shape=[jax.ShapeDtypeStruct((t, d), F32),
                   jax.ShapeDtypeStruct((t, d), BF16),
                   jax.ShapeDtypeStruct((2, t, SC_ROW_WORDS), U32),
                   jax.ShapeDtypeStruct((t, LANES), F32)],
        scratch_shapes=[pltpu.VMEM((tm + SUBLANES, d), F32)],
        compiler_params=_cparams(("arbitrary", "arbitrary")),
        name="merge",
    )(yn, p, p, p, p, p, x2d, mod3, wssd, wsc, wo, cw, lng, lnb, wr)


def _first_argmax_mask(v, idx, n):
    m = jnp.max(v, axis=0, keepdims=True)
    first = jnp.min(jnp.where(v == m, idx, n), axis=0, keepdims=True)
    return idx == first, m


def _route_kernel(lg_ref, bias_ref, su_ref, g_ref, dest_ref, cnt_ref, counts, carry, start):
    p = pl.program_id(0)
    i = pl.program_id(1)
    tm = lg_ref.shape[0]
    ne, ng, eg = N_EXPERTS, N_EXPERT_GROUPS, EXPERTS_PER_GROUP
    lt = lg_ref[...].T[0:ne, :]
    scores = jax.nn.sigmoid(lt)
    biased = scores + bias_ref[...][0:ne, 0:1]
    grp = biased.reshape(ng, eg, tm)
    idx_e = lax.broadcasted_iota(jnp.int32, (ng, eg, tm), 1)
    m1 = jnp.max(grp, axis=1, keepdims=True)
    first = jnp.min(jnp.where(grp == m1, idx_e, eg), axis=1, keepdims=True)
    m2 = jnp.max(jnp.where(idx_e == first, NEG_BIG, grp), axis=1, keepdims=True)
    gscore = m1 + m2
    idx_g = lax.broadcasted_iota(jnp.int32, (ng, 1, tm), 0)
    gsel = jnp.zeros((ng, 1, tm), F32)
    for _ in range(TOPK_EXPERT_GROUPS):
        hit, _m = _first_argmax_mask(gscore, idx_g, ng)
        gsel = jnp.where(hit, 1.0, gsel)
        gscore = jnp.where(hit, NEG_BIG, gscore)
    emask = jnp.broadcast_to(gsel, (ng, eg, tm)).reshape(ne, tm)
    cand = jnp.where(emask > 0.0, biased, NEG_BIG)
    idx_x = lax.broadcasted_iota(jnp.int32, (ne, tm), 0)
    sel = jnp.zeros((ne, tm), F32)
    picks = []
    for _ in range(TOP_K):
        hit, _m = _first_argmax_mask(cand, idx_x, ne)
        sel = jnp.where(hit, 1.0, sel)
        cand = jnp.where(hit, NEG_BIG, cand)
        picks.append(hit)
    n_e = jnp.broadcast_to(jnp.sum(sel, axis=1, keepdims=True), (ne, LANES))

    @pl.when((p == 0) & (i == 0))
    def _():
        counts[...] = jnp.zeros(counts.shape, F32)

    @pl.when(p == 0)
    def _():
        counts[...] += n_e

    @pl.when((p == 1) & (i == 0))
    def _():
        r = lax.broadcasted_iota(jnp.int32, (ne, ne), 0)
        cc = lax.broadcasted_iota(jnp.int32, (ne, ne), 1)
        below = jnp.where(cc < r, 1.0, 0.0).astype(BF16)
        padded = jnp.floor((counts[...] + (GMM_BLOCK - 1.0)) * (1.0 / GMM_BLOCK)) * GMM_BLOCK
        start[...] = _dot01_exact(below, padded)
        carry[...] = jnp.zeros(carry.shape, F32)
        cnt_ref[...] = counts[...]

    @pl.when(p == 1)
    def _():
        before = _dot(sel.astype(BF16), su_ref[...])
        slot = before + carry[:, 0:1] + start[:, 0:1]
        dest_rows = []
        gate_rows = []
        for hit in picks:
            dest_rows.append(jnp.sum(jnp.where(hit, slot, 0.0), axis=0, keepdims=True))
            gate_rows.append(jnp.sum(jnp.where(hit, scores, 0.0), axis=0, keepdims=True))
        dest_ref[...] = jnp.concatenate(dest_rows, axis=0).astype(jnp.int32)
        gate = jnp.concatenate(gate_rows, axis=0)
        gate = gate / jnp.sum(gate, axis=0, keepdims=True) * ROUTED_SCALE
        pad = jnp.zeros((LANES - TOP_K, tm), F32)
        g_ref[...] = jnp.concatenate([gate, pad], axis=0).T
        carry[...] += n_e


def _route(logits, bias_col, su, tm):
    t = logits.shape[0]
    return pl.pallas_call(
        _route_kernel,
        grid=(2, t // tm),
        in_specs=[pl.BlockSpec((tm, LANES), lambda p, i: (i, 0)),
                  pl.BlockSpec((LANES, LANES), lambda p, i: (0, 0)),
                  pl.BlockSpec((tm, tm), lambda p, i: (0, 0))],
        out_specs=[pl.BlockSpec((tm, LANES), lambda p, i: (i * p, 0)),
                   pl.BlockSpec((TOP_K, tm), lambda p, i: (0, i * p)),
                   pl.BlockSpec((N_EXPERTS, LANES), lambda p, i: (0, 0))],
        out_shape=[jax.ShapeDtypeStruct((t, LANES), F32),
                   jax.ShapeDtypeStruct((TOP_K, t), jnp.int32),
                   jax.ShapeDtypeStruct((N_EXPERTS, LANES), F32)],
        scratch_shapes=[pltpu.VMEM((N_EXPERTS, LANES), F32),
                        pltpu.VMEM((N_EXPERTS, LANES), F32),
                        pltpu.VMEM((N_EXPERTS, LANES), F32)],
        compiler_params=_cparams(("arbitrary", "arbitrary")),
        name="route",
    )(logits, bias_col, su)


def _sc_mesh():
    return plsc.VectorSubcoreMesh(core_axis_name="c", subcore_axis_name="s")


def _sc_scatter_rows(rows, idx, n_out):
    n_rows, w = rows.shape
    n_k = idx.shape[0]

    @pl.kernel(out_type=jax.ShapeDtypeStruct((n_out, w), rows.dtype), mesh=_sc_mesh(),
               scratch_types=[])
    def scatter(x_hbm, i_hbm, o_hbm):
        def body(x_vmem, i_vmem):
            for k in range(n_k):
                pltpu.sync_copy(x_vmem, o_hbm.at[i_vmem.at[k]])

        pltpu.emit_pipeline(
            body,
            grid=(n_rows // SC_WINDOW,),
            in_specs=[pl.BlockSpec((SC_WINDOW, w), index_map=lambda i: (i, 0)),
                      pl.BlockSpec((n_k, SC_WINDOW), index_map=lambda i: (0, i))],
            out_specs=[],
            core_axis_name=("c", "s"),
            dimension_semantics=(pltpu.PARALLEL,),
        )(x_hbm, i_hbm)

    return scatter(rows, idx)


def _sc_gather_rows(rows, idx):
    n = idx.shape[1]
    w = rows.shape[1]

    @pl.kernel(out_type=jax.ShapeDtypeStruct((n, w), rows.dtype), mesh=_sc_mesh(),
               scratch_types=[])
    def gather(x_hbm, i_hbm, o_hbm):
        def body(i_vmem, o_vmem):
            pltpu.sync_copy(x_hbm.at[i_vmem.at[0]], o_vmem)

        pltpu.emit_pipeline(
            body,
            grid=(n // SC_WINDOW,),
            in_specs=[pl.BlockSpec((1, SC_WINDOW), index_map=lambda i: (0, i))],
            out_specs=[pl.BlockSpec((SC_WINDOW, w), index_map=lambda i: (i, 0))],
            core_axis_name=("c", "s"),
            dimension_semantics=(pltpu.PARALLEL,),
        )(i_hbm, o_hbm)

    return gather(rows, idx)


def _gmm_kernel(vb_ref, ve_ref, vvalid_ref, vnew_ref, x_ref, wg_ref, wu_ref, wdn_ref,
                o_ref, wgu_ref, wd_ref):
    v = pl.program_id(0)
    valid = vvalid_ref[v]
    bm = x_ref.shape[1]
    half = D_MODEL // 2

    @pl.when(vnew_ref[v] == 1)
    def _():
        wgu_ref[:, 0:D_EXPERT] = wg_ref[0].astype(BF16)
        wgu_ref[:, D_EXPERT:2 * D_EXPERT] = wu_ref[0].astype(BF16)
        wd_ref[...] = wdn_ref[0].astype(BF16)

    @pl.when(valid > 0)
    def _():
        for c in range(bm // GMM_ROWS):
            rows = slice(c * GMM_ROWS, (c + 1) * GMM_ROWS)
            a0, b0 = _unpack_pair(x_ref[0, rows, :])
            a1, b1 = _unpack_pair(x_ref[1, rows, :])
            x = jnp.concatenate([a0.astype(BF16), a1.astype(BF16),
                                 b0.astype(BF16), b1.astype(BF16)], axis=1)
            a = _dot(x, wgu_ref[...])
            act = _silu(a[:, 0:D_EXPERT]) * a[:, D_EXPERT:2 * D_EXPERT]
            r = lax.broadcasted_iota(jnp.int32, (GMM_ROWS, D_EXPERT), 0) + c * GMM_ROWS
            act = jnp.where(r < valid, act, 0.0)
            y = _dot(act.astype(BF16), wd_ref[...])
            w = _pack_pair(y[:, 0:half], y[:, half:D_MODEL])
            o_ref[0, rows, :] = w[:, 0:SC_ROW_WORDS]
            o_ref[1, rows, :] = w[:, SC_ROW_WORDS:half]


def _gmm(xin, w_gate, w_up, w_down, vb, ve, vvalid, vnew, bm):
    _, n_rows, w = xin.shape
    d = D_MODEL
    n_visits = vb.shape[0]
    blk = lambda v, vb, ve, vvalid, vnew: (0, vb[v], 0)
    wsel = lambda v, vb, ve, vvalid, vnew: (ve[v], 0, 0)
    grid_spec = pltpu.PrefetchScalarGridSpec(
        num_scalar_prefetch=4,
        grid=(n_visits,),
        in_specs=[pl.BlockSpec((2, bm, w), blk),
                  pl.BlockSpec((1, d, D_EXPERT), wsel),
                  pl.BlockSpec((1, d, D_EXPERT), wsel),
                  pl.BlockSpec((1, D_EXPERT, d), wsel)],
        out_specs=pl.BlockSpec((2, bm, w), blk),
        scratch_shapes=[pltpu.VMEM((d, 2 * D_EXPERT), BF16),
                        pltpu.VMEM((D_EXPERT, d), BF16)],
    )
    return pl.pallas_call(
        _gmm_kernel,
        grid_spec=grid_spec,
        out_shape=jax.ShapeDtypeStruct((2, n_rows, w), U32),
        compiler_params=_cparams(("arbitrary",)),
        name="gmm",
    )(vb, ve, vvalid, vnew, xin, w_gate, w_up, w_down)


def _visit_plan(counts, n_blocks, bm):
    ne = counts.shape[0]
    nb = (counts + bm - 1) // bm
    vend = jnp.cumsum(nb)
    vstart = vend - nb
    total = vend[-1]
    v = jnp.arange(n_blocks, dtype=jnp.int32)
    vc = jnp.minimum(v, total - 1)
    e_v = jnp.minimum(jnp.sum((vend[None, :] <= vc[:, None]).astype(jnp.int32), axis=1), ne - 1)
    onehot = (e_v[:, None] == jnp.arange(ne, dtype=jnp.int32)[None, :]).astype(jnp.int32)
    pick = lambda tab: jnp.sum(onehot * tab[None, :].astype(jnp.int32), axis=1)
    valid = jnp.clip(pick(counts) - (vc - pick(vstart)) * bm, 0, bm)
    valid = jnp.where(v < total, valid, 0)
    i32 = lambda a: a.astype(jnp.int32)
    e_v = i32(e_v)
    new = jnp.concatenate([jnp.ones((1,), jnp.int32), i32(e_v[1:] != e_v[:-1])])
    return i32(vc), e_v, i32(valid), new


def _combine_kernel(yg_ref, g_ref, h_ref, x1_ref, mod_ref, wsgu_ref, wsd_ref, lng_ref,
                    lnb_ref, *rest):
    o_ref = rest[-1]
    q = D_MODEL // 4
    g = g_ref[...]
    a = _dot(h_ref[...], wsgu_ref[...])
    act = _silu(a[:, 0:D_EXPERT]) * a[:, D_EXPERT:2 * D_EXPERT]
    shared = _dot(act.astype(BF16), wsd_ref[...])
    f = [shared[:, j * q:(j + 1) * q] for j in range(4)]
    for k in range(TOP_K):
        a0, b0 = _unpack_pair(yg_ref[0, k])
        a1, b1 = _unpack_pair(yg_ref[1, k])
        gk = g[:, k:k + 1]
        f = [f[0] + gk * a0, f[1] + gk * a1, f[2] + gk * b0, f[3] + gk * b1]
    ffn = jnp.concatenate(f, axis=1)
    m = mod_ref[0]
    r = ALPHA * x1_ref[...] + m[5:6, :] * ffn
    o_ref[...] = _ln(r) * lng_ref[...] + lnb_ref[...]


def _combine(yg, gates, h2, x1, mod3, wsgu, wsd, lng, lnb, seq, tm, tok0, t, prev_out):
    tp, d = x1.shape
    b0 = tok0 // tm
    row = lambda i: (i, 0)
    const = lambda i: (0, 0)
    in_specs = [pl.BlockSpec((2, TOP_K, tm, SC_ROW_WORDS), lambda i: (0, 0, i, 0)),
                pl.BlockSpec((tm, LANES), row),
                pl.BlockSpec((tm, d), row),
                pl.BlockSpec((tm, d), row),
                pl.BlockSpec((1, 6, d), lambda i: (((b0 + i) * tm) // seq, 0, 0)),
                pl.BlockSpec((d, 2 * D_EXPERT), const),
                pl.BlockSpec((D_EXPERT, d), const),
                pl.BlockSpec((1, d), const),
                pl.BlockSpec((1, d), const)]
    args = [yg, gates, h2, x1, mod3, wsgu, wsd, lng, lnb]
    aliases = {}
    if prev_out is not None:
        in_specs.append(pl.BlockSpec(memory_space=pl.ANY))
        args.append(prev_out)
        aliases = {len(args) - 1: 0}
    return pl.pallas_call(
        _combine_kernel,
        grid=(tp // tm,),
        in_specs=in_specs,
        out_specs=pl.BlockSpec((tm, d), lambda i: (b0 + i, 0)),
        out_shape=jax.ShapeDtypeStruct((t, d), F32),
        input_output_aliases=aliases,
        compiler_params=_cparams(("arbitrary",)),
        name="combine",
    )(*args)


def _pad_lanes(v, fill=0.0):
    out = jnp.full((1, LANES), fill, F32)
    return out.at[0, :v.shape[0]].set(v.astype(F32))


def kernel(x, c, w_ada, b_ada, w_in, ssd_conv_w, ssd_conv_b, ssd_dt_bias, ssd_A_log, ssd_D,
           ssd_norm_w, w_ssd_out, sc_conv_w, w_sc_out, w_o, ln1_g, ln1_b, router_w,
           router_bias, w_gate, w_up, w_down, sh_gate, sh_up, sh_down, ln2_g, ln2_b):
    bsz, seq, d = x.shape
    t = bsz * seq
    depth = w_in.shape[0]
    tm_proj = min(1024, seq)
    tm_merge = min(512, seq)
    tm_route = min(1024, t)
    tm_comb = min(512, seq)
    bm_gmm = GMM_BLOCK
    n_groups = 2 if bsz % 2 == 0 else 1
    bg = bsz // n_groups
    tg = bg * seq
    tm_route = min(tm_route, tg)
    n_assign = tg * TOP_K
    n_rows = n_assign + N_EXPERTS * bm_gmm

    e01 = (np.arange(LANES)[:, None] == (np.arange(SSD_INNER)[None, :] // SSD_HEAD_DIM))
    e01 = jnp.asarray(np.concatenate([e01, e01], axis=0), BF16)
    su = jnp.asarray(np.arange(tm_route)[:, None] < np.arange(tm_route)[None, :], BF16)

    xf = x.reshape(t, d)
    for l in range(depth):
        w_l = w_in[l]
        w_main = jnp.concatenate([w_l[:, :DT_COL0], w_l[:, DT_COL0 + SSD_HEADS:]],
                                 axis=1).astype(BF16)
        w_dt = jnp.pad(w_l[:, DT_COL0:DT_COL0 + SSD_HEADS],
                       ((0, 0), (0, LANES - SSD_HEADS))).astype(BF16)
        wsgu = jnp.concatenate([sh_gate[l], sh_up[l]], axis=-1).astype(BF16)
        wsd = sh_down[l].astype(BF16)
        wr = jnp.pad(router_w[l], ((0, 0), (0, LANES - N_EXPERTS)))
        wr_hi = wr.astype(BF16)
        wr = jnp.concatenate([wr_hi, (wr - wr_hi.astype(F32)).astype(BF16)], axis=1)
        bias_col = jnp.zeros((LANES, LANES), F32).at[:N_EXPERTS, 0].set(router_bias[l])
        dvec = jnp.repeat(ssd_D[l], SSD_HEAD_DIM).reshape(1, SSD_INNER)

        mod3 = _mod(c, w_ada[l], b_ada[l]).reshape(bsz, 6, d)
        x_in = xf

        def mix_and_route(g):
            p, dtp = _inproj(x_in, mod3, w_main, w_dt, seq, tm_proj, w_main.shape[1] // 4,
                             g * tg, tg)
            yn = _ssd(p, dtp, ssd_conv_w[l], ssd_conv_b[l].reshape(1, -1),
                      _pad_lanes(ssd_dt_bias[l]), _pad_lanes(ssd_A_log[l]), dvec,
                      ssd_norm_w[l].reshape(1, -1), e01, bg, seq, 4 * SSD_CHUNK)
            x1, h2, h2p, logits = _merge(yn, p, x_in, mod3, w_ssd_out[l].astype(BF16),
                                         w_sc_out[l].astype(BF16), w_o[l].astype(BF16),
                                         sc_conv_w[l], ln1_g[l].reshape(1, d),
                                         ln1_b[l].reshape(1, d), wr, bg, seq, tm_merge, g * bg)
            gates, dest, cnt = _route(logits, bias_col, su, tm_route)
            dest_both = jnp.concatenate([dest, dest + n_rows], axis=1)
            plan = _visit_plan(cnt[:, 0].astype(jnp.int32), n_rows // bm_gmm, bm_gmm)
            xin = _sc_scatter_rows(h2p.reshape(2 * tg, SC_ROW_WORDS), dest_both, 2 * n_rows)
            return dict(x1=x1, h2=h2, gates=gates, dest=dest, plan=plan, xin=xin)

        def experts(st):
            yb = _gmm(st["xin"].reshape(2, n_rows, SC_ROW_WORDS), w_gate[l], w_up[l], w_down[l],
                      *st["plan"], bm_gmm)
            idx = st["dest"].reshape(1, n_assign)
            idx = jnp.concatenate([idx, idx + n_rows], axis=1)
            yg = _sc_gather_rows(yb.reshape(2 * n_rows, SC_ROW_WORDS), idx)
            return yg.reshape(2, TOP_K, tg, SC_ROW_WORDS)

        stages = [mix_and_route(g) for g in range(n_groups)]
        gathered = [experts(st) for st in stages]
        out = None
        for g, (st, yg) in enumerate(zip(stages, gathered)):
            out = _combine(yg, st["gates"], st["h2"], st["x1"], mod3, wsgu, wsd,
                           ln2_g[l].reshape(1, d), ln2_b[l].reshape(1, d), seq, tm_comb,
                           g * tg, t, out)
        xf = out
    return xf.reshape(bsz, seq, d)
```

```python
import jax
import jax.numpy as jnp
import numpy as np
from jax import lax
from jax.experimental import pallas as pl
from jax.experimental.pallas import tpu as pltpu
from jax.experimental.pallas import tpu_sc as plsc

F32 = jnp.float32
BF16 = jnp.bfloat16
U32 = jnp.uint32

LANES = 128
SUBLANES = 8
VMEM_LIMIT_BYTES = 56 * 1024 * 1024

D_MODEL = 1024
SSD_INNER = 2048
SSD_HEAD_DIM = 64
SSD_HEADS = 32
SSD_GROUPS = 8
SSD_STATE = 128
SSD_CONV = 4
SSD_CHUNK = 128
GROUP_W = SSD_INNER // SSD_GROUPS
HEADS_PER_GROUP = SSD_HEADS // SSD_GROUPS
SC_KERNEL = 3
N_EXPERTS = 64
TOP_K = 8
N_EXPERT_GROUPS = 8
TOPK_EXPERT_GROUPS = 4
EXPERTS_PER_GROUP = N_EXPERTS // N_EXPERT_GROUPS
D_EXPERT = 256
ROUTED_SCALE = 2.5
LN_EPS = 1e-5
RMS_EPS = 1e-5
ALPHA = 2.0 ** 0.25
DT_COL0 = 2048 + 4096
NEG_BIG = -1e30
SC_WINDOW = 128
SC_ROW_WORDS = 256
GMM_BLOCK = 1024
GMM_ROWS = 256
MERGE_ROWS = 256


def _cparams(sem):
    return pltpu.CompilerParams(dimension_semantics=sem,
                                vmem_limit_bytes=VMEM_LIMIT_BYTES)


def _ln(x):
    mu = jnp.mean(x, axis=-1, keepdims=True)
    xc = x - mu
    var = jnp.mean(xc * xc, axis=-1, keepdims=True)
    return xc * lax.rsqrt(var + LN_EPS)


def _silu(x):
    h = 0.5 * x
    return h + h * jnp.tanh(h)


def _split3(a):
    hi = a.astype(BF16)
    r1 = a - hi.astype(F32)
    mid = r1.astype(BF16)
    lo = (r1 - mid.astype(F32)).astype(BF16)
    return hi, mid, lo


def _dot(a, b):
    return jnp.dot(a, b, preferred_element_type=F32)


def _dot_exact01(a, m01):
    hi, mid, lo = _split3(a)
    return _dot(hi, m01) + _dot(mid, m01) + _dot(lo, m01)


def _dot01_exact(m01, a):
    hi, mid, lo = _split3(a)
    return _dot(m01, hi) + _dot(m01, mid) + _dot(m01, lo)


def _pack_pair(a, b):
    ab = pltpu.bitcast(a.astype(BF16).astype(F32), U32)
    bb = pltpu.bitcast(b.astype(BF16).astype(F32), U32)
    return ab | lax.shift_right_logical(bb, jnp.uint32(16))


def _unpack_pair(w):
    a = pltpu.bitcast(w & jnp.uint32(0xFFFF0000), F32)
    b = pltpu.bitcast(lax.shift_left(w, jnp.uint32(16)), F32)
    return a, b


def _mod_kernel(c_ref, w_ref, b_ref, o_ref):
    a = _silu(c_ref[...])
    o_ref[...] = jnp.dot(a, w_ref[...], precision=lax.Precision.HIGHEST,
                         preferred_element_type=F32) + b_ref[...]


def _mod(c, w, b):
    bsz, d = c.shape
    n = w.shape[1]
    tn = 1024
    return pl.pallas_call(
        _mod_kernel,
        grid=(n // tn,),
        in_specs=[pl.BlockSpec((bsz, d), lambda j: (0, 0)),
                  pl.BlockSpec((d, tn), lambda j: (0, j)),
                  pl.BlockSpec((1, tn), lambda j: (0, j))],
        out_specs=pl.BlockSpec((bsz, tn), lambda j: (0, j)),
        out_shape=jax.ShapeDtypeStruct((bsz, n), F32),
        compiler_params=_cparams(("arbitrary",)),
        name="mod",
    )(c, w, b.reshape(1, n))


def _inproj_kernel(x_ref, mod_ref, w_ref, wdt_ref, p_ref, dt_ref, h_scr):
    j = pl.program_id(1)

    @pl.when(j == 0)
    def _():
        m = mod_ref[0]
        h = _ln(x_ref[...]) * (1.0 + m[1:2, :]) + m[0:1, :]
        hb = h.astype(BF16)
        h_scr[...] = hb
        dt_ref[...] = _dot(hb, wdt_ref[...])

    p_ref[...] = _dot(h_scr[...], w_ref[...]).astype(p_ref.dtype)


def _inproj(x2d, mod3, w_main, w_dt, seq, tm, tn, tok0, t):
    d = x2d.shape[1]
    n = w_main.shape[1]
    i0 = tok0 // tm
    return pl.pallas_call(
        _inproj_kernel,
        grid=(t // tm, n // tn),
        in_specs=[pl.BlockSpec((tm, d), lambda i, j: (i0 + i, 0)),
                  pl.BlockSpec((1, 6, d), lambda i, j: (((i0 + i) * tm) // seq, 0, 0)),
                  pl.BlockSpec((d, tn), lambda i, j: (0, j)),
                  pl.BlockSpec((d, LANES), lambda i, j: (0, 0))],
        out_specs=[pl.BlockSpec((tm, tn), lambda i, j: (i, j)),
                   pl.BlockSpec((tm, LANES), lambda i, j: (i, 0))],
        out_shape=[jax.ShapeDtypeStruct((t, n), BF16),
                   jax.ShapeDtypeStruct((t, LANES), F32)],
        scratch_shapes=[pltpu.VMEM((tm, d), BF16)],
        compiler_params=_cparams(("arbitrary", "arbitrary")),
        name="inproj",
    )(x2d, mod3, w_main, w_dt)


def _shift_conv(cur, prev_scr, s_ref, w_ref, b_ref):
    rows = cur.shape[0]
    n_taps = w_ref.shape[0]
    shifted = _dot(s_ref[...], jnp.concatenate([prev_scr[...], cur], axis=0))
    acc = w_ref[n_taps - 1:n_taps, :] * cur.astype(F32) + b_ref[...]
    for k in range(n_taps - 1):
        acc = acc + w_ref[k:k + 1, :] * shifted[k * rows:(k + 1) * rows]
    prev_scr[...] = cur
    return acc


def _shift_matrix(rows, n_taps):
    s = np.zeros(((n_taps - 1) * rows, 2 * rows), np.float32)
    t = np.arange(rows)
    for k in range(n_taps - 1):
        s[k * rows + t, rows + t - (n_taps - 1) + k] = 1.0
    return jnp.asarray(s, BF16)


def _ssd_kernel(xs_ref, bc_ref, z_ref, dtp_ref, cw_xs_ref, cw_bc_ref, cb_xs_ref,
                cb_bc_ref, dtb_ref, alog_ref, dvec_ref, nw_ref, e_ref, s_ref,
                yn_ref, prev_xs, prev_bc, state):
    @pl.when(pl.program_id(1) == 0)
    def _():
        prev_xs[...] = jnp.zeros(prev_xs.shape, BF16)
        prev_bc[...] = jnp.zeros(prev_bc.shape, BF16)
        state[...] = jnp.zeros(state.shape, F32)

    for cc in range(xs_ref.shape[0] // SSD_CHUNK):
        _ssd_chunk(slice(cc * SSD_CHUNK, (cc + 1) * SSD_CHUNK), xs_ref, bc_ref, z_ref, dtp_ref,
                   cw_xs_ref, cw_bc_ref, cb_xs_ref, cb_bc_ref, dtb_ref, alog_ref, dvec_ref,
                   nw_ref, e_ref, s_ref, yn_ref, prev_xs, prev_bc, state)


def _ssd_chunk(rows, xs_ref, bc_ref, z_ref, dtp_ref, cw_xs_ref, cw_bc_ref, cb_xs_ref,
               cb_bc_ref, dtb_ref, alog_ref, dvec_ref, nw_ref, e_ref, s_ref,
               yn_ref, prev_xs, prev_bc, state):
    L = SSD_CHUNK
    xs = _silu(_shift_conv(xs_ref[rows, :], prev_xs, s_ref, cw_xs_ref, cb_xs_ref))
    bc = _silu(_shift_conv(bc_ref[rows, :], prev_bc, s_ref, cw_bc_ref, cb_bc_ref))

    x_dt = dtp_ref[rows, :] + dtb_ref[...]
    dt = jnp.maximum(x_dt, 0.0) + jnp.log1p(jnp.exp(-jnp.abs(x_dt)))
    a_neg = -jnp.exp(alog_ref[...])
    d_a = dt * a_neg
    row = lax.broadcasted_iota(jnp.int32, (L, L), 0)
    col = lax.broadcasted_iota(jnp.int32, (L, L), 1)
    causal = col <= row
    tri = jnp.where(causal, 1.0, 0.0).astype(BF16)
    hi, mid, lo = _split3(d_a)
    a_cs = _dot(tri, hi) + _dot(tri, mid) + _dot(tri, lo)
    a_cs_t = a_cs.T
    a_last = a_cs[L - 1:L, :]
    stack = jnp.concatenate([dt, jnp.exp(a_cs), jnp.exp(a_last - a_cs)], axis=0)
    s_hi = stack.astype(BF16)
    s_mid = (stack - s_hi.astype(F32)).astype(BF16)
    st_e = _dot(jnp.concatenate([s_hi, s_mid], axis=1), e_ref[...])
    dt_e = st_e[0:L]
    od_e = st_e[L:2 * L]
    ds_e = st_e[2 * L:3 * L]
    cd_e = od_e[L - 1:L, :]

    x_f = xs * dt_e
    x_b = x_f.astype(BF16)
    xd_b = (x_f * ds_e).astype(BF16)
    lane_head = lax.broadcasted_iota(jnp.int32, (L, GROUP_W), 1) // SSD_HEAD_DIM

    for g in range(SSD_GROUPS):
        gc = slice(g * GROUP_W, (g + 1) * GROUP_W)
        b_g = bc[:, g * SSD_STATE:(g + 1) * SSD_STATE]
        c_g = bc[:, SSD_GROUPS * SSD_STATE + g * SSD_STATE:
                 SSD_GROUPS * SSD_STATE + (g + 1) * SSD_STATE].astype(BF16)
        b_gt = b_g.T.astype(BF16)
        cb = _dot(c_g, b_gt)
        x_g = x_b[:, gc]
        lhs = []
        rhs = []
        for r in range(HEADS_PER_GROUP):
            h = g * HEADS_PER_GROUP + r
            seg = a_cs[:, h:h + 1] - a_cs_t[h:h + 1, :]
            lmat = jnp.exp(jnp.where(causal, seg, NEG_BIG))
            lhs.append((cb * lmat).astype(BF16))
            rhs.append(jnp.where(lane_head == r, x_g, jnp.zeros_like(x_g)))
        y_diag = _dot(jnp.concatenate(lhs, axis=1), jnp.concatenate(rhs, axis=0))
        st = state[g]
        y_off = _dot(c_g, st.astype(BF16)) * od_e[:, gc]
        state[g] = cd_e[:, gc] * st + _dot(b_gt, xd_b[:, gc])
        y = y_diag + y_off + dvec_ref[:, gc] * xs[:, gc]
        gt = y * _silu(z_ref[rows, gc].astype(F32))
        ms = jnp.mean(gt * gt, axis=-1, keepdims=True)
        yn_ref[rows, gc] = (gt * lax.rsqrt(ms + RMS_EPS) * nw_ref[:, gc]).astype(yn_ref.dtype)


def _ssd(p, dtp, cw, cb, dtb, alog, dvec, nw, e01, bsz, seq, rows):
    t = p.shape[0]
    L = SSD_CHUNK
    nc = seq // rows
    row_map = lambda b, c: (b * nc + c, 0)

    def pcol(k):
        return pl.BlockSpec((rows, SSD_INNER), lambda b, c: (b * nc + c, k))

    const = lambda b, c: (0, 0)
    return pl.pallas_call(
        _ssd_kernel,
        grid=(bsz, nc),
        in_specs=[pcol(1), pcol(2), pcol(0),
                  pl.BlockSpec((rows, LANES), row_map),
                  pl.BlockSpec((SSD_CONV, SSD_INNER), lambda b, c: (0, 0)),
                  pl.BlockSpec((SSD_CONV, SSD_INNER), lambda b, c: (0, 1)),
                  pl.BlockSpec((1, SSD_INNER), lambda b, c: (0, 0)),
                  pl.BlockSpec((1, SSD_INNER), lambda b, c: (0, 1)),
                  pl.BlockSpec((1, LANES), const),
                  pl.BlockSpec((1, LANES), const),
                  pl.BlockSpec((1, SSD_INNER), const),
                  pl.BlockSpec((1, SSD_INNER), const),
                  pl.BlockSpec((2 * LANES, SSD_INNER), const),
                  pl.BlockSpec(((SSD_CONV - 1) * L, 2 * L), const)],
        out_specs=pl.BlockSpec((rows, SSD_INNER), row_map),
        out_shape=jax.ShapeDtypeStruct((t, SSD_INNER), BF16),
        scratch_shapes=[pltpu.VMEM((L, SSD_INNER), BF16),
                        pltpu.VMEM((L, SSD_INNER), BF16),
                        pltpu.VMEM((SSD_GROUPS, SSD_STATE, GROUP_W), F32)],
        compiler_params=_cparams(("arbitrary", "arbitrary")),
        name="ssd",
    )(p, p, p, dtp, cw, cw, cb, cb, dtb, alog, dvec, nw, e01, _shift_matrix(L, SSD_CONV))


def _merge_kernel(yn_ref, scb_ref, scc_ref, sch_ref, ga_ref, gb_ref, x_ref, mod_ref,
                  wssd_ref, wsc_ref, wo_ref, cw_ref, lng_ref, lnb_ref, wr_ref,
                  x1_ref, h2_ref, h2p_ref, lg_ref, ext):
    s = pl.program_id(1)
    tm = x_ref.shape[0]

    @pl.when(s == 0)
    def _():
        ext[0:SUBLANES, :] = jnp.zeros((SUBLANES, D_MODEL), F32)

    ext[SUBLANES:SUBLANES + tm, :] = scc_ref[...].astype(F32) * sch_ref[...].astype(F32)
    m = mod_ref[0]
    for c in range(tm // MERGE_ROWS):
        r0 = c * MERGE_ROWS
        rows = slice(r0, r0 + MERGE_ROWS)
        u = None
        for k in range(SC_KERNEL):
            start = SUBLANES + r0 - (SC_KERNEL - 1) + k
            term = cw_ref[k:k + 1, :] * ext[start:start + MERGE_ROWS, :]
            u = term if u is None else u + term
        y_b = _dot((scb_ref[rows, :].astype(F32) * u).astype(BF16), wsc_ref[...])
        y_a = _dot(yn_ref[rows, :], wssd_ref[...])
        merged = (jax.nn.sigmoid(ga_ref[rows, :].astype(F32)) * y_a
                  + jax.nn.sigmoid(gb_ref[rows, :].astype(F32)) * y_b)
        mix = _dot(merged.astype(BF16), wo_ref[...])
        x1 = _ln(ALPHA * x_ref[rows, :] + m[2:3, :] * mix) * lng_ref[...] + lnb_ref[...]
        x1_ref[rows, :] = x1
        h2 = _ln(x1) * (1.0 + m[4:5, :]) + m[3:4, :]
        h2_hi = h2.astype(BF16)
        h2_ref[rows, :] = h2_hi
        w = _pack_pair(h2[:, 0:D_MODEL // 2], h2[:, D_MODEL // 2:D_MODEL])
        h2p_ref[0, rows, :] = w[:, 0:SC_ROW_WORDS]
        h2p_ref[1, rows, :] = w[:, SC_ROW_WORDS:2 * SC_ROW_WORDS]
        h2_lo = (h2 - h2_hi.astype(F32)).astype(BF16)
        both = _dot(h2_hi, wr_ref[...])
        lg_ref[rows, :] = (both[:, 0:LANES] + both[:, LANES:2 * LANES]
                           + _dot(h2_lo, wr_ref[:, 0:LANES]))
    ext[0:SUBLANES, :] = ext[tm:tm + SUBLANES, :]


def _merge(yn, p, x2d, mod3, wssd, wsc, wo, cw, lng, lnb, wr, bsz, seq, tm, b0):
    t = bsz * seq
    d = x2d.shape[1]
    ns = seq // tm
    row_map = lambda b, s: (b * ns + s, 0)
    const = lambda b, s: (0, 0)
    col0 = (2048 + 4096) // d

    def pcol(k):
        return pl.BlockSpec((tm, d), lambda b, s: (b * ns + s, col0 + k))

    return pl.pallas_call(
        _merge_kernel,
        grid=(bsz, ns),
        in_specs=[pl.BlockSpec((tm, SSD_INNER), row_map),
                  pcol(0), pcol(1), pcol(2), pcol(3), pcol(4),
                  pl.BlockSpec((tm, d), lambda b, s: ((b0 + b) * ns + s, 0)),
                  pl.BlockSpec((1, 6, d), lambda b, s: (b0 + b, 0, 0)),
                  pl.BlockSpec((SSD_INNER, d), const, pipeline_mode=pl.Buffered(1)),
                  pl.BlockSpec((d, d), const, pipeline_mode=pl.Buffered(1)),
                  pl.BlockSpec((d, d), const, pipeline_mode=pl.Buffered(1)),
                  pl.BlockSpec((SC_KERNEL, d), const),
                  pl.BlockSpec((1, d), const),
                  pl.BlockSpec((1, d), const),
                  pl.BlockSpec((d, 2 * LANES), const, pipeline_mode=pl.Buffered(1))],
        out_specs=[pl.BlockSpec((tm, d), row_map),
                   pl.BlockSpec((tm, d), row_map),
                   pl.BlockSpec((2, tm, SC_ROW_WORDS), lambda b, s: (0, b * ns + s, 0)),
                   pl.BlockSpec((tm, LANES), row_map)],
        out_shape=[jax.ShapeDtypeStruct((t, d), F32),
                   jax.ShapeDtypeStruct((t, d), BF16),
                   jax.ShapeDtypeStruct((2, t, SC_ROW_WORDS), U32),
                   jax.ShapeDtypeStruct((t, LANES), F32)],
        scratch_shapes=[pltpu.VMEM((tm + SUBLANES, d), F32)],
        compiler_params=_cparams(("arbitrary", "arbitrary")),
        name="merge",
    )(yn, p, p, p, p, p, x2d, mod3, wssd, wsc, wo, cw, lng, lnb, wr)


def _first_argmax_mask(v, idx, n):
    m = jnp.max(v, axis=0, keepdims=True)
    first = jnp.min(jnp.where(v == m, idx, n), axis=0, keepdims=True)
    return idx == first, m


def _route_kernel(lg_ref, bias_ref, su_ref, g_ref, dest_ref, cnt_ref, counts, carry, start):
    p = pl.program_id(0)
    i = pl.program_id(1)
    tm = lg_ref.shape[0]
    ne, ng, eg = N_EXPERTS, N_EXPERT_GROUPS, EXPERTS_PER_GROUP
    lt = lg_ref[...].T[0:ne, :]
    scores = jax.nn.sigmoid(lt)
    biased = scores + bias_ref[...][0:ne, 0:1]
    grp = biased.reshape(ng, eg, tm)
    idx_e = lax.broadcasted_iota(jnp.int32, (ng, eg, tm), 1)
    m1 = jnp.max(grp, axis=1, keepdims=True)
    first = jnp.min(jnp.where(grp == m1, idx_e, eg), axis=1, keepdims=True)
    m2 = jnp.max(jnp.where(idx_e == first, NEG_BIG, grp), axis=1, keepdims=True)
    gscore = m1 + m2
    idx_g = lax.broadcasted_iota(jnp.int32, (ng, 1, tm), 0)
    gsel = jnp.zeros((ng, 1, tm), F32)
    for _ in range(TOPK_EXPERT_GROUPS):
        hit, _m = _first_argmax_mask(gscore, idx_g, ng)
        gsel = jnp.where(hit, 1.0, gsel)
        gscore = jnp.where(hit, NEG_BIG, gscore)
    emask = jnp.broadcast_to(gsel, (ng, eg, tm)).reshape(ne, tm)
    cand = jnp.where(emask > 0.0, biased, NEG_BIG)
    idx_x = lax.broadcasted_iota(jnp.int32, (ne, tm), 0)
    sel = jnp.zeros((ne, tm), F32)
    picks = []
    for _ in range(TOP_K):
        hit, _m = _first_argmax_mask(cand, idx_x, ne)
        sel = jnp.where(hit, 1.0, sel)
        cand = jnp.where(hit, NEG_BIG, cand)
        picks.append(hit)
    n_e = jnp.broadcast_to(jnp.sum(sel, axis=1, keepdims=True), (ne, LANES))

    @pl.when((p == 0) & (i == 0))
    def _():
        counts[...] = jnp.zeros(counts.shape, F32)

    @pl.when(p == 0)
    def _():
        counts[...] += n_e

    @pl.when((p == 1) & (i == 0))
    def _():
        r = lax.broadcasted_iota(jnp.int32, (ne, ne), 0)
        cc = lax.broadcasted_iota(jnp.int32, (ne, ne), 1)
        below = jnp.where(cc < r, 1.0, 0.0).astype(BF16)
        padded = jnp.floor((counts[...] + (GMM_BLOCK - 1.0)) * (1.0 / GMM_BLOCK)) * GMM_BLOCK
        start[...] = _dot01_exact(below, padded)
        carry[...] = jnp.zeros(carry.shape, F32)
        cnt_ref[...] = counts[...]

    @pl.when(p == 1)
    def _():
        before = _dot(sel.astype(BF16), su_ref[...])
        slot = before + carry[:, 0:1] + start[:, 0:1]
        dest_rows = []
        gate_rows = []
        for hit in picks:
            dest_rows.append(jnp.sum(jnp.where(hit, slot, 0.0), axis=0, keepdims=True))
            gate_rows.append(jnp.sum(jnp.where(hit, scores, 0.0), axis=0, keepdims=True))
        dest_ref[...] = jnp.concatenate(dest_rows, axis=0).astype(jnp.int32)
        gate = jnp.concatenate(gate_rows, axis=0)
        gate = gate / jnp.sum(gate, axis=0, keepdims=True) * ROUTED_SCALE
        pad = jnp.zeros((LANES - TOP_K, tm), F32)
        g_ref[...] = jnp.concatenate([gate, pad], axis=0).T
        carry[...] += n_e


def _route(logits, bias_col, su, tm):
    t = logits.shape[0]
    return pl.pallas_call(
        _route_kernel,
        grid=(2, t // tm),
        in_specs=[pl.BlockSpec((tm, LANES), lambda p, i: (i, 0)),
                  pl.BlockSpec((LANES, LANES), lambda p, i: (0, 0)),
                  pl.BlockSpec((tm, tm), lambda p, i: (0, 0))],
        out_specs=[pl.BlockSpec((tm, LANES), lambda p, i: (i * p, 0)),
                   pl.BlockSpec((TOP_K, tm), lambda p, i: (0, i * p)),
                   pl.BlockSpec((N_EXPERTS, LANES), lambda p, i: (0, 0))],
        out_shape=[jax.ShapeDtypeStruct((t, LANES), F32),
                   jax.ShapeDtypeStruct((TOP_K, t), jnp.int32),
                   jax.ShapeDtypeStruct((N_EXPERTS, LANES), F32)],
        scratch_shapes=[pltpu.VMEM((N_EXPERTS, LANES), F32),
                        pltpu.VMEM((N_EXPERTS, LANES), F32),
                        pltpu.VMEM((N_EXPERTS, LANES), F32)],
        compiler_params=_cparams(("arbitrary", "arbitrary")),
        name="route",
    )(logits, bias_col, su)


def _sc_mesh():
    return plsc.VectorSubcoreMesh(core_axis_name="c", subcore_axis_name="s")


def _sc_scatter_rows(rows, idx, n_out):
    n_rows, w = rows.shape
    n_k = idx.shape[0]

    @pl.kernel(out_type=jax.ShapeDtypeStruct((n_out, w), rows.dtype), mesh=_sc_mesh(),
               scratch_types=[])
    def scatter(x_hbm, i_hbm, o_hbm):
        def body(x_vmem, i_vmem):
            for k in range(n_k):
                pltpu.sync_copy(x_vmem, o_hbm.at[i_vmem.at[k]])

        pltpu.emit_pipeline(
            body,
            grid=(n_rows // SC_WINDOW,),
            in_specs=[pl.BlockSpec((SC_WINDOW, w), index_map=lambda i: (i, 0)),
                      pl.BlockSpec((n_k, SC_WINDOW), index_map=lambda i: (0, i))],
            out_specs=[],
            core_axis_name=("c", "s"),
            dimension_semantics=(pltpu.PARALLEL,),
        )(x_hbm, i_hbm)

    return scatter(rows, idx)


def _sc_gather_rows(rows, idx):
    n = idx.shape[1]
    w = rows.shape[1]

    @pl.kernel(out_type=jax.ShapeDtypeStruct((n, w), rows.dtype), mesh=_sc_mesh(),
               scratch_types=[])
    def gather(x_hbm, i_hbm, o_hbm):
        def body(i_vmem, o_vmem):
            pltpu.sync_copy(x_hbm.at[i_vmem.at[0]], o_vmem)

        pltpu.emit_pipeline(
            body,
            grid=(n // SC_WINDOW,),
            in_specs=[pl.BlockSpec((1, SC_WINDOW), index_map=lambda i: (0, i))],
            out_specs=[pl.BlockSpec((SC_WINDOW, w), index_map=lambda i: (i, 0))],
            core_axis_name=("c", "s"),
            dimension_semantics=(pltpu.PARALLEL,),
        )(i_hbm, o_hbm)

    return gather(rows, idx)


def _gmm_kernel(vb_ref, ve_ref, vvalid_ref, vnew_ref, x_ref, wg_ref, wu_ref, wdn_ref,
                o_ref, wgu_ref, wd_ref):
    v = pl.program_id(0)
    valid = vvalid_ref[v]
    bm = x_ref.shape[1]
    half = D_MODEL // 2

    @pl.when(vnew_ref[v] == 1)
    def _():
        wgu_ref[:, 0:D_EXPERT] = wg_ref[0].astype(BF16)
        wgu_ref[:, D_EXPERT:2 * D_EXPERT] = wu_ref[0].astype(BF16)
        wd_ref[...] = wdn_ref[0].astype(BF16)

    @pl.when(valid > 0)
    def _():
        for c in range(bm // GMM_ROWS):
            rows = slice(c * GMM_ROWS, (c + 1) * GMM_ROWS)
            a0, b0 = _unpack_pair(x_ref[0, rows, :])
            a1, b1 = _unpack_pair(x_ref[1, rows, :])
            x = jnp.concatenate([a0.astype(BF16), a1.astype(BF16),
                                 b0.astype(BF16), b1.astype(BF16)], axis=1)
            a = _dot(x, wgu_ref[...])
            act = _silu(a[:, 0:D_EXPERT]) * a[:, D_EXPERT:2 * D_EXPERT]
            r = lax.broadcasted_iota(jnp.int32, (GMM_ROWS, D_EXPERT), 0) + c * GMM_ROWS
            act = jnp.where(r < valid, act, 0.0)
            y = _dot(act.astype(BF16), wd_ref[...])
            w = _pack_pair(y[:, 0:half], y[:, half:D_MODEL])
            o_ref[0, rows, :] = w[:, 0:SC_ROW_WORDS]
            o_ref[1, rows, :] = w[:, SC_ROW_WORDS:half]


def _gmm(xin, w_gate, w_up, w_down, vb, ve, vvalid, vnew, bm):
    _, n_rows, w = xin.shape
    d = D_MODEL
    n_visits = vb.shape[0]
    blk = lambda v, vb, ve, vvalid, vnew: (0, vb[v], 0)
    wsel = lambda v, vb, ve, vvalid, vnew: (ve[v], 0, 0)
    grid_spec = pltpu.PrefetchScalarGridSpec(
        num_scalar_prefetch=4,
        grid=(n_visits,),
        in_specs=[pl.BlockSpec((2, bm, w), blk),
                  pl.BlockSpec((1, d, D_EXPERT), wsel),
                  pl.BlockSpec((1, d, D_EXPERT), wsel),
                  pl.BlockSpec((1, D_EXPERT, d), wsel)],
        out_specs=pl.BlockSpec((2, bm, w), blk),
        scratch_shapes=[pltpu.VMEM((d, 2 * D_EXPERT), BF16),
                        pltpu.VMEM((D_EXPERT, d), BF16)],
    )
    return pl.pallas_call(
        _gmm_kernel,
        grid_spec=grid_spec,
        out_shape=jax.ShapeDtypeStruct((2, n_rows, w), U32),
        compiler_params=_cparams(("arbitrary",)),
        name="gmm",
    )(vb, ve, vvalid, vnew, xin, w_gate, w_up, w_down)


def _visit_plan(counts, n_blocks, bm):
    ne = counts.shape[0]
    nb = (counts + bm - 1) // bm
    vend = jnp.cumsum(nb)
    vstart = vend - nb
    total = vend[-1]
    v = jnp.arange(n_blocks, dtype=jnp.int32)
    vc = jnp.minimum(v, total - 1)
    e_v = jnp.minimum(jnp.sum((vend[None, :] <= vc[:, None]).astype(jnp.int32), axis=1), ne - 1)
    onehot = (e_v[:, None] == jnp.arange(ne, dtype=jnp.int32)[None, :]).astype(jnp.int32)
    pick = lambda tab: jnp.sum(onehot * tab[None, :].astype(jnp.int32), axis=1)
    valid = jnp.clip(pick(counts) - (vc - pick(vstart)) * bm, 0, bm)
    valid = jnp.where(v < total, valid, 0)
    i32 = lambda a: a.astype(jnp.int32)
    e_v = i32(e_v)
    new = jnp.concatenate([jnp.ones((1,), jnp.int32), i32(e_v[1:] != e_v[:-1])])
    return i32(vc), e_v, i32(valid), new


def _combine_kernel(yg_ref, g_ref, h_ref, x1_ref, mod_ref, wsgu_ref, wsd_ref, lng_ref,
                    lnb_ref, *rest):
    o_ref = rest[-1]
    q = D_MODEL // 4
    g = g_ref[...]
    a = _dot(h_ref[...], wsgu_ref[...])
    act = _silu(a[:, 0:D_EXPERT]) * a[:, D_EXPERT:2 * D_EXPERT]
    shared = _dot(act.astype(BF16), wsd_ref[...])
    f = [shared[:, j * q:(j + 1) * q] for j in range(4)]
    for k in range(TOP_K):
        a0, b0 = _unpack_pair(yg_ref[0, k])
        a1, b1 = _unpack_pair(yg_ref[1, k])
        gk = g[:, k:k + 1]
        f = [f[0] + gk * a0, f[1] + gk * a1, f[2] + gk * b0, f[3] + gk * b1]
    ffn = jnp.concatenate(f, axis=1)
    m = mod_ref[0]
    r = ALPHA * x1_ref[...] + m[5:6, :] * ffn
    o_ref[...] = _ln(r) * lng_ref[...] + lnb_ref[...]


def _combine(yg, gates, h2, x1, mod3, wsgu, wsd, lng, lnb, seq, tm, tok0, t, prev_out):
    tp, d = x1.shape
    b0 = tok0 // tm
    row = lambda i: (i, 0)
    const = lambda i: (0, 0)
    in_specs = [pl.BlockSpec((2, TOP_K, tm, SC_ROW_WORDS), lambda i: (0, 0, i, 0)),
                pl.BlockSpec((tm, LANES), row),
                pl.BlockSpec((tm, d), row),
                pl.BlockSpec((tm, d), row),
                pl.BlockSpec((1, 6, d), lambda i: (((b0 + i) * tm) // seq, 0, 0)),
                pl.BlockSpec((d, 2 * D_EXPERT), const),
                pl.BlockSpec((D_EXPERT, d), const),
                pl.BlockSpec((1, d), const),
                pl.BlockSpec((1, d), const)]
    args = [yg, gates, h2, x1, mod3, wsgu, wsd, lng, lnb]
    aliases = {}
    if prev_out is not None:
        in_specs.append(pl.BlockSpec(memory_space=pl.ANY))
        args.append(prev_out)
        aliases = {len(args) - 1: 0}
    return pl.pallas_call(
        _combine_kernel,
        grid=(tp // tm,),
        in_specs=in_specs,
        out_specs=pl.BlockSpec((tm, d), lambda i: (b0 + i, 0)),
        out_shape=jax.ShapeDtypeStruct((t, d), F32),
        input_output_aliases=aliases,
        compiler_params=_cparams(("arbitrary",)),
        name="combine",
    )(*args)


def _pad_lanes(v, fill=0.0):
    out = jnp.full((1, LANES), fill, F32)
    return out.at[0, :v.shape[0]].set(v.astype(F32))


def kernel(x, c, w_ada, b_ada, w_in, ssd_conv_w, ssd_conv_b, ssd_dt_bias, ssd_A_log, ssd_D,
           ssd_norm_w, w_ssd_out, sc_conv_w, w_sc_out, w_o, ln1_g, ln1_b, router_w,
           router_bias, w_gate, w_up, w_down, sh_gate, sh_up, sh_down, ln2_g, ln2_b):
    bsz, seq, d = x.shape
    t = bsz * seq
    depth = w_in.shape[0]
    tm_proj = min(1024, seq)
    tm_merge = min(512, seq)
    tm_route = min(1024, seq)
    tm_comb = min(512, seq)
    bm_gmm = GMM_BLOCK
    first = (5 * bsz) // 8 if bsz >= 2 else bsz
    groups = [(0, first)] + ([(first, bsz - first)] if bsz > first else [])

    e01 = (np.arange(LANES)[:, None] == (np.arange(SSD_INNER)[None, :] // SSD_HEAD_DIM))
    e01 = jnp.asarray(np.concatenate([e01, e01], axis=0), BF16)
    su = jnp.asarray(np.arange(tm_route)[:, None] < np.arange(tm_route)[None, :], BF16)

    xf = x.reshape(t, d)
    for l in range(depth):
        w_l = w_in[l]
        w_main = jnp.concatenate([w_l[:, :DT_COL0], w_l[:, DT_COL0 + SSD_HEADS:]],
                                 axis=1).astype(BF16)
        w_dt = jnp.pad(w_l[:, DT_COL0:DT_COL0 + SSD_HEADS],
                       ((0, 0), (0, LANES - SSD_HEADS))).astype(BF16)
        wsgu = jnp.concatenate([sh_gate[l], sh_up[l]], axis=-1).astype(BF16)
        wsd = sh_down[l].astype(BF16)
        wr = jnp.pad(router_w[l], ((0, 0), (0, LANES - N_EXPERTS)))
        wr_hi = wr.astype(BF16)
        wr = jnp.concatenate([wr_hi, (wr - wr_hi.astype(F32)).astype(BF16)], axis=1)
        bias_col = jnp.zeros((LANES, LANES), F32).at[:N_EXPERTS, 0].set(router_bias[l])
        dvec = jnp.repeat(ssd_D[l], SSD_HEAD_DIM).reshape(1, SSD_INNER)

        mod3 = _mod(c, w_ada[l], b_ada[l]).reshape(bsz, 6, d)
        x_in = xf

        def mix_and_route(b0, bg):
            tg = bg * seq
            n_assign = tg * TOP_K
            n_rows = n_assign + N_EXPERTS * bm_gmm
            p, dtp = _inproj(x_in, mod3, w_main, w_dt, seq, tm_proj, w_main.shape[1] // 4,
                             b0 * seq, tg)
            yn = _ssd(p, dtp, ssd_conv_w[l], ssd_conv_b[l].reshape(1, -1),
                      _pad_lanes(ssd_dt_bias[l]), _pad_lanes(ssd_A_log[l]), dvec,
                      ssd_norm_w[l].reshape(1, -1), e01, bg, seq, 4 * SSD_CHUNK)
            x1, h2, h2p, logits = _merge(yn, p, x_in, mod3, w_ssd_out[l].astype(BF16),
                                         w_sc_out[l].astype(BF16), w_o[l].astype(BF16),
                                         sc_conv_w[l], ln1_g[l].reshape(1, d),
                                         ln1_b[l].reshape(1, d), wr, bg, seq, tm_merge, b0)
            gates, dest, cnt = _route(logits, bias_col, su, tm_route)
            dest_both = jnp.concatenate([dest, dest + n_rows], axis=1)
            plan = _visit_plan(cnt[:, 0].astype(jnp.int32), n_rows // bm_gmm, bm_gmm)
            xin = _sc_scatter_rows(h2p.reshape(2 * tg, SC_ROW_WORDS), dest_both, 2 * n_rows)
            return dict(x1=x1, h2=h2, gates=gates, dest=dest, plan=plan, xin=xin, tok0=b0 * seq,
                        tg=tg, n_assign=n_assign, n_rows=n_rows)

        def experts(st):
            tg, n_assign, n_rows = st["tg"], st["n_assign"], st["n_rows"]
            yb = _gmm(st["xin"].reshape(2, n_rows, SC_ROW_WORDS), w_gate[l], w_up[l], w_down[l],
                      *st["plan"], bm_gmm)
            idx = st["dest"].reshape(1, n_assign)
            idx = jnp.concatenate([idx, idx + n_rows], axis=1)
            yg = _sc_gather_rows(yb.reshape(2 * n_rows, SC_ROW_WORDS), idx)
            return yg.reshape(2, TOP_K, tg, SC_ROW_WORDS)

        stages = [mix_and_route(b0, bg) for b0, bg in groups]
        gathered = [experts(st) for st in stages]
        out = None
        for st, yg in zip(stages, gathered):
            out = _combine(yg, st["gates"], st["h2"], st["x1"], mod3, wsgu, wsd,
                           ln2_g[l].reshape(1, d), ln2_b[l].reshape(1, d), seq, tm_comb,
                           st["tok0"], t, out)
        xf = out
    return xf.reshape(bsz, seq, d)
```

```python
import jax
import jax.numpy as jnp
import numpy as np
from jax import lax
from jax.experimental import pallas as pl
from jax.experimental.pallas import tpu as pltpu
from jax.experimental.pallas import tpu_sc as plsc

F32 = jnp.float32
BF16 = jnp.bfloat16
U32 = jnp.uint32

LANES = 128
SUBLANES = 8
VMEM_LIMIT_BYTES = 56 * 1024 * 1024

D_MODEL = 1024
SSD_INNER = 2048
SSD_HEAD_DIM = 64
SSD_HEADS = 32
SSD_GROUPS = 8
SSD_STATE = 128
SSD_CONV = 4
SSD_CHUNK = 128
GROUP_W = SSD_INNER // SSD_GROUPS
HEADS_PER_GROUP = SSD_HEADS // SSD_GROUPS
SC_KERNEL = 3
N_EXPERTS = 64
TOP_K = 8
N_EXPERT_GROUPS = 8
TOPK_EXPERT_GROUPS = 4
EXPERTS_PER_GROUP = N_EXPERTS // N_EXPERT_GROUPS
D_EXPERT = 256
ROUTED_SCALE = 2.5
LN_EPS = 1e-5
RMS_EPS = 1e-5
ALPHA = 2.0 ** 0.25
DT_COL0 = 2048 + 4096
NEG_BIG = -1e30
SC_WINDOW = 128
SC_ROW_WORDS = 256
GMM_BLOCK = 512
GMM_ROWS = 256
MERGE_ROWS = 256


def _cparams(sem):
    return pltpu.CompilerParams(dimension_semantics=sem,
                                vmem_limit_bytes=VMEM_LIMIT_BYTES)


def _ln(x):
    mu = jnp.mean(x, axis=-1, keepdims=True)
    xc = x - mu
    var = jnp.mean(xc * xc, axis=-1, keepdims=True)
    return xc * lax.rsqrt(var + LN_EPS)


def _silu(x):
    h = 0.5 * x
    return h + h * jnp.tanh(h)


def _split3(a):
    hi = a.astype(BF16)
    r1 = a - hi.astype(F32)
    mid = r1.astype(BF16)
    lo = (r1 - mid.astype(F32)).astype(BF16)
    return hi, mid, lo


def _dot(a, b):
    return jnp.dot(a, b, preferred_element_type=F32)


def _dot_exact01(a, m01):
    hi, mid, lo = _split3(a)
    return _dot(hi, m01) + _dot(mid, m01) + _dot(lo, m01)


def _dot01_exact(m01, a):
    hi, mid, lo = _split3(a)
    return _dot(m01, hi) + _dot(m01, mid) + _dot(m01, lo)


def _pack_pair(a, b):
    ab = pltpu.bitcast(a.astype(BF16).astype(F32), U32)
    bb = pltpu.bitcast(b.astype(BF16).astype(F32), U32)
    return ab | lax.shift_right_logical(bb, jnp.uint32(16))


def _unpack_pair(w):
    a = pltpu.bitcast(w & jnp.uint32(0xFFFF0000), F32)
    b = pltpu.bitcast(lax.shift_left(w, jnp.uint32(16)), F32)
    return a, b


def _mod_kernel(c_ref, w_ref, b_ref, o_ref):
    a = _silu(c_ref[...])
    o_ref[...] = jnp.dot(a, w_ref[...], precision=lax.Precision.HIGHEST,
                         preferred_element_type=F32) + b_ref[...]


def _mod(c, w, b):
    bsz, d = c.shape
    n = w.shape[1]
    tn = 1024
    return pl.pallas_call(
        _mod_kernel,
        grid=(n // tn,),
        in_specs=[pl.BlockSpec((bsz, d), lambda j: (0, 0)),
                  pl.BlockSpec((d, tn), lambda j: (0, j)),
                  pl.BlockSpec((1, tn), lambda j: (0, j))],
        out_specs=pl.BlockSpec((bsz, tn), lambda j: (0, j)),
        out_shape=jax.ShapeDtypeStruct((bsz, n), F32),
        compiler_params=_cparams(("arbitrary",)),
        name="mod",
    )(c, w, b.reshape(1, n))


def _inproj_kernel(x_ref, mod_ref, w_ref, wdt_ref, p_ref, dt_ref, h_scr):
    j = pl.program_id(1)

    @pl.when(j == 0)
    def _():
        m = mod_ref[0]
        h = _ln(x_ref[...]) * (1.0 + m[1:2, :]) + m[0:1, :]
        hb = h.astype(BF16)
        h_scr[...] = hb
        dt_ref[...] = _dot(hb, wdt_ref[...])

    p_ref[...] = _dot(h_scr[...], w_ref[...]).astype(p_ref.dtype)


def _inproj(x2d, mod3, w_main, w_dt, seq, tm, tn, tok0, t):
    d = x2d.shape[1]
    n = w_main.shape[1]
    i0 = tok0 // tm
    return pl.pallas_call(
        _inproj_kernel,
        grid=(t // tm, n // tn),
        in_specs=[pl.BlockSpec((tm, d), lambda i, j: (i0 + i, 0)),
                  pl.BlockSpec((1, 6, d), lambda i, j: (((i0 + i) * tm) // seq, 0, 0)),
                  pl.BlockSpec((d, tn), lambda i, j: (0, j)),
                  pl.BlockSpec((d, LANES), lambda i, j: (0, 0))],
        out_specs=[pl.BlockSpec((tm, tn), lambda i, j: (i, j)),
                   pl.BlockSpec((tm, LANES), lambda i, j: (i, 0))],
        out_shape=[jax.ShapeDtypeStruct((t, n), BF16),
                   jax.ShapeDtypeStruct((t, LANES), F32)],
        scratch_shapes=[pltpu.VMEM((tm, d), BF16)],
        compiler_params=_cparams(("arbitrary", "arbitrary")),
        name="inproj",
    )(x2d, mod3, w_main, w_dt)


def _shift_conv(cur, prev_scr, s_ref, w_ref, b_ref):
    rows = cur.shape[0]
    n_taps = w_ref.shape[0]
    shifted = _dot(s_ref[...], jnp.concatenate([prev_scr[...], cur], axis=0))
    acc = w_ref[n_taps - 1:n_taps, :] * cur.astype(F32) + b_ref[...]
    for k in range(n_taps - 1):
        acc = acc + w_ref[k:k + 1, :] * shifted[k * rows:(k + 1) * rows]
    prev_scr[...] = cur
    return acc


def _shift_matrix(rows, n_taps):
    s = np.zeros(((n_taps - 1) * rows, 2 * rows), np.float32)
    t = np.arange(rows)
    for k in range(n_taps - 1):
        s[k * rows + t, rows + t - (n_taps - 1) + k] = 1.0
    return jnp.asarray(s, BF16)


def _ssd_kernel(xs_ref, bc_ref, z_ref, dtp_ref, cw_xs_ref, cw_bc_ref, cb_xs_ref,
                cb_bc_ref, dtb_ref, alog_ref, dvec_ref, nw_ref, e_ref, s_ref,
                yn_ref, prev_xs, prev_bc, state):
    @pl.when(pl.program_id(1) == 0)
    def _():
        prev_xs[...] = jnp.zeros(prev_xs.shape, BF16)
        prev_bc[...] = jnp.zeros(prev_bc.shape, BF16)
        state[...] = jnp.zeros(state.shape, F32)

    for cc in range(xs_ref.shape[0] // SSD_CHUNK):
        _ssd_chunk(slice(cc * SSD_CHUNK, (cc + 1) * SSD_CHUNK), xs_ref, bc_ref, z_ref, dtp_ref,
                   cw_xs_ref, cw_bc_ref, cb_xs_ref, cb_bc_ref, dtb_ref, alog_ref, dvec_ref,
                   nw_ref, e_ref, s_ref, yn_ref, prev_xs, prev_bc, state)


def _ssd_chunk(rows, xs_ref, bc_ref, z_ref, dtp_ref, cw_xs_ref, cw_bc_ref, cb_xs_ref,
               cb_bc_ref, dtb_ref, alog_ref, dvec_ref, nw_ref, e_ref, s_ref,
               yn_ref, prev_xs, prev_bc, state):
    L = SSD_CHUNK
    xs = _silu(_shift_conv(xs_ref[rows, :], prev_xs, s_ref, cw_xs_ref, cb_xs_ref))
    bc = _silu(_shift_conv(bc_ref[rows, :], prev_bc, s_ref, cw_bc_ref, cb_bc_ref))

    x_dt = dtp_ref[rows, :] + dtb_ref[...]
    dt = jnp.maximum(x_dt, 0.0) + jnp.log1p(jnp.exp(-jnp.abs(x_dt)))
    a_neg = -jnp.exp(alog_ref[...])
    d_a = dt * a_neg
    row = lax.broadcasted_iota(jnp.int32, (L, L), 0)
    col = lax.broadcasted_iota(jnp.int32, (L, L), 1)
    causal = col <= row
    tri = jnp.where(causal, 1.0, 0.0).astype(BF16)
    hi, mid, lo = _split3(d_a)
    a_cs = _dot(tri, hi) + _dot(tri, mid) + _dot(tri, lo)
    a_cs_t = a_cs.T
    a_last = a_cs[L - 1:L, :]
    stack = jnp.concatenate([dt, jnp.exp(a_cs), jnp.exp(a_last - a_cs)], axis=0)
    s_hi = stack.astype(BF16)
    s_mid = (stack - s_hi.astype(F32)).astype(BF16)
    st_e = _dot(jnp.concatenate([s_hi, s_mid], axis=1), e_ref[...])
    dt_e = st_e[0:L]
    od_e = st_e[L:2 * L]
    ds_e = st_e[2 * L:3 * L]
    cd_e = od_e[L - 1:L, :]

    x_f = xs * dt_e
    x_b = x_f.astype(BF16)
    xd_b = (x_f * ds_e).astype(BF16)
    lane_head = lax.broadcasted_iota(jnp.int32, (L, GROUP_W), 1) // SSD_HEAD_DIM

    for g in range(SSD_GROUPS):
        gc = slice(g * GROUP_W, (g + 1) * GROUP_W)
        b_g = bc[:, g * SSD_STATE:(g + 1) * SSD_STATE]
        c_g = bc[:, SSD_GROUPS * SSD_STATE + g * SSD_STATE:
                 SSD_GROUPS * SSD_STATE + (g + 1) * SSD_STATE].astype(BF16)
        b_gt = b_g.T.astype(BF16)
        cb = _dot(c_g, b_gt)
        x_g = x_b[:, gc]
        lhs = []
        rhs = []
        for r in range(HEADS_PER_GROUP):
            h = g * HEADS_PER_GROUP + r
            seg = a_cs[:, h:h + 1] - a_cs_t[h:h + 1, :]
            lmat = jnp.exp(jnp.where(causal, seg, NEG_BIG))
            lhs.append((cb * lmat).astype(BF16))
            rhs.append(jnp.where(lane_head == r, x_g, jnp.zeros_like(x_g)))
        y_diag = _dot(jnp.concatenate(lhs, axis=1), jnp.concatenate(rhs, axis=0))
        st = state[g]
        y_off = _dot(c_g, st.astype(BF16)) * od_e[:, gc]
        state[g] = cd_e[:, gc] * st + _dot(b_gt, xd_b[:, gc])
        y = y_diag + y_off + dvec_ref[:, gc] * xs[:, gc]
        gt = y * _silu(z_ref[rows, gc].astype(F32))
        ms = jnp.mean(gt * gt, axis=-1, keepdims=True)
        yn_ref[rows, gc] = (gt * lax.rsqrt(ms + RMS_EPS) * nw_ref[:, gc]).astype(yn_ref.dtype)


def _ssd(p, dtp, cw, cb, dtb, alog, dvec, nw, e01, bsz, seq, rows):
    t = p.shape[0]
    L = SSD_CHUNK
    nc = seq // rows
    row_map = lambda b, c: (b * nc + c, 0)

    def pcol(k):
        return pl.BlockSpec((rows, SSD_INNER), lambda b, c: (b * nc + c, k))

    const = lambda b, c: (0, 0)
    return pl.pallas_call(
        _ssd_kernel,
        grid=(bsz, nc),
        in_specs=[pcol(1), pcol(2), pcol(0),
                  pl.BlockSpec((rows, LANES), row_map),
                  pl.BlockSpec((SSD_CONV, SSD_INNER), lambda b, c: (0, 0)),
                  pl.BlockSpec((SSD_CONV, SSD_INNER), lambda b, c: (0, 1)),
                  pl.BlockSpec((1, SSD_INNER), lambda b, c: (0, 0)),
                  pl.BlockSpec((1, SSD_INNER), lambda b, c: (0, 1)),
                  pl.BlockSpec((1, LANES), const),
                  pl.BlockSpec((1, LANES), const),
                  pl.BlockSpec((1, SSD_INNER), const),
                  pl.BlockSpec((1, SSD_INNER), const),
                  pl.BlockSpec((2 * LANES, SSD_INNER), const),
                  pl.BlockSpec(((SSD_CONV - 1) * L, 2 * L), const)],
        out_specs=pl.BlockSpec((rows, SSD_INNER), row_map),
        out_shape=jax.ShapeDtypeStruct((t, SSD_INNER), BF16),
        scratch_shapes=[pltpu.VMEM((L, SSD_INNER), BF16),
                        pltpu.VMEM((L, SSD_INNER), BF16),
                        pltpu.VMEM((SSD_GROUPS, SSD_STATE, GROUP_W), F32)],
        compiler_params=_cparams(("arbitrary", "arbitrary")),
        name="ssd",
    )(p, p, p, dtp, cw, cw, cb, cb, dtb, alog, dvec, nw, e01, _shift_matrix(L, SSD_CONV))


def _merge_kernel(yn_ref, scb_ref, scc_ref, sch_ref, ga_ref, gb_ref, x_ref, mod_ref,
                  wssd_ref, wsc_ref, wo_ref, cw_ref, lng_ref, lnb_ref, wr_ref,
                  x1_ref, h2_ref, h2p_ref, lg_ref, ext):
    s = pl.program_id(1)
    tm = x_ref.shape[0]

    @pl.when(s == 0)
    def _():
        ext[0:SUBLANES, :] = jnp.zeros((SUBLANES, D_MODEL), F32)

    ext[SUBLANES:SUBLANES + tm, :] = scc_ref[...].astype(F32) * sch_ref[...].astype(F32)
    m = mod_ref[0]
    for c in range(tm // MERGE_ROWS):
        r0 = c * MERGE_ROWS
        rows = slice(r0, r0 + MERGE_ROWS)
        u = None
        for k in range(SC_KERNEL):
            start = SUBLANES + r0 - (SC_KERNEL - 1) + k
            term = cw_ref[k:k + 1, :] * ext[start:start + MERGE_ROWS, :]
            u = term if u is None else u + term
        y_b = _dot((scb_ref[rows, :].astype(F32) * u).astype(BF16), wsc_ref[...])
        y_a = _dot(yn_ref[rows, :], wssd_ref[...])
        merged = (jax.nn.sigmoid(ga_ref[rows, :].astype(F32)) * y_a
                  + jax.nn.sigmoid(gb_ref[rows, :].astype(F32)) * y_b)
        mix = _dot(merged.astype(BF16), wo_ref[...])
        x1 = _ln(ALPHA * x_ref[rows, :] + m[2:3, :] * mix) * lng_ref[...] + lnb_ref[...]
        x1_ref[rows, :] = x1
        h2 = _ln(x1) * (1.0 + m[4:5, :]) + m[3:4, :]
        h2_hi = h2.astype(BF16)
        h2_ref[rows, :] = h2_hi
        w = _pack_pair(h2[:, 0:D_MODEL // 2], h2[:, D_MODEL // 2:D_MODEL])
        h2p_ref[0, rows, :] = w[:, 0:SC_ROW_WORDS]
        h2p_ref[1, rows, :] = w[:, SC_ROW_WORDS:2 * SC_ROW_WORDS]
        h2_lo = (h2 - h2_hi.astype(F32)).astype(BF16)
        both = _dot(h2_hi, wr_ref[...])
        lg_ref[rows, :] = (both[:, 0:LANES] + both[:, LANES:2 * LANES]
                           + _dot(h2_lo, wr_ref[:, 0:LANES]))
    ext[0:SUBLANES, :] = ext[tm:tm + SUBLANES, :]


def _merge(yn, p, x2d, mod3, wssd, wsc, wo, cw, lng, lnb, wr, bsz, seq, tm, b0):
    t = bsz * seq
    d = x2d.shape[1]
    ns = seq // tm
    row_map = lambda b, s: (b * ns + s, 0)
    const = lambda b, s: (0, 0)
    col0 = (2048 + 4096) // d

    def pcol(k):
        return pl.BlockSpec((tm, d), lambda b, s: (b * ns + s, col0 + k))

    return pl.pallas_call(
        _merge_kernel,
        grid=(bsz, ns),
        in_specs=[pl.BlockSpec((tm, SSD_INNER), row_map),
                  pcol(0), pcol(1), pcol(2), pcol(3), pcol(4),
                  pl.BlockSpec((tm, d), lambda b, s: ((b0 + b) * ns + s, 0)),
                  pl.BlockSpec((1, 6, d), lambda b, s: (b0 + b, 0, 0)),
                  pl.BlockSpec((SSD_INNER, d), const, pipeline_mode=pl.Buffered(1)),
                  pl.BlockSpec((d, d), const, pipeline_mode=pl.Buffered(1)),
                  pl.BlockSpec((d, d), const, pipeline_mode=pl.Buffered(1)),
                  pl.BlockSpec((SC_KERNEL, d), const),
                  pl.BlockSpec((1, d), const),
                  pl.BlockSpec((1, d), const),
                  pl.BlockSpec((d, 2 * LANES), const, pipeline_mode=pl.Buffered(1))],
        out_specs=[pl.BlockSpec((tm, d), row_map),
                   pl.BlockSpec((tm, d), row_map),
                   pl.BlockSpec((2, tm, SC_ROW_WORDS), lambda b, s: (0, b * ns + s, 0)),
                   pl.BlockSpec((tm, LANES), row_map)],
        out_shape=[jax.ShapeDtypeStruct((t, d), F32),
                   jax.ShapeDtypeStruct((t, d), BF16),
                   jax.ShapeDtypeStruct((2, t, SC_ROW_WORDS), U32),
                   jax.ShapeDtypeStruct((t, LANES), F32)],
        scratch_shapes=[pltpu.VMEM((tm + SUBLANES, d), F32)],
        compiler_params=_cparams(("arbitrary", "arbitrary")),
        name="merge",
    )(yn, p, p, p, p, p, x2d, mod3, wssd, wsc, wo, cw, lng, lnb, wr)


def _first_argmax_mask(v, idx, n):
    m = jnp.max(v, axis=0, keepdims=True)
    first = jnp.min(jnp.where(v == m, idx, n), axis=0, keepdims=True)
    return idx == first, m


def _route_kernel(lg_ref, bias_ref, su_ref, g_ref, dest_ref, cnt_ref, counts, carry, start):
    p = pl.program_id(0)
    i = pl.program_id(1)
    tm = lg_ref.shape[0]
    ne, ng, eg = N_EXPERTS, N_EXPERT_GROUPS, EXPERTS_PER_GROUP
    lt = lg_ref[...].T[0:ne, :]
    scores = jax.nn.sigmoid(lt)
    biased = scores + bias_ref[...][0:ne, 0:1]
    grp = biased.reshape(ng, eg, tm)
    idx_e = lax.broadcasted_iota(jnp.int32, (ng, eg, tm), 1)
    m1 = jnp.max(grp, axis=1, keepdims=True)
    first = jnp.min(jnp.where(grp == m1, idx_e, eg), axis=1, keepdims=True)
    m2 = jnp.max(jnp.where(idx_e == first, NEG_BIG, grp), axis=1, keepdims=True)
    gscore = m1 + m2
    idx_g = lax.broadcasted_iota(jnp.int32, (ng, 1, tm), 0)
    gsel = jnp.zeros((ng, 1, tm), F32)
    for _ in range(TOPK_EXPERT_GROUPS):
        hit, _m = _first_argmax_mask(gscore, idx_g, ng)
        gsel = jnp.where(hit, 1.0, gsel)
        gscore = jnp.where(hit, NEG_BIG, gscore)
    emask = jnp.broadcast_to(gsel, (ng, eg, tm)).reshape(ne, tm)
    cand = jnp.where(emask > 0.0, biased, NEG_BIG)
    idx_x = lax.broadcasted_iota(jnp.int32, (ne, tm), 0)
    sel = jnp.zeros((ne, tm), F32)
    picks = []
    for _ in range(TOP_K):
        hit, _m = _first_argmax_mask(cand, idx_x, ne)
        sel = jnp.where(hit, 1.0, sel)
        cand = jnp.where(hit, NEG_BIG, cand)
        picks.append(hit)
    n_e = jnp.broadcast_to(jnp.sum(sel, axis=1, keepdims=True), (ne, LANES))

    @pl.when((p == 0) & (i == 0))
    def _():
        counts[...] = jnp.zeros(counts.shape, F32)

    @pl.when(p == 0)
    def _():
        counts[...] += n_e

    @pl.when((p == 1) & (i == 0))
    def _():
        r = lax.broadcasted_iota(jnp.int32, (ne, ne), 0)
        cc = lax.broadcasted_iota(jnp.int32, (ne, ne), 1)
        below = jnp.where(cc < r, 1.0, 0.0).astype(BF16)
        padded = jnp.floor((counts[...] + (GMM_BLOCK - 1.0)) * (1.0 / GMM_BLOCK)) * GMM_BLOCK
        start[...] = _dot01_exact(below, padded)
        carry[...] = jnp.zeros(carry.shape, F32)
        cnt_ref[...] = counts[...]

    @pl.when(p == 1)
    def _():
        before = _dot(sel.astype(BF16), su_ref[...])
        slot = before + carry[:, 0:1] + start[:, 0:1]
        dest_rows = []
        gate_rows = []
        for hit in picks:
            dest_rows.append(jnp.sum(jnp.where(hit, slot, 0.0), axis=0, keepdims=True))
            gate_rows.append(jnp.sum(jnp.where(hit, scores, 0.0), axis=0, keepdims=True))
        dest_ref[...] = jnp.concatenate(dest_rows, axis=0).astype(jnp.int32)
        gate = jnp.concatenate(gate_rows, axis=0)
        gate = gate / jnp.sum(gate, axis=0, keepdims=True) * ROUTED_SCALE
        pad = jnp.zeros((LANES - TOP_K, tm), F32)
        g_ref[...] = jnp.concatenate([gate, pad], axis=0).T
        carry[...] += n_e


def _route(logits, bias_col, su, tm):
    t = logits.shape[0]
    return pl.pallas_call(
        _route_kernel,
        grid=(2, t // tm),
        in_specs=[pl.BlockSpec((tm, LANES), lambda p, i: (i, 0)),
                  pl.BlockSpec((LANES, LANES), lambda p, i: (0, 0)),
                  pl.BlockSpec((tm, tm), lambda p, i: (0, 0))],
        out_specs=[pl.BlockSpec((tm, LANES), lambda p, i: (i * p, 0)),
                   pl.BlockSpec((TOP_K, tm), lambda p, i: (0, i * p)),
                   pl.BlockSpec((N_EXPERTS, LANES), lambda p, i: (0, 0))],
        out_shape=[jax.ShapeDtypeStruct((t, LANES), F32),
                   jax.ShapeDtypeStruct((TOP_K, t), jnp.int32),
                   jax.ShapeDtypeStruct((N_EXPERTS, LANES), F32)],
        scratch_shapes=[pltpu.VMEM((N_EXPERTS, LANES), F32),
                        pltpu.VMEM((N_EXPERTS, LANES), F32),
                        pltpu.VMEM((N_EXPERTS, LANES), F32)],
        compiler_params=_cparams(("arbitrary", "arbitrary")),
        name="route",
    )(logits, bias_col, su)


def _sc_mesh():
    return plsc.VectorSubcoreMesh(core_axis_name="c", subcore_axis_name="s")


def _sc_scatter_rows(rows, idx, n_out):
    n_rows, w = rows.shape
    n_k = idx.shape[0]

    @pl.kernel(out_type=jax.ShapeDtypeStruct((n_out, w), rows.dtype), mesh=_sc_mesh(),
               scratch_types=[])
    def scatter(x_hbm, i_hbm, o_hbm):
        def body(x_vmem, i_vmem):
            for k in range(n_k):
                pltpu.sync_copy(x_vmem, o_hbm.at[i_vmem.at[k]])

        pltpu.emit_pipeline(
            body,
            grid=(n_rows // SC_WINDOW,),
            in_specs=[pl.BlockSpec((SC_WINDOW, w), index_map=lambda i: (i, 0)),
                      pl.BlockSpec((n_k, SC_WINDOW), index_map=lambda i: (0, i))],
            out_specs=[],
            core_axis_name=("c", "s"),
            dimension_semantics=(pltpu.PARALLEL,),
        )(x_hbm, i_hbm)

    return scatter(rows, idx)


def _sc_gather_rows(rows, idx):
    n = idx.shape[1]
    w = rows.shape[1]

    @pl.kernel(out_type=jax.ShapeDtypeStruct((n, w), rows.dtype), mesh=_sc_mesh(),
               scratch_types=[])
    def gather(x_hbm, i_hbm, o_hbm):
        def body(i_vmem, o_vmem):
            pltpu.sync_copy(x_hbm.at[i_vmem.at[0]], o_vmem)

        pltpu.emit_pipeline(
            body,
            grid=(n // SC_WINDOW,),
            in_specs=[pl.BlockSpec((1, SC_WINDOW), index_map=lambda i: (0, i))],
            out_specs=[pl.BlockSpec((SC_WINDOW, w), index_map=lambda i: (i, 0))],
            core_axis_name=("c", "s"),
            dimension_semantics=(pltpu.PARALLEL,),
        )(i_hbm, o_hbm)

    return gather(rows, idx)


def _gmm_kernel(vb_ref, ve_ref, vvalid_ref, vnew_ref, x_ref, wg_ref, wu_ref, wdn_ref,
                o_ref, wgu_ref, wd_ref):
    v = pl.program_id(0)
    valid = vvalid_ref[v]
    bm = x_ref.shape[1]
    half = D_MODEL // 2

    @pl.when(vnew_ref[v] == 1)
    def _():
        wgu_ref[:, 0:D_EXPERT] = wg_ref[0].astype(BF16)
        wgu_ref[:, D_EXPERT:2 * D_EXPERT] = wu_ref[0].astype(BF16)
        wd_ref[...] = wdn_ref[0].astype(BF16)

    @pl.when(valid > 0)
    def _():
        for c in range(bm // GMM_ROWS):
            rows = slice(c * GMM_ROWS, (c + 1) * GMM_ROWS)
            a0, b0 = _unpack_pair(x_ref[0, rows, :])
            a1, b1 = _unpack_pair(x_ref[1, rows, :])
            x = jnp.concatenate([a0.astype(BF16), a1.astype(BF16),
                                 b0.astype(BF16), b1.astype(BF16)], axis=1)
            a = _dot(x, wgu_ref[...])
            act = _silu(a[:, 0:D_EXPERT]) * a[:, D_EXPERT:2 * D_EXPERT]
            r = lax.broadcasted_iota(jnp.int32, (GMM_ROWS, D_EXPERT), 0) + c * GMM_ROWS
            act = jnp.where(r < valid, act, 0.0)
            y = _dot(act.astype(BF16), wd_ref[...])
            w = _pack_pair(y[:, 0:half], y[:, half:D_MODEL])
            o_ref[0, rows, :] = w[:, 0:SC_ROW_WORDS]
            o_ref[1, rows, :] = w[:, SC_ROW_WORDS:half]


def _gmm(xin, w_gate, w_up, w_down, vb, ve, vvalid, vnew, bm):
    _, n_rows, w = xin.shape
    d = D_MODEL
    n_visits = vb.shape[0]
    blk = lambda v, vb, ve, vvalid, vnew: (0, vb[v], 0)
    wsel = lambda v, vb, ve, vvalid, vnew: (ve[v], 0, 0)
    grid_spec = pltpu.PrefetchScalarGridSpec(
        num_scalar_prefetch=4,
        grid=(n_visits,),
        in_specs=[pl.BlockSpec((2, bm, w), blk),
                  pl.BlockSpec((1, d, D_EXPERT), wsel),
                  pl.BlockSpec((1, d, D_EXPERT), wsel),
                  pl.BlockSpec((1, D_EXPERT, d), wsel)],
        out_specs=pl.BlockSpec((2, bm, w), blk),
        scratch_shapes=[pltpu.VMEM((d, 2 * D_EXPERT), BF16),
                        pltpu.VMEM((D_EXPERT, d), BF16)],
    )
    return pl.pallas_call(
        _gmm_kernel,
        grid_spec=grid_spec,
        out_shape=jax.ShapeDtypeStruct((2, n_rows, w), U32),
        compiler_params=_cparams(("arbitrary",)),
        name="gmm",
    )(vb, ve, vvalid, vnew, xin, w_gate, w_up, w_down)


def _visit_plan(counts, n_blocks, bm):
    ne = counts.shape[0]
    nb = (counts + bm - 1) // bm
    vend = jnp.cumsum(nb)
    vstart = vend - nb
    total = vend[-1]
    v = jnp.arange(n_blocks, dtype=jnp.int32)
    vc = jnp.minimum(v, total - 1)
    e_v = jnp.minimum(jnp.sum((vend[None, :] <= vc[:, None]).astype(jnp.int32), axis=1), ne - 1)
    onehot = (e_v[:, None] == jnp.arange(ne, dtype=jnp.int32)[None, :]).astype(jnp.int32)
    pick = lambda tab: jnp.sum(onehot * tab[None, :].astype(jnp.int32), axis=1)
    valid = jnp.clip(pick(counts) - (vc - pick(vstart)) * bm, 0, bm)
    valid = jnp.where(v < total, valid, 0)
    i32 = lambda a: a.astype(jnp.int32)
    e_v = i32(e_v)
    new = jnp.concatenate([jnp.ones((1,), jnp.int32), i32(e_v[1:] != e_v[:-1])])
    return i32(vc), e_v, i32(valid), new


def _combine_kernel(yg_ref, g_ref, h_ref, x1_ref, mod_ref, wsgu_ref, wsd_ref, lng_ref,
                    lnb_ref, *rest):
    o_ref = rest[-1]
    q = D_MODEL // 4
    g = g_ref[...]
    a = _dot(h_ref[...], wsgu_ref[...])
    act = _silu(a[:, 0:D_EXPERT]) * a[:, D_EXPERT:2 * D_EXPERT]
    shared = _dot(act.astype(BF16), wsd_ref[...])
    f = [shared[:, j * q:(j + 1) * q] for j in range(4)]
    for k in range(TOP_K):
        a0, b0 = _unpack_pair(yg_ref[0, k])
        a1, b1 = _unpack_pair(yg_ref[1, k])
        gk = g[:, k:k + 1]
        f = [f[0] + gk * a0, f[1] + gk * a1, f[2] + gk * b0, f[3] + gk * b1]
    ffn = jnp.concatenate(f, axis=1)
    m = mod_ref[0]
    r = ALPHA * x1_ref[...] + m[5:6, :] * ffn
    o_ref[...] = _ln(r) * lng_ref[...] + lnb_ref[...]


def _combine(yg, gates, h2, x1, mod3, wsgu, wsd, lng, lnb, seq, tm, tok0, t, prev_out):
    tp, d = x1.shape
    b0 = tok0 // tm
    row = lambda i: (i, 0)
    const = lambda i: (0, 0)
    in_specs = [pl.BlockSpec((2, TOP_K, tm, SC_ROW_WORDS), lambda i: (0, 0, i, 0)),
                pl.BlockSpec((tm, LANES), row),
                pl.BlockSpec((tm, d), row),
                pl.BlockSpec((tm, d), row),
                pl.BlockSpec((1, 6, d), lambda i: (((b0 + i) * tm) // seq, 0, 0)),
                pl.BlockSpec((d, 2 * D_EXPERT), const),
                pl.BlockSpec((D_EXPERT, d), const),
                pl.BlockSpec((1, d), const),
                pl.BlockSpec((1, d), const)]
    args = [yg, gates, h2, x1, mod3, wsgu, wsd, lng, lnb]
    aliases = {}
    if prev_out is not None:
        in_specs.append(pl.BlockSpec(memory_space=pl.ANY))
        args.append(prev_out)
        aliases = {len(args) - 1: 0}
    return pl.pallas_call(
        _combine_kernel,
        grid=(tp // tm,),
        in_specs=in_specs,
        out_specs=pl.BlockSpec((tm, d), lambda i: (b0 + i, 0)),
        out_shape=jax.ShapeDtypeStruct((t, d), F32),
        input_output_aliases=aliases,
        compiler_params=_cparams(("arbitrary",)),
        name="combine",
    )(*args)


def _pad_lanes(v, fill=0.0):
    out = jnp.full((1, LANES), fill, F32)
    return out.at[0, :v.shape[0]].set(v.astype(F32))


def kernel(x, c, w_ada, b_ada, w_in, ssd_conv_w, ssd_conv_b, ssd_dt_bias, ssd_A_log, ssd_D,
           ssd_norm_w, w_ssd_out, sc_conv_w, w_sc_out, w_o, ln1_g, ln1_b, router_w,
           router_bias, w_gate, w_up, w_down, sh_gate, sh_up, sh_down, ln2_g, ln2_b):
    bsz, seq, d = x.shape
    t = bsz * seq
    depth = w_in.shape[0]
    tm_proj = min(1024, seq)
    tm_merge = min(512, seq)
    tm_route = min(1024, seq)
    tm_comb = min(512, seq)
    bm_gmm = GMM_BLOCK
    first = (5 * bsz) // 8 if bsz >= 2 else bsz
    groups = [(0, first)] + ([(first, bsz - first)] if bsz > first else [])

    e01 = (np.arange(LANES)[:, None] == (np.arange(SSD_INNER)[None, :] // SSD_HEAD_DIM))
    e01 = jnp.asarray(np.concatenate([e01, e01], axis=0), BF16)
    su = jnp.asarray(np.arange(tm_route)[:, None] < np.arange(tm_route)[None, :], BF16)

    xf = x.reshape(t, d)
    for l in range(depth):
        w_l = w_in[l]
        w_main = jnp.concatenate([w_l[:, :DT_COL0], w_l[:, DT_COL0 + SSD_HEADS:]],
                                 axis=1).astype(BF16)
        w_dt = jnp.pad(w_l[:, DT_COL0:DT_COL0 + SSD_HEADS],
                       ((0, 0), (0, LANES - SSD_HEADS))).astype(BF16)
        wsgu = jnp.concatenate([sh_gate[l], sh_up[l]], axis=-1).astype(BF16)
        wsd = sh_down[l].astype(BF16)
        wr = jnp.pad(router_w[l], ((0, 0), (0, LANES - N_EXPERTS)))
        wr_hi = wr.astype(BF16)
        wr = jnp.concatenate([wr_hi, (wr - wr_hi.astype(F32)).astype(BF16)], axis=1)
        bias_col = jnp.zeros((LANES, LANES), F32).at[:N_EXPERTS, 0].set(router_bias[l])
        dvec = jnp.repeat(ssd_D[l], SSD_HEAD_DIM).reshape(1, SSD_INNER)

        mod3 = _mod(c, w_ada[l], b_ada[l]).reshape(bsz, 6, d)
        x_in = xf

        def mix_and_route(b0, bg):
            tg = bg * seq
            n_assign = tg * TOP_K
            n_rows = n_assign + N_EXPERTS * bm_gmm
            p, dtp = _inproj(x_in, mod3, w_main, w_dt, seq, tm_proj, w_main.shape[1] // 4,
                             b0 * seq, tg)
            yn = _ssd(p, dtp, ssd_conv_w[l], ssd_conv_b[l].reshape(1, -1),
                      _pad_lanes(ssd_dt_bias[l]), _pad_lanes(ssd_A_log[l]), dvec,
                      ssd_norm_w[l].reshape(1, -1), e01, bg, seq, 4 * SSD_CHUNK)
            x1, h2, h2p, logits = _merge(yn, p, x_in, mod3, w_ssd_out[l].astype(BF16),
                                         w_sc_out[l].astype(BF16), w_o[l].astype(BF16),
                                         sc_conv_w[l], ln1_g[l].reshape(1, d),
                                         ln1_b[l].reshape(1, d), wr, bg, seq, tm_merge, b0)
            gates, dest, cnt = _route(logits, bias_col, su, tm_route)
            dest_both = jnp.concatenate([dest, dest + n_rows], axis=1)
            plan = _visit_plan(cnt[:, 0].astype(jnp.int32), n_rows // bm_gmm, bm_gmm)
            xin = _sc_scatter_rows(h2p.reshape(2 * tg, SC_ROW_WORDS), dest_both, 2 * n_rows)
            return dict(x1=x1, h2=h2, gates=gates, dest=dest, plan=plan, xin=xin, tok0=b0 * seq,
                        tg=tg, n_assign=n_assign, n_rows=n_rows)

        def experts(st):
            tg, n_assign, n_rows = st["tg"], st["n_assign"], st["n_rows"]
            yb = _gmm(st["xin"].reshape(2, n_rows, SC_ROW_WORDS), w_gate[l], w_up[l], w_down[l],
                      *st["plan"], bm_gmm)
            idx = st["dest"].reshape(1, n_assign)
            idx = jnp.concatenate([idx, idx + n_rows], axis=1)
            yg = _sc_gather_rows(yb.reshape(2 * n_rows, SC_ROW_WORDS), idx)
            return yg.reshape(2, TOP_K, tg, SC_ROW_WORDS)

        stages = [mix_and_route(b0, bg) for b0, bg in groups]
        gathered = [experts(st) for st in stages]
        out = None
        for st, yg in zip(stages, gathered):
            out = _combine(yg, st["gates"], st["h2"], st["x1"], mod3, wsgu, wsd,
                           ln2_g[l].reshape(1, d), ln2_b[l].reshape(1, d), seq, tm_comb,
                           st["tok0"], t, out)
        xf = out
    return xf.reshape(bsz, seq, d)
```

```python
import jax
import jax.numpy as jnp
import numpy as np
from jax import lax
from jax.experimental import pallas as pl
from jax.experimental.pallas import tpu as pltpu
from jax.experimental.pallas import tpu_sc as plsc

F32 = jnp.float32
BF16 = jnp.bfloat16
U32 = jnp.uint32

LANES = 128
SUBLANES = 8
VMEM_LIMIT_BYTES = 56 * 1024 * 1024

D_MODEL = 1024
SSD_INNER = 2048
SSD_HEAD_DIM = 64
SSD_HEADS = 32
SSD_GROUPS = 8
SSD_STATE = 128
SSD_CONV = 4
SSD_CHUNK = 128
GROUP_W = SSD_INNER // SSD_GROUPS
HEADS_PER_GROUP = SSD_HEADS // SSD_GROUPS
SC_KERNEL = 3
N_EXPERTS = 64
TOP_K = 8
N_EXPERT_GROUPS = 8
TOPK_EXPERT_GROUPS = 4
EXPERTS_PER_GROUP = N_EXPERTS // N_EXPERT_GROUPS
D_EXPERT = 256
ROUTED_SCALE = 2.5
LN_EPS = 1e-5
RMS_EPS = 1e-5
ALPHA = 2.0 ** 0.25
DT_COL0 = 2048 + 4096
NEG_BIG = -1e30
SC_WINDOW = 128
SC_ROW_WORDS = 256
GMM_BLOCK = 1024
GMM_ROWS = 256
MERGE_ROWS = 256


def _cparams(sem):
    return pltpu.CompilerParams(dimension_semantics=sem,
                                vmem_limit_bytes=VMEM_LIMIT_BYTES)


def _ln(x):
    mu = jnp.mean(x, axis=-1, keepdims=True)
    xc = x - mu
    var = jnp.mean(xc * xc, axis=-1, keepdims=True)
    return xc * lax.rsqrt(var + LN_EPS)


def _silu(x):
    h = 0.5 * x
    return h + h * jnp.tanh(h)


def _split3(a):
    hi = a.astype(BF16)
    r1 = a - hi.astype(F32)
    mid = r1.astype(BF16)
    lo = (r1 - mid.astype(F32)).astype(BF16)
    return hi, mid, lo


def _dot(a, b):
    return jnp.dot(a, b, preferred_element_type=F32)


def _dot_exact01(a, m01):
    hi, mid, lo = _split3(a)
    return _dot(hi, m01) + _dot(mid, m01) + _dot(lo, m01)


def _dot01_exact(m01, a):
    hi, mid, lo = _split3(a)
    return _dot(m01, hi) + _dot(m01, mid) + _dot(m01, lo)


def _pack_pair(a, b):
    ab = pltpu.bitcast(a.astype(BF16).astype(F32), U32)
    bb = pltpu.bitcast(b.astype(BF16).astype(F32), U32)
    return ab | lax.shift_right_logical(bb, jnp.uint32(16))


def _unpack_pair(w):
    a = pltpu.bitcast(w & jnp.uint32(0xFFFF0000), F32)
    b = pltpu.bitcast(lax.shift_left(w, jnp.uint32(16)), F32)
    return a, b


def _mod_kernel(c_ref, w_ref, b_ref, o_ref):
    a = _silu(c_ref[...])
    o_ref[...] = jnp.dot(a, w_ref[...], precision=lax.Precision.HIGHEST,
                         preferred_element_type=F32) + b_ref[...]


def _mod(c, w, b):
    bsz, d = c.shape
    n = w.shape[1]
    tn = 1024
    return pl.pallas_call(
        _mod_kernel,
        grid=(n // tn,),
        in_specs=[pl.BlockSpec((bsz, d), lambda j: (0, 0)),
                  pl.BlockSpec((d, tn), lambda j: (0, j)),
                  pl.BlockSpec((1, tn), lambda j: (0, j))],
        out_specs=pl.BlockSpec((bsz, tn), lambda j: (0, j)),
        out_shape=jax.ShapeDtypeStruct((bsz, n), F32),
        compiler_params=_cparams(("arbitrary",)),
        name="mod",
    )(c, w, b.reshape(1, n))


def _inproj_kernel(x_ref, mod_ref, w_ref, wdt_ref, p_ref, dt_ref, h_scr):
    j = pl.program_id(1)

    @pl.when(j == 0)
    def _():
        m = mod_ref[0]
        h = _ln(x_ref[...]) * (1.0 + m[1:2, :]) + m[0:1, :]
        hb = h.astype(BF16)
        h_scr[...] = hb
        dt_ref[...] = _dot(hb, wdt_ref[...])

    p_ref[...] = _dot(h_scr[...], w_ref[...]).astype(p_ref.dtype)


def _inproj(x2d, mod3, w_main, w_dt, seq, tm, tn, tok0, t):
    d = x2d.shape[1]
    n = w_main.shape[1]
    i0 = tok0 // tm
    return pl.pallas_call(
        _inproj_kernel,
        grid=(t // tm, n // tn),
        in_specs=[pl.BlockSpec((tm, d), lambda i, j: (i0 + i, 0)),
                  pl.BlockSpec((1, 6, d), lambda i, j: (((i0 + i) * tm) // seq, 0, 0)),
                  pl.BlockSpec((d, tn), lambda i, j: (0, j)),
                  pl.BlockSpec((d, LANES), lambda i, j: (0, 0))],
        out_specs=[pl.BlockSpec((tm, tn), lambda i, j: (i, j)),
                   pl.BlockSpec((tm, LANES), lambda i, j: (i, 0))],
        out_shape=[jax.ShapeDtypeStruct((t, n), BF16),
                   jax.ShapeDtypeStruct((t, LANES), F32)],
        scratch_shapes=[pltpu.VMEM((tm, d), BF16)],
        compiler_params=_cparams(("arbitrary", "arbitrary")),
        name="inproj",
    )(x2d, mod3, w_main, w_dt)


def _shift_conv(cur, prev_scr, s_ref, w_ref, b_ref):
    rows = cur.shape[0]
    n_taps = w_ref.shape[0]
    shifted = _dot(s_ref[...], jnp.concatenate([prev_scr[...], cur], axis=0))
    acc = w_ref[n_taps - 1:n_taps, :] * cur.astype(F32) + b_ref[...]
    for k in range(n_taps - 1):
        acc = acc + w_ref[k:k + 1, :] * shifted[k * rows:(k + 1) * rows]
    prev_scr[...] = cur
    return acc


def _shift_matrix(rows, n_taps):
    s = np.zeros(((n_taps - 1) * rows, 2 * rows), np.float32)
    t = np.arange(rows)
    for k in range(n_taps - 1):
        s[k * rows + t, rows + t - (n_taps - 1) + k] = 1.0
    return jnp.asarray(s, BF16)


def _ssd_kernel(xs_ref, bc_ref, z_ref, dtp_ref, cw_xs_ref, cw_bc_ref, cb_xs_ref,
                cb_bc_ref, dtb_ref, alog_ref, dvec_ref, nw_ref, e_ref, s_ref,
                yn_ref, prev_xs, prev_bc, state):
    @pl.when(pl.program_id(1) == 0)
    def _():
        prev_xs[...] = jnp.zeros(prev_xs.shape, BF16)
        prev_bc[...] = jnp.zeros(prev_bc.shape, BF16)
        state[...] = jnp.zeros(state.shape, F32)

    for cc in range(xs_ref.shape[0] // SSD_CHUNK):
        _ssd_chunk(slice(cc * SSD_CHUNK, (cc + 1) * SSD_CHUNK), xs_ref, bc_ref, z_ref, dtp_ref,
                   cw_xs_ref, cw_bc_ref, cb_xs_ref, cb_bc_ref, dtb_ref, alog_ref, dvec_ref,
                   nw_ref, e_ref, s_ref, yn_ref, prev_xs, prev_bc, state)


def _ssd_chunk(rows, xs_ref, bc_ref, z_ref, dtp_ref, cw_xs_ref, cw_bc_ref, cb_xs_ref,
               cb_bc_ref, dtb_ref, alog_ref, dvec_ref, nw_ref, e_ref, s_ref,
               yn_ref, prev_xs, prev_bc, state):
    L = SSD_CHUNK
    xs = _silu(_shift_conv(xs_ref[rows, :], prev_xs, s_ref, cw_xs_ref, cb_xs_ref))
    bc = _silu(_shift_conv(bc_ref[rows, :], prev_bc, s_ref, cw_bc_ref, cb_bc_ref))

    x_dt = dtp_ref[rows, :] + dtb_ref[...]
    dt = jnp.maximum(x_dt, 0.0) + jnp.log1p(jnp.exp(-jnp.abs(x_dt)))
    a_neg = -jnp.exp(alog_ref[...])
    d_a = dt * a_neg
    row = lax.broadcasted_iota(jnp.int32, (L, L), 0)
    col = lax.broadcasted_iota(jnp.int32, (L, L), 1)
    causal = col <= row
    tri = jnp.where(causal, 1.0, 0.0).astype(BF16)
    hi, mid, lo = _split3(d_a)
    a_cs = _dot(tri, hi) + _dot(tri, mid) + _dot(tri, lo)
    a_cs_t = a_cs.T
    a_last = a_cs[L - 1:L, :]
    stack = jnp.concatenate([dt, jnp.exp(a_cs), jnp.exp(a_last - a_cs)], axis=0)
    s_hi = stack.astype(BF16)
    s_mid = (stack - s_hi.astype(F32)).astype(BF16)
    st_e = _dot(jnp.concatenate([s_hi, s_mid], axis=1), e_ref[...])
    dt_e = st_e[0:L]
    od_e = st_e[L:2 * L]
    ds_e = st_e[2 * L:3 * L]
    cd_e = od_e[L - 1:L, :]

    x_f = xs * dt_e
    x_b = x_f.astype(BF16)
    xd_b = (x_f * ds_e).astype(BF16)
    lane_head = lax.broadcasted_iota(jnp.int32, (L, GROUP_W), 1) // SSD_HEAD_DIM

    for g in range(SSD_GROUPS):
        gc = slice(g * GROUP_W, (g + 1) * GROUP_W)
        b_g = bc[:, g * SSD_STATE:(g + 1) * SSD_STATE]
        c_g = bc[:, SSD_GROUPS * SSD_STATE + g * SSD_STATE:
                 SSD_GROUPS * SSD_STATE + (g + 1) * SSD_STATE].astype(BF16)
        b_gt = b_g.T.astype(BF16)
        cb = _dot(c_g, b_gt)
        x_g = x_b[:, gc]
        lhs = []
        rhs = []
        for r in range(HEADS_PER_GROUP):
            h = g * HEADS_PER_GROUP + r
            seg = a_cs[:, h:h + 1] - a_cs_t[h:h + 1, :]
            lmat = jnp.exp(jnp.where(causal, seg, NEG_BIG))
            lhs.append((cb * lmat).astype(BF16))
            rhs.append(jnp.where(lane_head == r, x_g, jnp.zeros_like(x_g)))
        y_diag = _dot(jnp.concatenate(lhs, axis=1), jnp.concatenate(rhs, axis=0))
        st = state[g]
        y_off = _dot(c_g, st.astype(BF16)) * od_e[:, gc]
        state[g] = cd_e[:, gc] * st + _dot(b_gt, xd_b[:, gc])
        y = y_diag + y_off + dvec_ref[:, gc] * xs[:, gc]
        gt = y * _silu(z_ref[rows, gc].astype(F32))
        ms = jnp.mean(gt * gt, axis=-1, keepdims=True)
        yn_ref[rows, gc] = (gt * lax.rsqrt(ms + RMS_EPS) * nw_ref[:, gc]).astype(yn_ref.dtype)


def _ssd(p, dtp, cw, cb, dtb, alog, dvec, nw, e01, bsz, seq, rows):
    t = p.shape[0]
    L = SSD_CHUNK
    nc = seq // rows
    row_map = lambda b, c: (b * nc + c, 0)

    def pcol(k):
        return pl.BlockSpec((rows, SSD_INNER), lambda b, c: (b * nc + c, k))

    const = lambda b, c: (0, 0)
    return pl.pallas_call(
        _ssd_kernel,
        grid=(bsz, nc),
        in_specs=[pcol(1), pcol(2), pcol(0),
                  pl.BlockSpec((rows, LANES), row_map),
                  pl.BlockSpec((SSD_CONV, SSD_INNER), lambda b, c: (0, 0)),
                  pl.BlockSpec((SSD_CONV, SSD_INNER), lambda b, c: (0, 1)),
                  pl.BlockSpec((1, SSD_INNER), lambda b, c: (0, 0)),
                  pl.BlockSpec((1, SSD_INNER), lambda b, c: (0, 1)),
                  pl.BlockSpec((1, LANES), const),
                  pl.BlockSpec((1, LANES), const),
                  pl.BlockSpec((1, SSD_INNER), const),
                  pl.BlockSpec((1, SSD_INNER), const),
                  pl.BlockSpec((2 * LANES, SSD_INNER), const),
                  pl.BlockSpec(((SSD_CONV - 1) * L, 2 * L), const)],
        out_specs=pl.BlockSpec((rows, SSD_INNER), row_map),
        out_shape=jax.ShapeDtypeStruct((t, SSD_INNER), BF16),
        scratch_shapes=[pltpu.VMEM((L, SSD_INNER), BF16),
                        pltpu.VMEM((L, SSD_INNER), BF16),
                        pltpu.VMEM((SSD_GROUPS, SSD_STATE, GROUP_W), F32)],
        compiler_params=_cparams(("arbitrary", "arbitrary")),
        name="ssd",
    )(p, p, p, dtp, cw, cw, cb, cb, dtb, alog, dvec, nw, e01, _shift_matrix(L, SSD_CONV))


def _merge_kernel(yn_ref, scb_ref, scc_ref, sch_ref, ga_ref, gb_ref, x_ref, mod_ref,
                  wssd_ref, wsc_ref, wo_ref, cw_ref, lng_ref, lnb_ref, wr_ref,
                  x1_ref, h2_ref, h2p_ref, lg_ref, ext):
    s = pl.program_id(1)
    tm = x_ref.shape[0]

    @pl.when(s == 0)
    def _():
        ext[0:SUBLANES, :] = jnp.zeros((SUBLANES, D_MODEL), F32)

    ext[SUBLANES:SUBLANES + tm, :] = scc_ref[...].astype(F32) * sch_ref[...].astype(F32)
    m = mod_ref[0]
    for c in range(tm // MERGE_ROWS):
        r0 = c * MERGE_ROWS
        rows = slice(r0, r0 + MERGE_ROWS)
        u = None
        for k in range(SC_KERNEL):
            start = SUBLANES + r0 - (SC_KERNEL - 1) + k
            term = cw_ref[k:k + 1, :] * ext[start:start + MERGE_ROWS, :]
            u = term if u is None else u + term
        y_b = _dot((scb_ref[rows, :].astype(F32) * u).astype(BF16), wsc_ref[...])
        y_a = _dot(yn_ref[rows, :], wssd_ref[...])
        merged = (jax.nn.sigmoid(ga_ref[rows, :].astype(F32)) * y_a
                  + jax.nn.sigmoid(gb_ref[rows, :].astype(F32)) * y_b)
        mix = _dot(merged.astype(BF16), wo_ref[...])
        x1 = _ln(ALPHA * x_ref[rows, :] + m[2:3, :] * mix) * lng_ref[...] + lnb_ref[...]
        x1_ref[rows, :] = x1
        h2 = _ln(x1) * (1.0 + m[4:5, :]) + m[3:4, :]
        h2_hi = h2.astype(BF16)
        h2_ref[rows, :] = h2_hi
        w = _pack_pair(h2[:, 0:D_MODEL // 2], h2[:, D_MODEL // 2:D_MODEL])
        h2p_ref[0, rows, :] = w[:, 0:SC_ROW_WORDS]
        h2p_ref[1, rows, :] = w[:, SC_ROW_WORDS:2 * SC_ROW_WORDS]
        h2_lo = (h2 - h2_hi.astype(F32)).astype(BF16)
        both = _dot(h2_hi, wr_ref[...])
        lg_ref[rows, :] = (both[:, 0:LANES] + both[:, LANES:2 * LANES]
                           + _dot(h2_lo, wr_ref[:, 0:LANES]))
    ext[0:SUBLANES, :] = ext[tm:tm + SUBLANES, :]


def _merge(yn, p, x2d, mod3, wssd, wsc, wo, cw, lng, lnb, wr, bsz, seq, tm, b0):
    t = bsz * seq
    d = x2d.shape[1]
    ns = seq // tm
    row_map = lambda b, s: (b * ns + s, 0)
    const = lambda b, s: (0, 0)
    col0 = (2048 + 4096) // d

    def pcol(k):
        return pl.BlockSpec((tm, d), lambda b, s: (b * ns + s, col0 + k))

    return pl.pallas_call(
        _merge_kernel,
        grid=(bsz, ns),
        in_specs=[pl.BlockSpec((tm, SSD_INNER), row_map),
                  pcol(0), pcol(1), pcol(2), pcol(3), pcol(4),
                  pl.BlockSpec((tm, d), lambda b, s: ((b0 + b) * ns + s, 0)),
                  pl.BlockSpec((1, 6, d), lambda b, s: (b0 + b, 0, 0)),
                  pl.BlockSpec((SSD_INNER, d), const, pipeline_mode=pl.Buffered(1)),
                  pl.BlockSpec((d, d), const, pipeline_mode=pl.Buffered(1)),
                  pl.BlockSpec((d, d), const, pipeline_mode=pl.Buffered(1)),
                  pl.BlockSpec((SC_KERNEL, d), const),
                  pl.BlockSpec((1, d), const),
                  pl.BlockSpec((1, d), const),
                  pl.BlockSpec((d, 2 * LANES), const, pipeline_mode=pl.Buffered(1))],
        out_specs=[pl.BlockSpec((tm, d), row_map),
                   pl.BlockSpec((tm, d), row_map),
                   pl.BlockSpec((2, tm, SC_ROW_WORDS), lambda b, s: (0, b * ns + s, 0)),
                   pl.BlockSpec((tm, LANES), row_map)],
        out_shape=[jax.ShapeDtypeStruct((t, d), F32),
                   jax.ShapeDtypeStruct((t, d), BF16),
                   jax.ShapeDtypeStruct((2, t, SC_ROW_WORDS), U32),
                   jax.ShapeDtypeStruct((t, LANES), F32)],
        scratch_shapes=[pltpu.VMEM((tm + SUBLANES, d), F32)],
        compiler_params=_cparams(("arbitrary", "arbitrary")),
        name="merge",
    )(yn, p, p, p, p, p, x2d, mod3, wssd, wsc, wo, cw, lng, lnb, wr)


def _first_argmax_mask(v, idx, n):
    m = jnp.max(v, axis=0, keepdims=True)
    first = jnp.min(jnp.where(v == m, idx, n), axis=0, keepdims=True)
    return idx == first, m


def _route_kernel(lg_ref, bias_ref, su_ref, g_ref, dest_ref, cnt_ref, counts, carry, start):
    p = pl.program_id(0)
    i = pl.program_id(1)
    tm = lg_ref.shape[0]
    ne, ng, eg = N_EXPERTS, N_EXPERT_GROUPS, EXPERTS_PER_GROUP
    lt = lg_ref[...].T[0:ne, :]
    scores = jax.nn.sigmoid(lt)
    biased = scores + bias_ref[...][0:ne, 0:1]
    grp = biased.reshape(ng, eg, tm)
    idx_e = lax.broadcasted_iota(jnp.int32, (ng, eg, tm), 1)
    m1 = jnp.max(grp, axis=1, keepdims=True)
    first = jnp.min(jnp.where(grp == m1, idx_e, eg), axis=1, keepdims=True)
    m2 = jnp.max(jnp.where(idx_e == first, NEG_BIG, grp), axis=1, keepdims=True)
    gscore = m1 + m2
    idx_g = lax.broadcasted_iota(jnp.int32, (ng, 1, tm), 0)
    gsel = jnp.zeros((ng, 1, tm), F32)
    for _ in range(TOPK_EXPERT_GROUPS):
        hit, _m = _first_argmax_mask(gscore, idx_g, ng)
        gsel = jnp.where(hit, 1.0, gsel)
        gscore = jnp.where(hit, NEG_BIG, gscore)
    emask = jnp.broadcast_to(gsel, (ng, eg, tm)).reshape(ne, tm)
    cand = jnp.where(emask > 0.0, biased, NEG_BIG)
    idx_x = lax.broadcasted_iota(jnp.int32, (ne, tm), 0)
    sel = jnp.zeros((ne, tm), F32)
    picks = []
    for _ in range(TOP_K):
        hit, _m = _first_argmax_mask(cand, idx_x, ne)
        sel = jnp.where(hit, 1.0, sel)
        cand = jnp.where(hit, NEG_BIG, cand)
        picks.append(hit)
    n_e = jnp.broadcast_to(jnp.sum(sel, axis=1, keepdims=True), (ne, LANES))

    @pl.when((p == 0) & (i == 0))
    def _():
        counts[...] = jnp.zeros(counts.shape, F32)

    @pl.when(p == 0)
    def _():
        counts[...] += n_e

    @pl.when((p == 1) & (i == 0))
    def _():
        r = lax.broadcasted_iota(jnp.int32, (ne, ne), 0)
        cc = lax.broadcasted_iota(jnp.int32, (ne, ne), 1)
        below = jnp.where(cc < r, 1.0, 0.0).astype(BF16)
        padded = jnp.floor((counts[...] + (GMM_BLOCK - 1.0)) * (1.0 / GMM_BLOCK)) * GMM_BLOCK
        start[...] = _dot01_exact(below, padded)
        carry[...] = jnp.zeros(carry.shape, F32)
        cnt_ref[...] = counts[...]

    @pl.when(p == 1)
    def _():
        before = _dot(sel.astype(BF16), su_ref[...])
        slot = before + carry[:, 0:1] + start[:, 0:1]
        dest_rows = []
        gate_rows = []
        for hit in picks:
            dest_rows.append(jnp.sum(jnp.where(hit, slot, 0.0), axis=0, keepdims=True))
            gate_rows.append(jnp.sum(jnp.where(hit, scores, 0.0), axis=0, keepdims=True))
        dest_ref[...] = jnp.concatenate(dest_rows, axis=0).astype(jnp.int32)
        gate = jnp.concatenate(gate_rows, axis=0)
        gate = gate / jnp.sum(gate, axis=0, keepdims=True) * ROUTED_SCALE
        pad = jnp.zeros((LANES - TOP_K, tm), F32)
        g_ref[...] = jnp.concatenate([gate, pad], axis=0).T
        carry[...] += n_e


def _route(logits, bias_col, su, tm):
    t = logits.shape[0]
    return pl.pallas_call(
        _route_kernel,
        grid=(2, t // tm),
        in_specs=[pl.BlockSpec((tm, LANES), lambda p, i: (i, 0)),
                  pl.BlockSpec((LANES, LANES), lambda p, i: (0, 0)),
                  pl.BlockSpec((tm, tm), lambda p, i: (0, 0))],
        out_specs=[pl.BlockSpec((tm, LANES), lambda p, i: (i * p, 0)),
                   pl.BlockSpec((TOP_K, tm), lambda p, i: (0, i * p)),
                   pl.BlockSpec((N_EXPERTS, LANES), lambda p, i: (0, 0))],
        out_shape=[jax.ShapeDtypeStruct((t, LANES), F32),
                   jax.ShapeDtypeStruct((TOP_K, t), jnp.int32),
                   jax.ShapeDtypeStruct((N_EXPERTS, LANES), F32)],
        scratch_shapes=[pltpu.VMEM((N_EXPERTS, LANES), F32),
                        pltpu.VMEM((N_EXPERTS, LANES), F32),
                        pltpu.VMEM((N_EXPERTS, LANES), F32)],
        compiler_params=_cparams(("arbitrary", "arbitrary")),
        name="route",
    )(logits, bias_col, su)


def _sc_mesh():
    return plsc.VectorSubcoreMesh(core_axis_name="c", subcore_axis_name="s")


def _sc_scatter_rows(rows, idx, n_out):
    n_rows, w = rows.shape
    n_k = idx.shape[0]

    @pl.kernel(out_type=jax.ShapeDtypeStruct((n_out, w), rows.dtype), mesh=_sc_mesh(),
               scratch_types=[])
    def scatter(x_hbm, i_hbm, o_hbm):
        def body(x_vmem, i_vmem):
            for k in range(n_k):
                pltpu.sync_copy(x_vmem, o_hbm.at[i_vmem.at[k]])

        pltpu.emit_pipeline(
            body,
            grid=(n_rows // SC_WINDOW,),
            in_specs=[pl.BlockSpec((SC_WINDOW, w), index_map=lambda i: (i, 0)),
                      pl.BlockSpec((n_k, SC_WINDOW), index_map=lambda i: (0, i))],
            out_specs=[],
            core_axis_name=("c", "s"),
            dimension_semantics=(pltpu.PARALLEL,),
        )(x_hbm, i_hbm)

    return scatter(rows, idx)


def _sc_gather_rows(rows, idx):
    n = idx.shape[1]
    w = rows.shape[1]

    @pl.kernel(out_type=jax.ShapeDtypeStruct((n, w), rows.dtype), mesh=_sc_mesh(),
               scratch_types=[])
    def gather(x_hbm, i_hbm, o_hbm):
        def body(i_vmem, o_vmem):
            pltpu.sync_copy(x_hbm.at[i_vmem.at[0]], o_vmem)

        pltpu.emit_pipeline(
            body,
            grid=(n // SC_WINDOW,),
            in_specs=[pl.BlockSpec((1, SC_WINDOW), index_map=lambda i: (0, i))],
            out_specs=[pl.BlockSpec((SC_WINDOW, w), index_map=lambda i: (i, 0))],
            core_axis_name=("c", "s"),
            dimension_semantics=(pltpu.PARALLEL,),
        )(i_hbm, o_hbm)

    return gather(rows, idx)


def _gmm_kernel(vb_ref, ve_ref, vvalid_ref, vnew_ref, x_ref, wg_ref, wu_ref, wdn_ref,
                o_ref, wgu_ref, wd_ref):
    v = pl.program_id(0)
    valid = vvalid_ref[v]
    bm = x_ref.shape[1]
    half = D_MODEL // 2

    @pl.when(vnew_ref[v] == 1)
    def _():
        wgu_ref[:, 0:D_EXPERT] = wg_ref[0].astype(BF16)
        wgu_ref[:, D_EXPERT:2 * D_EXPERT] = wu_ref[0].astype(BF16)
        wd_ref[...] = wdn_ref[0].astype(BF16)

    @pl.when(valid > 0)
    def _():
        for c in range(bm // GMM_ROWS):
            rows = slice(c * GMM_ROWS, (c + 1) * GMM_ROWS)
            a0, b0 = _unpack_pair(x_ref[0, rows, :])
            a1, b1 = _unpack_pair(x_ref[1, rows, :])
            x = jnp.concatenate([a0.astype(BF16), a1.astype(BF16),
                                 b0.astype(BF16), b1.astype(BF16)], axis=1)
            a = _dot(x, wgu_ref[...])
            act = _silu(a[:, 0:D_EXPERT]) * a[:, D_EXPERT:2 * D_EXPERT]
            r = lax.broadcasted_iota(jnp.int32, (GMM_ROWS, D_EXPERT), 0) + c * GMM_ROWS
            act = jnp.where(r < valid, act, 0.0)
            y = _dot(act.astype(BF16), wd_ref[...])
            w = _pack_pair(y[:, 0:half], y[:, half:D_MODEL])
            o_ref[0, rows, :] = w[:, 0:SC_ROW_WORDS]
            o_ref[1, rows, :] = w[:, SC_ROW_WORDS:half]


def _gmm(xin, w_gate, w_up, w_down, vb, ve, vvalid, vnew, bm):
    _, n_rows, w = xin.shape
    d = D_MODEL
    n_visits = vb.shape[0]
    blk = lambda v, vb, ve, vvalid, vnew: (0, vb[v], 0)
    wsel = lambda v, vb, ve, vvalid, vnew: (ve[v], 0, 0)
    grid_spec = pltpu.PrefetchScalarGridSpec(
        num_scalar_prefetch=4,
        grid=(n_visits,),
        in_specs=[pl.BlockSpec((2, bm, w), blk),
                  pl.BlockSpec((1, d, D_EXPERT), wsel),
                  pl.BlockSpec((1, d, D_EXPERT), wsel),
                  pl.BlockSpec((1, D_EXPERT, d), wsel)],
        out_specs=pl.BlockSpec((2, bm, w), blk),
        scratch_shapes=[pltpu.VMEM((d, 2 * D_EXPERT), BF16),
                        pltpu.VMEM((D_EXPERT, d), BF16)],
    )
    return pl.pallas_call(
        _gmm_kernel,
        grid_spec=grid_spec,
        out_shape=jax.ShapeDtypeStruct((2, n_rows, w), U32),
        compiler_params=_cparams(("arbitrary",)),
        name="gmm",
    )(vb, ve, vvalid, vnew, xin, w_gate, w_up, w_down)


def _visit_plan(counts, n_blocks, bm):
    ne = counts.shape[0]
    nb = (counts + bm - 1) // bm
    vend = jnp.cumsum(nb)
    vstart = vend - nb
    total = vend[-1]
    v = jnp.arange(n_blocks, dtype=jnp.int32)
    vc = jnp.minimum(v, total - 1)
    e_v = jnp.minimum(jnp.sum((vend[None, :] <= vc[:, None]).astype(jnp.int32), axis=1), ne - 1)
    onehot = (e_v[:, None] == jnp.arange(ne, dtype=jnp.int32)[None, :]).astype(jnp.int32)
    pick = lambda tab: jnp.sum(onehot * tab[None, :].astype(jnp.int32), axis=1)
    valid = jnp.clip(pick(counts) - (vc - pick(vstart)) * bm, 0, bm)
    valid = jnp.where(v < total, valid, 0)
    i32 = lambda a: a.astype(jnp.int32)
    e_v = i32(e_v)
    new = jnp.concatenate([jnp.ones((1,), jnp.int32), i32(e_v[1:] != e_v[:-1])])
    return i32(vc), e_v, i32(valid), new


def _combine_kernel(yg_ref, g_ref, h_ref, x1_ref, mod_ref, wsgu_ref, wsd_ref, lng_ref,
                    lnb_ref, *rest):
    o_ref = rest[-1]
    q = D_MODEL // 4
    g = g_ref[...]
    a = _dot(h_ref[...], wsgu_ref[...])
    act = _silu(a[:, 0:D_EXPERT]) * a[:, D_EXPERT:2 * D_EXPERT]
    shared = _dot(act.astype(BF16), wsd_ref[...])
    f = [shared[:, j * q:(j + 1) * q] for j in range(4)]
    for k in range(TOP_K):
        a0, b0 = _unpack_pair(yg_ref[0, k])
        a1, b1 = _unpack_pair(yg_ref[1, k])
        gk = g[:, k:k + 1]
        f = [f[0] + gk * a0, f[1] + gk * a1, f[2] + gk * b0, f[3] + gk * b1]
    ffn = jnp.concatenate(f, axis=1)
    m = mod_ref[0]
    r = ALPHA * x1_ref[...] + m[5:6, :] * ffn
    o_ref[...] = _ln(r) * lng_ref[...] + lnb_ref[...]


def _combine(yg, gates, h2, x1, mod3, wsgu, wsd, lng, lnb, seq, tm, tok0, t, prev_out):
    tp, d = x1.shape
    b0 = tok0 // tm
    row = lambda i: (i, 0)
    const = lambda i: (0, 0)
    in_specs = [pl.BlockSpec((2, TOP_K, tm, SC_ROW_WORDS), lambda i: (0, 0, i, 0)),
                pl.BlockSpec((tm, LANES), row),
                pl.BlockSpec((tm, d), row),
                pl.BlockSpec((tm, d), row),
                pl.BlockSpec((1, 6, d), lambda i: (((b0 + i) * tm) // seq, 0, 0)),
                pl.BlockSpec((d, 2 * D_EXPERT), const),
                pl.BlockSpec((D_EXPERT, d), const),
                pl.BlockSpec((1, d), const),
                pl.BlockSpec((1, d), const)]
    args = [yg, gates, h2, x1, mod3, wsgu, wsd, lng, lnb]
    aliases = {}
    if prev_out is not None:
        in_specs.append(pl.BlockSpec(memory_space=pl.ANY))
        args.append(prev_out)
        aliases = {len(args) - 1: 0}
    return pl.pallas_call(
        _combine_kernel,
        grid=(tp // tm,),
        in_specs=in_specs,
        out_specs=pl.BlockSpec((tm, d), lambda i: (b0 + i, 0)),
        out_shape=jax.ShapeDtypeStruct((t, d), F32),
        input_output_aliases=aliases,
        compiler_params=_cparams(("arbitrary",)),
        name="combine",
    )(*args)


def _pad_lanes(v, fill=0.0):
    out = jnp.full((1, LANES), fill, F32)
    return out.at[0, :v.shape[0]].set(v.astype(F32))


def kernel(x, c, w_ada, b_ada, w_in, ssd_conv_w, ssd_conv_b, ssd_dt_bias, ssd_A_log, ssd_D,
           ssd_norm_w, w_ssd_out, sc_conv_w, w_sc_out, w_o, ln1_g, ln1_b, router_w,
           router_bias, w_gate, w_up, w_down, sh_gate, sh_up, sh_down, ln2_g, ln2_b):
    bsz, seq, d = x.shape
    t = bsz * seq
    depth = w_in.shape[0]
    tm_proj = min(1024, seq)
    tm_merge = min(512, seq)
    tm_route = min(1024, seq)
    tm_comb = min(512, seq)
    bm_gmm = GMM_BLOCK
    first = (5 * bsz) // 8 if bsz >= 2 else bsz
    groups = [(0, first)] + ([(first, bsz - first)] if bsz > first else [])

    e01 = (np.arange(LANES)[:, None] == (np.arange(SSD_INNER)[None, :] // SSD_HEAD_DIM))
    e01 = jnp.asarray(np.concatenate([e01, e01], axis=0), BF16)
    su = jnp.asarray(np.arange(tm_route)[:, None] < np.arange(tm_route)[None, :], BF16)

    xf = x.reshape(t, d)
    for l in range(depth):
        w_l = w_in[l].astype(BF16)
        w_main = jnp.concatenate([w_l[:, :DT_COL0], w_l[:, DT_COL0 + SSD_HEADS:]], axis=1)
        w_dt = jnp.pad(w_l[:, DT_COL0:DT_COL0 + SSD_HEADS], ((0, 0), (0, LANES - SSD_HEADS)))
        wsgu = jnp.concatenate([sh_gate[l], sh_up[l]], axis=-1).astype(BF16)
        wsd = sh_down[l].astype(BF16)
        wr = jnp.pad(router_w[l], ((0, 0), (0, LANES - N_EXPERTS)))
        wr_hi = wr.astype(BF16)
        wr = jnp.concatenate([wr_hi, (wr - wr_hi.astype(F32)).astype(BF16)], axis=1)
        bias_col = jnp.zeros((LANES, LANES), F32).at[:N_EXPERTS, 0].set(router_bias[l])
        dvec = jnp.repeat(ssd_D[l], SSD_HEAD_DIM).reshape(1, SSD_INNER)

        mod3 = _mod(c, w_ada[l], b_ada[l]).reshape(bsz, 6, d)
        x_in = xf

        def mix_and_route(b0, bg):
            tg = bg * seq
            n_assign = tg * TOP_K
            n_rows = n_assign + N_EXPERTS * bm_gmm
            p, dtp = _inproj(x_in, mod3, w_main, w_dt, seq, tm_proj, w_main.shape[1] // 4,
                             b0 * seq, tg)
            yn = _ssd(p, dtp, ssd_conv_w[l], ssd_conv_b[l].reshape(1, -1),
                      _pad_lanes(ssd_dt_bias[l]), _pad_lanes(ssd_A_log[l]), dvec,
                      ssd_norm_w[l].reshape(1, -1), e01, bg, seq, 4 * SSD_CHUNK)
            x1, h2, h2p, logits = _merge(yn, p, x_in, mod3, w_ssd_out[l].astype(BF16),
                                         w_sc_out[l].astype(BF16), w_o[l].astype(BF16),
                                         sc_conv_w[l], ln1_g[l].reshape(1, d),
                                         ln1_b[l].reshape(1, d), wr, bg, seq, tm_merge, b0)
            gates, dest, cnt = _route(logits, bias_col, su, tm_route)
            dest_both = jnp.concatenate([dest, dest + n_rows], axis=1)
            plan = _visit_plan(cnt[:, 0].astype(jnp.int32), n_rows // bm_gmm, bm_gmm)
            xin = _sc_scatter_rows(h2p.reshape(2 * tg, SC_ROW_WORDS), dest_both, 2 * n_rows)
            return dict(x1=x1, h2=h2, gates=gates, dest=dest, plan=plan, xin=xin, tok0=b0 * seq,
                        tg=tg, n_assign=n_assign, n_rows=n_rows)

        def experts(st):
            tg, n_assign, n_rows = st["tg"], st["n_assign"], st["n_rows"]
            yb = _gmm(st["xin"].reshape(2, n_rows, SC_ROW_WORDS), w_gate[l], w_up[l], w_down[l],
                      *st["plan"], bm_gmm)
            idx = st["dest"].reshape(1, n_assign)
            idx = jnp.concatenate([idx, idx + n_rows], axis=1)
            yg = _sc_gather_rows(yb.reshape(2 * n_rows, SC_ROW_WORDS), idx)
            return yg.reshape(2, TOP_K, tg, SC_ROW_WORDS)

        stages = [mix_and_route(b0, bg) for b0, bg in groups]
        gathered = [experts(st) for st in stages]
        out = None
        for st, yg in zip(stages, gathered):
            out = _combine(yg, st["gates"], st["h2"], st["x1"], mod3, wsgu, wsd,
                           ln2_g[l].reshape(1, d), ln2_b[l].reshape(1, d), seq, tm_comb,
                           st["tok0"], t, out)
        xf = out
    return xf.reshape(bsz, seq, d)
```

```python
import jax
import jax.numpy as jnp
import numpy as np
from jax import lax
from jax.experimental import pallas as pl
from jax.experimental.pallas import tpu as pltpu
from jax.experimental.pallas import tpu_sc as plsc

F32 = jnp.float32
BF16 = jnp.bfloat16
U32 = jnp.uint32

LANES = 128
SUBLANES = 8
VMEM_LIMIT_BYTES = 56 * 1024 * 1024

D_MODEL = 1024
SSD_INNER = 2048
SSD_HEAD_DIM = 64
SSD_HEADS = 32
SSD_GROUPS = 8
SSD_STATE = 128
SSD_CONV = 4
SSD_CHUNK = 128
GROUP_W = SSD_INNER // SSD_GROUPS
HEADS_PER_GROUP = SSD_HEADS // SSD_GROUPS
SC_KERNEL = 3
N_EXPERTS = 64
TOP_K = 8
N_EXPERT_GROUPS = 8
TOPK_EXPERT_GROUPS = 4
EXPERTS_PER_GROUP = N_EXPERTS // N_EXPERT_GROUPS
D_EXPERT = 256
ROUTED_SCALE = 2.5
LN_EPS = 1e-5
RMS_EPS = 1e-5
ALPHA = 2.0 ** 0.25
DT_COL0 = 2048 + 4096
NEG_BIG = -1e30
SC_WINDOW = 128
SC_ROW_WORDS = 256
GMM_BLOCK = 1024
GMM_ROWS = 256
MERGE_ROWS = 256


def _cparams(sem):
    return pltpu.CompilerParams(dimension_semantics=sem,
                                vmem_limit_bytes=VMEM_LIMIT_BYTES)


def _ln(x):
    mu = jnp.mean(x, axis=-1, keepdims=True)
    xc = x - mu
    var = jnp.mean(xc * xc, axis=-1, keepdims=True)
    return xc * lax.rsqrt(var + LN_EPS)


def _silu(x):
    h = 0.5 * x
    return h + h * jnp.tanh(h)


def _split3(a):
    hi = a.astype(BF16)
    r1 = a - hi.astype(F32)
    mid = r1.astype(BF16)
    lo = (r1 - mid.astype(F32)).astype(BF16)
    return hi, mid, lo


def _dot(a, b):
    return jnp.dot(a, b, preferred_element_type=F32)


def _dot_exact01(a, m01):
    hi, mid, lo = _split3(a)
    return _dot(hi, m01) + _dot(mid, m01) + _dot(lo, m01)


def _dot01_exact(m01, a):
    hi, mid, lo = _split3(a)
    return _dot(m01, hi) + _dot(m01, mid) + _dot(m01, lo)


def _pack_pair(a, b):
    ab = pltpu.bitcast(a.astype(BF16).astype(F32), U32)
    bb = pltpu.bitcast(b.astype(BF16).astype(F32), U32)
    return ab | lax.shift_right_logical(bb, jnp.uint32(16))


def _unpack_pair(w):
    a = pltpu.bitcast(w & jnp.uint32(0xFFFF0000), F32)
    b = pltpu.bitcast(lax.shift_left(w, jnp.uint32(16)), F32)
    return a, b


def _mod_kernel(c_ref, w_ref, b_ref, o_ref):
    a = _silu(c_ref[...])
    o_ref[...] = jnp.dot(a, w_ref[...], precision=lax.Precision.HIGHEST,
                         preferred_element_type=F32) + b_ref[...]


def _mod(c, w, b):
    bsz, d = c.shape
    n = w.shape[1]
    tn = 1024
    return pl.pallas_call(
        _mod_kernel,
        grid=(n // tn,),
        in_specs=[pl.BlockSpec((bsz, d), lambda j: (0, 0)),
                  pl.BlockSpec((d, tn), lambda j: (0, j)),
                  pl.BlockSpec((1, tn), lambda j: (0, j))],
        out_specs=pl.BlockSpec((bsz, tn), lambda j: (0, j)),
        out_shape=jax.ShapeDtypeStruct((bsz, n), F32),
        compiler_params=_cparams(("arbitrary",)),
        name="mod",
    )(c, w, b.reshape(1, n))


def _inproj_kernel(x_ref, mod_ref, w_ref, wdt_ref, p_ref, dt_ref, h_scr):
    j = pl.program_id(1)

    @pl.when(j == 0)
    def _():
        m = mod_ref[0]
        h = _ln(x_ref[...]) * (1.0 + m[1:2, :]) + m[0:1, :]
        hb = h.astype(BF16)
        h_scr[...] = hb
        dt_ref[...] = _dot(hb, wdt_ref[...])

    p_ref[...] = _dot(h_scr[...], w_ref[...]).astype(p_ref.dtype)


def _inproj(x2d, mod3, w_main, w_dt, seq, tm, tn, tok0, t):
    d = x2d.shape[1]
    n = w_main.shape[1]
    i0 = tok0 // tm
    return pl.pallas_call(
        _inproj_kernel,
        grid=(t // tm, n // tn),
        in_specs=[pl.BlockSpec((tm, d), lambda i, j: (i0 + i, 0)),
                  pl.BlockSpec((1, 6, d), lambda i, j: (((i0 + i) * tm) // seq, 0, 0)),
                  pl.BlockSpec((d, tn), lambda i, j: (0, j)),
                  pl.BlockSpec((d, LANES), lambda i, j: (0, 0))],
        out_specs=[pl.BlockSpec((tm, tn), lambda i, j: (i, j)),
                   pl.BlockSpec((tm, LANES), lambda i, j: (i, 0))],
        out_shape=[jax.ShapeDtypeStruct((t, n), BF16),
                   jax.ShapeDtypeStruct((t, LANES), F32)],
        scratch_shapes=[pltpu.VMEM((tm, d), BF16)],
        compiler_params=_cparams(("arbitrary", "arbitrary")),
        name="inproj",
    )(x2d, mod3, w_main, w_dt)


def _shift_conv(cur, prev_scr, s_ref, w_ref, b_ref):
    rows = cur.shape[0]
    n_taps = w_ref.shape[0]
    shifted = _dot(s_ref[...], jnp.concatenate([prev_scr[...], cur], axis=0))
    acc = w_ref[n_taps - 1:n_taps, :] * cur.astype(F32) + b_ref[...]
    for k in range(n_taps - 1):
        acc = acc + w_ref[k:k + 1, :] * shifted[k * rows:(k + 1) * rows]
    prev_scr[...] = cur
    return acc


def _shift_matrix(rows, n_taps):
    s = np.zeros(((n_taps - 1) * rows, 2 * rows), np.float32)
    t = np.arange(rows)
    for k in range(n_taps - 1):
        s[k * rows + t, rows + t - (n_taps - 1) + k] = 1.0
    return jnp.asarray(s, BF16)


def _ssd_kernel(xs_ref, bc_ref, z_ref, dtp_ref, cw_xs_ref, cw_bc_ref, cb_xs_ref,
                cb_bc_ref, dtb_ref, alog_ref, dvec_ref, nw_ref, e_ref, s_ref,
                yn_ref, prev_xs, prev_bc, state):
    @pl.when(pl.program_id(1) == 0)
    def _():
        prev_xs[...] = jnp.zeros(prev_xs.shape, BF16)
        prev_bc[...] = jnp.zeros(prev_bc.shape, BF16)
        state[...] = jnp.zeros(state.shape, F32)

    for cc in range(xs_ref.shape[0] // SSD_CHUNK):
        _ssd_chunk(slice(cc * SSD_CHUNK, (cc + 1) * SSD_CHUNK), xs_ref, bc_ref, z_ref, dtp_ref,
                   cw_xs_ref, cw_bc_ref, cb_xs_ref, cb_bc_ref, dtb_ref, alog_ref, dvec_ref,
                   nw_ref, e_ref, s_ref, yn_ref, prev_xs, prev_bc, state)


def _ssd_chunk(rows, xs_ref, bc_ref, z_ref, dtp_ref, cw_xs_ref, cw_bc_ref, cb_xs_ref,
               cb_bc_ref, dtb_ref, alog_ref, dvec_ref, nw_ref, e_ref, s_ref,
               yn_ref, prev_xs, prev_bc, state):
    L = SSD_CHUNK
    xs = _silu(_shift_conv(xs_ref[rows, :], prev_xs, s_ref, cw_xs_ref, cb_xs_ref))
    bc = _silu(_shift_conv(bc_ref[rows, :], prev_bc, s_ref, cw_bc_ref, cb_bc_ref))

    x_dt = dtp_ref[rows, :] + dtb_ref[...]
    dt = jnp.maximum(x_dt, 0.0) + jnp.log1p(jnp.exp(-jnp.abs(x_dt)))
    a_neg = -jnp.exp(alog_ref[...])
    d_a = dt * a_neg
    row = lax.broadcasted_iota(jnp.int32, (L, L), 0)
    col = lax.broadcasted_iota(jnp.int32, (L, L), 1)
    causal = col <= row
    tri = jnp.where(causal, 1.0, 0.0).astype(BF16)
    hi, mid, lo = _split3(d_a)
    a_cs = _dot(tri, hi) + _dot(tri, mid) + _dot(tri, lo)
    a_cs_t = a_cs.T
    a_last = a_cs[L - 1:L, :]
    stack = jnp.concatenate([dt, jnp.exp(a_cs), jnp.exp(a_last - a_cs)], axis=0)
    s_hi = stack.astype(BF16)
    s_mid = (stack - s_hi.astype(F32)).astype(BF16)
    st_e = _dot(jnp.concatenate([s_hi, s_mid], axis=1), e_ref[...])
    dt_e = st_e[0:L]
    od_e = st_e[L:2 * L]
    ds_e = st_e[2 * L:3 * L]
    cd_e = od_e[L - 1:L, :]

    x_f = xs * dt_e
    x_b = x_f.astype(BF16)
    xd_b = (x_f * ds_e).astype(BF16)
    lane_head = lax.broadcasted_iota(jnp.int32, (L, GROUP_W), 1) // SSD_HEAD_DIM

    for g in range(SSD_GROUPS):
        gc = slice(g * GROUP_W, (g + 1) * GROUP_W)
        b_g = bc[:, g * SSD_STATE:(g + 1) * SSD_STATE]
        c_g = bc[:, SSD_GROUPS * SSD_STATE + g * SSD_STATE:
                 SSD_GROUPS * SSD_STATE + (g + 1) * SSD_STATE].astype(BF16)
        b_gt = b_g.T.astype(BF16)
        cb = _dot(c_g, b_gt)
        x_g = x_b[:, gc]
        lhs = []
        rhs = []
        for r in range(HEADS_PER_GROUP):
            h = g * HEADS_PER_GROUP + r
            seg = a_cs[:, h:h + 1] - a_cs_t[h:h + 1, :]
            lmat = jnp.exp(jnp.where(causal, seg, NEG_BIG))
            lhs.append((cb * lmat).astype(BF16))
            rhs.append(jnp.where(lane_head == r, x_g, jnp.zeros_like(x_g)))
        y_diag = _dot(jnp.concatenate(lhs, axis=1), jnp.concatenate(rhs, axis=0))
        st = state[g]
        y_off = _dot(c_g, st.astype(BF16)) * od_e[:, gc]
        state[g] = cd_e[:, gc] * st + _dot(b_gt, xd_b[:, gc])
        y = y_diag + y_off + dvec_ref[:, gc] * xs[:, gc]
        gt = y * _silu(z_ref[rows, gc].astype(F32))
        ms = jnp.mean(gt * gt, axis=-1, keepdims=True)
        yn_ref[rows, gc] = (gt * lax.rsqrt(ms + RMS_EPS) * nw_ref[:, gc]).astype(yn_ref.dtype)


def _ssd(p, dtp, cw, cb, dtb, alog, dvec, nw, e01, bsz, seq, rows):
    t = p.shape[0]
    L = SSD_CHUNK
    nc = seq // rows
    row_map = lambda b, c: (b * nc + c, 0)

    def pcol(k):
        return pl.BlockSpec((rows, SSD_INNER), lambda b, c: (b * nc + c, k))

    const = lambda b, c: (0, 0)
    return pl.pallas_call(
        _ssd_kernel,
        grid=(bsz, nc),
        in_specs=[pcol(1), pcol(2), pcol(0),
                  pl.BlockSpec((rows, LANES), row_map),
                  pl.BlockSpec((SSD_CONV, SSD_INNER), lambda b, c: (0, 0)),
                  pl.BlockSpec((SSD_CONV, SSD_INNER), lambda b, c: (0, 1)),
                  pl.BlockSpec((1, SSD_INNER), lambda b, c: (0, 0)),
                  pl.BlockSpec((1, SSD_INNER), lambda b, c: (0, 1)),
                  pl.BlockSpec((1, LANES), const),
                  pl.BlockSpec((1, LANES), const),
                  pl.BlockSpec((1, SSD_INNER), const),
                  pl.BlockSpec((1, SSD_INNER), const),
                  pl.BlockSpec((2 * LANES, SSD_INNER), const),
                  pl.BlockSpec(((SSD_CONV - 1) * L, 2 * L), const)],
        out_specs=pl.BlockSpec((rows, SSD_INNER), row_map),
        out_shape=jax.ShapeDtypeStruct((t, SSD_INNER), BF16),
        scratch_shapes=[pltpu.VMEM((L, SSD_INNER), BF16),
                        pltpu.VMEM((L, SSD_INNER), BF16),
                        pltpu.VMEM((SSD_GROUPS, SSD_STATE, GROUP_W), F32)],
        compiler_params=_cparams(("arbitrary", "arbitrary")),
        name="ssd",
    )(p, p, p, dtp, cw, cw, cb, cb, dtb, alog, dvec, nw, e01, _shift_matrix(L, SSD_CONV))


def _merge_kernel(yn_ref, scb_ref, scc_ref, sch_ref, ga_ref, gb_ref, x_ref, mod_ref,
                  wssd_ref, wsc_ref, wo_ref, cw_ref, lng_ref, lnb_ref, wr_ref,
                  x1_ref, h2_ref, h2p_ref, lg_ref, ext):
    s = pl.program_id(1)
    tm = x_ref.shape[0]

    @pl.when(s == 0)
    def _():
        ext[0:SUBLANES, :] = jnp.zeros((SUBLANES, D_MODEL), F32)

    ext[SUBLANES:SUBLANES + tm, :] = scc_ref[...].astype(F32) * sch_ref[...].astype(F32)
    m = mod_ref[0]
    for c in range(tm // MERGE_ROWS):
        r0 = c * MERGE_ROWS
        rows = slice(r0, r0 + MERGE_ROWS)
        u = None
        for k in range(SC_KERNEL):
            start = SUBLANES + r0 - (SC_KERNEL - 1) + k
            term = cw_ref[k:k + 1, :] * ext[start:start + MERGE_ROWS, :]
            u = term if u is None else u + term
        y_b = _dot((scb_ref[rows, :].astype(F32) * u).astype(BF16), wsc_ref[...])
        y_a = _dot(yn_ref[rows, :], wssd_ref[...])
        merged = (jax.nn.sigmoid(ga_ref[rows, :].astype(F32)) * y_a
                  + jax.nn.sigmoid(gb_ref[rows, :].astype(F32)) * y_b)
        mix = _dot(merged.astype(BF16), wo_ref[...])
        x1 = _ln(ALPHA * x_ref[rows, :] + m[2:3, :] * mix) * lng_ref[...] + lnb_ref[...]
        x1_ref[rows, :] = x1
        h2 = _ln(x1) * (1.0 + m[4:5, :]) + m[3:4, :]
        h2_hi = h2.astype(BF16)
        h2_ref[rows, :] = h2_hi
        w = _pack_pair(h2[:, 0:D_MODEL // 2], h2[:, D_MODEL // 2:D_MODEL])
        h2p_ref[0, rows, :] = w[:, 0:SC_ROW_WORDS]
        h2p_ref[1, rows, :] = w[:, SC_ROW_WORDS:2 * SC_ROW_WORDS]
        h2_lo = (h2 - h2_hi.astype(F32)).astype(BF16)
        both = _dot(h2_hi, wr_ref[...])
        lg_ref[rows, :] = (both[:, 0:LANES] + both[:, LANES:2 * LANES]
                           + _dot(h2_lo, wr_ref[:, 0:LANES]))
    ext[0:SUBLANES, :] = ext[tm:tm + SUBLANES, :]


def _merge(yn, p, x2d, mod3, wssd, wsc, wo, cw, lng, lnb, wr, bsz, seq, tm, b0):
    t = bsz * seq
    d = x2d.shape[1]
    ns = seq // tm
    row_map = lambda b, s: (b * ns + s, 0)
    const = lambda b, s: (0, 0)
    col0 = (2048 + 4096) // d

    def pcol(k):
        return pl.BlockSpec((tm, d), lambda b, s: (b * ns + s, col0 + k))

    return pl.pallas_call(
        _merge_kernel,
        grid=(bsz, ns),
        in_specs=[pl.BlockSpec((tm, SSD_INNER), row_map),
                  pcol(0), pcol(1), pcol(2), pcol(3), pcol(4),
                  pl.BlockSpec((tm, d), lambda b, s: ((b0 + b) * ns + s, 0)),
                  pl.BlockSpec((1, 6, d), lambda b, s: (b0 + b, 0, 0)),
                  pl.BlockSpec((SSD_INNER, d), const, pipeline_mode=pl.Buffered(1)),
                  pl.BlockSpec((d, d), const, pipeline_mode=pl.Buffered(1)),
                  pl.BlockSpec((d, d), const, pipeline_mode=pl.Buffered(1)),
                  pl.BlockSpec((SC_KERNEL, d), const),
                  pl.BlockSpec((1, d), const),
                  pl.BlockSpec((1, d), const),
                  pl.BlockSpec((d, 2 * LANES), const, pipeline_mode=pl.Buffered(1))],
        out_specs=[pl.BlockSpec((tm, d), row_map),
                   pl.BlockSpec((tm, d), row_map),
                   pl.BlockSpec((2, tm, SC_ROW_WORDS), lambda b, s: (0, b * ns + s, 0)),
                   pl.BlockSpec((tm, LANES), row_map)],
        out_shape=[jax.ShapeDtypeStruct((t, d), F32),
                   jax.ShapeDtypeStruct((t, d), BF16),
                   jax.ShapeDtypeStruct((2, t, SC_ROW_WORDS), U32),
                   jax.ShapeDtypeStruct((t, LANES), F32)],
        scratch_shapes=[pltpu.VMEM((tm + SUBLANES, d), F32)],
        compiler_params=_cparams(("arbitrary", "arbitrary")),
        name="merge",
    )(yn, p, p, p, p, p, x2d, mod3, wssd, wsc, wo, cw, lng, lnb, wr)


def _first_argmax(v, idx, n):
    m = jnp.max(v, axis=0, keepdims=True)
    first = jnp.min(jnp.where(v == m, idx, n), axis=0, keepdims=True)
    return first, idx == first


def _route_select_kernel(lg_ref, bias_ref, sel_ref, ek_ref, gr_ref, cnt_ref):
    i = pl.program_id(0)
    tm = lg_ref.shape[0]
    ne, ng, eg = N_EXPERTS, N_EXPERT_GROUPS, EXPERTS_PER_GROUP
    lt = lg_ref[...].T[0:ne, :]
    scores = jax.nn.sigmoid(lt)
    biased = scores + bias_ref[...][0:ne, 0:1]
    grp = biased.reshape(ng, eg, tm)
    idx_e = lax.broadcasted_iota(jnp.int32, (ng, eg, tm), 1)
    m1 = jnp.max(grp, axis=1, keepdims=True)
    first = jnp.min(jnp.where(grp == m1, idx_e, eg), axis=1, keepdims=True)
    m2 = jnp.max(jnp.where(idx_e == first, NEG_BIG, grp), axis=1, keepdims=True)
    gscore = m1 + m2
    idx_g = lax.broadcasted_iota(jnp.int32, (ng, 1, tm), 0)
    gsel = jnp.zeros((ng, 1, tm), F32)
    for _ in range(TOPK_EXPERT_GROUPS):
        _f, hit = _first_argmax(gscore, idx_g, ng)
        gsel = jnp.where(hit, 1.0, gsel)
        gscore = jnp.where(hit, NEG_BIG, gscore)
    emask = jnp.broadcast_to(gsel, (ng, eg, tm)).reshape(ne, tm)
    cand = jnp.where(emask > 0.0, biased, NEG_BIG)
    idx_x = lax.broadcasted_iota(jnp.int32, (ne, tm), 0)
    sel = jnp.zeros((ne, tm), F32)
    firsts = []
    gate_rows = []
    for _ in range(TOP_K):
        f, hit = _first_argmax(cand, idx_x, ne)
        sel = jnp.where(hit, 1.0, sel)
        cand = jnp.where(hit, NEG_BIG, cand)
        firsts.append(f)
        gate_rows.append(jnp.sum(jnp.where(hit, scores, 0.0), axis=0, keepdims=True))
    sel_ref[...] = sel.astype(BF16)
    ek_ref[...] = jnp.concatenate(firsts, axis=0)
    gr_ref[...] = jnp.concatenate(gate_rows, axis=0)

    @pl.when(i == 0)
    def _():
        cnt_ref[...] = jnp.zeros(cnt_ref.shape, F32)

    cnt_ref[...] += jnp.broadcast_to(jnp.sum(sel, axis=1, keepdims=True), (ne, LANES))


def _route_slots_kernel(sel_ref, ek_ref, gr_ref, cnt_ref, su_ref, g_ref, dest_ref, carry, start):
    i = pl.program_id(0)
    tm = sel_ref.shape[1]
    ne = N_EXPERTS

    @pl.when(i == 0)
    def _():
        r = lax.broadcasted_iota(jnp.int32, (ne, ne), 0)
        cc = lax.broadcasted_iota(jnp.int32, (ne, ne), 1)
        below = jnp.where(cc < r, 1.0, 0.0).astype(BF16)
        padded = jnp.floor((cnt_ref[...] + (GMM_BLOCK - 1.0)) * (1.0 / GMM_BLOCK)) * GMM_BLOCK
        start[...] = _dot01_exact(below, padded)
        carry[...] = jnp.zeros(carry.shape, F32)

    sel = sel_ref[...]
    before = _dot(sel, su_ref[...])
    slot = before + carry[:, 0:1] + start[:, 0:1]
    idx_x = lax.broadcasted_iota(jnp.int32, (ne, tm), 0)
    ek = ek_ref[...]
    dest_rows = []
    for k in range(TOP_K):
        hit = idx_x == ek[k:k + 1, :]
        dest_rows.append(jnp.sum(jnp.where(hit, slot, 0.0), axis=0, keepdims=True))
    dest_ref[...] = jnp.concatenate(dest_rows, axis=0).astype(jnp.int32)
    gate = gr_ref[...]
    gate = gate / jnp.sum(gate, axis=0, keepdims=True) * ROUTED_SCALE
    pad = jnp.zeros((LANES - TOP_K, tm), F32)
    g_ref[...] = jnp.concatenate([gate, pad], axis=0).T
    carry[...] += jnp.broadcast_to(jnp.sum(sel.astype(F32), axis=1, keepdims=True), (ne, LANES))


def _route(logits, bias_col, su, tm):
    t = logits.shape[0]
    sel, ek, gr, cnt = pl.pallas_call(
        _route_select_kernel,
        grid=(t // tm,),
        in_specs=[pl.BlockSpec((tm, LANES), lambda i: (i, 0)),
                  pl.BlockSpec((LANES, LANES), lambda i: (0, 0))],
        out_specs=[pl.BlockSpec((N_EXPERTS, tm), lambda i: (0, i)),
                   pl.BlockSpec((TOP_K, tm), lambda i: (0, i)),
                   pl.BlockSpec((TOP_K, tm), lambda i: (0, i)),
                   pl.BlockSpec((N_EXPERTS, LANES), lambda i: (0, 0))],
        out_shape=[jax.ShapeDtypeStruct((N_EXPERTS, t), BF16),
                   jax.ShapeDtypeStruct((TOP_K, t), jnp.int32),
                   jax.ShapeDtypeStruct((TOP_K, t), F32),
                   jax.ShapeDtypeStruct((N_EXPERTS, LANES), F32)],
        compiler_params=_cparams(("arbitrary",)),
        name="route_select",
    )(logits, bias_col)
    gates, dest = pl.pallas_call(
        _route_slots_kernel,
        grid=(t // tm,),
        in_specs=[pl.BlockSpec((N_EXPERTS, tm), lambda i: (0, i)),
                  pl.BlockSpec((TOP_K, tm), lambda i: (0, i)),
                  pl.BlockSpec((TOP_K, tm), lambda i: (0, i)),
                  pl.BlockSpec((N_EXPERTS, LANES), lambda i: (0, 0)),
                  pl.BlockSpec((tm, tm), lambda i: (0, 0))],
        out_specs=[pl.BlockSpec((tm, LANES), lambda i: (i, 0)),
                   pl.BlockSpec((TOP_K, tm), lambda i: (0, i))],
        out_shape=[jax.ShapeDtypeStruct((t, LANES), F32),
                   jax.ShapeDtypeStruct((TOP_K, t), jnp.int32)],
        scratch_shapes=[pltpu.VMEM((N_EXPERTS, LANES), F32),
                        pltpu.VMEM((N_EXPERTS, LANES), F32)],
        compiler_params=_cparams(("arbitrary",)),
        name="route_slots",
    )(sel, ek, gr, cnt, su)
    return gates, dest, cnt


def _sc_mesh():
    return plsc.VectorSubcoreMesh(core_axis_name="c", subcore_axis_name="s")


def _sc_scatter_rows(rows, idx, n_out):
    n_rows, w = rows.shape
    n_k = idx.shape[0]

    @pl.kernel(out_type=jax.ShapeDtypeStruct((n_out, w), rows.dtype), mesh=_sc_mesh(),
               scratch_types=[])
    def scatter(x_hbm, i_hbm, o_hbm):
        def body(x_vmem, i_vmem):
            for k in range(n_k):
                pltpu.sync_copy(x_vmem, o_hbm.at[i_vmem.at[k]])

        pltpu.emit_pipeline(
            body,
            grid=(n_rows // SC_WINDOW,),
            in_specs=[pl.BlockSpec((SC_WINDOW, w), index_map=lambda i: (i, 0)),
                      pl.BlockSpec((n_k, SC_WINDOW), index_map=lambda i: (0, i))],
            out_specs=[],
            core_axis_name=("c", "s"),
            dimension_semantics=(pltpu.PARALLEL,),
        )(x_hbm, i_hbm)

    return scatter(rows, idx)


def _sc_gather_rows(rows, idx):
    n = idx.shape[1]
    w = rows.shape[1]

    @pl.kernel(out_type=jax.ShapeDtypeStruct((n, w), rows.dtype), mesh=_sc_mesh(),
               scratch_types=[])
    def gather(x_hbm, i_hbm, o_hbm):
        def body(i_vmem, o_vmem):
            pltpu.sync_copy(x_hbm.at[i_vmem.at[0]], o_vmem)

        pltpu.emit_pipeline(
            body,
            grid=(n // SC_WINDOW,),
            in_specs=[pl.BlockSpec((1, SC_WINDOW), index_map=lambda i: (0, i))],
            out_specs=[pl.BlockSpec((SC_WINDOW, w), index_map=lambda i: (i, 0))],
            core_axis_name=("c", "s"),
            dimension_semantics=(pltpu.PARALLEL,),
        )(i_hbm, o_hbm)

    return gather(rows, idx)


def _gmm_kernel(vb_ref, ve_ref, vvalid_ref, vnew_ref, x_ref, wg_ref, wu_ref, wdn_ref,
                o_ref, wgu_ref, wd_ref):
    v = pl.program_id(0)
    valid = vvalid_ref[v]
    bm = x_ref.shape[1]
    half = D_MODEL // 2

    @pl.when(vnew_ref[v] == 1)
    def _():
        wgu_ref[:, 0:D_EXPERT] = wg_ref[0].astype(BF16)
        wgu_ref[:, D_EXPERT:2 * D_EXPERT] = wu_ref[0].astype(BF16)
        wd_ref[...] = wdn_ref[0].astype(BF16)

    @pl.when(valid > 0)
    def _():
        for c in range(bm // GMM_ROWS):
            rows = slice(c * GMM_ROWS, (c + 1) * GMM_ROWS)
            a0, b0 = _unpack_pair(x_ref[0, rows, :])
            a1, b1 = _unpack_pair(x_ref[1, rows, :])
            x = jnp.concatenate([a0.astype(BF16), a1.astype(BF16),
                                 b0.astype(BF16), b1.astype(BF16)], axis=1)
            a = _dot(x, wgu_ref[...])
            act = _silu(a[:, 0:D_EXPERT]) * a[:, D_EXPERT:2 * D_EXPERT]
            r = lax.broadcasted_iota(jnp.int32, (GMM_ROWS, D_EXPERT), 0) + c * GMM_ROWS
            act = jnp.where(r < valid, act, 0.0)
            y = _dot(act.astype(BF16), wd_ref[...])
            w = _pack_pair(y[:, 0:half], y[:, half:D_MODEL])
            o_ref[0, rows, :] = w[:, 0:SC_ROW_WORDS]
            o_ref[1, rows, :] = w[:, SC_ROW_WORDS:half]


def _gmm(xin, w_gate, w_up, w_down, vb, ve, vvalid, vnew, bm):
    _, n_rows, w = xin.shape
    d = D_MODEL
    n_visits = vb.shape[0]
    blk = lambda v, vb, ve, vvalid, vnew: (0, vb[v], 0)
    wsel = lambda v, vb, ve, vvalid, vnew: (ve[v], 0, 0)
    grid_spec = pltpu.PrefetchScalarGridSpec(
        num_scalar_prefetch=4,
        grid=(n_visits,),
        in_specs=[pl.BlockSpec((2, bm, w), blk),
                  pl.BlockSpec((1, d, D_EXPERT), wsel),
                  pl.BlockSpec((1, d, D_EXPERT), wsel),
                  pl.BlockSpec((1, D_EXPERT, d), wsel)],
        out_specs=pl.BlockSpec((2, bm, w), blk),
        scratch_shapes=[pltpu.VMEM((d, 2 * D_EXPERT), BF16),
                        pltpu.VMEM((D_EXPERT, d), BF16)],
    )
    return pl.pallas_call(
        _gmm_kernel,
        grid_spec=grid_spec,
        out_shape=jax.ShapeDtypeStruct((2, n_rows, w), U32),
        compiler_params=_cparams(("arbitrary",)),
        name="gmm",
    )(vb, ve, vvalid, vnew, xin, w_gate, w_up, w_down)


def _visit_plan(counts, n_blocks, bm):
    ne = counts.shape[0]
    nb = (counts + bm - 1) // bm
    vend = jnp.cumsum(nb)
    vstart = vend - nb
    total = vend[-1]
    v = jnp.arange(n_blocks, dtype=jnp.int32)
    vc = jnp.minimum(v, total - 1)
    e_v = jnp.minimum(jnp.sum((vend[None, :] <= vc[:, None]).astype(jnp.int32), axis=1), ne - 1)
    onehot = (e_v[:, None] == jnp.arange(ne, dtype=jnp.int32)[None, :]).astype(jnp.int32)
    pick = lambda tab: jnp.sum(onehot * tab[None, :].astype(jnp.int32), axis=1)
    valid = jnp.clip(pick(counts) - (vc - pick(vstart)) * bm, 0, bm)
    valid = jnp.where(v < total, valid, 0)
    i32 = lambda a: a.astype(jnp.int32)
    e_v = i32(e_v)
    new = jnp.concatenate([jnp.ones((1,), jnp.int32), i32(e_v[1:] != e_v[:-1])])
    return i32(vc), e_v, i32(valid), new


def _combine_kernel(yg_ref, g_ref, h_ref, x1_ref, mod_ref, wsgu_ref, wsd_ref, lng_ref,
                    lnb_ref, *rest):
    o_ref = rest[-1]
    q = D_MODEL // 4
    g = g_ref[...]
    a = _dot(h_ref[...], wsgu_ref[...])
    act = _silu(a[:, 0:D_EXPERT]) * a[:, D_EXPERT:2 * D_EXPERT]
    shared = _dot(act.astype(BF16), wsd_ref[...])
    f = [shared[:, j * q:(j + 1) * q] for j in range(4)]
    for k in range(TOP_K):
        a0, b0 = _unpack_pair(yg_ref[0, k])
        a1, b1 = _unpack_pair(yg_ref[1, k])
        gk = g[:, k:k + 1]
        f = [f[0] + gk * a0, f[1] + gk * a1, f[2] + gk * b0, f[3] + gk * b1]
    ffn = jnp.concatenate(f, axis=1)
    m = mod_ref[0]
    r = ALPHA * x1_ref[...] + m[5:6, :] * ffn
    o_ref[...] = _ln(r) * lng_ref[...] + lnb_ref[...]


def _combine(yg, gates, h2, x1, mod3, wsgu, wsd, lng, lnb, seq, tm, tok0, t, prev_out):
    tp, d = x1.shape
    b0 = tok0 // tm
    row = lambda i: (i, 0)
    const = lambda i: (0, 0)
    in_specs = [pl.BlockSpec((2, TOP_K, tm, SC_ROW_WORDS), lambda i: (0, 0, i, 0)),
                pl.BlockSpec((tm, LANES), row),
                pl.BlockSpec((tm, d), row),
                pl.BlockSpec((tm, d), row),
                pl.BlockSpec((1, 6, d), lambda i: (((b0 + i) * tm) // seq, 0, 0)),
                pl.BlockSpec((d, 2 * D_EXPERT), const),
                pl.BlockSpec((D_EXPERT, d), const),
                pl.BlockSpec((1, d), const),
                pl.BlockSpec((1, d), const)]
    args = [yg, gates, h2, x1, mod3, wsgu, wsd, lng, lnb]
    aliases = {}
    if prev_out is not None:
        in_specs.append(pl.BlockSpec(memory_space=pl.ANY))
        args.append(prev_out)
        aliases = {len(args) - 1: 0}
    return pl.pallas_call(
        _combine_kernel,
        grid=(tp // tm,),
        in_specs=in_specs,
        out_specs=pl.BlockSpec((tm, d), lambda i: (b0 + i, 0)),
        out_shape=jax.ShapeDtypeStruct((t, d), F32),
        input_output_aliases=aliases,
        compiler_params=_cparams(("arbitrary",)),
        name="combine",
    )(*args)


def _pad_lanes(v, fill=0.0):
    out = jnp.full((1, LANES), fill, F32)
    return out.at[0, :v.shape[0]].set(v.astype(F32))


def kernel(x, c, w_ada, b_ada, w_in, ssd_conv_w, ssd_conv_b, ssd_dt_bias, ssd_A_log, ssd_D,
           ssd_norm_w, w_ssd_out, sc_conv_w, w_sc_out, w_o, ln1_g, ln1_b, router_w,
           router_bias, w_gate, w_up, w_down, sh_gate, sh_up, sh_down, ln2_g, ln2_b):
    bsz, seq, d = x.shape
    t = bsz * seq
    depth = w_in.shape[0]
    tm_proj = min(1024, seq)
    tm_merge = min(512, seq)
    tm_route = min(1024, seq)
    tm_comb = min(512, seq)
    bm_gmm = GMM_BLOCK
    first = (5 * bsz) // 8 if bsz >= 2 else bsz
    groups = [(0, first)] + ([(first, bsz - first)] if bsz > first else [])

    e01 = (np.arange(LANES)[:, None] == (np.arange(SSD_INNER)[None, :] // SSD_HEAD_DIM))
    e01 = jnp.asarray(np.concatenate([e01, e01], axis=0), BF16)
    su = jnp.asarray(np.arange(tm_route)[:, None] < np.arange(tm_route)[None, :], BF16)

    xf = x.reshape(t, d)
    for l in range(depth):
        w_l = w_in[l]
        w_main = jnp.concatenate([w_l[:, :DT_COL0], w_l[:, DT_COL0 + SSD_HEADS:]],
                                 axis=1).astype(BF16)
        w_dt = jnp.pad(w_l[:, DT_COL0:DT_COL0 + SSD_HEADS],
                       ((0, 0), (0, LANES - SSD_HEADS))).astype(BF16)
        wsgu = jnp.concatenate([sh_gate[l], sh_up[l]], axis=-1).astype(BF16)
        wsd = sh_down[l].astype(BF16)
        wr = jnp.pad(router_w[l], ((0, 0), (0, LANES - N_EXPERTS)))
        wr_hi = wr.astype(BF16)
        wr = jnp.concatenate([wr_hi, (wr - wr_hi.astype(F32)).astype(BF16)], axis=1)
        bias_col = jnp.zeros((LANES, LANES), F32).at[:N_EXPERTS, 0].set(router_bias[l])
        dvec = jnp.repeat(ssd_D[l], SSD_HEAD_DIM).reshape(1, SSD_INNER)

        mod3 = _mod(c, w_ada[l], b_ada[l]).reshape(bsz, 6, d)
        x_in = xf

        def mix_and_route(b0, bg):
            tg = bg * seq
            n_assign = tg * TOP_K
            n_rows = n_assign + N_EXPERTS * bm_gmm
            p, dtp = _inproj(x_in, mod3, w_main, w_dt, seq, tm_proj, w_main.shape[1] // 4,
                             b0 * seq, tg)
            yn = _ssd(p, dtp, ssd_conv_w[l], ssd_conv_b[l].reshape(1, -1),
                      _pad_lanes(ssd_dt_bias[l]), _pad_lanes(ssd_A_log[l]), dvec,
                      ssd_norm_w[l].reshape(1, -1), e01, bg, seq, 4 * SSD_CHUNK)
            x1, h2, h2p, logits = _merge(yn, p, x_in, mod3, w_ssd_out[l].astype(BF16),
                                         w_sc_out[l].astype(BF16), w_o[l].astype(BF16),
                                         sc_conv_w[l], ln1_g[l].reshape(1, d),
                                         ln1_b[l].reshape(1, d), wr, bg, seq, tm_merge, b0)
            gates, dest, cnt = _route(logits, bias_col, su, tm_route)
            dest_both = jnp.concatenate([dest, dest + n_rows], axis=1)
            plan = _visit_plan(cnt[:, 0].astype(jnp.int32), n_rows // bm_gmm, bm_gmm)
            xin = _sc_scatter_rows(h2p.reshape(2 * tg, SC_ROW_WORDS), dest_both, 2 * n_rows)
            return dict(x1=x1, h2=h2, gates=gates, dest=dest, plan=plan, xin=xin, tok0=b0 * seq,
                        tg=tg, n_assign=n_assign, n_rows=n_rows)

        def experts(st):
            tg, n_assign, n_rows = st["tg"], st["n_assign"], st["n_rows"]
            yb = _gmm(st["xin"].reshape(2, n_rows, SC_ROW_WORDS), w_gate[l], w_up[l], w_down[l],
                      *st["plan"], bm_gmm)
            idx = st["dest"].reshape(1, n_assign)
            idx = jnp.concatenate([idx, idx + n_rows], axis=1)
            yg = _sc_gather_rows(yb.reshape(2 * n_rows, SC_ROW_WORDS), idx)
            return yg.reshape(2, TOP_K, tg, SC_ROW_WORDS)

        stages = [mix_and_route(b0, bg) for b0, bg in groups]
        gathered = [experts(st) for st in stages]
        out = None
        for st, yg in zip(stages, gathered):
            out = _combine(yg, st["gates"], st["h2"], st["x1"], mod3, wsgu, wsd,
                           ln2_g[l].reshape(1, d), ln2_b[l].reshape(1, d), seq, tm_comb,
                           st["tok0"], t, out)
        xf = out
    return xf.reshape(bsz, seq, d)
```

```python
import jax
import jax.numpy as jnp
import numpy as np
from jax import lax
from jax.experimental import pallas as pl
from jax.experimental.pallas import tpu as pltpu
from jax.experimental.pallas import tpu_sc as plsc

F32 = jnp.float32
BF16 = jnp.bfloat16
U32 = jnp.uint32

LANES = 128
SUBLANES = 8
VMEM_LIMIT_BYTES = 56 * 1024 * 1024

D_MODEL = 1024
SSD_INNER = 2048
SSD_HEAD_DIM = 64
SSD_HEADS = 32
SSD_GROUPS = 8
SSD_STATE = 128
SSD_CONV = 4
SSD_CHUNK = 128
GROUP_W = SSD_INNER // SSD_GROUPS
HEADS_PER_GROUP = SSD_HEADS // SSD_GROUPS
SC_KERNEL = 3
N_EXPERTS = 64
TOP_K = 8
N_EXPERT_GROUPS = 8
TOPK_EXPERT_GROUPS = 4
EXPERTS_PER_GROUP = N_EXPERTS // N_EXPERT_GROUPS
D_EXPERT = 256
ROUTED_SCALE = 2.5
LN_EPS = 1e-5
RMS_EPS = 1e-5
ALPHA = 2.0 ** 0.25
DT_COL0 = 2048 + 4096
NEG_BIG = -1e30
SC_WINDOW = 128
SC_ROW_WORDS = 256
GMM_BLOCK = 1024
GMM_ROWS = 256
MERGE_ROWS = 256


def _cparams(sem):
    return pltpu.CompilerParams(dimension_semantics=sem,
                                vmem_limit_bytes=VMEM_LIMIT_BYTES)


def _ln(x):
    mu = jnp.mean(x, axis=-1, keepdims=True)
    xc = x - mu
    var = jnp.mean(xc * xc, axis=-1, keepdims=True)
    return xc * lax.rsqrt(var + LN_EPS)


def _silu(x):
    h = 0.5 * x
    return h + h * jnp.tanh(h)


def _split3(a):
    hi = a.astype(BF16)
    r1 = a - hi.astype(F32)
    mid = r1.astype(BF16)
    lo = (r1 - mid.astype(F32)).astype(BF16)
    return hi, mid, lo


def _dot(a, b):
    return jnp.dot(a, b, preferred_element_type=F32)


def _dot_exact01(a, m01):
    hi, mid, lo = _split3(a)
    return _dot(hi, m01) + _dot(mid, m01) + _dot(lo, m01)


def _dot01_exact(m01, a):
    hi, mid, lo = _split3(a)
    return _dot(m01, hi) + _dot(m01, mid) + _dot(m01, lo)


def _pack_pair(a, b):
    ab = pltpu.bitcast(a.astype(BF16).astype(F32), U32)
    bb = pltpu.bitcast(b.astype(BF16).astype(F32), U32)
    return ab | lax.shift_right_logical(bb, jnp.uint32(16))


def _unpack_pair(w):
    a = pltpu.bitcast(w & jnp.uint32(0xFFFF0000), F32)
    b = pltpu.bitcast(lax.shift_left(w, jnp.uint32(16)), F32)
    return a, b


def _mod_kernel(c_ref, w_ref, b_ref, o_ref):
    a = _silu(c_ref[...])
    o_ref[...] = jnp.dot(a, w_ref[...], precision=lax.Precision.HIGHEST,
                         preferred_element_type=F32) + b_ref[...]


def _mod(c, w, b):
    bsz, d = c.shape
    n = w.shape[1]
    tn = 1024
    return pl.pallas_call(
        _mod_kernel,
        grid=(n // tn,),
        in_specs=[pl.BlockSpec((bsz, d), lambda j: (0, 0)),
                  pl.BlockSpec((d, tn), lambda j: (0, j)),
                  pl.BlockSpec((1, tn), lambda j: (0, j))],
        out_specs=pl.BlockSpec((bsz, tn), lambda j: (0, j)),
        out_shape=jax.ShapeDtypeStruct((bsz, n), F32),
        compiler_params=_cparams(("arbitrary",)),
        name="mod",
    )(c, w, b.reshape(1, n))


def _inproj_kernel(x_ref, mod_ref, w_ref, wdt_ref, p_ref, dt_ref, h_scr):
    j = pl.program_id(1)

    @pl.when(j == 0)
    def _():
        m = mod_ref[0]
        h = _ln(x_ref[...]) * (1.0 + m[1:2, :]) + m[0:1, :]
        hb = h.astype(BF16)
        h_scr[...] = hb
        dt_ref[...] = _dot(hb, wdt_ref[...])

    p_ref[...] = _dot(h_scr[...], w_ref[...]).astype(p_ref.dtype)


def _inproj(x2d, mod3, w_main, w_dt, seq, tm, tn, tok0, t):
    d = x2d.shape[1]
    n = w_main.shape[1]
    i0 = tok0 // tm
    return pl.pallas_call(
        _inproj_kernel,
        grid=(t // tm, n // tn),
        in_specs=[pl.BlockSpec((tm, d), lambda i, j: (i0 + i, 0)),
                  pl.BlockSpec((1, 6, d), lambda i, j: (((i0 + i) * tm) // seq, 0, 0)),
                  pl.BlockSpec((d, tn), lambda i, j: (0, j)),
                  pl.BlockSpec((d, LANES), lambda i, j: (0, 0))],
        out_specs=[pl.BlockSpec((tm, tn), lambda i, j: (i, j)),
                   pl.BlockSpec((tm, LANES), lambda i, j: (i, 0))],
        out_shape=[jax.ShapeDtypeStruct((t, n), BF16),
                   jax.ShapeDtypeStruct((t, LANES), F32)],
        scratch_shapes=[pltpu.VMEM((tm, d), BF16)],
        compiler_params=_cparams(("arbitrary", "arbitrary")),
        name="inproj",
    )(x2d, mod3, w_main, w_dt)


def _shift_conv(cur, prev_scr, s_ref, w_ref, b_ref):
    rows = cur.shape[0]
    n_taps = w_ref.shape[0]
    shifted = _dot(s_ref[...], jnp.concatenate([prev_scr[...], cur], axis=0))
    acc = w_ref[n_taps - 1:n_taps, :] * cur.astype(F32) + b_ref[...]
    for k in range(n_taps - 1):
        acc = acc + w_ref[k:k + 1, :] * shifted[k * rows:(k + 1) * rows]
    prev_scr[...] = cur
    return acc


def _shift_matrix(rows, n_taps):
    s = np.zeros(((n_taps - 1) * rows, 2 * rows), np.float32)
    t = np.arange(rows)
    for k in range(n_taps - 1):
        s[k * rows + t, rows + t - (n_taps - 1) + k] = 1.0
    return jnp.asarray(s, BF16)


def _ssd_kernel(xs_ref, bc_ref, z_ref, dtp_ref, cw_xs_ref, cw_bc_ref, cb_xs_ref,
                cb_bc_ref, dtb_ref, alog_ref, dvec_ref, nw_ref, e_ref, s_ref,
                yn_ref, prev_xs, prev_bc, state):
    @pl.when(pl.program_id(1) == 0)
    def _():
        prev_xs[...] = jnp.zeros(prev_xs.shape, BF16)
        prev_bc[...] = jnp.zeros(prev_bc.shape, BF16)
        state[...] = jnp.zeros(state.shape, F32)

    for cc in range(xs_ref.shape[0] // SSD_CHUNK):
        _ssd_chunk(slice(cc * SSD_CHUNK, (cc + 1) * SSD_CHUNK), xs_ref, bc_ref, z_ref, dtp_ref,
                   cw_xs_ref, cw_bc_ref, cb_xs_ref, cb_bc_ref, dtb_ref, alog_ref, dvec_ref,
                   nw_ref, e_ref, s_ref, yn_ref, prev_xs, prev_bc, state)


def _ssd_chunk(rows, xs_ref, bc_ref, z_ref, dtp_ref, cw_xs_ref, cw_bc_ref, cb_xs_ref,
               cb_bc_ref, dtb_ref, alog_ref, dvec_ref, nw_ref, e_ref, s_ref,
               yn_ref, prev_xs, prev_bc, state):
    L = SSD_CHUNK
    xs = _silu(_shift_conv(xs_ref[rows, :], prev_xs, s_ref, cw_xs_ref, cb_xs_ref))
    bc = _silu(_shift_conv(bc_ref[rows, :], prev_bc, s_ref, cw_bc_ref, cb_bc_ref))

    x_dt = dtp_ref[rows, :] + dtb_ref[...]
    dt = jnp.maximum(x_dt, 0.0) + jnp.log1p(jnp.exp(-jnp.abs(x_dt)))
    a_neg = -jnp.exp(alog_ref[...])
    d_a = dt * a_neg
    row = lax.broadcasted_iota(jnp.int32, (L, L), 0)
    col = lax.broadcasted_iota(jnp.int32, (L, L), 1)
    causal = col <= row
    tri = jnp.where(causal, 1.0, 0.0).astype(BF16)
    hi, mid, lo = _split3(d_a)
    a_cs = _dot(tri, hi) + _dot(tri, mid) + _dot(tri, lo)
    a_cs_t = a_cs.T
    a_last = a_cs[L - 1:L, :]
    stack = jnp.concatenate([dt, jnp.exp(a_cs), jnp.exp(a_last - a_cs)], axis=0)
    s_hi = stack.astype(BF16)
    s_mid = (stack - s_hi.astype(F32)).astype(BF16)
    st_e = _dot(jnp.concatenate([s_hi, s_mid], axis=1), e_ref[...])
    dt_e = st_e[0:L]
    od_e = st_e[L:2 * L]
    ds_e = st_e[2 * L:3 * L]
    cd_e = od_e[L - 1:L, :]

    x_f = xs * dt_e
    x_b = x_f.astype(BF16)
    xd_b = (x_f * ds_e).astype(BF16)
    lane_head = lax.broadcasted_iota(jnp.int32, (L, GROUP_W), 1) // SSD_HEAD_DIM

    for g in range(SSD_GROUPS):
        gc = slice(g * GROUP_W, (g + 1) * GROUP_W)
        b_g = bc[:, g * SSD_STATE:(g + 1) * SSD_STATE]
        c_g = bc[:, SSD_GROUPS * SSD_STATE + g * SSD_STATE:
                 SSD_GROUPS * SSD_STATE + (g + 1) * SSD_STATE].astype(BF16)
        b_gt = b_g.T.astype(BF16)
        cb = _dot(c_g, b_gt)
        x_g = x_b[:, gc]
        lhs = []
        rhs = []
        for r in range(HEADS_PER_GROUP):
            h = g * HEADS_PER_GROUP + r
            seg = a_cs[:, h:h + 1] - a_cs_t[h:h + 1, :]
            lmat = jnp.exp(jnp.where(causal, seg, NEG_BIG))
            lhs.append((cb * lmat).astype(BF16))
            rhs.append(jnp.where(lane_head == r, x_g, jnp.zeros_like(x_g)))
        y_diag = _dot(jnp.concatenate(lhs, axis=1), jnp.concatenate(rhs, axis=0))
        st = state[g]
        y_off = _dot(c_g, st.astype(BF16)) * od_e[:, gc]
        state[g] = cd_e[:, gc] * st + _dot(b_gt, xd_b[:, gc])
        y = y_diag + y_off + dvec_ref[:, gc] * xs[:, gc]
        gt = y * _silu(z_ref[rows, gc].astype(F32))
        ms = jnp.mean(gt * gt, axis=-1, keepdims=True)
        yn_ref[rows, gc] = (gt * lax.rsqrt(ms + RMS_EPS) * nw_ref[:, gc]).astype(yn_ref.dtype)


def _ssd(p, dtp, cw, cb, dtb, alog, dvec, nw, e01, bsz, seq, rows):
    t = p.shape[0]
    L = SSD_CHUNK
    nc = seq // rows
    row_map = lambda b, c: (b * nc + c, 0)

    def pcol(k):
        return pl.BlockSpec((rows, SSD_INNER), lambda b, c: (b * nc + c, k))

    const = lambda b, c: (0, 0)
    return pl.pallas_call(
        _ssd_kernel,
        grid=(bsz, nc),
        in_specs=[pcol(1), pcol(2), pcol(0),
                  pl.BlockSpec((rows, LANES), row_map),
                  pl.BlockSpec((SSD_CONV, SSD_INNER), lambda b, c: (0, 0)),
                  pl.BlockSpec((SSD_CONV, SSD_INNER), lambda b, c: (0, 1)),
                  pl.BlockSpec((1, SSD_INNER), lambda b, c: (0, 0)),
                  pl.BlockSpec((1, SSD_INNER), lambda b, c: (0, 1)),
                  pl.BlockSpec((1, LANES), const),
                  pl.BlockSpec((1, LANES), const),
                  pl.BlockSpec((1, SSD_INNER), const),
                  pl.BlockSpec((1, SSD_INNER), const),
                  pl.BlockSpec((2 * LANES, SSD_INNER), const),
                  pl.BlockSpec(((SSD_CONV - 1) * L, 2 * L), const)],
        out_specs=pl.BlockSpec((rows, SSD_INNER), row_map),
        out_shape=jax.ShapeDtypeStruct((t, SSD_INNER), BF16),
        scratch_shapes=[pltpu.VMEM((L, SSD_INNER), BF16),
                        pltpu.VMEM((L, SSD_INNER), BF16),
                        pltpu.VMEM((SSD_GROUPS, SSD_STATE, GROUP_W), F32)],
        compiler_params=_cparams(("arbitrary", "arbitrary")),
        name="ssd",
    )(p, p, p, dtp, cw, cw, cb, cb, dtb, alog, dvec, nw, e01, _shift_matrix(L, SSD_CONV))


def _merge_kernel(yn_ref, scb_ref, scc_ref, sch_ref, ga_ref, gb_ref, x_ref, mod_ref,
                  wssd_ref, wsc_ref, wo_ref, cw_ref, lng_ref, lnb_ref, wr_ref,
                  x1_ref, h2_ref, h2p_ref, lg_ref, ext):
    s = pl.program_id(1)
    tm = x_ref.shape[0]

    @pl.when(s == 0)
    def _():
        ext[0:SUBLANES, :] = jnp.zeros((SUBLANES, D_MODEL), F32)

    ext[SUBLANES:SUBLANES + tm, :] = scc_ref[...].astype(F32) * sch_ref[...].astype(F32)
    m = mod_ref[0]
    for c in range(tm // MERGE_ROWS):
        r0 = c * MERGE_ROWS
        rows = slice(r0, r0 + MERGE_ROWS)
        u = None
        for k in range(SC_KERNEL):
            start = SUBLANES + r0 - (SC_KERNEL - 1) + k
            term = cw_ref[k:k + 1, :] * ext[start:start + MERGE_ROWS, :]
            u = term if u is None else u + term
        y_b = _dot((scb_ref[rows, :].astype(F32) * u).astype(BF16), wsc_ref[...])
        y_a = _dot(yn_ref[rows, :], wssd_ref[...])
        merged = (jax.nn.sigmoid(ga_ref[rows, :].astype(F32)) * y_a
                  + jax.nn.sigmoid(gb_ref[rows, :].astype(F32)) * y_b)
        mix = _dot(merged.astype(BF16), wo_ref[...])
        x1 = _ln(ALPHA * x_ref[rows, :] + m[2:3, :] * mix) * lng_ref[...] + lnb_ref[...]
        x1_ref[rows, :] = x1
        h2 = _ln(x1) * (1.0 + m[4:5, :]) + m[3:4, :]
        h2_hi = h2.astype(BF16)
        h2_ref[rows, :] = h2_hi
        w = _pack_pair(h2[:, 0:D_MODEL // 2], h2[:, D_MODEL // 2:D_MODEL])
        h2p_ref[0, rows, :] = w[:, 0:SC_ROW_WORDS]
        h2p_ref[1, rows, :] = w[:, SC_ROW_WORDS:2 * SC_ROW_WORDS]
        h2_lo = (h2 - h2_hi.astype(F32)).astype(BF16)
        both = _dot(h2_hi, wr_ref[...])
        lg_ref[rows, :] = (both[:, 0:LANES] + both[:, LANES:2 * LANES]
                           + _dot(h2_lo, wr_ref[:, 0:LANES]))
    ext[0:SUBLANES, :] = ext[tm:tm + SUBLANES, :]


def _merge(yn, p, x2d, mod3, wssd, wsc, wo, cw, lng, lnb, wr, bsz, seq, tm, b0):
    t = bsz * seq
    d = x2d.shape[1]
    ns = seq // tm
    row_map = lambda b, s: (b * ns + s, 0)
    const = lambda b, s: (0, 0)
    col0 = (2048 + 4096) // d

    def pcol(k):
        return pl.BlockSpec((tm, d), lambda b, s: (b * ns + s, col0 + k))

    return pl.pallas_call(
        _merge_kernel,
        grid=(bsz, ns),
        in_specs=[pl.BlockSpec((tm, SSD_INNER), row_map),
                  pcol(0), pcol(1), pcol(2), pcol(3), pcol(4),
                  pl.BlockSpec((tm, d), lambda b, s: ((b0 + b) * ns + s, 0)),
                  pl.BlockSpec((1, 6, d), lambda b, s: (b0 + b, 0, 0)),
                  pl.BlockSpec((SSD_INNER, d), const, pipeline_mode=pl.Buffered(1)),
                  pl.BlockSpec((d, d), const, pipeline_mode=pl.Buffered(1)),
                  pl.BlockSpec((d, d), const, pipeline_mode=pl.Buffered(1)),
                  pl.BlockSpec((SC_KERNEL, d), const),
                  pl.BlockSpec((1, d), const),
                  pl.BlockSpec((1, d), const),
                  pl.BlockSpec((d, 2 * LANES), const, pipeline_mode=pl.Buffered(1))],
        out_specs=[pl.BlockSpec((tm, d), row_map),
                   pl.BlockSpec((tm, d), row_map),
                   pl.BlockSpec((2, tm, SC_ROW_WORDS), lambda b, s: (0, b * ns + s, 0)),
                   pl.BlockSpec((tm, LANES), row_map)],
        out_shape=[jax.ShapeDtypeStruct((t, d), F32),
                   jax.ShapeDtypeStruct((t, d), BF16),
                   jax.ShapeDtypeStruct((2, t, SC_ROW_WORDS), U32),
                   jax.ShapeDtypeStruct((t, LANES), F32)],
        scratch_shapes=[pltpu.VMEM((tm + SUBLANES, d), F32)],
        compiler_params=_cparams(("arbitrary", "arbitrary")),
        name="merge",
    )(yn, p, p, p, p, p, x2d, mod3, wssd, wsc, wo, cw, lng, lnb, wr)


def _first_argmax(v, idx, n):
    m = jnp.max(v, axis=0, keepdims=True)
    first = jnp.min(jnp.where(v == m, idx, n), axis=0, keepdims=True)
    return first, idx == first


def _route_select_kernel(lg_ref, bias_ref, sel_ref, ek_ref, gr_ref, cnt_ref):
    i = pl.program_id(0)
    tm = lg_ref.shape[0]
    ne, ng, eg = N_EXPERTS, N_EXPERT_GROUPS, EXPERTS_PER_GROUP
    lt = lg_ref[...].T[0:ne, :]
    scores = jax.nn.sigmoid(lt)
    biased = scores + bias_ref[...][0:ne, 0:1]
    grp = biased.reshape(ng, eg, tm)
    idx_e = lax.broadcasted_iota(jnp.int32, (ng, eg, tm), 1)
    m1 = jnp.max(grp, axis=1, keepdims=True)
    first = jnp.min(jnp.where(grp == m1, idx_e, eg), axis=1, keepdims=True)
    m2 = jnp.max(jnp.where(idx_e == first, NEG_BIG, grp), axis=1, keepdims=True)
    gscore = m1 + m2
    idx_g = lax.broadcasted_iota(jnp.int32, (ng, 1, tm), 0)
    gsel = jnp.zeros((ng, 1, tm), F32)
    for _ in range(TOPK_EXPERT_GROUPS):
        _f, hit = _first_argmax(gscore, idx_g, ng)
        gsel = jnp.where(hit, 1.0, gsel)
        gscore = jnp.where(hit, NEG_BIG, gscore)
    emask = jnp.broadcast_to(gsel, (ng, eg, tm)).reshape(ne, tm)
    cand = jnp.where(emask > 0.0, biased, NEG_BIG)
    idx_x = lax.broadcasted_iota(jnp.int32, (ne, tm), 0)
    sel = jnp.zeros((ne, tm), F32)
    firsts = []
    gate_rows = []
    for _ in range(TOP_K):
        f, hit = _first_argmax(cand, idx_x, ne)
        sel = jnp.where(hit, 1.0, sel)
        cand = jnp.where(hit, NEG_BIG, cand)
        firsts.append(f)
        gate_rows.append(jnp.sum(jnp.where(hit, scores, 0.0), axis=0, keepdims=True))
    sel_ref[...] = sel.astype(BF16)
    ek_ref[...] = jnp.concatenate(firsts, axis=0)
    gr_ref[...] = jnp.concatenate(gate_rows, axis=0)

    @pl.when(i == 0)
    def _():
        cnt_ref[...] = jnp.zeros(cnt_ref.shape, F32)

    cnt_ref[...] += jnp.broadcast_to(jnp.sum(sel, axis=1, keepdims=True), (ne, LANES))


def _route_slots_kernel(sel_ref, ek_ref, gr_ref, cnt_ref, su_ref, g_ref, dest_ref, carry, start):
    i = pl.program_id(0)
    tm = sel_ref.shape[1]
    ne = N_EXPERTS

    @pl.when(i == 0)
    def _():
        r = lax.broadcasted_iota(jnp.int32, (ne, ne), 0)
        cc = lax.broadcasted_iota(jnp.int32, (ne, ne), 1)
        below = jnp.where(cc < r, 1.0, 0.0).astype(BF16)
        padded = jnp.floor((cnt_ref[...] + (GMM_BLOCK - 1.0)) * (1.0 / GMM_BLOCK)) * GMM_BLOCK
        start[...] = _dot01_exact(below, padded)
        carry[...] = jnp.zeros(carry.shape, F32)

    sel = sel_ref[...]
    before = _dot(sel, su_ref[...])
    slot = before + carry[:, 0:1] + start[:, 0:1]
    idx_x = lax.broadcasted_iota(jnp.int32, (ne, tm), 0)
    ek = ek_ref[...]
    dest_rows = []
    for k in range(TOP_K):
        hit = idx_x == ek[k:k + 1, :]
        dest_rows.append(jnp.sum(jnp.where(hit, slot, 0.0), axis=0, keepdims=True))
    dest_ref[...] = jnp.concatenate(dest_rows, axis=0).astype(jnp.int32)
    gate = gr_ref[...]
    gate = gate / jnp.sum(gate, axis=0, keepdims=True) * ROUTED_SCALE
    pad = jnp.zeros((LANES - TOP_K, tm), F32)
    g_ref[...] = jnp.concatenate([gate, pad], axis=0).T
    carry[...] += jnp.broadcast_to(jnp.sum(sel.astype(F32), axis=1, keepdims=True), (ne, LANES))


def _route(logits, bias_col, su, tm):
    t = logits.shape[0]
    sel, ek, gr, cnt = pl.pallas_call(
        _route_select_kernel,
        grid=(t // tm,),
        in_specs=[pl.BlockSpec((tm, LANES), lambda i: (i, 0)),
                  pl.BlockSpec((LANES, LANES), lambda i: (0, 0))],
        out_specs=[pl.BlockSpec((N_EXPERTS, tm), lambda i: (0, i)),
                   pl.BlockSpec((TOP_K, tm), lambda i: (0, i)),
                   pl.BlockSpec((TOP_K, tm), lambda i: (0, i)),
                   pl.BlockSpec((N_EXPERTS, LANES), lambda i: (0, 0))],
        out_shape=[jax.ShapeDtypeStruct((N_EXPERTS, t), BF16),
                   jax.ShapeDtypeStruct((TOP_K, t), jnp.int32),
                   jax.ShapeDtypeStruct((TOP_K, t), F32),
                   jax.ShapeDtypeStruct((N_EXPERTS, LANES), F32)],
        compiler_params=_cparams(("arbitrary",)),
        name="route_select",
    )(logits, bias_col)
    gates, dest = pl.pallas_call(
        _route_slots_kernel,
        grid=(t // tm,),
        in_specs=[pl.BlockSpec((N_EXPERTS, tm), lambda i: (0, i)),
                  pl.BlockSpec((TOP_K, tm), lambda i: (0, i)),
                  pl.BlockSpec((TOP_K, tm), lambda i: (0, i)),
                  pl.BlockSpec((N_EXPERTS, LANES), lambda i: (0, 0)),
                  pl.BlockSpec((tm, tm), lambda i: (0, 0))],
        out_specs=[pl.BlockSpec((tm, LANES), lambda i: (i, 0)),
                   pl.BlockSpec((TOP_K, tm), lambda i: (0, i))],
        out_shape=[jax.ShapeDtypeStruct((t, LANES), F32),
                   jax.ShapeDtypeStruct((TOP_K, t), jnp.int32)],
        scratch_shapes=[pltpu.VMEM((N_EXPERTS, LANES), F32),
                        pltpu.VMEM((N_EXPERTS, LANES), F32)],
        compiler_params=_cparams(("arbitrary",)),
        name="route_slots",
    )(sel, ek, gr, cnt, su)
    return gates, dest, cnt


def _sc_mesh():
    return plsc.VectorSubcoreMesh(core_axis_name="c", subcore_axis_name="s")


def _sc_scatter_rows(rows, idx, n_out):
    n_rows, w = rows.shape
    n_k = idx.shape[0]

    @pl.kernel(out_type=jax.ShapeDtypeStruct((n_out, w), rows.dtype), mesh=_sc_mesh(),
               scratch_types=[])
    def scatter(x_hbm, i_hbm, o_hbm):
        def body(x_vmem, i_vmem):
            for k in range(n_k):
                pltpu.sync_copy(x_vmem, o_hbm.at[i_vmem.at[k]])

        pltpu.emit_pipeline(
            body,
            grid=(n_rows // SC_WINDOW,),
            in_specs=[pl.BlockSpec((SC_WINDOW, w), index_map=lambda i: (i, 0)),
                      pl.BlockSpec((n_k, SC_WINDOW), index_map=lambda i: (0, i))],
            out_specs=[],
            core_axis_name=("c", "s"),
            dimension_semantics=(pltpu.PARALLEL,),
        )(x_hbm, i_hbm)

    return scatter(rows, idx)


def _sc_gather_rows(rows, idx):
    n = idx.shape[1]
    w = rows.shape[1]

    @pl.kernel(out_type=jax.ShapeDtypeStruct((n, w), rows.dtype), mesh=_sc_mesh(),
               scratch_types=[])
    def gather(x_hbm, i_hbm, o_hbm):
        def body(i_vmem, o_vmem):
            pltpu.sync_copy(x_hbm.at[i_vmem.at[0]], o_vmem)

        pltpu.emit_pipeline(
            body,
            grid=(n // SC_WINDOW,),
            in_specs=[pl.BlockSpec((1, SC_WINDOW), index_map=lambda i: (0, i))],
            out_specs=[pl.BlockSpec((SC_WINDOW, w), index_map=lambda i: (i, 0))],
            core_axis_name=("c", "s"),
            dimension_semantics=(pltpu.PARALLEL,),
        )(i_hbm, o_hbm)

    return gather(rows, idx)


def _gmm_kernel(vb_ref, ve_ref, vvalid_ref, vnew_ref, x_ref, wg_ref, wu_ref, wdn_ref,
                o_ref, wgu_ref, wd_ref):
    v = pl.program_id(0)
    valid = vvalid_ref[v]
    bm = x_ref.shape[1]
    half = D_MODEL // 2

    @pl.when(vnew_ref[v] == 1)
    def _():
        wgu_ref[:, 0:D_EXPERT] = wg_ref[0].astype(BF16)
        wgu_ref[:, D_EXPERT:2 * D_EXPERT] = wu_ref[0].astype(BF16)
        wd_ref[...] = wdn_ref[0].astype(BF16)

    @pl.when(valid > 0)
    def _():
        for c in range(bm // GMM_ROWS):
            rows = slice(c * GMM_ROWS, (c + 1) * GMM_ROWS)
            a0, b0 = _unpack_pair(x_ref[0, rows, :])
            a1, b1 = _unpack_pair(x_ref[1, rows, :])
            x = jnp.concatenate([a0.astype(BF16), a1.astype(BF16),
                                 b0.astype(BF16), b1.astype(BF16)], axis=1)
            a = _dot(x, wgu_ref[...])
            act = _silu(a[:, 0:D_EXPERT]) * a[:, D_EXPERT:2 * D_EXPERT]
            r = lax.broadcasted_iota(jnp.int32, (GMM_ROWS, D_EXPERT), 0) + c * GMM_ROWS
            act = jnp.where(r < valid, act, 0.0)
            y = _dot(act.astype(BF16), wd_ref[...])
            w = _pack_pair(y[:, 0:half], y[:, half:D_MODEL])
            o_ref[0, rows, :] = w[:, 0:SC_ROW_WORDS]
            o_ref[1, rows, :] = w[:, SC_ROW_WORDS:half]


def _gmm(xin, w_gate, w_up, w_down, vb, ve, vvalid, vnew, bm):
    _, n_rows, w = xin.shape
    d = D_MODEL
    n_visits = vb.shape[0]
    blk = lambda v, vb, ve, vvalid, vnew: (0, vb[v], 0)
    wsel = lambda v, vb, ve, vvalid, vnew: (ve[v], 0, 0)
    grid_spec = pltpu.PrefetchScalarGridSpec(
        num_scalar_prefetch=4,
        grid=(n_visits,),
        in_specs=[pl.BlockSpec((2, bm, w), blk),
                  pl.BlockSpec((1, d, D_EXPERT), wsel),
                  pl.BlockSpec((1, d, D_EXPERT), wsel),
                  pl.BlockSpec((1, D_EXPERT, d), wsel)],
        out_specs=pl.BlockSpec((2, bm, w), blk),
        scratch_shapes=[pltpu.VMEM((d, 2 * D_EXPERT), BF16),
                        pltpu.VMEM((D_EXPERT, d), BF16)],
    )
    return pl.pallas_call(
        _gmm_kernel,
        grid_spec=grid_spec,
        out_shape=jax.ShapeDtypeStruct((2, n_rows, w), U32),
        compiler_params=_cparams(("arbitrary",)),
        name="gmm",
    )(vb, ve, vvalid, vnew, xin, w_gate, w_up, w_down)


def _visit_plan(counts, n_blocks, bm):
    ne = counts.shape[0]
    nb = (counts + bm - 1) // bm
    vend = jnp.cumsum(nb)
    vstart = vend - nb
    total = vend[-1]
    v = jnp.arange(n_blocks, dtype=jnp.int32)
    vc = jnp.minimum(v, total - 1)
    e_v = jnp.minimum(jnp.sum((vend[None, :] <= vc[:, None]).astype(jnp.int32), axis=1), ne - 1)
    onehot = (e_v[:, None] == jnp.arange(ne, dtype=jnp.int32)[None, :]).astype(jnp.int32)
    pick = lambda tab: jnp.sum(onehot * tab[None, :].astype(jnp.int32), axis=1)
    valid = jnp.clip(pick(counts) - (vc - pick(vstart)) * bm, 0, bm)
    valid = jnp.where(v < total, valid, 0)
    i32 = lambda a: a.astype(jnp.int32)
    e_v = i32(e_v)
    new = jnp.concatenate([jnp.ones((1,), jnp.int32), i32(e_v[1:] != e_v[:-1])])
    return i32(vc), e_v, i32(valid), new


def _combine_kernel(yg_ref, g_ref, h_ref, x1_ref, mod_ref, wsgu_ref, wsd_ref, lng_ref,
                    lnb_ref, *rest):
    o_ref = rest[-1]
    q = D_MODEL // 4
    g = g_ref[...]
    a = _dot(h_ref[...], wsgu_ref[...])
    act = _silu(a[:, 0:D_EXPERT]) * a[:, D_EXPERT:2 * D_EXPERT]
    shared = _dot(act.astype(BF16), wsd_ref[...])
    f = [shared[:, j * q:(j + 1) * q] for j in range(4)]
    for k in range(TOP_K):
        a0, b0 = _unpack_pair(yg_ref[0, k])
        a1, b1 = _unpack_pair(yg_ref[1, k])
        gk = g[:, k:k + 1]
        f = [f[0] + gk * a0, f[1] + gk * a1, f[2] + gk * b0, f[3] + gk * b1]
    ffn = jnp.concatenate(f, axis=1)
    m = mod_ref[0]
    r = ALPHA * x1_ref[...] + m[5:6, :] * ffn
    o_ref[...] = _ln(r) * lng_ref[...] + lnb_ref[...]


def _combine(yg, gates, h2, x1, mod3, wsgu, wsd, lng, lnb, seq, tm, tok0, t, prev_out):
    tp, d = x1.shape
    b0 = tok0 // tm
    row = lambda i: (i, 0)
    const = lambda i: (0, 0)
    in_specs = [pl.BlockSpec((2, TOP_K, tm, SC_ROW_WORDS), lambda i: (0, 0, i, 0)),
                pl.BlockSpec((tm, LANES), row),
                pl.BlockSpec((tm, d), row),
                pl.BlockSpec((tm, d), row),
                pl.BlockSpec((1, 6, d), lambda i: (((b0 + i) * tm) // seq, 0, 0)),
                pl.BlockSpec((d, 2 * D_EXPERT), const),
                pl.BlockSpec((D_EXPERT, d), const),
                pl.BlockSpec((1, d), const),
                pl.BlockSpec((1, d), const)]
    args = [yg, gates, h2, x1, mod3, wsgu, wsd, lng, lnb]
    aliases = {}
    if prev_out is not None:
        in_specs.append(pl.BlockSpec(memory_space=pl.ANY))
        args.append(prev_out)
        aliases = {len(args) - 1: 0}
    return pl.pallas_call(
        _combine_kernel,
        grid=(tp // tm,),
        in_specs=in_specs,
        out_specs=pl.BlockSpec((tm, d), lambda i: (b0 + i, 0)),
        out_shape=jax.ShapeDtypeStruct((t, d), F32),
        input_output_aliases=aliases,
        compiler_params=_cparams(("arbitrary",)),
        name="combine",
    )(*args)


def _pad_lanes(v, fill=0.0):
    out = jnp.full((1, LANES), fill, F32)
    return out.at[0, :v.shape[0]].set(v.astype(F32))


def kernel(x, c, w_ada, b_ada, w_in, ssd_conv_w, ssd_conv_b, ssd_dt_bias, ssd_A_log, ssd_D,
           ssd_norm_w, w_ssd_out, sc_conv_w, w_sc_out, w_o, ln1_g, ln1_b, router_w,
           router_bias, w_gate, w_up, w_down, sh_gate, sh_up, sh_down, ln2_g, ln2_b):
    bsz, seq, d = x.shape
    t = bsz * seq
    depth = w_in.shape[0]
    tm_proj = min(1024, seq)
    tm_merge = min(512, seq)
    tm_route = min(1024, seq)
    tm_comb = min(512, seq)
    bm_gmm = GMM_BLOCK
    first = (6 * bsz) // 8 if bsz >= 2 else bsz
    groups = [(0, first)] + ([(first, bsz - first)] if bsz > first else [])

    e01 = (np.arange(LANES)[:, None] == (np.arange(SSD_INNER)[None, :] // SSD_HEAD_DIM))
    e01 = jnp.asarray(np.concatenate([e01, e01], axis=0), BF16)
    su = jnp.asarray(np.arange(tm_route)[:, None] < np.arange(tm_route)[None, :], BF16)

    xf = x.reshape(t, d)
    for l in range(depth):
        w_l = w_in[l]
        w_main = jnp.concatenate([w_l[:, :DT_COL0], w_l[:, DT_COL0 + SSD_HEADS:]],
                                 axis=1).astype(BF16)
        w_dt = jnp.pad(w_l[:, DT_COL0:DT_COL0 + SSD_HEADS],
                       ((0, 0), (0, LANES - SSD_HEADS))).astype(BF16)
        wsgu = jnp.concatenate([sh_gate[l], sh_up[l]], axis=-1).astype(BF16)
        wsd = sh_down[l].astype(BF16)
        wr = jnp.pad(router_w[l], ((0, 0), (0, LANES - N_EXPERTS)))
        wr_hi = wr.astype(BF16)
        wr = jnp.concatenate([wr_hi, (wr - wr_hi.astype(F32)).astype(BF16)], axis=1)
        bias_col = jnp.zeros((LANES, LANES), F32).at[:N_EXPERTS, 0].set(router_bias[l])
        dvec = jnp.repeat(ssd_D[l], SSD_HEAD_DIM).reshape(1, SSD_INNER)

        mod3 = _mod(c, w_ada[l], b_ada[l]).reshape(bsz, 6, d)
        x_in = xf

        def mix_and_route(b0, bg):
            tg = bg * seq
            n_assign = tg * TOP_K
            n_rows = n_assign + N_EXPERTS * bm_gmm
            p, dtp = _inproj(x_in, mod3, w_main, w_dt, seq, tm_proj, w_main.shape[1] // 4,
                             b0 * seq, tg)
            yn = _ssd(p, dtp, ssd_conv_w[l], ssd_conv_b[l].reshape(1, -1),
                      _pad_lanes(ssd_dt_bias[l]), _pad_lanes(ssd_A_log[l]), dvec,
                      ssd_norm_w[l].reshape(1, -1), e01, bg, seq, 4 * SSD_CHUNK)
            x1, h2, h2p, logits = _merge(yn, p, x_in, mod3, w_ssd_out[l].astype(BF16),
                                         w_sc_out[l].astype(BF16), w_o[l].astype(BF16),
                                         sc_conv_w[l], ln1_g[l].reshape(1, d),
                                         ln1_b[l].reshape(1, d), wr, bg, seq, tm_merge, b0)
            gates, dest, cnt = _route(logits, bias_col, su, tm_route)
            dest_both = jnp.concatenate([dest, dest + n_rows], axis=1)
            plan = _visit_plan(cnt[:, 0].astype(jnp.int32), n_rows // bm_gmm, bm_gmm)
            xin = _sc_scatter_rows(h2p.reshape(2 * tg, SC_ROW_WORDS), dest_both, 2 * n_rows)
            return dict(x1=x1, h2=h2, gates=gates, dest=dest, plan=plan, xin=xin, tok0=b0 * seq,
                        tg=tg, n_assign=n_assign, n_rows=n_rows)

        def experts(st):
            tg, n_assign, n_rows = st["tg"], st["n_assign"], st["n_rows"]
            yb = _gmm(st["xin"].reshape(2, n_rows, SC_ROW_WORDS), w_gate[l], w_up[l], w_down[l],
                      *st["plan"], bm_gmm)
            idx = st["dest"].reshape(1, n_assign)
            idx = jnp.concatenate([idx, idx + n_rows], axis=1)
            yg = _sc_gather_rows(yb.reshape(2 * n_rows, SC_ROW_WORDS), idx)
            return yg.reshape(2, TOP_K, tg, SC_ROW_WORDS)

        stages = [mix_and_route(b0, bg) for b0, bg in groups]
        gathered = [experts(st) for st in stages]
        out = None
        for st, yg in zip(stages, gathered):
            out = _combine(yg, st["gates"], st["h2"], st["x1"], mod3, wsgu, wsd,
                           ln2_g[l].reshape(1, d), ln2_b[l].reshape(1, d), seq, tm_comb,
                           st["tok0"], t, out)
        xf = out
    return xf.reshape(bsz, seq, d)
```

```python
import jax
import jax.numpy as jnp
import numpy as np
from jax import lax
from jax.experimental import pallas as pl
from jax.experimental.pallas import tpu as pltpu
from jax.experimental.pallas import tpu_sc as plsc

F32 = jnp.float32
BF16 = jnp.bfloat16
U32 = jnp.uint32

LANES = 128
SUBLANES = 8
VMEM_LIMIT_BYTES = 56 * 1024 * 1024

D_MODEL = 1024
SSD_INNER = 2048
SSD_HEAD_DIM = 64
SSD_HEADS = 32
SSD_GROUPS = 8
SSD_STATE = 128
SSD_CONV = 4
SSD_CHUNK = 128
GROUP_W = SSD_INNER // SSD_GROUPS
HEADS_PER_GROUP = SSD_HEADS // SSD_GROUPS
SC_KERNEL = 3
N_EXPERTS = 64
TOP_K = 8
N_EXPERT_GROUPS = 8
TOPK_EXPERT_GROUPS = 4
EXPERTS_PER_GROUP = N_EXPERTS // N_EXPERT_GROUPS
D_EXPERT = 256
ROUTED_SCALE = 2.5
LN_EPS = 1e-5
RMS_EPS = 1e-5
ALPHA = 2.0 ** 0.25
DT_COL0 = 2048 + 4096
NEG_BIG = -1e30
SC_WINDOW = 128
SC_ROW_WORDS = 256
GMM_BLOCK = 1024
GMM_ROWS = 256
MERGE_ROWS = 256


def _cparams(sem):
    return pltpu.CompilerParams(dimension_semantics=sem,
                                vmem_limit_bytes=VMEM_LIMIT_BYTES)


def _ln(x):
    mu = jnp.mean(x, axis=-1, keepdims=True)
    xc = x - mu
    var = jnp.mean(xc * xc, axis=-1, keepdims=True)
    return xc * lax.rsqrt(var + LN_EPS)


def _silu(x):
    h = 0.5 * x
    return h + h * jnp.tanh(h)


def _split3(a):
    hi = a.astype(BF16)
    r1 = a - hi.astype(F32)
    mid = r1.astype(BF16)
    lo = (r1 - mid.astype(F32)).astype(BF16)
    return hi, mid, lo


def _dot(a, b):
    return jnp.dot(a, b, preferred_element_type=F32)


def _dot01_exact(m01, a):
    hi, mid, lo = _split3(a)
    return _dot(m01, hi) + _dot(m01, mid) + _dot(m01, lo)


def _pack_pair(a, b):
    ab = pltpu.bitcast(a.astype(BF16).astype(F32), U32)
    bb = pltpu.bitcast(b.astype(BF16).astype(F32), U32)
    return ab | lax.shift_right_logical(bb, jnp.uint32(16))


def _unpack_pair(w):
    a = pltpu.bitcast(w & jnp.uint32(0xFFFF0000), F32)
    b = pltpu.bitcast(lax.shift_left(w, jnp.uint32(16)), F32)
    return a, b


def _mod_kernel(c_ref, w_ref, b_ref, o_ref):
    a = _silu(c_ref[...])
    o_ref[...] = jnp.dot(a, w_ref[...], precision=lax.Precision.HIGHEST,
                         preferred_element_type=F32) + b_ref[...]


def _mod(c, w, b):
    bsz, d = c.shape
    n = w.shape[1]
    tn = 1024
    return pl.pallas_call(
        _mod_kernel,
        grid=(n // tn,),
        in_specs=[pl.BlockSpec((bsz, d), lambda j: (0, 0)),
                  pl.BlockSpec((d, tn), lambda j: (0, j)),
                  pl.BlockSpec((1, tn), lambda j: (0, j))],
        out_specs=pl.BlockSpec((bsz, tn), lambda j: (0, j)),
        out_shape=jax.ShapeDtypeStruct((bsz, n), F32),
        compiler_params=_cparams(("arbitrary",)),
        name="mod",
    )(c, w, b.reshape(1, n))


def _inproj_kernel(x_ref, mod_ref, w_ref, wdt_ref, p_ref, dt_ref, h_scr):
    j = pl.program_id(1)

    @pl.when(j == 0)
    def _():
        m = mod_ref[0]
        h = _ln(x_ref[...]) * (1.0 + m[1:2, :]) + m[0:1, :]
        hb = h.astype(BF16)
        h_scr[...] = hb
        dt_ref[...] = _dot(hb, wdt_ref[...])

    p_ref[...] = _dot(h_scr[...], w_ref[...]).astype(p_ref.dtype)


def _inproj(x2d, mod3, w_main, w_dt, seq, tm, tn, tok0, t):
    d = x2d.shape[1]
    n = w_main.shape[1]
    i0 = tok0 // tm
    return pl.pallas_call(
        _inproj_kernel,
        grid=(t // tm, n // tn),
        in_specs=[pl.BlockSpec((tm, d), lambda i, j: (i0 + i, 0)),
                  pl.BlockSpec((1, 6, d), lambda i, j: (((i0 + i) * tm) // seq, 0, 0)),
                  pl.BlockSpec((d, tn), lambda i, j: (0, j)),
                  pl.BlockSpec((d, LANES), lambda i, j: (0, 0))],
        out_specs=[pl.BlockSpec((tm, tn), lambda i, j: (i, j)),
                   pl.BlockSpec((tm, LANES), lambda i, j: (i, 0))],
        out_shape=[jax.ShapeDtypeStruct((t, n), BF16),
                   jax.ShapeDtypeStruct((t, LANES), F32)],
        scratch_shapes=[pltpu.VMEM((tm, d), BF16)],
        compiler_params=_cparams(("arbitrary", "arbitrary")),
        name="inproj",
    )(x2d, mod3, w_main, w_dt)


def _shift_conv(cur, prev_scr, s_ref, w_ref, b_ref):
    rows = cur.shape[0]
    n_taps = w_ref.shape[0]
    shifted = _dot(s_ref[...], jnp.concatenate([prev_scr[...], cur], axis=0))
    acc = w_ref[n_taps - 1:n_taps, :] * cur.astype(F32) + b_ref[...]
    for k in range(n_taps - 1):
        acc = acc + w_ref[k:k + 1, :] * shifted[k * rows:(k + 1) * rows]
    prev_scr[...] = cur
    return acc


def _shift_matrix(rows, n_taps):
    s = np.zeros(((n_taps - 1) * rows, 2 * rows), np.float32)
    t = np.arange(rows)
    for k in range(n_taps - 1):
        s[k * rows + t, rows + t - (n_taps - 1) + k] = 1.0
    return jnp.asarray(s, BF16)


def _ssd_kernel(xs_ref, bc_ref, z_ref, dtp_ref, cw_xs_ref, cw_bc_ref, cb_xs_ref,
                cb_bc_ref, dtb_ref, alog_ref, dvec_ref, nw_ref, e_ref, s_ref,
                yn_ref, prev_xs, prev_bc, state):
    @pl.when(pl.program_id(1) == 0)
    def _():
        prev_xs[...] = jnp.zeros(prev_xs.shape, BF16)
        prev_bc[...] = jnp.zeros(prev_bc.shape, BF16)
        state[...] = jnp.zeros(state.shape, F32)

    for cc in range(xs_ref.shape[0] // SSD_CHUNK):
        _ssd_chunk(slice(cc * SSD_CHUNK, (cc + 1) * SSD_CHUNK), xs_ref, bc_ref, z_ref, dtp_ref,
                   cw_xs_ref, cw_bc_ref, cb_xs_ref, cb_bc_ref, dtb_ref, alog_ref, dvec_ref,
                   nw_ref, e_ref, s_ref, yn_ref, prev_xs, prev_bc, state)


def _ssd_chunk(rows, xs_ref, bc_ref, z_ref, dtp_ref, cw_xs_ref, cw_bc_ref, cb_xs_ref,
               cb_bc_ref, dtb_ref, alog_ref, dvec_ref, nw_ref, e_ref, s_ref,
               yn_ref, prev_xs, prev_bc, state):
    L = SSD_CHUNK
    xs = _silu(_shift_conv(xs_ref[rows, :], prev_xs, s_ref, cw_xs_ref, cb_xs_ref))
    bc = _silu(_shift_conv(bc_ref[rows, :], prev_bc, s_ref, cw_bc_ref, cb_bc_ref))

    x_dt = dtp_ref[rows, :] + dtb_ref[...]
    dt = jnp.maximum(x_dt, 0.0) + jnp.log1p(jnp.exp(-jnp.abs(x_dt)))
    a_neg = -jnp.exp(alog_ref[...])
    d_a = dt * a_neg
    row = lax.broadcasted_iota(jnp.int32, (L, L), 0)
    col = lax.broadcasted_iota(jnp.int32, (L, L), 1)
    causal = col <= row
    tri = jnp.where(causal, 1.0, 0.0).astype(BF16)
    hi, mid, lo = _split3(d_a)
    a_cs = _dot(tri, hi) + _dot(tri, mid) + _dot(tri, lo)
    a_cs_t = a_cs.T
    a_last = a_cs[L - 1:L, :]
    stack = jnp.concatenate([dt, jnp.exp(a_cs), jnp.exp(a_last - a_cs)], axis=0)
    s_hi = stack.astype(BF16)
    s_mid = (stack - s_hi.astype(F32)).astype(BF16)
    st_e = _dot(jnp.concatenate([s_hi, s_mid], axis=1), e_ref[...])
    dt_e = st_e[0:L]
    od_e = st_e[L:2 * L]
    ds_e = st_e[2 * L:3 * L]
    cd_e = od_e[L - 1:L, :]

    x_f = xs * dt_e
    x_b = x_f.astype(BF16)
    xd_b = (x_f * ds_e).astype(BF16)
    lane_head = lax.broadcasted_iota(jnp.int32, (L, GROUP_W), 1) // SSD_HEAD_DIM

    for g in range(SSD_GROUPS):
        gc = slice(g * GROUP_W, (g + 1) * GROUP_W)
        b_g = bc[:, g * SSD_STATE:(g + 1) * SSD_STATE]
        c_g = bc[:, SSD_GROUPS * SSD_STATE + g * SSD_STATE:
                 SSD_GROUPS * SSD_STATE + (g + 1) * SSD_STATE].astype(BF16)
        b_gt = b_g.T.astype(BF16)
        cb = _dot(c_g, b_gt)
        x_g = x_b[:, gc]
        lhs = []
        rhs = []
        for r in range(HEADS_PER_GROUP):
            h = g * HEADS_PER_GROUP + r
            seg = a_cs[:, h:h + 1] - a_cs_t[h:h + 1, :]
            lmat = jnp.exp(jnp.where(causal, seg, NEG_BIG))
            lhs.append((cb * lmat).astype(BF16))
            rhs.append(jnp.where(lane_head == r, x_g, jnp.zeros_like(x_g)))
        y_diag = _dot(jnp.concatenate(lhs, axis=1), jnp.concatenate(rhs, axis=0))
        st = state[g]
        y_off = _dot(c_g, st.astype(BF16)) * od_e[:, gc]
        state[g] = cd_e[:, gc] * st + _dot(b_gt, xd_b[:, gc])
        y = y_diag + y_off + dvec_ref[:, gc] * xs[:, gc]
        gt = y * _silu(z_ref[rows, gc].astype(F32))
        ms = jnp.mean(gt * gt, axis=-1, keepdims=True)
        yn_ref[rows, gc] = (gt * lax.rsqrt(ms + RMS_EPS) * nw_ref[:, gc]).astype(yn_ref.dtype)


def _ssd(p, dtp, cw, cb, dtb, alog, dvec, nw, e01, bsz, seq, rows):
    t = p.shape[0]
    L = SSD_CHUNK
    nc = seq // rows
    row_map = lambda b, c: (b * nc + c, 0)

    def pcol(k):
        return pl.BlockSpec((rows, SSD_INNER), lambda b, c: (b * nc + c, k))

    const = lambda b, c: (0, 0)
    return pl.pallas_call(
        _ssd_kernel,
        grid=(bsz, nc),
        in_specs=[pcol(1), pcol(2), pcol(0),
                  pl.BlockSpec((rows, LANES), row_map),
                  pl.BlockSpec((SSD_CONV, SSD_INNER), lambda b, c: (0, 0)),
                  pl.BlockSpec((SSD_CONV, SSD_INNER), lambda b, c: (0, 1)),
                  pl.BlockSpec((1, SSD_INNER), lambda b, c: (0, 0)),
                  pl.BlockSpec((1, SSD_INNER), lambda b, c: (0, 1)),
                  pl.BlockSpec((1, LANES), const),
                  pl.BlockSpec((1, LANES), const),
                  pl.BlockSpec((1, SSD_INNER), const),
                  pl.BlockSpec((1, SSD_INNER), const),
                  pl.BlockSpec((2 * LANES, SSD_INNER), const),
                  pl.BlockSpec(((SSD_CONV - 1) * L, 2 * L), const)],
        out_specs=pl.BlockSpec((rows, SSD_INNER), row_map),
        out_shape=jax.ShapeDtypeStruct((t, SSD_INNER), BF16),
        scratch_shapes=[pltpu.VMEM((L, SSD_INNER), BF16),
                        pltpu.VMEM((L, SSD_INNER), BF16),
                        pltpu.VMEM((SSD_GROUPS, SSD_STATE, GROUP_W), F32)],
        compiler_params=_cparams(("arbitrary", "arbitrary")),
        name="ssd",
    )(p, p, p, dtp, cw, cw, cb, cb, dtb, alog, dvec, nw, e01, _shift_matrix(L, SSD_CONV))


def _merge_kernel(yn_ref, scb_ref, scc_ref, sch_ref, ga_ref, gb_ref, x_ref, mod_ref,
                  wssd_ref, wsc_ref, wo_ref, cw_ref, lng_ref, lnb_ref, wr_ref,
                  x1_ref, h2_ref, h2p_ref, lg_ref, ext):
    s = pl.program_id(1)
    tm = x_ref.shape[0]

    @pl.when(s == 0)
    def _():
        ext[0:SUBLANES, :] = jnp.zeros((SUBLANES, D_MODEL), F32)

    ext[SUBLANES:SUBLANES + tm, :] = scc_ref[...].astype(F32) * sch_ref[...].astype(F32)
    m = mod_ref[0]
    for c in range(tm // MERGE_ROWS):
        r0 = c * MERGE_ROWS
        rows = slice(r0, r0 + MERGE_ROWS)
        u = None
        for k in range(SC_KERNEL):
            start = SUBLANES + r0 - (SC_KERNEL - 1) + k
            term = cw_ref[k:k + 1, :] * ext[start:start + MERGE_ROWS, :]
            u = term if u is None else u + term
        y_b = _dot((scb_ref[rows, :].astype(F32) * u).astype(BF16), wsc_ref[...])
        y_a = _dot(yn_ref[rows, :], wssd_ref[...])
        merged = (jax.nn.sigmoid(ga_ref[rows, :].astype(F32)) * y_a
                  + jax.nn.sigmoid(gb_ref[rows, :].astype(F32)) * y_b)
        mix = _dot(merged.astype(BF16), wo_ref[...])
        x1 = _ln(ALPHA * x_ref[rows, :] + m[2:3, :] * mix) * lng_ref[...] + lnb_ref[...]
        x1_ref[rows, :] = x1
        h2 = _ln(x1) * (1.0 + m[4:5, :]) + m[3:4, :]
        h2_hi = h2.astype(BF16)
        h2_ref[rows, :] = h2_hi
        w = _pack_pair(h2[:, 0:D_MODEL // 2], h2[:, D_MODEL // 2:D_MODEL])
        h2p_ref[0, rows, :] = w[:, 0:SC_ROW_WORDS]
        h2p_ref[1, rows, :] = w[:, SC_ROW_WORDS:2 * SC_ROW_WORDS]
        h2_lo = (h2 - h2_hi.astype(F32)).astype(BF16)
        both = _dot(h2_hi, wr_ref[...])
        lg_ref[rows, :] = (both[:, 0:LANES] + both[:, LANES:2 * LANES]
                           + _dot(h2_lo, wr_ref[:, 0:LANES]))
    ext[0:SUBLANES, :] = ext[tm:tm + SUBLANES, :]


def _merge(yn, p, x2d, mod3, wssd, wsc, wo, cw, lng, lnb, wr, bsz, seq, tm, b0):
    t = bsz * seq
    d = x2d.shape[1]
    ns = seq // tm
    row_map = lambda b, s: (b * ns + s, 0)
    const = lambda b, s: (0, 0)
    col0 = (2048 + 4096) // d

    def pcol(k):
        return pl.BlockSpec((tm, d), lambda b, s: (b * ns + s, col0 + k))

    return pl.pallas_call(
        _merge_kernel,
        grid=(bsz, ns),
        in_specs=[pl.BlockSpec((tm, SSD_INNER), row_map),
                  pcol(0), pcol(1), pcol(2), pcol(3), pcol(4),
                  pl.BlockSpec((tm, d), lambda b, s: ((b0 + b) * ns + s, 0)),
                  pl.BlockSpec((1, 6, d), lambda b, s: (b0 + b, 0, 0)),
                  pl.BlockSpec((SSD_INNER, d), const, pipeline_mode=pl.Buffered(1)),
                  pl.BlockSpec((d, d), const, pipeline_mode=pl.Buffered(1)),
                  pl.BlockSpec((d, d), const, pipeline_mode=pl.Buffered(1)),
                  pl.BlockSpec((SC_KERNEL, d), const),
                  pl.BlockSpec((1, d), const),
                  pl.BlockSpec((1, d), const),
                  pl.BlockSpec((d, 2 * LANES), const, pipeline_mode=pl.Buffered(1))],
        out_specs=[pl.BlockSpec((tm, d), row_map),
                   pl.BlockSpec((tm, d), row_map),
                   pl.BlockSpec((2, tm, SC_ROW_WORDS), lambda b, s: (0, b * ns + s, 0)),
                   pl.BlockSpec((tm, LANES), row_map)],
        out_shape=[jax.ShapeDtypeStruct((t, d), F32),
                   jax.ShapeDtypeStruct((t, d), BF16),
                   jax.ShapeDtypeStruct((2, t, SC_ROW_WORDS), U32),
                   jax.ShapeDtypeStruct((t, LANES), F32)],
        scratch_shapes=[pltpu.VMEM((tm + SUBLANES, d), F32)],
        compiler_params=_cparams(("arbitrary", "arbitrary")),
        name="merge",
    )(yn, p, p, p, p, p, x2d, mod3, wssd, wsc, wo, cw, lng, lnb, wr)


def _first_argmax(v, idx, n):
    m = jnp.max(v, axis=0, keepdims=True)
    first = jnp.min(jnp.where(v == m, idx, n), axis=0, keepdims=True)
    return first, idx == first


def _route_select_kernel(lg_ref, bias_ref, sel_ref, ek_ref, gr_ref, cnt_ref):
    i = pl.program_id(0)
    tm = lg_ref.shape[0]
    ne, ng, eg = N_EXPERTS, N_EXPERT_GROUPS, EXPERTS_PER_GROUP
    lt = lg_ref[...].T[0:ne, :]
    scores = jax.nn.sigmoid(lt)
    biased = scores + bias_ref[...][0:ne, 0:1]
    grp = biased.reshape(ng, eg, tm)
    idx_e = lax.broadcasted_iota(jnp.int32, (ng, eg, tm), 1)
    m1 = jnp.max(grp, axis=1, keepdims=True)
    first = jnp.min(jnp.where(grp == m1, idx_e, eg), axis=1, keepdims=True)
    m2 = jnp.max(jnp.where(idx_e == first, NEG_BIG, grp), axis=1, keepdims=True)
    gscore = m1 + m2
    idx_g = lax.broadcasted_iota(jnp.int32, (ng, 1, tm), 0)
    gsel = jnp.zeros((ng, 1, tm), F32)
    for _ in range(TOPK_EXPERT_GROUPS):
        _f, hit = _first_argmax(gscore, idx_g, ng)
        gsel = jnp.where(hit, 1.0, gsel)
        gscore = jnp.where(hit, NEG_BIG, gscore)
    emask = jnp.broadcast_to(gsel, (ng, eg, tm)).reshape(ne, tm)
    cand = jnp.where(emask > 0.0, biased, NEG_BIG)
    idx_x = lax.broadcasted_iota(jnp.int32, (ne, tm), 0)
    sel = jnp.zeros((ne, tm), F32)
    firsts = []
    gate_rows = []
    for _ in range(TOP_K):
        f, hit = _first_argmax(cand, idx_x, ne)
        sel = jnp.where(hit, 1.0, sel)
        cand = jnp.where(hit, NEG_BIG, cand)
        firsts.append(f)
        gate_rows.append(jnp.sum(jnp.where(hit, scores, 0.0), axis=0, keepdims=True))
    sel_ref[...] = sel.astype(BF16)
    ek_ref[...] = jnp.concatenate(firsts, axis=0)
    gr_ref[...] = jnp.concatenate(gate_rows, axis=0)

    @pl.when(i == 0)
    def _():
        cnt_ref[...] = jnp.zeros(cnt_ref.shape, F32)

    cnt_ref[...] += jnp.broadcast_to(jnp.sum(sel, axis=1, keepdims=True), (ne, LANES))


def _route_slots_kernel(sel_ref, ek_ref, gr_ref, cnt_ref, su_ref, g_ref, dest_ref, carry, start):
    i = pl.program_id(0)
    tm = sel_ref.shape[1]
    ne = N_EXPERTS

    @pl.when(i == 0)
    def _():
        r = lax.broadcasted_iota(jnp.int32, (ne, ne), 0)
        cc = lax.broadcasted_iota(jnp.int32, (ne, ne), 1)
        below = jnp.where(cc < r, 1.0, 0.0).astype(BF16)
        padded = jnp.floor((cnt_ref[...] + (GMM_BLOCK - 1.0)) * (1.0 / GMM_BLOCK)) * GMM_BLOCK
        start[...] = _dot01_exact(below, padded)
        carry[...] = jnp.zeros(carry.shape, F32)

    sel = sel_ref[...]
    before = _dot(sel, su_ref[...])
    slot = before + carry[:, 0:1] + start[:, 0:1]
    idx_x = lax.broadcasted_iota(jnp.int32, (ne, tm), 0)
    ek = ek_ref[...]
    dest_rows = []
    for k in range(TOP_K):
        hit = idx_x == ek[k:k + 1, :]
        dest_rows.append(jnp.sum(jnp.where(hit, slot, 0.0), axis=0, keepdims=True))
    dest_ref[...] = jnp.concatenate(dest_rows, axis=0).astype(jnp.int32)
    gate = gr_ref[...]
    gate = gate / jnp.sum(gate, axis=0, keepdims=True) * ROUTED_SCALE
    pad = jnp.zeros((LANES - TOP_K, tm), F32)
    g_ref[...] = jnp.concatenate([gate, pad], axis=0).T
    carry[...] += jnp.broadcast_to(jnp.sum(sel.astype(F32), axis=1, keepdims=True), (ne, LANES))


def _route(logits, bias_col, su, tm):
    t = logits.shape[0]
    sel, ek, gr, cnt = pl.pallas_call(
        _route_select_kernel,
        grid=(t // tm,),
        in_specs=[pl.BlockSpec((tm, LANES), lambda i: (i, 0)),
                  pl.BlockSpec((LANES, LANES), lambda i: (0, 0))],
        out_specs=[pl.BlockSpec((N_EXPERTS, tm), lambda i: (0, i)),
                   pl.BlockSpec((TOP_K, tm), lambda i: (0, i)),
                   pl.BlockSpec((TOP_K, tm), lambda i: (0, i)),
                   pl.BlockSpec((N_EXPERTS, LANES), lambda i: (0, 0))],
        out_shape=[jax.ShapeDtypeStruct((N_EXPERTS, t), BF16),
                   jax.ShapeDtypeStruct((TOP_K, t), jnp.int32),
                   jax.ShapeDtypeStruct((TOP_K, t), F32),
                   jax.ShapeDtypeStruct((N_EXPERTS, LANES), F32)],
        compiler_params=_cparams(("arbitrary",)),
        name="route_select",
    )(logits, bias_col)
    gates, dest = pl.pallas_call(
        _route_slots_kernel,
        grid=(t // tm,),
        in_specs=[pl.BlockSpec((N_EXPERTS, tm), lambda i: (0, i)),
                  pl.BlockSpec((TOP_K, tm), lambda i: (0, i)),
                  pl.BlockSpec((TOP_K, tm), lambda i: (0, i)),
                  pl.BlockSpec((N_EXPERTS, LANES), lambda i: (0, 0)),
                  pl.BlockSpec((tm, tm), lambda i: (0, 0))],
        out_specs=[pl.BlockSpec((tm, LANES), lambda i: (i, 0)),
                   pl.BlockSpec((TOP_K, tm), lambda i: (0, i))],
        out_shape=[jax.ShapeDtypeStruct((t, LANES), F32),
                   jax.ShapeDtypeStruct((TOP_K, t), jnp.int32)],
        scratch_shapes=[pltpu.VMEM((N_EXPERTS, LANES), F32),
                        pltpu.VMEM((N_EXPERTS, LANES), F32)],
        compiler_params=_cparams(("arbitrary",)),
        name="route_slots",
    )(sel, ek, gr, cnt, su)
    return gates, dest, cnt


def _sc_mesh():
    return plsc.VectorSubcoreMesh(core_axis_name="c", subcore_axis_name="s")


def _sc_scatter_rows(rows, idx, n_out):
    n_rows, w = rows.shape
    n_k = idx.shape[0]

    @pl.kernel(out_type=jax.ShapeDtypeStruct((n_out, w), rows.dtype), mesh=_sc_mesh(),
               scratch_types=[])
    def scatter(x_hbm, i_hbm, o_hbm):
        def body(x_vmem, i_vmem):
            for k in range(n_k):
                pltpu.sync_copy(x_vmem, o_hbm.at[i_vmem.at[k]])

        pltpu.emit_pipeline(
            body,
            grid=(n_rows // SC_WINDOW,),
            in_specs=[pl.BlockSpec((SC_WINDOW, w), index_map=lambda i: (i, 0)),
                      pl.BlockSpec((n_k, SC_WINDOW), index_map=lambda i: (0, i))],
            out_specs=[],
            core_axis_name=("c", "s"),
            dimension_semantics=(pltpu.PARALLEL,),
        )(x_hbm, i_hbm)

    return scatter(rows, idx)


def _sc_gather_rows(rows, idx):
    n = idx.shape[1]
    w = rows.shape[1]

    @pl.kernel(out_type=jax.ShapeDtypeStruct((n, w), rows.dtype), mesh=_sc_mesh(),
               scratch_types=[])
    def gather(x_hbm, i_hbm, o_hbm):
        def body(i_vmem, o_vmem):
            pltpu.sync_copy(x_hbm.at[i_vmem.at[0]], o_vmem)

        pltpu.emit_pipeline(
            body,
            grid=(n // SC_WINDOW,),
            in_specs=[pl.BlockSpec((1, SC_WINDOW), index_map=lambda i: (0, i))],
            out_specs=[pl.BlockSpec((SC_WINDOW, w), index_map=lambda i: (i, 0))],
            core_axis_name=("c", "s"),
            dimension_semantics=(pltpu.PARALLEL,),
        )(i_hbm, o_hbm)

    return gather(rows, idx)


def _gmm_kernel(vb_ref, ve_ref, vvalid_ref, vnew_ref, x_ref, wg_ref, wu_ref, wdn_ref,
                o_ref, wgu_ref, wd_ref):
    v = pl.program_id(0)
    valid = vvalid_ref[v]
    bm = x_ref.shape[1]
    half = D_MODEL // 2

    @pl.when(vnew_ref[v] == 1)
    def _():
        wgu_ref[:, 0:D_EXPERT] = wg_ref[0].astype(BF16)
        wgu_ref[:, D_EXPERT:2 * D_EXPERT] = wu_ref[0].astype(BF16)
        wd_ref[...] = wdn_ref[0].astype(BF16)

    @pl.when(valid > 0)
    def _():
        for c in range(bm // GMM_ROWS):
            rows = slice(c * GMM_ROWS, (c + 1) * GMM_ROWS)
            a0, b0 = _unpack_pair(x_ref[0, rows, :])
            a1, b1 = _unpack_pair(x_ref[1, rows, :])
            x = jnp.concatenate([a0.astype(BF16), a1.astype(BF16),
                                 b0.astype(BF16), b1.astype(BF16)], axis=1)
            a = _dot(x, wgu_ref[...])
            act = _silu(a[:, 0:D_EXPERT]) * a[:, D_EXPERT:2 * D_EXPERT]
            r = lax.broadcasted_iota(jnp.int32, (GMM_ROWS, D_EXPERT), 0) + c * GMM_ROWS
            act = jnp.where(r < valid, act, 0.0)
            y = _dot(act.astype(BF16), wd_ref[...])
            w = _pack_pair(y[:, 0:half], y[:, half:D_MODEL])
            o_ref[0, rows, :] = w[:, 0:SC_ROW_WORDS]
            o_ref[1, rows, :] = w[:, SC_ROW_WORDS:half]


def _gmm(xin, w_gate, w_up, w_down, vb, ve, vvalid, vnew, bm):
    _, n_rows, w = xin.shape
    d = D_MODEL
    n_visits = vb.shape[0]
    blk = lambda v, vb, ve, vvalid, vnew: (0, vb[v], 0)
    wsel = lambda v, vb, ve, vvalid, vnew: (ve[v], 0, 0)
    grid_spec = pltpu.PrefetchScalarGridSpec(
        num_scalar_prefetch=4,
        grid=(n_visits,),
        in_specs=[pl.BlockSpec((2, bm, w), blk),
                  pl.BlockSpec((1, d, D_EXPERT), wsel),
                  pl.BlockSpec((1, d, D_EXPERT), wsel),
                  pl.BlockSpec((1, D_EXPERT, d), wsel)],
        out_specs=pl.BlockSpec((2, bm, w), blk),
        scratch_shapes=[pltpu.VMEM((d, 2 * D_EXPERT), BF16),
                        pltpu.VMEM((D_EXPERT, d), BF16)],
    )
    return pl.pallas_call(
        _gmm_kernel,
        grid_spec=grid_spec,
        out_shape=jax.ShapeDtypeStruct((2, n_rows, w), U32),
        compiler_params=_cparams(("arbitrary",)),
        name="gmm",
    )(vb, ve, vvalid, vnew, xin, w_gate, w_up, w_down)


def _visit_plan(counts, n_blocks, bm):
    ne = counts.shape[0]
    nb = (counts + bm - 1) // bm
    vend = jnp.cumsum(nb)
    vstart = vend - nb
    total = vend[-1]
    v = jnp.arange(n_blocks, dtype=jnp.int32)
    vc = jnp.minimum(v, total - 1)
    e_v = jnp.minimum(jnp.sum((vend[None, :] <= vc[:, None]).astype(jnp.int32), axis=1), ne - 1)
    onehot = (e_v[:, None] == jnp.arange(ne, dtype=jnp.int32)[None, :]).astype(jnp.int32)
    pick = lambda tab: jnp.sum(onehot * tab[None, :].astype(jnp.int32), axis=1)
    valid = jnp.clip(pick(counts) - (vc - pick(vstart)) * bm, 0, bm)
    valid = jnp.where(v < total, valid, 0)
    i32 = lambda a: a.astype(jnp.int32)
    e_v = i32(e_v)
    new = jnp.concatenate([jnp.ones((1,), jnp.int32), i32(e_v[1:] != e_v[:-1])])
    return i32(vc), e_v, i32(valid), new


def _combine_kernel(yg_ref, g_ref, h_ref, x1_ref, mod_ref, wsgu_ref, wsd_ref, lng_ref,
                    lnb_ref, *rest):
    o_ref = rest[-1]
    q = D_MODEL // 4
    g = g_ref[...]
    a = _dot(h_ref[...], wsgu_ref[...])
    act = _silu(a[:, 0:D_EXPERT]) * a[:, D_EXPERT:2 * D_EXPERT]
    shared = _dot(act.astype(BF16), wsd_ref[...])
    f = [shared[:, j * q:(j + 1) * q] for j in range(4)]
    for k in range(TOP_K):
        a0, b0 = _unpack_pair(yg_ref[0, k])
        a1, b1 = _unpack_pair(yg_ref[1, k])
        gk = g[:, k:k + 1]
        f = [f[0] + gk * a0, f[1] + gk * a1, f[2] + gk * b0, f[3] + gk * b1]
    ffn = jnp.concatenate(f, axis=1)
    m = mod_ref[0]
    r = ALPHA * x1_ref[...] + m[5:6, :] * ffn
    o_ref[...] = _ln(r) * lng_ref[...] + lnb_ref[...]


def _combine(yg, gates, h2, x1, mod3, wsgu, wsd, lng, lnb, seq, tm, tok0, t, prev_out):
    tp, d = x1.shape
    b0 = tok0 // tm
    row = lambda i: (i, 0)
    const = lambda i: (0, 0)
    in_specs = [pl.BlockSpec((2, TOP_K, tm, SC_ROW_WORDS), lambda i: (0, 0, i, 0)),
                pl.BlockSpec((tm, LANES), row),
                pl.BlockSpec((tm, d), row),
                pl.BlockSpec((tm, d), row),
                pl.BlockSpec((1, 6, d), lambda i: (((b0 + i) * tm) // seq, 0, 0)),
                pl.BlockSpec((d, 2 * D_EXPERT), const),
                pl.BlockSpec((D_EXPERT, d), const),
                pl.BlockSpec((1, d), const),
                pl.BlockSpec((1, d), const)]
    args = [yg, gates, h2, x1, mod3, wsgu, wsd, lng, lnb]
    aliases = {}
    if prev_out is not None:
        in_specs.append(pl.BlockSpec(memory_space=pl.ANY))
        args.append(prev_out)
        aliases = {len(args) - 1: 0}
    return pl.pallas_call(
        _combine_kernel,
        grid=(tp // tm,),
        in_specs=in_specs,
        out_specs=pl.BlockSpec((tm, d), lambda i: (b0 + i, 0)),
        out_shape=jax.ShapeDtypeStruct((t, d), F32),
        input_output_aliases=aliases,
        compiler_params=_cparams(("arbitrary",)),
        name="combine",
    )(*args)


def _pad_lanes(v):
    return jnp.zeros((1, LANES), F32).at[0, :v.shape[0]].set(v.astype(F32))


def kernel(x, c, w_ada, b_ada, w_in, ssd_conv_w, ssd_conv_b, ssd_dt_bias, ssd_A_log, ssd_D,
           ssd_norm_w, w_ssd_out, sc_conv_w, w_sc_out, w_o, ln1_g, ln1_b, router_w,
           router_bias, w_gate, w_up, w_down, sh_gate, sh_up, sh_down, ln2_g, ln2_b):
    bsz, seq, d = x.shape
    t = bsz * seq
    depth = w_in.shape[0]
    ssd_rows = 4 * SSD_CHUNK
    assert d == D_MODEL and seq % ssd_rows == 0, (x.shape,)
    tm_proj = min(1024, seq)
    tm_merge = min(512, seq)
    tm_route = min(1024, seq)
    tm_comb = min(512, seq)
    bm_gmm = GMM_BLOCK
    first = (6 * bsz) // 8 if bsz >= 2 else bsz
    groups = [(0, first)] + ([(first, bsz - first)] if bsz > first else [])

    e01 = (np.arange(LANES)[:, None] == (np.arange(SSD_INNER)[None, :] // SSD_HEAD_DIM))
    e01 = jnp.asarray(np.concatenate([e01, e01], axis=0), BF16)
    su = jnp.asarray(np.arange(tm_route)[:, None] < np.arange(tm_route)[None, :], BF16)

    xf = x.reshape(t, d)
    for l in range(depth):
        w_l = w_in[l]
        w_main = jnp.concatenate([w_l[:, :DT_COL0], w_l[:, DT_COL0 + SSD_HEADS:]],
                                 axis=1).astype(BF16)
        w_dt = jnp.pad(w_l[:, DT_COL0:DT_COL0 + SSD_HEADS],
                       ((0, 0), (0, LANES - SSD_HEADS))).astype(BF16)
        wsgu = jnp.concatenate([sh_gate[l], sh_up[l]], axis=-1).astype(BF16)
        wsd = sh_down[l].astype(BF16)
        wr = jnp.pad(router_w[l], ((0, 0), (0, LANES - N_EXPERTS)))
        wr_hi = wr.astype(BF16)
        wr = jnp.concatenate([wr_hi, (wr - wr_hi.astype(F32)).astype(BF16)], axis=1)
        bias_col = jnp.zeros((LANES, LANES), F32).at[:N_EXPERTS, 0].set(router_bias[l])
        dvec = jnp.repeat(ssd_D[l], SSD_HEAD_DIM).reshape(1, SSD_INNER)

        mod3 = _mod(c, w_ada[l], b_ada[l]).reshape(bsz, 6, d)
        x_in = xf

        def mix_and_route(b0, bg):
            tg = bg * seq
            n_assign = tg * TOP_K
            n_rows = n_assign + N_EXPERTS * bm_gmm
            p, dtp = _inproj(x_in, mod3, w_main, w_dt, seq, tm_proj, w_main.shape[1] // 4,
                             b0 * seq, tg)
            yn = _ssd(p, dtp, ssd_conv_w[l], ssd_conv_b[l].reshape(1, -1),
                      _pad_lanes(ssd_dt_bias[l]), _pad_lanes(ssd_A_log[l]), dvec,
                      ssd_norm_w[l].reshape(1, -1), e01, bg, seq, ssd_rows)
            x1, h2, h2p, logits = _merge(yn, p, x_in, mod3, w_ssd_out[l].astype(BF16),
                                         w_sc_out[l].astype(BF16), w_o[l].astype(BF16),
                                         sc_conv_w[l], ln1_g[l].reshape(1, d),
                                         ln1_b[l].reshape(1, d), wr, bg, seq, tm_merge, b0)
            gates, dest, cnt = _route(logits, bias_col, su, tm_route)
            dest_both = jnp.concatenate([dest, dest + n_rows], axis=1)
            plan = _visit_plan(cnt[:, 0].astype(jnp.int32), n_rows // bm_gmm, bm_gmm)
            xin = _sc_scatter_rows(h2p.reshape(2 * tg, SC_ROW_WORDS), dest_both, 2 * n_rows)
            return dict(x1=x1, h2=h2, gates=gates, dest=dest, plan=plan, xin=xin, tok0=b0 * seq,
                        tg=tg, n_assign=n_assign, n_rows=n_rows)

        def experts(st):
            tg, n_assign, n_rows = st["tg"], st["n_assign"], st["n_rows"]
            yb = _gmm(st["xin"].reshape(2, n_rows, SC_ROW_WORDS), w_gate[l], w_up[l], w_down[l],
                      *st["plan"], bm_gmm)
            idx = st["dest"].reshape(1, n_assign)
            idx = jnp.concatenate([idx, idx + n_rows], axis=1)
            yg = _sc_gather_rows(yb.reshape(2 * n_rows, SC_ROW_WORDS), idx)
            return yg.reshape(2, TOP_K, tg, SC_ROW_WORDS)

        stages = [mix_and_route(b0, bg) for b0, bg in groups]
        gathered = [experts(st) for st in stages]
        out = None
        for st, yg in zip(stages, gathered):
            out = _combine(yg, st["gates"], st["h2"], st["x1"], mod3, wsgu, wsd,
                           ln2_g[l].reshape(1, d), ln2_b[l].reshape(1, d), seq, tm_comb,
                           st["tok0"], t, out)
        xf = out
    return xf.reshape(bsz, seq, d)
```

```python
import jax
import jax.numpy as jnp
import numpy as np
from jax import lax
from jax.experimental import pallas as pl
from jax.experimental.pallas import tpu as pltpu
from jax.experimental.pallas import tpu_sc as plsc

F32 = jnp.float32
BF16 = jnp.bfloat16
U32 = jnp.uint32

LANES = 128
SUBLANES = 8
VMEM_LIMIT_BYTES = 56 * 1024 * 1024

D_MODEL = 1024
SSD_INNER = 2048
SSD_HEAD_DIM = 64
SSD_HEADS = 32
SSD_GROUPS = 8
SSD_STATE = 128
SSD_CONV = 4
SSD_CHUNK = 128
GROUP_W = SSD_INNER // SSD_GROUPS
HEADS_PER_GROUP = SSD_HEADS // SSD_GROUPS
SC_KERNEL = 3
N_EXPERTS = 64
TOP_K = 8
N_EXPERT_GROUPS = 8
TOPK_EXPERT_GROUPS = 4
EXPERTS_PER_GROUP = N_EXPERTS // N_EXPERT_GROUPS
D_EXPERT = 256
ROUTED_SCALE = 2.5
LN_EPS = 1e-5
RMS_EPS = 1e-5
ALPHA = 2.0 ** 0.25
DT_COL0 = 2048 + 4096
NEG_BIG = -1e30
SC_WINDOW = 128
SC_ROW_WORDS = 256
GMM_BLOCK = 1024
GMM_ROWS = 256
MERGE_ROWS = 256


def _cparams(sem):
    return pltpu.CompilerParams(dimension_semantics=sem,
                                vmem_limit_bytes=VMEM_LIMIT_BYTES)


def _ln(x):
    mu = jnp.mean(x, axis=-1, keepdims=True)
    xc = x - mu
    var = jnp.mean(xc * xc, axis=-1, keepdims=True)
    return xc * lax.rsqrt(var + LN_EPS)


def _silu(x):
    h = 0.5 * x
    return h + h * jnp.tanh(h)


def _split3(a):
    hi = a.astype(BF16)
    r1 = a - hi.astype(F32)
    mid = r1.astype(BF16)
    lo = (r1 - mid.astype(F32)).astype(BF16)
    return hi, mid, lo


def _dot(a, b):
    return jnp.dot(a, b, preferred_element_type=F32)


def _dot01_exact(m01, a):
    hi, mid, lo = _split3(a)
    return _dot(m01, hi) + _dot(m01, mid) + _dot(m01, lo)


def _pack_pair(a, b):
    ab = pltpu.bitcast(a.astype(BF16).astype(F32), U32)
    bb = pltpu.bitcast(b.astype(BF16).astype(F32), U32)
    return ab | lax.shift_right_logical(bb, jnp.uint32(16))


def _unpack_pair(w):
    a = pltpu.bitcast(w & jnp.uint32(0xFFFF0000), F32)
    b = pltpu.bitcast(lax.shift_left(w, jnp.uint32(16)), F32)
    return a, b


def _mod_kernel(c_ref, w_ref, b_ref, o_ref):
    a = _silu(c_ref[...])
    o_ref[...] = jnp.dot(a, w_ref[...], precision=lax.Precision.HIGHEST,
                         preferred_element_type=F32) + b_ref[...]


def _mod(c, w, b):
    bsz, d = c.shape
    n = w.shape[1]
    tn = 1024
    return pl.pallas_call(
        _mod_kernel,
        grid=(n // tn,),
        in_specs=[pl.BlockSpec((bsz, d), lambda j: (0, 0)),
                  pl.BlockSpec((d, tn), lambda j: (0, j)),
                  pl.BlockSpec((1, tn), lambda j: (0, j))],
        out_specs=pl.BlockSpec((bsz, tn), lambda j: (0, j)),
        out_shape=jax.ShapeDtypeStruct((bsz, n), F32),
        compiler_params=_cparams(("arbitrary",)),
        name="mod",
    )(c, w, b.reshape(1, n))


def _inproj_kernel(x_ref, mod_ref, w_ref, wdt_ref, p_ref, dt_ref, h_scr):
    j = pl.program_id(1)

    @pl.when(j == 0)
    def _():
        m = mod_ref[0]
        h = _ln(x_ref[...]) * (1.0 + m[1:2, :]) + m[0:1, :]
        hb = h.astype(BF16)
        h_scr[...] = hb
        dt_ref[...] = _dot(hb, wdt_ref[...])

    p_ref[...] = _dot(h_scr[...], w_ref[...]).astype(p_ref.dtype)


def _inproj(x2d, mod3, w_main, w_dt, seq, tm, tn, tok0, t):
    d = x2d.shape[1]
    n = w_main.shape[1]
    i0 = tok0 // tm
    return pl.pallas_call(
        _inproj_kernel,
        grid=(t // tm, n // tn),
        in_specs=[pl.BlockSpec((tm, d), lambda i, j: (i0 + i, 0)),
                  pl.BlockSpec((1, 6, d), lambda i, j: (((i0 + i) * tm) // seq, 0, 0)),
                  pl.BlockSpec((d, tn), lambda i, j: (0, j)),
                  pl.BlockSpec((d, LANES), lambda i, j: (0, 0))],
        out_specs=[pl.BlockSpec((tm, tn), lambda i, j: (i, j)),
                   pl.BlockSpec((tm, LANES), lambda i, j: (i, 0))],
        out_shape=[jax.ShapeDtypeStruct((t, n), BF16),
                   jax.ShapeDtypeStruct((t, LANES), F32)],
        scratch_shapes=[pltpu.VMEM((tm, d), BF16)],
        compiler_params=_cparams(("arbitrary", "arbitrary")),
        name="inproj",
    )(x2d, mod3, w_main, w_dt)


def _shift_conv(cur, prev_scr, s_ref, w_ref, b_ref):
    rows = cur.shape[0]
    n_taps = w_ref.shape[0]
    shifted = _dot(s_ref[...], jnp.concatenate([prev_scr[...], cur], axis=0))
    acc = w_ref[n_taps - 1:n_taps, :] * cur.astype(F32) + b_ref[...]
    for k in range(n_taps - 1):
        acc = acc + w_ref[k:k + 1, :] * shifted[k * rows:(k + 1) * rows]
    prev_scr[...] = cur
    return acc


def _shift_matrix(rows, n_taps):
    s = np.zeros(((n_taps - 1) * rows, 2 * rows), np.float32)
    t = np.arange(rows)
    for k in range(n_taps - 1):
        s[k * rows + t, rows + t - (n_taps - 1) + k] = 1.0
    return jnp.asarray(s, BF16)


def _ssd_kernel(xs_ref, bc_ref, z_ref, dtp_ref, cw_xs_ref, cw_bc_ref, cb_xs_ref,
                cb_bc_ref, dtb_ref, alog_ref, dvec_ref, nw_ref, e_ref, s_ref,
                yn_ref, prev_xs, prev_bc, state):
    @pl.when(pl.program_id(1) == 0)
    def _():
        prev_xs[...] = jnp.zeros(prev_xs.shape, BF16)
        prev_bc[...] = jnp.zeros(prev_bc.shape, BF16)
        state[...] = jnp.zeros(state.shape, F32)

    for cc in range(xs_ref.shape[0] // SSD_CHUNK):
        _ssd_chunk(slice(cc * SSD_CHUNK, (cc + 1) * SSD_CHUNK), xs_ref, bc_ref, z_ref, dtp_ref,
                   cw_xs_ref, cw_bc_ref, cb_xs_ref, cb_bc_ref, dtb_ref, alog_ref, dvec_ref,
                   nw_ref, e_ref, s_ref, yn_ref, prev_xs, prev_bc, state)


def _ssd_chunk(rows, xs_ref, bc_ref, z_ref, dtp_ref, cw_xs_ref, cw_bc_ref, cb_xs_ref,
               cb_bc_ref, dtb_ref, alog_ref, dvec_ref, nw_ref, e_ref, s_ref,
               yn_ref, prev_xs, prev_bc, state):
    L = SSD_CHUNK
    xs = _silu(_shift_conv(xs_ref[rows, :], prev_xs, s_ref, cw_xs_ref, cb_xs_ref))
    bc = _silu(_shift_conv(bc_ref[rows, :], prev_bc, s_ref, cw_bc_ref, cb_bc_ref))

    x_dt = dtp_ref[rows, :] + dtb_ref[...]
    dt = jnp.maximum(x_dt, 0.0) + jnp.log1p(jnp.exp(-jnp.abs(x_dt)))
    a_neg = -jnp.exp(alog_ref[...])
    d_a = dt * a_neg
    row = lax.broadcasted_iota(jnp.int32, (L, L), 0)
    col = lax.broadcasted_iota(jnp.int32, (L, L), 1)
    causal = col <= row
    tri = jnp.where(causal, 1.0, 0.0).astype(BF16)
    hi, mid, lo = _split3(d_a)
    a_cs = _dot(tri, hi) + _dot(tri, mid) + _dot(tri, lo)
    a_cs_t = a_cs.T
    a_last = a_cs[L - 1:L, :]
    stack = jnp.concatenate([dt, jnp.exp(a_cs), jnp.exp(a_last - a_cs)], axis=0)
    s_hi = stack.astype(BF16)
    s_mid = (stack - s_hi.astype(F32)).astype(BF16)
    st_e = _dot(jnp.concatenate([s_hi, s_mid], axis=1), e_ref[...])
    dt_e = st_e[0:L]
    od_e = st_e[L:2 * L]
    ds_e = st_e[2 * L:3 * L]
    cd_e = od_e[L - 1:L, :]

    x_f = xs * dt_e
    x_b = x_f.astype(BF16)
    xd_b = (x_f * ds_e).astype(BF16)
    lane_head = lax.broadcasted_iota(jnp.int32, (L, GROUP_W), 1) // SSD_HEAD_DIM

    for g in range(SSD_GROUPS):
        gc = slice(g * GROUP_W, (g + 1) * GROUP_W)
        b_g = bc[:, g * SSD_STATE:(g + 1) * SSD_STATE]
        c_g = bc[:, SSD_GROUPS * SSD_STATE + g * SSD_STATE:
                 SSD_GROUPS * SSD_STATE + (g + 1) * SSD_STATE].astype(BF16)
        b_gt = b_g.T.astype(BF16)
        cb = _dot(c_g, b_gt)
        x_g = x_b[:, gc]
        lhs = []
        rhs = []
        for r in range(HEADS_PER_GROUP):
            h = g * HEADS_PER_GROUP + r
            seg = a_cs[:, h:h + 1] - a_cs_t[h:h + 1, :]
            lmat = jnp.exp(jnp.where(causal, seg, NEG_BIG))
            lhs.append((cb * lmat).astype(BF16))
            rhs.append(jnp.where(lane_head == r, x_g, jnp.zeros_like(x_g)))
        y_diag = _dot(jnp.concatenate(lhs, axis=1), jnp.concatenate(rhs, axis=0))
        st = state[g]
        y_off = _dot(c_g, st.astype(BF16)) * od_e[:, gc]
        state[g] = cd_e[:, gc] * st + _dot(b_gt, xd_b[:, gc])
        y = y_diag + y_off + dvec_ref[:, gc] * xs[:, gc]
        gt = y * _silu(z_ref[rows, gc].astype(F32))
        ms = jnp.mean(gt * gt, axis=-1, keepdims=True)
        yn_ref[rows, gc] = (gt * lax.rsqrt(ms + RMS_EPS) * nw_ref[:, gc]).astype(yn_ref.dtype)


def _ssd(p, dtp, cw, cb, dtb, alog, dvec, nw, e01, bsz, seq, rows):
    t = p.shape[0]
    L = SSD_CHUNK
    nc = seq // rows
    row_map = lambda b, c: (b * nc + c, 0)

    def pcol(k):
        return pl.BlockSpec((rows, SSD_INNER), lambda b, c: (b * nc + c, k))

    const = lambda b, c: (0, 0)
    return pl.pallas_call(
        _ssd_kernel,
        grid=(bsz, nc),
        in_specs=[pcol(1), pcol(2), pcol(0),
                  pl.BlockSpec((rows, LANES), row_map),
                  pl.BlockSpec((SSD_CONV, SSD_INNER), lambda b, c: (0, 0)),
                  pl.BlockSpec((SSD_CONV, SSD_INNER), lambda b, c: (0, 1)),
                  pl.BlockSpec((1, SSD_INNER), lambda b, c: (0, 0)),
                  pl.BlockSpec((1, SSD_INNER), lambda b, c: (0, 1)),
                  pl.BlockSpec((1, LANES), const),
                  pl.BlockSpec((1, LANES), const),
                  pl.BlockSpec((1, SSD_INNER), const),
                  pl.BlockSpec((1, SSD_INNER), const),
                  pl.BlockSpec((2 * LANES, SSD_INNER), const),
                  pl.BlockSpec(((SSD_CONV - 1) * L, 2 * L), const)],
        out_specs=pl.BlockSpec((rows, SSD_INNER), row_map),
        out_shape=jax.ShapeDtypeStruct((t, SSD_INNER), BF16),
        scratch_shapes=[pltpu.VMEM((L, SSD_INNER), BF16),
                        pltpu.VMEM((L, SSD_INNER), BF16),
                        pltpu.VMEM((SSD_GROUPS, SSD_STATE, GROUP_W), F32)],
        compiler_params=_cparams(("arbitrary", "arbitrary")),
        name="ssd",
    )(p, p, p, dtp, cw, cw, cb, cb, dtb, alog, dvec, nw, e01, _shift_matrix(L, SSD_CONV))


def _merge_kernel(yn_ref, scb_ref, scc_ref, sch_ref, ga_ref, gb_ref, x_ref, mod_ref,
                  wssd_ref, wsc_ref, wo_ref, cw_ref, lng_ref, lnb_ref, wr_ref,
                  x1_ref, h2_ref, h2p_ref, lg_ref, ext):
    s = pl.program_id(1)
    tm = x_ref.shape[0]

    @pl.when(s == 0)
    def _():
        ext[0:SUBLANES, :] = jnp.zeros((SUBLANES, D_MODEL), F32)

    ext[SUBLANES:SUBLANES + tm, :] = scc_ref[...].astype(F32) * sch_ref[...].astype(F32)
    m = mod_ref[0]
    for c in range(tm // MERGE_ROWS):
        r0 = c * MERGE_ROWS
        rows = slice(r0, r0 + MERGE_ROWS)
        u = None
        for k in range(SC_KERNEL):
            start = SUBLANES + r0 - (SC_KERNEL - 1) + k
            term = cw_ref[k:k + 1, :] * ext[start:start + MERGE_ROWS, :]
            u = term if u is None else u + term
        y_b = _dot((scb_ref[rows, :].astype(F32) * u).astype(BF16), wsc_ref[...])
        y_a = _dot(yn_ref[rows, :], wssd_ref[...])
        merged = (jax.nn.sigmoid(ga_ref[rows, :].astype(F32)) * y_a
                  + jax.nn.sigmoid(gb_ref[rows, :].astype(F32)) * y_b)
        mix = _dot(merged.astype(BF16), wo_ref[...])
        x1 = _ln(ALPHA * x_ref[rows, :] + m[2:3, :] * mix) * lng_ref[...] + lnb_ref[...]
        x1_ref[rows, :] = x1
        h2 = _ln(x1) * (1.0 + m[4:5, :]) + m[3:4, :]
        h2_hi = h2.astype(BF16)
        h2_ref[rows, :] = h2_hi
        w = _pack_pair(h2[:, 0:D_MODEL // 2], h2[:, D_MODEL // 2:D_MODEL])
        h2p_ref[0, rows, :] = w[:, 0:SC_ROW_WORDS]
        h2p_ref[1, rows, :] = w[:, SC_ROW_WORDS:2 * SC_ROW_WORDS]
        h2_lo = (h2 - h2_hi.astype(F32)).astype(BF16)
        both = _dot(h2_hi, wr_ref[...])
        lg_ref[rows, :] = (both[:, 0:LANES] + both[:, LANES:2 * LANES]
                           + _dot(h2_lo, wr_ref[:, 0:LANES]))
    ext[0:SUBLANES, :] = ext[tm:tm + SUBLANES, :]


def _merge(yn, p, x2d, mod3, wssd, wsc, wo, cw, lng, lnb, wr, bsz, seq, tm, b0):
    t = bsz * seq
    d = x2d.shape[1]
    ns = seq // tm
    row_map = lambda b, s: (b * ns + s, 0)
    const = lambda b, s: (0, 0)
    col0 = (2048 + 4096) // d

    def pcol(k):
        return pl.BlockSpec((tm, d), lambda b, s: (b * ns + s, col0 + k))

    return pl.pallas_call(
        _merge_kernel,
        grid=(bsz, ns),
        in_specs=[pl.BlockSpec((tm, SSD_INNER), row_map),
                  pcol(0), pcol(1), pcol(2), pcol(3), pcol(4),
                  pl.BlockSpec((tm, d), lambda b, s: ((b0 + b) * ns + s, 0)),
                  pl.BlockSpec((1, 6, d), lambda b, s: (b0 + b, 0, 0)),
                  pl.BlockSpec((SSD_INNER, d), const, pipeline_mode=pl.Buffered(1)),
                  pl.BlockSpec((d, d), const, pipeline_mode=pl.Buffered(1)),
                  pl.BlockSpec((d, d), const, pipeline_mode=pl.Buffered(1)),
                  pl.BlockSpec((SC_KERNEL, d), const),
                  pl.BlockSpec((1, d), const),
                  pl.BlockSpec((1, d), const),
                  pl.BlockSpec((d, 2 * LANES), const, pipeline_mode=pl.Buffered(1))],
        out_specs=[pl.BlockSpec((tm, d), row_map),
                   pl.BlockSpec((tm, d), row_map),
                   pl.BlockSpec((2, tm, SC_ROW_WORDS), lambda b, s: (0, b * ns + s, 0)),
                   pl.BlockSpec((tm, LANES), row_map)],
        out_shape=[jax.ShapeDtypeStruct((t, d), F32),
                   jax.ShapeDtypeStruct((t, d), BF16),
                   jax.ShapeDtypeStruct((2, t, SC_ROW_WORDS), U32),
                   jax.ShapeDtypeStruct((t, LANES), F32)],
        scratch_shapes=[pltpu.VMEM((tm + SUBLANES, d), F32)],
        compiler_params=_cparams(("arbitrary", "arbitrary")),
        name="merge",
    )(yn, p, p, p, p, p, x2d, mod3, wssd, wsc, wo, cw, lng, lnb, wr)


def _first_argmax(v, idx, n):
    m = jnp.max(v, axis=0, keepdims=True)
    first = jnp.min(jnp.where(v == m, idx, n), axis=0, keepdims=True)
    return first, idx == first


def _route_select_kernel(lg_ref, bias_ref, sel_ref, ek_ref, gr_ref, cnt_ref):
    i = pl.program_id(0)
    tm = lg_ref.shape[0]
    ne, ng, eg = N_EXPERTS, N_EXPERT_GROUPS, EXPERTS_PER_GROUP
    lt = lg_ref[...].T[0:ne, :]
    scores = jax.nn.sigmoid(lt)
    biased = scores + bias_ref[...][0:ne, 0:1]
    grp = biased.reshape(ng, eg, tm)
    idx_e = lax.broadcasted_iota(jnp.int32, (ng, eg, tm), 1)
    m1 = jnp.max(grp, axis=1, keepdims=True)
    first = jnp.min(jnp.where(grp == m1, idx_e, eg), axis=1, keepdims=True)
    m2 = jnp.max(jnp.where(idx_e == first, NEG_BIG, grp), axis=1, keepdims=True)
    gscore = m1 + m2
    idx_g = lax.broadcasted_iota(jnp.int32, (ng, 1, tm), 0)
    gsel = jnp.zeros((ng, 1, tm), F32)
    for _ in range(TOPK_EXPERT_GROUPS):
        _f, hit = _first_argmax(gscore, idx_g, ng)
        gsel = jnp.where(hit, 1.0, gsel)
        gscore = jnp.where(hit, NEG_BIG, gscore)
    emask = jnp.broadcast_to(gsel, (ng, eg, tm)).reshape(ne, tm)
    cand = jnp.where(emask > 0.0, biased, NEG_BIG)
    idx_x = lax.broadcasted_iota(jnp.int32, (ne, tm), 0)
    sel = jnp.zeros((ne, tm), F32)
    firsts = []
    gate_rows = []
    for _ in range(TOP_K):
        f, hit = _first_argmax(cand, idx_x, ne)
        sel = jnp.where(hit, 1.0, sel)
        cand = jnp.where(hit, NEG_BIG, cand)
        firsts.append(f)
        gate_rows.append(jnp.sum(jnp.where(hit, scores, 0.0), axis=0, keepdims=True))
    sel_ref[...] = sel.astype(BF16)
    ek_ref[...] = jnp.concatenate(firsts, axis=0)
    gr_ref[...] = jnp.concatenate(gate_rows, axis=0)

    @pl.when(i == 0)
    def _():
        cnt_ref[...] = jnp.zeros(cnt_ref.shape, F32)

    cnt_ref[...] += jnp.broadcast_to(jnp.sum(sel, axis=1, keepdims=True), (ne, LANES))


def _route_slots_kernel(sel_ref, ek_ref, gr_ref, cnt_ref, su_ref, g_ref, dest_ref, carry, start):
    i = pl.program_id(0)
    tm = sel_ref.shape[1]
    ne = N_EXPERTS

    @pl.when(i == 0)
    def _():
        r = lax.broadcasted_iota(jnp.int32, (ne, ne), 0)
        cc = lax.broadcasted_iota(jnp.int32, (ne, ne), 1)
        below = jnp.where(cc < r, 1.0, 0.0).astype(BF16)
        padded = jnp.floor((cnt_ref[...] + (GMM_BLOCK - 1.0)) * (1.0 / GMM_BLOCK)) * GMM_BLOCK
        start[...] = _dot01_exact(below, padded)
        carry[...] = jnp.zeros(carry.shape, F32)

    sel = sel_ref[...]
    before = _dot(sel, su_ref[...])
    slot = before + carry[:, 0:1] + start[:, 0:1]
    idx_x = lax.broadcasted_iota(jnp.int32, (ne, tm), 0)
    ek = ek_ref[...]
    dest_rows = []
    for k in range(TOP_K):
        hit = idx_x == ek[k:k + 1, :]
        dest_rows.append(jnp.sum(jnp.where(hit, slot, 0.0), axis=0, keepdims=True))
    dest_ref[...] = jnp.concatenate(dest_rows, axis=0).astype(jnp.int32)
    gate = gr_ref[...]
    gate = gate / jnp.sum(gate, axis=0, keepdims=True) * ROUTED_SCALE
    pad = jnp.zeros((LANES - TOP_K, tm), F32)
    g_ref[...] = jnp.concatenate([gate, pad], axis=0).T
    carry[...] += jnp.broadcast_to(jnp.sum(sel.astype(F32), axis=1, keepdims=True), (ne, LANES))


def _route(logits, bias_col, su, tm):
    t = logits.shape[0]
    sel, ek, gr, cnt = pl.pallas_call(
        _route_select_kernel,
        grid=(t // tm,),
        in_specs=[pl.BlockSpec((tm, LANES), lambda i: (i, 0)),
                  pl.BlockSpec((LANES, LANES), lambda i: (0, 0))],
        out_specs=[pl.BlockSpec((N_EXPERTS, tm), lambda i: (0, i)),
                   pl.BlockSpec((TOP_K, tm), lambda i: (0, i)),
                   pl.BlockSpec((TOP_K, tm), lambda i: (0, i)),
                   pl.BlockSpec((N_EXPERTS, LANES), lambda i: (0, 0))],
        out_shape=[jax.ShapeDtypeStruct((N_EXPERTS, t), BF16),
                   jax.ShapeDtypeStruct((TOP_K, t), jnp.int32),
                   jax.ShapeDtypeStruct((TOP_K, t), F32),
                   jax.ShapeDtypeStruct((N_EXPERTS, LANES), F32)],
        compiler_params=_cparams(("arbitrary",)),
        name="route_select",
    )(logits, bias_col)
    gates, dest = pl.pallas_call(
        _route_slots_kernel,
        grid=(t // tm,),
        in_specs=[pl.BlockSpec((N_EXPERTS, tm), lambda i: (0, i)),
                  pl.BlockSpec((TOP_K, tm), lambda i: (0, i)),
                  pl.BlockSpec((TOP_K, tm), lambda i: (0, i)),
                  pl.BlockSpec((N_EXPERTS, LANES), lambda i: (0, 0)),
                  pl.BlockSpec((tm, tm), lambda i: (0, 0))],
        out_specs=[pl.BlockSpec((tm, LANES), lambda i: (i, 0)),
                   pl.BlockSpec((TOP_K, tm), lambda i: (0, i))],
        out_shape=[jax.ShapeDtypeStruct((t, LANES), F32),
                   jax.ShapeDtypeStruct((TOP_K, t), jnp.int32)],
        scratch_shapes=[pltpu.VMEM((N_EXPERTS, LANES), F32),
                        pltpu.VMEM((N_EXPERTS, LANES), F32)],
        compiler_params=_cparams(("arbitrary",)),
        name="route_slots",
    )(sel, ek, gr, cnt, su)
    return gates, dest, cnt


def _sc_mesh():
    return plsc.VectorSubcoreMesh(core_axis_name="c", subcore_axis_name="s")


def _sc_scatter_rows(rows, idx, n_out):
    n_rows, w = rows.shape
    n_k = idx.shape[0]

    @pl.kernel(out_type=jax.ShapeDtypeStruct((n_out, w), rows.dtype), mesh=_sc_mesh(),
               scratch_types=[])
    def scatter(x_hbm, i_hbm, o_hbm):
        def body(x_vmem, i_vmem):
            for k in range(n_k):
                pltpu.sync_copy(x_vmem, o_hbm.at[i_vmem.at[k]])

        pltpu.emit_pipeline(
            body,
            grid=(n_rows // SC_WINDOW,),
            in_specs=[pl.BlockSpec((SC_WINDOW, w), index_map=lambda i: (i, 0)),
                      pl.BlockSpec((n_k, SC_WINDOW), index_map=lambda i: (0, i))],
            out_specs=[],
            core_axis_name=("c", "s"),
            dimension_semantics=(pltpu.PARALLEL,),
        )(x_hbm, i_hbm)

    return scatter(rows, idx)


def _sc_gather_rows(rows, idx):
    n = idx.shape[1]
    w = rows.shape[1]

    @pl.kernel(out_type=jax.ShapeDtypeStruct((n, w), rows.dtype), mesh=_sc_mesh(),
               scratch_types=[])
    def gather(x_hbm, i_hbm, o_hbm):
        def body(i_vmem, o_vmem):
            pltpu.sync_copy(x_hbm.at[i_vmem.at[0]], o_vmem)

        pltpu.emit_pipeline(
            body,
            grid=(n // SC_WINDOW,),
            in_specs=[pl.BlockSpec((1, SC_WINDOW), index_map=lambda i: (0, i))],
            out_specs=[pl.BlockSpec((SC_WINDOW, w), index_map=lambda i: (i, 0))],
            core_axis_name=("c", "s"),
            dimension_semantics=(pltpu.PARALLEL,),
        )(i_hbm, o_hbm)

    return gather(rows, idx)


def _gmm_kernel(vb_ref, ve_ref, vvalid_ref, vnew_ref, x_ref, wg_ref, wu_ref, wdn_ref,
                o_ref, wgu_ref, wd_ref):
    v = pl.program_id(0)
    valid = vvalid_ref[v]
    bm = x_ref.shape[1]
    half = D_MODEL // 2

    @pl.when(vnew_ref[v] == 1)
    def _():
        wgu_ref[:, 0:D_EXPERT] = wg_ref[0].astype(BF16)
        wgu_ref[:, D_EXPERT:2 * D_EXPERT] = wu_ref[0].astype(BF16)
        wd_ref[...] = wdn_ref[0].astype(BF16)

    @pl.when(valid > 0)
    def _():
        for c in range(bm // GMM_ROWS):
            rows = slice(c * GMM_ROWS, (c + 1) * GMM_ROWS)
            a0, b0 = _unpack_pair(x_ref[0, rows, :])
            a1, b1 = _unpack_pair(x_ref[1, rows, :])
            x = jnp.concatenate([a0.astype(BF16), a1.astype(BF16),
                                 b0.astype(BF16), b1.astype(BF16)], axis=1)
            a = _dot(x, wgu_ref[...])
            act = _silu(a[:, 0:D_EXPERT]) * a[:, D_EXPERT:2 * D_EXPERT]
            r = lax.broadcasted_iota(jnp.int32, (GMM_ROWS, D_EXPERT), 0) + c * GMM_ROWS
            act = jnp.where(r < valid, act, 0.0)
            y = _dot(act.astype(BF16), wd_ref[...])
            w = _pack_pair(y[:, 0:half], y[:, half:D_MODEL])
            o_ref[0, rows, :] = w[:, 0:SC_ROW_WORDS]
            o_ref[1, rows, :] = w[:, SC_ROW_WORDS:half]


def _gmm(xin, w_gate, w_up, w_down, vb, ve, vvalid, vnew, bm):
    _, n_rows, w = xin.shape
    d = D_MODEL
    n_visits = vb.shape[0]
    blk = lambda v, vb, ve, vvalid, vnew: (0, vb[v], 0)
    wsel = lambda v, vb, ve, vvalid, vnew: (ve[v], 0, 0)
    grid_spec = pltpu.PrefetchScalarGridSpec(
        num_scalar_prefetch=4,
        grid=(n_visits,),
        in_specs=[pl.BlockSpec((2, bm, w), blk),
                  pl.BlockSpec((1, d, D_EXPERT), wsel),
                  pl.BlockSpec((1, d, D_EXPERT), wsel),
                  pl.BlockSpec((1, D_EXPERT, d), wsel)],
        out_specs=pl.BlockSpec((2, bm, w), blk),
        scratch_shapes=[pltpu.VMEM((d, 2 * D_EXPERT), BF16),
                        pltpu.VMEM((D_EXPERT, d), BF16)],
    )
    return pl.pallas_call(
        _gmm_kernel,
        grid_spec=grid_spec,
        out_shape=jax.ShapeDtypeStruct((2, n_rows, w), U32),
        compiler_params=_cparams(("arbitrary",)),
        name="gmm",
    )(vb, ve, vvalid, vnew, xin, w_gate, w_up, w_down)


def _visit_plan(counts, n_blocks, bm):
    ne = counts.shape[0]
    nb = (counts + bm - 1) // bm
    vend = jnp.cumsum(nb)
    vstart = vend - nb
    total = vend[-1]
    v = jnp.arange(n_blocks, dtype=jnp.int32)
    vc = jnp.minimum(v, total - 1)
    e_v = jnp.minimum(jnp.sum((vend[None, :] <= vc[:, None]).astype(jnp.int32), axis=1), ne - 1)
    onehot = (e_v[:, None] == jnp.arange(ne, dtype=jnp.int32)[None, :]).astype(jnp.int32)
    pick = lambda tab: jnp.sum(onehot * tab[None, :].astype(jnp.int32), axis=1)
    valid = jnp.clip(pick(counts) - (vc - pick(vstart)) * bm, 0, bm)
    valid = jnp.where(v < total, valid, 0)
    i32 = lambda a: a.astype(jnp.int32)
    e_v = i32(e_v)
    new = jnp.concatenate([jnp.ones((1,), jnp.int32), i32(e_v[1:] != e_v[:-1])])
    return i32(vc), e_v, i32(valid), new


def _combine_kernel(yg_ref, g_ref, h_ref, x1_ref, mod_ref, wsgu_ref, wsd_ref, lng_ref,
                    lnb_ref, *rest):
    o_ref = rest[-1]
    q = D_MODEL // 4
    g = g_ref[...]
    a = _dot(h_ref[...], wsgu_ref[...])
    act = _silu(a[:, 0:D_EXPERT]) * a[:, D_EXPERT:2 * D_EXPERT]
    shared = _dot(act.astype(BF16), wsd_ref[...])
    f = [shared[:, j * q:(j + 1) * q] for j in range(4)]
    for k in range(TOP_K):
        a0, b0 = _unpack_pair(yg_ref[0, k])
        a1, b1 = _unpack_pair(yg_ref[1, k])
        gk = g[:, k:k + 1]
        f = [f[0] + gk * a0, f[1] + gk * a1, f[2] + gk * b0, f[3] + gk * b1]
    ffn = jnp.concatenate(f, axis=1)
    m = mod_ref[0]
    r = ALPHA * x1_ref[...] + m[5:6, :] * ffn
    o_ref[...] = _ln(r) * lng_ref[...] + lnb_ref[...]


def _combine(yg, gates, h2, x1, mod3, wsgu, wsd, lng, lnb, seq, tm, tok0, t, prev_out):
    tp, d = x1.shape
    b0 = tok0 // tm
    row = lambda i: (i, 0)
    const = lambda i: (0, 0)
    in_specs = [pl.BlockSpec((2, TOP_K, tm, SC_ROW_WORDS), lambda i: (0, 0, i, 0)),
                pl.BlockSpec((tm, LANES), row),
                pl.BlockSpec((tm, d), row),
                pl.BlockSpec((tm, d), row),
                pl.BlockSpec((1, 6, d), lambda i: (((b0 + i) * tm) // seq, 0, 0)),
                pl.BlockSpec((d, 2 * D_EXPERT), const),
                pl.BlockSpec((D_EXPERT, d), const),
                pl.BlockSpec((1, d), const),
                pl.BlockSpec((1, d), const)]
    args = [yg, gates, h2, x1, mod3, wsgu, wsd, lng, lnb]
    aliases = {}
    if prev_out is not None:
        in_specs.append(pl.BlockSpec(memory_space=pl.ANY))
        args.append(prev_out)
        aliases = {len(args) - 1: 0}
    return pl.pallas_call(
        _combine_kernel,
        grid=(tp // tm,),
        in_specs=in_specs,
        out_specs=pl.BlockSpec((tm, d), lambda i: (b0 + i, 0)),
        out_shape=jax.ShapeDtypeStruct((t, d), F32),
        input_output_aliases=aliases,
        compiler_params=_cparams(("arbitrary",)),
        name="combine",
    )(*args)


def _pad_lanes(v):
    return jnp.zeros((1, LANES), F32).at[0, :v.shape[0]].set(v.astype(F32))


def kernel(x, c, w_ada, b_ada, w_in, ssd_conv_w, ssd_conv_b, ssd_dt_bias, ssd_A_log, ssd_D,
           ssd_norm_w, w_ssd_out, sc_conv_w, w_sc_out, w_o, ln1_g, ln1_b, router_w,
           router_bias, w_gate, w_up, w_down, sh_gate, sh_up, sh_down, ln2_g, ln2_b):
    bsz, seq, d = x.shape
    t = bsz * seq
    depth = w_in.shape[0]
    ssd_rows = 4 * SSD_CHUNK
    assert d == D_MODEL and seq % ssd_rows == 0, (x.shape,)
    tm_proj = min(2048, seq)
    tm_merge = min(512, seq)
    tm_route = min(1024, seq)
    tm_comb = min(512, seq)
    bm_gmm = GMM_BLOCK
    first = (6 * bsz) // 8 if bsz >= 2 else bsz
    groups = [(0, first)] + ([(first, bsz - first)] if bsz > first else [])

    e01 = (np.arange(LANES)[:, None] == (np.arange(SSD_INNER)[None, :] // SSD_HEAD_DIM))
    e01 = jnp.asarray(np.concatenate([e01, e01], axis=0), BF16)
    su = jnp.asarray(np.arange(tm_route)[:, None] < np.arange(tm_route)[None, :], BF16)

    xf = x.reshape(t, d)
    for l in range(depth):
        w_l = w_in[l]
        w_main = jnp.concatenate([w_l[:, :DT_COL0], w_l[:, DT_COL0 + SSD_HEADS:]],
                                 axis=1).astype(BF16)
        w_dt = jnp.pad(w_l[:, DT_COL0:DT_COL0 + SSD_HEADS],
                       ((0, 0), (0, LANES - SSD_HEADS))).astype(BF16)
        wsgu = jnp.concatenate([sh_gate[l], sh_up[l]], axis=-1).astype(BF16)
        wsd = sh_down[l].astype(BF16)
        wr = jnp.pad(router_w[l], ((0, 0), (0, LANES - N_EXPERTS)))
        wr_hi = wr.astype(BF16)
        wr = jnp.concatenate([wr_hi, (wr - wr_hi.astype(F32)).astype(BF16)], axis=1)
        bias_col = jnp.zeros((LANES, LANES), F32).at[:N_EXPERTS, 0].set(router_bias[l])
        dvec = jnp.repeat(ssd_D[l], SSD_HEAD_DIM).reshape(1, SSD_INNER)

        mod3 = _mod(c, w_ada[l], b_ada[l]).reshape(bsz, 6, d)
        x_in = xf

        def mix_and_route(b0, bg):
            tg = bg * seq
            n_assign = tg * TOP_K
            n_rows = n_assign + N_EXPERTS * bm_gmm
            p, dtp = _inproj(x_in, mod3, w_main, w_dt, seq, tm_proj, w_main.shape[1] // 8,
                             b0 * seq, tg)
            yn = _ssd(p, dtp, ssd_conv_w[l], ssd_conv_b[l].reshape(1, -1),
                      _pad_lanes(ssd_dt_bias[l]), _pad_lanes(ssd_A_log[l]), dvec,
                      ssd_norm_w[l].reshape(1, -1), e01, bg, seq, ssd_rows)
            x1, h2, h2p, logits = _merge(yn, p, x_in, mod3, w_ssd_out[l].astype(BF16),
                                         w_sc_out[l].astype(BF16), w_o[l].astype(BF16),
                                         sc_conv_w[l], ln1_g[l].reshape(1, d),
                                         ln1_b[l].reshape(1, d), wr, bg, seq, tm_merge, b0)
            gates, dest, cnt = _route(logits, bias_col, su, tm_route)
            dest_both = jnp.concatenate([dest, dest + n_rows], axis=1)
            plan = _visit_plan(cnt[:, 0].astype(jnp.int32), n_rows // bm_gmm, bm_gmm)
            xin = _sc_scatter_rows(h2p.reshape(2 * tg, SC_ROW_WORDS), dest_both, 2 * n_rows)
            return dict(x1=x1, h2=h2, gates=gates, dest=dest, plan=plan, xin=xin, tok0=b0 * seq,
                        tg=tg, n_assign=n_assign, n_rows=n_rows)

        def experts(st):
            tg, n_assign, n_rows = st["tg"], st["n_assign"], st["n_rows"]
            yb = _gmm(st["xin"].reshape(2, n_rows, SC_ROW_WORDS), w_gate[l], w_up[l], w_down[l],
                      *st["plan"], bm_gmm)
            idx = st["dest"].reshape(1, n_assign)
            idx = jnp.concatenate([idx, idx + n_rows], axis=1)
            yg = _sc_gather_rows(yb.reshape(2 * n_rows, SC_ROW_WORDS), idx)
            return yg.reshape(2, TOP_K, tg, SC_ROW_WORDS)

        stages = [mix_and_route(b0, bg) for b0, bg in groups]
        gathered = [experts(st) for st in stages]
        out = None
        for st, yg in zip(stages, gathered):
            out = _combine(yg, st["gates"], st["h2"], st["x1"], mod3, wsgu, wsd,
                           ln2_g[l].reshape(1, d), ln2_b[l].reshape(1, d), seq, tm_comb,
                           st["tok0"], t, out)
        xf = out
    return xf.reshape(bsz, seq, d)
```

```python
import jax
import jax.numpy as jnp
import numpy as np
from jax import lax
from jax.experimental import pallas as pl
from jax.experimental.pallas import tpu as pltpu
from jax.experimental.pallas import tpu_sc as plsc

F32 = jnp.float32
BF16 = jnp.bfloat16
U32 = jnp.uint32

LANES = 128
SUBLANES = 8
VMEM_LIMIT_BYTES = 56 * 1024 * 1024

D_MODEL = 1024
SSD_INNER = 2048
SSD_HEAD_DIM = 64
SSD_HEADS = 32
SSD_GROUPS = 8
SSD_STATE = 128
SSD_CONV = 4
SSD_CHUNK = 128
GROUP_W = SSD_INNER // SSD_GROUPS
HEADS_PER_GROUP = SSD_HEADS // SSD_GROUPS
SC_KERNEL = 3
N_EXPERTS = 64
TOP_K = 8
N_EXPERT_GROUPS = 8
TOPK_EXPERT_GROUPS = 4
EXPERTS_PER_GROUP = N_EXPERTS // N_EXPERT_GROUPS
D_EXPERT = 256
ROUTED_SCALE = 2.5
LN_EPS = 1e-5
RMS_EPS = 1e-5
ALPHA = 2.0 ** 0.25
DT_COL0 = 2048 + 4096
NEG_BIG = -1e30
SC_WINDOW = 128
SC_ROW_WORDS = 256
GMM_BLOCK = 1024
GMM_ROWS = 512
MERGE_ROWS = 256


def _cparams(sem):
    return pltpu.CompilerParams(dimension_semantics=sem,
                                vmem_limit_bytes=VMEM_LIMIT_BYTES)


def _ln(x):
    mu = jnp.mean(x, axis=-1, keepdims=True)
    xc = x - mu
    var = jnp.mean(xc * xc, axis=-1, keepdims=True)
    return xc * lax.rsqrt(var + LN_EPS)


def _silu(x):
    h = 0.5 * x
    return h + h * jnp.tanh(h)


def _split3(a):
    hi = a.astype(BF16)
    r1 = a - hi.astype(F32)
    mid = r1.astype(BF16)
    lo = (r1 - mid.astype(F32)).astype(BF16)
    return hi, mid, lo


def _dot(a, b):
    return jnp.dot(a, b, preferred_element_type=F32)


def _dot01_exact(m01, a):
    hi, mid, lo = _split3(a)
    return _dot(m01, hi) + _dot(m01, mid) + _dot(m01, lo)


def _pack_pair(a, b):
    ab = pltpu.bitcast(a.astype(BF16).astype(F32), U32)
    bb = pltpu.bitcast(b.astype(BF16).astype(F32), U32)
    return ab | lax.shift_right_logical(bb, jnp.uint32(16))


def _unpack_pair(w):
    a = pltpu.bitcast(w & jnp.uint32(0xFFFF0000), F32)
    b = pltpu.bitcast(lax.shift_left(w, jnp.uint32(16)), F32)
    return a, b


def _mod_kernel(c_ref, w_ref, b_ref, o_ref):
    a = _silu(c_ref[...])
    o_ref[...] = jnp.dot(a, w_ref[...], precision=lax.Precision.HIGHEST,
                         preferred_element_type=F32) + b_ref[...]


def _mod(c, w, b):
    bsz, d = c.shape
    n = w.shape[1]
    tn = 1024
    return pl.pallas_call(
        _mod_kernel,
        grid=(n // tn,),
        in_specs=[pl.BlockSpec((bsz, d), lambda j: (0, 0)),
                  pl.BlockSpec((d, tn), lambda j: (0, j)),
                  pl.BlockSpec((1, tn), lambda j: (0, j))],
        out_specs=pl.BlockSpec((bsz, tn), lambda j: (0, j)),
        out_shape=jax.ShapeDtypeStruct((bsz, n), F32),
        compiler_params=_cparams(("arbitrary",)),
        name="mod",
    )(c, w, b.reshape(1, n))


def _inproj_kernel(x_ref, mod_ref, w_ref, wdt_ref, p_ref, dt_ref, h_scr):
    j = pl.program_id(1)

    @pl.when(j == 0)
    def _():
        m = mod_ref[0]
        h = _ln(x_ref[...]) * (1.0 + m[1:2, :]) + m[0:1, :]
        hb = h.astype(BF16)
        h_scr[...] = hb
        dt_ref[...] = _dot(hb, wdt_ref[...])

    p_ref[...] = _dot(h_scr[...], w_ref[...]).astype(p_ref.dtype)


def _inproj(x2d, mod3, w_main, w_dt, seq, tm, tn, tok0, t):
    d = x2d.shape[1]
    n = w_main.shape[1]
    i0 = tok0 // tm
    return pl.pallas_call(
        _inproj_kernel,
        grid=(t // tm, n // tn),
        in_specs=[pl.BlockSpec((tm, d), lambda i, j: (i0 + i, 0)),
                  pl.BlockSpec((1, 6, d), lambda i, j: (((i0 + i) * tm) // seq, 0, 0)),
                  pl.BlockSpec((d, tn), lambda i, j: (0, j)),
                  pl.BlockSpec((d, LANES), lambda i, j: (0, 0))],
        out_specs=[pl.BlockSpec((tm, tn), lambda i, j: (i, j)),
                   pl.BlockSpec((tm, LANES), lambda i, j: (i, 0))],
        out_shape=[jax.ShapeDtypeStruct((t, n), BF16),
                   jax.ShapeDtypeStruct((t, LANES), F32)],
        scratch_shapes=[pltpu.VMEM((tm, d), BF16)],
        compiler_params=_cparams(("arbitrary", "arbitrary")),
        name="inproj",
    )(x2d, mod3, w_main, w_dt)


def _shift_conv(cur, prev_scr, s_ref, w_ref, b_ref):
    rows = cur.shape[0]
    n_taps = w_ref.shape[0]
    shifted = _dot(s_ref[...], jnp.concatenate([prev_scr[...], cur], axis=0))
    acc = w_ref[n_taps - 1:n_taps, :] * cur.astype(F32) + b_ref[...]
    for k in range(n_taps - 1):
        acc = acc + w_ref[k:k + 1, :] * shifted[k * rows:(k + 1) * rows]
    prev_scr[...] = cur
    return acc


def _shift_matrix(rows, n_taps):
    s = np.zeros(((n_taps - 1) * rows, 2 * rows), np.float32)
    t = np.arange(rows)
    for k in range(n_taps - 1):
        s[k * rows + t, rows + t - (n_taps - 1) + k] = 1.0
    return jnp.asarray(s, BF16)


def _ssd_kernel(xs_ref, bc_ref, z_ref, dtp_ref, cw_xs_ref, cw_bc_ref, cb_xs_ref,
                cb_bc_ref, dtb_ref, alog_ref, dvec_ref, nw_ref, e_ref, s_ref,
                yn_ref, prev_xs, prev_bc, state):
    @pl.when(pl.program_id(1) == 0)
    def _():
        prev_xs[...] = jnp.zeros(prev_xs.shape, BF16)
        prev_bc[...] = jnp.zeros(prev_bc.shape, BF16)
        state[...] = jnp.zeros(state.shape, F32)

    for cc in range(xs_ref.shape[0] // SSD_CHUNK):
        _ssd_chunk(slice(cc * SSD_CHUNK, (cc + 1) * SSD_CHUNK), xs_ref, bc_ref, z_ref, dtp_ref,
                   cw_xs_ref, cw_bc_ref, cb_xs_ref, cb_bc_ref, dtb_ref, alog_ref, dvec_ref,
                   nw_ref, e_ref, s_ref, yn_ref, prev_xs, prev_bc, state)


def _ssd_chunk(rows, xs_ref, bc_ref, z_ref, dtp_ref, cw_xs_ref, cw_bc_ref, cb_xs_ref,
               cb_bc_ref, dtb_ref, alog_ref, dvec_ref, nw_ref, e_ref, s_ref,
               yn_ref, prev_xs, prev_bc, state):
    L = SSD_CHUNK
    xs = _silu(_shift_conv(xs_ref[rows, :], prev_xs, s_ref, cw_xs_ref, cb_xs_ref))
    bc = _silu(_shift_conv(bc_ref[rows, :], prev_bc, s_ref, cw_bc_ref, cb_bc_ref))

    x_dt = dtp_ref[rows, :] + dtb_ref[...]
    dt = jnp.maximum(x_dt, 0.0) + jnp.log1p(jnp.exp(-jnp.abs(x_dt)))
    a_neg = -jnp.exp(alog_ref[...])
    d_a = dt * a_neg
    row = lax.broadcasted_iota(jnp.int32, (L, L), 0)
    col = lax.broadcasted_iota(jnp.int32, (L, L), 1)
    causal = col <= row
    tri = jnp.where(causal, 1.0, 0.0).astype(BF16)
    hi, mid, lo = _split3(d_a)
    a_cs = _dot(tri, hi) + _dot(tri, mid) + _dot(tri, lo)
    a_cs_t = a_cs.T
    a_last = a_cs[L - 1:L, :]
    stack = jnp.concatenate([dt, jnp.exp(a_cs), jnp.exp(a_last - a_cs)], axis=0)
    s_hi = stack.astype(BF16)
    s_mid = (stack - s_hi.astype(F32)).astype(BF16)
    st_e = _dot(jnp.concatenate([s_hi, s_mid], axis=1), e_ref[...])
    dt_e = st_e[0:L]
    od_e = st_e[L:2 * L]
    ds_e = st_e[2 * L:3 * L]
    cd_e = od_e[L - 1:L, :]

    x_f = xs * dt_e
    x_b = x_f.astype(BF16)
    xd_b = (x_f * ds_e).astype(BF16)
    lane_head = lax.broadcasted_iota(jnp.int32, (L, GROUP_W), 1) // SSD_HEAD_DIM

    for g in range(SSD_GROUPS):
        gc = slice(g * GROUP_W, (g + 1) * GROUP_W)
        b_g = bc[:, g * SSD_STATE:(g + 1) * SSD_STATE]
        c_g = bc[:, SSD_GROUPS * SSD_STATE + g * SSD_STATE:
                 SSD_GROUPS * SSD_STATE + (g + 1) * SSD_STATE].astype(BF16)
        b_gt = b_g.T.astype(BF16)
        cb = _dot(c_g, b_gt)
        x_g = x_b[:, gc]
        lhs = []
        rhs = []
        for r in range(HEADS_PER_GROUP):
            h = g * HEADS_PER_GROUP + r
            seg = a_cs[:, h:h + 1] - a_cs_t[h:h + 1, :]
            lmat = jnp.exp(jnp.where(causal, seg, NEG_BIG))
            lhs.append((cb * lmat).astype(BF16))
            rhs.append(jnp.where(lane_head == r, x_g, jnp.zeros_like(x_g)))
        y_diag = _dot(jnp.concatenate(lhs, axis=1), jnp.concatenate(rhs, axis=0))
        st = state[g]
        y_off = _dot(c_g, st.astype(BF16)) * od_e[:, gc]
        state[g] = cd_e[:, gc] * st + _dot(b_gt, xd_b[:, gc])
        y = y_diag + y_off + dvec_ref[:, gc] * xs[:, gc]
        gt = y * _silu(z_ref[rows, gc].astype(F32))
        ms = jnp.mean(gt * gt, axis=-1, keepdims=True)
        yn_ref[rows, gc] = (gt * lax.rsqrt(ms + RMS_EPS) * nw_ref[:, gc]).astype(yn_ref.dtype)


def _ssd(p, dtp, cw, cb, dtb, alog, dvec, nw, e01, bsz, seq, rows):
    t = p.shape[0]
    L = SSD_CHUNK
    nc = seq // rows
    row_map = lambda b, c: (b * nc + c, 0)

    def pcol(k):
        return pl.BlockSpec((rows, SSD_INNER), lambda b, c: (b * nc + c, k))

    const = lambda b, c: (0, 0)
    return pl.pallas_call(
        _ssd_kernel,
        grid=(bsz, nc),
        in_specs=[pcol(1), pcol(2), pcol(0),
                  pl.BlockSpec((rows, LANES), row_map),
                  pl.BlockSpec((SSD_CONV, SSD_INNER), lambda b, c: (0, 0)),
                  pl.BlockSpec((SSD_CONV, SSD_INNER), lambda b, c: (0, 1)),
                  pl.BlockSpec((1, SSD_INNER), lambda b, c: (0, 0)),
                  pl.BlockSpec((1, SSD_INNER), lambda b, c: (0, 1)),
                  pl.BlockSpec((1, LANES), const),
                  pl.BlockSpec((1, LANES), const),
                  pl.BlockSpec((1, SSD_INNER), const),
                  pl.BlockSpec((1, SSD_INNER), const),
                  pl.BlockSpec((2 * LANES, SSD_INNER), const),
                  pl.BlockSpec(((SSD_CONV - 1) * L, 2 * L), const)],
        out_specs=pl.BlockSpec((rows, SSD_INNER), row_map),
        out_shape=jax.ShapeDtypeStruct((t, SSD_INNER), BF16),
        scratch_shapes=[pltpu.VMEM((L, SSD_INNER), BF16),
                        pltpu.VMEM((L, SSD_INNER), BF16),
                        pltpu.VMEM((SSD_GROUPS, SSD_STATE, GROUP_W), F32)],
        compiler_params=_cparams(("arbitrary", "arbitrary")),
        name="ssd",
    )(p, p, p, dtp, cw, cw, cb, cb, dtb, alog, dvec, nw, e01, _shift_matrix(L, SSD_CONV))


def _merge_kernel(yn_ref, scb_ref, scc_ref, sch_ref, ga_ref, gb_ref, x_ref, mod_ref,
                  wssd_ref, wsc_ref, wo_ref, cw_ref, lng_ref, lnb_ref, wr_ref,
                  x1_ref, h2_ref, h2p_ref, lg_ref, ext):
    s = pl.program_id(1)
    tm = x_ref.shape[0]

    @pl.when(s == 0)
    def _():
        ext[0:SUBLANES, :] = jnp.zeros((SUBLANES, D_MODEL), F32)

    ext[SUBLANES:SUBLANES + tm, :] = scc_ref[...].astype(F32) * sch_ref[...].astype(F32)
    m = mod_ref[0]
    for c in range(tm // MERGE_ROWS):
        r0 = c * MERGE_ROWS
        rows = slice(r0, r0 + MERGE_ROWS)
        u = None
        for k in range(SC_KERNEL):
            start = SUBLANES + r0 - (SC_KERNEL - 1) + k
            term = cw_ref[k:k + 1, :] * ext[start:start + MERGE_ROWS, :]
            u = term if u is None else u + term
        y_b = _dot((scb_ref[rows, :].astype(F32) * u).astype(BF16), wsc_ref[...])
        y_a = _dot(yn_ref[rows, :], wssd_ref[...])
        merged = (jax.nn.sigmoid(ga_ref[rows, :].astype(F32)) * y_a
                  + jax.nn.sigmoid(gb_ref[rows, :].astype(F32)) * y_b)
        mix = _dot(merged.astype(BF16), wo_ref[...])
        x1 = _ln(ALPHA * x_ref[rows, :] + m[2:3, :] * mix) * lng_ref[...] + lnb_ref[...]
        x1_ref[rows, :] = x1
        h2 = _ln(x1) * (1.0 + m[4:5, :]) + m[3:4, :]
        h2_hi = h2.astype(BF16)
        h2_ref[rows, :] = h2_hi
        w = _pack_pair(h2[:, 0:D_MODEL // 2], h2[:, D_MODEL // 2:D_MODEL])
        h2p_ref[0, rows, :] = w[:, 0:SC_ROW_WORDS]
        h2p_ref[1, rows, :] = w[:, SC_ROW_WORDS:2 * SC_ROW_WORDS]
        h2_lo = (h2 - h2_hi.astype(F32)).astype(BF16)
        both = _dot(h2_hi, wr_ref[...])
        lg_ref[rows, :] = (both[:, 0:LANES] + both[:, LANES:2 * LANES]
                           + _dot(h2_lo, wr_ref[:, 0:LANES]))
    ext[0:SUBLANES, :] = ext[tm:tm + SUBLANES, :]


def _merge(yn, p, x2d, mod3, wssd, wsc, wo, cw, lng, lnb, wr, bsz, seq, tm, b0):
    t = bsz * seq
    d = x2d.shape[1]
    ns = seq // tm
    row_map = lambda b, s: (b * ns + s, 0)
    const = lambda b, s: (0, 0)
    col0 = (2048 + 4096) // d

    def pcol(k):
        return pl.BlockSpec((tm, d), lambda b, s: (b * ns + s, col0 + k))

    return pl.pallas_call(
        _merge_kernel,
        grid=(bsz, ns),
        in_specs=[pl.BlockSpec((tm, SSD_INNER), row_map),
                  pcol(0), pcol(1), pcol(2), pcol(3), pcol(4),
                  pl.BlockSpec((tm, d), lambda b, s: ((b0 + b) * ns + s, 0)),
                  pl.BlockSpec((1, 6, d), lambda b, s: (b0 + b, 0, 0)),
                  pl.BlockSpec((SSD_INNER, d), const, pipeline_mode=pl.Buffered(1)),
                  pl.BlockSpec((d, d), const, pipeline_mode=pl.Buffered(1)),
                  pl.BlockSpec((d, d), const, pipeline_mode=pl.Buffered(1)),
                  pl.BlockSpec((SC_KERNEL, d), const),
                  pl.BlockSpec((1, d), const),
                  pl.BlockSpec((1, d), const),
                  pl.BlockSpec((d, 2 * LANES), const, pipeline_mode=pl.Buffered(1))],
        out_specs=[pl.BlockSpec((tm, d), row_map),
                   pl.BlockSpec((tm, d), row_map),
                   pl.BlockSpec((2, tm, SC_ROW_WORDS), lambda b, s: (0, b * ns + s, 0)),
                   pl.BlockSpec((tm, LANES), row_map)],
        out_shape=[jax.ShapeDtypeStruct((t, d), F32),
                   jax.ShapeDtypeStruct((t, d), BF16),
                   jax.ShapeDtypeStruct((2, t, SC_ROW_WORDS), U32),
                   jax.ShapeDtypeStruct((t, LANES), F32)],
        scratch_shapes=[pltpu.VMEM((tm + SUBLANES, d), F32)],
        compiler_params=_cparams(("arbitrary", "arbitrary")),
        name="merge",
    )(yn, p, p, p, p, p, x2d, mod3, wssd, wsc, wo, cw, lng, lnb, wr)


def _first_argmax(v, idx, n):
    m = jnp.max(v, axis=0, keepdims=True)
    first = jnp.min(jnp.where(v == m, idx, n), axis=0, keepdims=True)
    return first, idx == first


def _route_select_kernel(lg_ref, bias_ref, sel_ref, ek_ref, gr_ref, cnt_ref):
    i = pl.program_id(0)
    tm = lg_ref.shape[0]
    ne, ng, eg = N_EXPERTS, N_EXPERT_GROUPS, EXPERTS_PER_GROUP
    lt = lg_ref[...].T[0:ne, :]
    scores = jax.nn.sigmoid(lt)
    biased = scores + bias_ref[...][0:ne, 0:1]
    grp = biased.reshape(ng, eg, tm)
    idx_e = lax.broadcasted_iota(jnp.int32, (ng, eg, tm), 1)
    m1 = jnp.max(grp, axis=1, keepdims=True)
    first = jnp.min(jnp.where(grp == m1, idx_e, eg), axis=1, keepdims=True)
    m2 = jnp.max(jnp.where(idx_e == first, NEG_BIG, grp), axis=1, keepdims=True)
    gscore = m1 + m2
    idx_g = lax.broadcasted_iota(jnp.int32, (ng, 1, tm), 0)
    gsel = jnp.zeros((ng, 1, tm), F32)
    for _ in range(TOPK_EXPERT_GROUPS):
        _f, hit = _first_argmax(gscore, idx_g, ng)
        gsel = jnp.where(hit, 1.0, gsel)
        gscore = jnp.where(hit, NEG_BIG, gscore)
    emask = jnp.broadcast_to(gsel, (ng, eg, tm)).reshape(ne, tm)
    cand = jnp.where(emask > 0.0, biased, NEG_BIG)
    idx_x = lax.broadcasted_iota(jnp.int32, (ne, tm), 0)
    sel = jnp.zeros((ne, tm), F32)
    firsts = []
    gate_rows = []
    for _ in range(TOP_K):
        f, hit = _first_argmax(cand, idx_x, ne)
        sel = jnp.where(hit, 1.0, sel)
        cand = jnp.where(hit, NEG_BIG, cand)
        firsts.append(f)
        gate_rows.append(jnp.sum(jnp.where(hit, scores, 0.0), axis=0, keepdims=True))
    sel_ref[...] = sel.astype(BF16)
    ek_ref[...] = jnp.concatenate(firsts, axis=0)
    gr_ref[...] = jnp.concatenate(gate_rows, axis=0)

    @pl.when(i == 0)
    def _():
        cnt_ref[...] = jnp.zeros(cnt_ref.shape, F32)

    cnt_ref[...] += jnp.broadcast_to(jnp.sum(sel, axis=1, keepdims=True), (ne, LANES))


def _route_slots_kernel(sel_ref, ek_ref, gr_ref, cnt_ref, su_ref, g_ref, dest_ref, carry, start):
    i = pl.program_id(0)
    tm = sel_ref.shape[1]
    ne = N_EXPERTS

    @pl.when(i == 0)
    def _():
        r = lax.broadcasted_iota(jnp.int32, (ne, ne), 0)
        cc = lax.broadcasted_iota(jnp.int32, (ne, ne), 1)
        below = jnp.where(cc < r, 1.0, 0.0).astype(BF16)
        padded = jnp.floor((cnt_ref[...] + (GMM_BLOCK - 1.0)) * (1.0 / GMM_BLOCK)) * GMM_BLOCK
        start[...] = _dot01_exact(below, padded)
        carry[...] = jnp.zeros(carry.shape, F32)

    sel = sel_ref[...]
    before = _dot(sel, su_ref[...])
    slot = before + carry[:, 0:1] + start[:, 0:1]
    idx_x = lax.broadcasted_iota(jnp.int32, (ne, tm), 0)
    ek = ek_ref[...]
    dest_rows = []
    for k in range(TOP_K):
        hit = idx_x == ek[k:k + 1, :]
        dest_rows.append(jnp.sum(jnp.where(hit, slot, 0.0), axis=0, keepdims=True))
    dest_ref[...] = jnp.concatenate(dest_rows, axis=0).astype(jnp.int32)
    gate = gr_ref[...]
    gate = gate / jnp.sum(gate, axis=0, keepdims=True) * ROUTED_SCALE
    pad = jnp.zeros((LANES - TOP_K, tm), F32)
    g_ref[...] = jnp.concatenate([gate, pad], axis=0).T
    carry[...] += jnp.broadcast_to(jnp.sum(sel.astype(F32), axis=1, keepdims=True), (ne, LANES))


def _route(logits, bias_col, su, tm):
    t = logits.shape[0]
    sel, ek, gr, cnt = pl.pallas_call(
        _route_select_kernel,
        grid=(t // tm,),
        in_specs=[pl.BlockSpec((tm, LANES), lambda i: (i, 0)),
                  pl.BlockSpec((LANES, LANES), lambda i: (0, 0))],
        out_specs=[pl.BlockSpec((N_EXPERTS, tm), lambda i: (0, i)),
                   pl.BlockSpec((TOP_K, tm), lambda i: (0, i)),
                   pl.BlockSpec((TOP_K, tm), lambda i: (0, i)),
                   pl.BlockSpec((N_EXPERTS, LANES), lambda i: (0, 0))],
        out_shape=[jax.ShapeDtypeStruct((N_EXPERTS, t), BF16),
                   jax.ShapeDtypeStruct((TOP_K, t), jnp.int32),
                   jax.ShapeDtypeStruct((TOP_K, t), F32),
                   jax.ShapeDtypeStruct((N_EXPERTS, LANES), F32)],
        compiler_params=_cparams(("arbitrary",)),
        name="route_select",
    )(logits, bias_col)
    gates, dest = pl.pallas_call(
        _route_slots_kernel,
        grid=(t // tm,),
        in_specs=[pl.BlockSpec((N_EXPERTS, tm), lambda i: (0, i)),
                  pl.BlockSpec((TOP_K, tm), lambda i: (0, i)),
                  pl.BlockSpec((TOP_K, tm), lambda i: (0, i)),
                  pl.BlockSpec((N_EXPERTS, LANES), lambda i: (0, 0)),
                  pl.BlockSpec((tm, tm), lambda i: (0, 0))],
        out_specs=[pl.BlockSpec((tm, LANES), lambda i: (i, 0)),
                   pl.BlockSpec((TOP_K, tm), lambda i: (0, i))],
        out_shape=[jax.ShapeDtypeStruct((t, LANES), F32),
                   jax.ShapeDtypeStruct((TOP_K, t), jnp.int32)],
        scratch_shapes=[pltpu.VMEM((N_EXPERTS, LANES), F32),
                        pltpu.VMEM((N_EXPERTS, LANES), F32)],
        compiler_params=_cparams(("arbitrary",)),
        name="route_slots",
    )(sel, ek, gr, cnt, su)
    return gates, dest, cnt


def _sc_mesh():
    return plsc.VectorSubcoreMesh(core_axis_name="c", subcore_axis_name="s")


def _sc_scatter_rows(rows, idx, n_out):
    n_rows, w = rows.shape
    n_k = idx.shape[0]

    @pl.kernel(out_type=jax.ShapeDtypeStruct((n_out, w), rows.dtype), mesh=_sc_mesh(),
               scratch_types=[])
    def scatter(x_hbm, i_hbm, o_hbm):
        def body(x_vmem, i_vmem):
            for k in range(n_k):
                pltpu.sync_copy(x_vmem, o_hbm.at[i_vmem.at[k]])

        pltpu.emit_pipeline(
            body,
            grid=(n_rows // SC_WINDOW,),
            in_specs=[pl.BlockSpec((SC_WINDOW, w), index_map=lambda i: (i, 0)),
                      pl.BlockSpec((n_k, SC_WINDOW), index_map=lambda i: (0, i))],
            out_specs=[],
            core_axis_name=("c", "s"),
            dimension_semantics=(pltpu.PARALLEL,),
        )(x_hbm, i_hbm)

    return scatter(rows, idx)


def _sc_gather_rows(rows, idx):
    n = idx.shape[1]
    w = rows.shape[1]

    @pl.kernel(out_type=jax.ShapeDtypeStruct((n, w), rows.dtype), mesh=_sc_mesh(),
               scratch_types=[])
    def gather(x_hbm, i_hbm, o_hbm):
        def body(i_vmem, o_vmem):
            pltpu.sync_copy(x_hbm.at[i_vmem.at[0]], o_vmem)

        pltpu.emit_pipeline(
            body,
            grid=(n // SC_WINDOW,),
            in_specs=[pl.BlockSpec((1, SC_WINDOW), index_map=lambda i: (0, i))],
            out_specs=[pl.BlockSpec((SC_WINDOW, w), index_map=lambda i: (i, 0))],
            core_axis_name=("c", "s"),
            dimension_semantics=(pltpu.PARALLEL,),
        )(i_hbm, o_hbm)

    return gather(rows, idx)


def _gmm_kernel(vb_ref, ve_ref, vvalid_ref, vnew_ref, x_ref, wg_ref, wu_ref, wdn_ref,
                o_ref, wgu_ref, wd_ref):
    v = pl.program_id(0)
    valid = vvalid_ref[v]
    bm = x_ref.shape[1]
    half = D_MODEL // 2

    @pl.when(vnew_ref[v] == 1)
    def _():
        wgu_ref[:, 0:D_EXPERT] = wg_ref[0].astype(BF16)
        wgu_ref[:, D_EXPERT:2 * D_EXPERT] = wu_ref[0].astype(BF16)
        wd_ref[...] = wdn_ref[0].astype(BF16)

    @pl.when(valid > 0)
    def _():
        for c in range(bm // GMM_ROWS):
            rows = slice(c * GMM_ROWS, (c + 1) * GMM_ROWS)
            a0, b0 = _unpack_pair(x_ref[0, rows, :])
            a1, b1 = _unpack_pair(x_ref[1, rows, :])
            x = jnp.concatenate([a0.astype(BF16), a1.astype(BF16),
                                 b0.astype(BF16), b1.astype(BF16)], axis=1)
            a = _dot(x, wgu_ref[...])
            act = _silu(a[:, 0:D_EXPERT]) * a[:, D_EXPERT:2 * D_EXPERT]
            r = lax.broadcasted_iota(jnp.int32, (GMM_ROWS, D_EXPERT), 0) + c * GMM_ROWS
            act = jnp.where(r < valid, act, 0.0)
            y = _dot(act.astype(BF16), wd_ref[...])
            w = _pack_pair(y[:, 0:half], y[:, half:D_MODEL])
            o_ref[0, rows, :] = w[:, 0:SC_ROW_WORDS]
            o_ref[1, rows, :] = w[:, SC_ROW_WORDS:half]


def _gmm(xin, w_gate, w_up, w_down, vb, ve, vvalid, vnew, bm):
    _, n_rows, w = xin.shape
    d = D_MODEL
    n_visits = vb.shape[0]
    blk = lambda v, vb, ve, vvalid, vnew: (0, vb[v], 0)
    wsel = lambda v, vb, ve, vvalid, vnew: (ve[v], 0, 0)
    grid_spec = pltpu.PrefetchScalarGridSpec(
        num_scalar_prefetch=4,
        grid=(n_visits,),
        in_specs=[pl.BlockSpec((2, bm, w), blk),
                  pl.BlockSpec((1, d, D_EXPERT), wsel),
                  pl.BlockSpec((1, d, D_EXPERT), wsel),
                  pl.BlockSpec((1, D_EXPERT, d), wsel)],
        out_specs=pl.BlockSpec((2, bm, w), blk),
        scratch_shapes=[pltpu.VMEM((d, 2 * D_EXPERT), BF16),
                        pltpu.VMEM((D_EXPERT, d), BF16)],
    )
    return pl.pallas_call(
        _gmm_kernel,
        grid_spec=grid_spec,
        out_shape=jax.ShapeDtypeStruct((2, n_rows, w), U32),
        compiler_params=_cparams(("arbitrary",)),
        name="gmm",
    )(vb, ve, vvalid, vnew, xin, w_gate, w_up, w_down)


def _visit_plan(counts, n_blocks, bm):
    ne = counts.shape[0]
    nb = (counts + bm - 1) // bm
    vend = jnp.cumsum(nb)
    vstart = vend - nb
    total = vend[-1]
    v = jnp.arange(n_blocks, dtype=jnp.int32)
    vc = jnp.minimum(v, total - 1)
    e_v = jnp.minimum(jnp.sum((vend[None, :] <= vc[:, None]).astype(jnp.int32), axis=1), ne - 1)
    onehot = (e_v[:, None] == jnp.arange(ne, dtype=jnp.int32)[None, :]).astype(jnp.int32)
    pick = lambda tab: jnp.sum(onehot * tab[None, :].astype(jnp.int32), axis=1)
    valid = jnp.clip(pick(counts) - (vc - pick(vstart)) * bm, 0, bm)
    valid = jnp.where(v < total, valid, 0)
    i32 = lambda a: a.astype(jnp.int32)
    e_v = i32(e_v)
    new = jnp.concatenate([jnp.ones((1,), jnp.int32), i32(e_v[1:] != e_v[:-1])])
    return i32(vc), e_v, i32(valid), new


def _combine_kernel(yg_ref, g_ref, h_ref, x1_ref, mod_ref, wsgu_ref, wsd_ref, lng_ref,
                    lnb_ref, *rest):
    o_ref = rest[-1]
    q = D_MODEL // 4
    g = g_ref[...]
    a = _dot(h_ref[...], wsgu_ref[...])
    act = _silu(a[:, 0:D_EXPERT]) * a[:, D_EXPERT:2 * D_EXPERT]
    shared = _dot(act.astype(BF16), wsd_ref[...])
    f = [shared[:, j * q:(j + 1) * q] for j in range(4)]
    for k in range(TOP_K):
        a0, b0 = _unpack_pair(yg_ref[0, k])
        a1, b1 = _unpack_pair(yg_ref[1, k])
        gk = g[:, k:k + 1]
        f = [f[0] + gk * a0, f[1] + gk * a1, f[2] + gk * b0, f[3] + gk * b1]
    ffn = jnp.concatenate(f, axis=1)
    m = mod_ref[0]
    r = ALPHA * x1_ref[...] + m[5:6, :] * ffn
    o_ref[...] = _ln(r) * lng_ref[...] + lnb_ref[...]


def _combine(yg, gates, h2, x1, mod3, wsgu, wsd, lng, lnb, seq, tm, tok0, t, prev_out):
    tp, d = x1.shape
    b0 = tok0 // tm
    row = lambda i: (i, 0)
    const = lambda i: (0, 0)
    in_specs = [pl.BlockSpec((2, TOP_K, tm, SC_ROW_WORDS), lambda i: (0, 0, i, 0)),
                pl.BlockSpec((tm, LANES), row),
                pl.BlockSpec((tm, d), row),
                pl.BlockSpec((tm, d), row),
                pl.BlockSpec((1, 6, d), lambda i: (((b0 + i) * tm) // seq, 0, 0)),
                pl.BlockSpec((d, 2 * D_EXPERT), const),
                pl.BlockSpec((D_EXPERT, d), const),
                pl.BlockSpec((1, d), const),
                pl.BlockSpec((1, d), const)]
    args = [yg, gates, h2, x1, mod3, wsgu, wsd, lng, lnb]
    aliases = {}
    if prev_out is not None:
        in_specs.append(pl.BlockSpec(memory_space=pl.ANY))
        args.append(prev_out)
        aliases = {len(args) - 1: 0}
    return pl.pallas_call(
        _combine_kernel,
        grid=(tp // tm,),
        in_specs=in_specs,
        out_specs=pl.BlockSpec((tm, d), lambda i: (b0 + i, 0)),
        out_shape=jax.ShapeDtypeStruct((t, d), F32),
        input_output_aliases=aliases,
        compiler_params=_cparams(("arbitrary",)),
        name="combine",
    )(*args)


def _pad_lanes(v):
    return jnp.zeros((1, LANES), F32).at[0, :v.shape[0]].set(v.astype(F32))


def kernel(x, c, w_ada, b_ada, w_in, ssd_conv_w, ssd_conv_b, ssd_dt_bias, ssd_A_log, ssd_D,
           ssd_norm_w, w_ssd_out, sc_conv_w, w_sc_out, w_o, ln1_g, ln1_b, router_w,
           router_bias, w_gate, w_up, w_down, sh_gate, sh_up, sh_down, ln2_g, ln2_b):
    bsz, seq, d = x.shape
    t = bsz * seq
    depth = w_in.shape[0]
    ssd_rows = 4 * SSD_CHUNK
    assert d == D_MODEL and seq % ssd_rows == 0, (x.shape,)
    tm_proj = min(1024, seq)
    tm_merge = min(512, seq)
    tm_route = min(1024, seq)
    tm_comb = min(512, seq)
    bm_gmm = GMM_BLOCK
    first = (6 * bsz) // 8 if bsz >= 2 else bsz
    groups = [(0, first)] + ([(first, bsz - first)] if bsz > first else [])

    e01 = (np.arange(LANES)[:, None] == (np.arange(SSD_INNER)[None, :] // SSD_HEAD_DIM))
    e01 = jnp.asarray(np.concatenate([e01, e01], axis=0), BF16)
    su = jnp.asarray(np.arange(tm_route)[:, None] < np.arange(tm_route)[None, :], BF16)

    xf = x.reshape(t, d)
    for l in range(depth):
        w_l = w_in[l]
        w_main = jnp.concatenate([w_l[:, :DT_COL0], w_l[:, DT_COL0 + SSD_HEADS:]],
                                 axis=1).astype(BF16)
        w_dt = jnp.pad(w_l[:, DT_COL0:DT_COL0 + SSD_HEADS],
                       ((0, 0), (0, LANES - SSD_HEADS))).astype(BF16)
        wsgu = jnp.concatenate([sh_gate[l], sh_up[l]], axis=-1).astype(BF16)
        wsd = sh_down[l].astype(BF16)
        wr = jnp.pad(router_w[l], ((0, 0), (0, LANES - N_EXPERTS)))
        wr_hi = wr.astype(BF16)
        wr = jnp.concatenate([wr_hi, (wr - wr_hi.astype(F32)).astype(BF16)], axis=1)
        bias_col = jnp.zeros((LANES, LANES), F32).at[:N_EXPERTS, 0].set(router_bias[l])
        dvec = jnp.repeat(ssd_D[l], SSD_HEAD_DIM).reshape(1, SSD_INNER)

        mod3 = _mod(c, w_ada[l], b_ada[l]).reshape(bsz, 6, d)
        x_in = xf

        def mix_and_route(b0, bg):
            tg = bg * seq
            n_assign = tg * TOP_K
            n_rows = n_assign + N_EXPERTS * bm_gmm
            p, dtp = _inproj(x_in, mod3, w_main, w_dt, seq, tm_proj, w_main.shape[1] // 4,
                             b0 * seq, tg)
            yn = _ssd(p, dtp, ssd_conv_w[l], ssd_conv_b[l].reshape(1, -1),
                      _pad_lanes(ssd_dt_bias[l]), _pad_lanes(ssd_A_log[l]), dvec,
                      ssd_norm_w[l].reshape(1, -1), e01, bg, seq, ssd_rows)
            x1, h2, h2p, logits = _merge(yn, p, x_in, mod3, w_ssd_out[l].astype(BF16),
                                         w_sc_out[l].astype(BF16), w_o[l].astype(BF16),
                                         sc_conv_w[l], ln1_g[l].reshape(1, d),
                                         ln1_b[l].reshape(1, d), wr, bg, seq, tm_merge, b0)
            gates, dest, cnt = _route(logits, bias_col, su, tm_route)
            dest_both = jnp.concatenate([dest, dest + n_rows], axis=1)
            plan = _visit_plan(cnt[:, 0].astype(jnp.int32), n_rows // bm_gmm, bm_gmm)
            xin = _sc_scatter_rows(h2p.reshape(2 * tg, SC_ROW_WORDS), dest_both, 2 * n_rows)
            return dict(x1=x1, h2=h2, gates=gates, dest=dest, plan=plan, xin=xin, tok0=b0 * seq,
                        tg=tg, n_assign=n_assign, n_rows=n_rows)

        def experts(st):
            tg, n_assign, n_rows = st["tg"], st["n_assign"], st["n_rows"]
            yb = _gmm(st["xin"].reshape(2, n_rows, SC_ROW_WORDS), w_gate[l], w_up[l], w_down[l],
                      *st["plan"], bm_gmm)
            idx = st["dest"].reshape(1, n_assign)
            idx = jnp.concatenate([idx, idx + n_rows], axis=1)
            yg = _sc_gather_rows(yb.reshape(2 * n_rows, SC_ROW_WORDS), idx)
            return yg.reshape(2, TOP_K, tg, SC_ROW_WORDS)

        stages = [mix_and_route(b0, bg) for b0, bg in groups]
        gathered = [experts(st) for st in stages]
        out = None
        for st, yg in zip(stages, gathered):
            out = _combine(yg, st["gates"], st["h2"], st["x1"], mod3, wsgu, wsd,
                           ln2_g[l].reshape(1, d), ln2_b[l].reshape(1, d), seq, tm_comb,
                           st["tok0"], t, out)
        xf = out
    return xf.reshape(bsz, seq, d)
```
